```python
import math
import jax
import jax.numpy as jnp
from jax import lax
import numpy as np

D_MODEL = 1024
BATCH = 2
SEQ = 8192
DEPTH = 2

N_A_LAYERS = DEPTH // 2
N_B_LAYERS = DEPTH - N_A_LAYERS
NORM_EPS = 1e-6

RWKV_HEAD = 64
RWKV_HEADS = D_MODEL // RWKV_HEAD
DECAY_LORA = max(32, int(round(1.8 * (D_MODEL ** 0.5) / 32)) * 32)
AAA_LORA = max(32, int(round(1.8 * (D_MODEL ** 0.5) / 32)) * 32)
GATE_LORA = max(32, int(round(0.6 * (D_MODEL ** 0.8) / 32)) * 32)
GN_EPS = RWKV_HEAD * 1e-5
N_SHIFT_MIX = 6

DIL_PATTERNS = ((128, 1), (512, 4), (2048, 16))
N_DIL_GROUPS = len(DIL_PATTERNS)
DIL_HEADS = 8
DIL_HEAD_DIM = 64
Q_WIDTH = N_DIL_GROUPS * DIL_HEADS * DIL_HEAD_DIM
ATTN_OUT_WIDTH = DIL_HEADS * DIL_HEAD_DIM

N_EXPERT_GROUPS = 4
EXPERTS_PER_GROUP = 4
N_EXPERTS = N_EXPERT_GROUPS * EXPERTS_PER_GROUP
TOP_K_INNER = 2
EXPERT_HIDDEN = D_MODEL // 2

kernel_name = "yoco_rwkv7_dilated_hmoe"


def rmsnorm(x, g):
    xf = x.astype(jnp.float32)
    y = xf * lax.rsqrt(jnp.mean(xf * xf, axis=-1, keepdims=True) + NORM_EPS)
    return y.astype(x.dtype) * g


def rwkv7_time_mix(x, mu, w_r, w_k, w_v, w0, w1, w2, a0, a1, a2, g1, g2,
                   k_k, k_a, r_k, ln_w, ln_b, w_o):
    bsz, seq, d = x.shape
    f32 = jnp.float32
    xx = jnp.pad(x, ((0, 0), (1, 0), (0, 0)))[:, :-1] - x
    xr, xw, xk, xv, xa, xg = [x + xx * mu[i] for i in range(N_SHIFT_MIX)]
    r = xr @ w_r
    k = xk @ w_k
    v = xv @ w_v
    w_log = -jax.nn.softplus(-(w0 + jnp.tanh(xw @ w1) @ w2)) - 0.5
    a = jax.nn.sigmoid(a0 + (xa @ a1) @ a2)
    g = jax.nn.sigmoid(xg @ g1) @ g2

    def heads(t):
        return t.astype(f32).reshape(bsz, seq, RWKV_HEADS, RWKV_HEAD)

    r, k, v, a, w_log = heads(r), heads(k), heads(v), heads(a), heads(w_log)
    k_k_h = k_k.astype(f32).reshape(RWKV_HEADS, RWKV_HEAD)
    k_a_h = k_a.astype(f32).reshape(RWKV_HEADS, RWKV_HEAD)
    kk = k * k_k_h
    kk = kk / jnp.maximum(jnp.sqrt(jnp.sum(kk * kk, axis=-1, keepdims=True)), 1e-12)
    k = k * (1.0 + (a - 1.0) * k_a_h)
    decay = jnp.exp(-jnp.exp(w_log))
    b = kk * a

    def step(state, inp):
        r_t, dec_t, k_t, v_t, kk_t, b_t = inp
        sa = jnp.einsum("bhvk,bhk->bhv", state, kk_t)
        state = (state * dec_t[:, :, None, :]
                 - sa[..., None] * b_t[:, :, None, :]
                 + v_t[..., None] * k_t[:, :, None, :])
        y_t = jnp.einsum("bhvk,bhk->bhv", state, r_t)
        return state, y_t

    tm = lambda t: jnp.moveaxis(t, 1, 0)
    state0 = jnp.zeros((bsz, RWKV_HEADS, RWKV_HEAD, RWKV_HEAD), f32)
    _, y = lax.scan(step, state0, (tm(r), tm(decay), tm(k), tm(v), tm(kk), tm(b)))
    y = jnp.moveaxis(y, 0, 1)
    mean = jnp.mean(y, axis=-1, keepdims=True)
    var = jnp.mean(jnp.square(y - mean), axis=-1, keepdims=True)
    y = ((y - mean) * lax.rsqrt(var + GN_EPS)).reshape(bsz, seq, d)
    y = y * ln_w.astype(f32) + ln_b.astype(f32)
    bonus = jnp.sum(r * k * r_k.astype(f32), axis=-1, keepdims=True) * v
    y = y + bonus.reshape(bsz, seq, d)
    return (y * g.astype(f32)).astype(x.dtype) @ w_o


def dilated_window_group(q, k, v, window, dilation):
    bsz, seq, nh, hd = q.shape
    n_keys = window // dilation
    span = dilation * n_keys
    s_pad = -(-seq // span) * span
    sub_len = s_pad // dilation
    n_blk = sub_len // n_keys
    scale = 1.0 / math.sqrt(hd)

    def to_sub(t):
        t = jnp.pad(t, ((0, 0), (0, s_pad - seq), (0, 0), (0, 0)))
        t = jnp.moveaxis(t.reshape(bsz, sub_len, dilation, nh, hd), 2, 1)
        return t.reshape(bsz, dilation, n_blk, n_keys, nh, hd)

    def with_prev(t):
        prev = jnp.pad(t, ((0, 0), (0, 0), (1, 0), (0, 0), (0, 0), (0, 0)))[:, :, :-1]
        return jnp.concatenate([prev, t], axis=3)

    qb = to_sub(q)
    kc = with_prev(to_sub(k))
    vc = with_prev(to_sub(v))
    s = jnp.einsum("brnqhd,brnkhd->brnhqk", qb, kc,
                   preferred_element_type=jnp.float32) * scale
    q_loc = jnp.arange(n_keys)[:, None]
    k_loc = jnp.arange(2 * n_keys)[None, :]
    band = (k_loc >= q_loc) & (k_loc <= q_loc + n_keys)
    not_before_start = (jnp.arange(n_blk)[:, None] > 0) | (jnp.arange(2 * n_keys)[None, :] >= n_keys)
    mask = band[None, :, :] & not_before_start[:, None, :]
    s = jnp.where(mask[None, None, :, None, :, :], s, jnp.finfo(jnp.float32).min)
    m = jnp.max(s, axis=-1, keepdims=True)
    p = jnp.exp(s - m)
    l = jnp.sum(p, axis=-1, keepdims=True)
    o = jnp.einsum("brnhqk,brnkhd->brnqhd", p / l, vc.astype(jnp.float32))
    lse = (m + jnp.log(l))[..., 0]
    o = jnp.moveaxis(o.reshape(bsz, dilation, sub_len, nh, hd), 1, 2)
    o = o.reshape(bsz, s_pad, nh, hd)[:, :seq]
    lse = jnp.transpose(lse, (0, 1, 2, 4, 3)).reshape(bsz, dilation, sub_len, nh)
    lse = jnp.moveaxis(lse, 1, 2).reshape(bsz, s_pad, nh)[:, :seq]
    return o, lse


def dilated_attention(xn, k_sh, v_sh, w_q, w_o):
    bsz, seq, _ = xn.shape
    q = (xn @ w_q).reshape(bsz, seq, N_DIL_GROUPS, DIL_HEADS, DIL_HEAD_DIM)
    outs, lses = [], []
    for gi, (window, dilation) in enumerate(DIL_PATTERNS):
        o, lse = dilated_window_group(q[:, :, gi], k_sh[:, :, gi], v_sh[:, :, gi], window, dilation)
        outs.append(o)
        lses.append(lse)
    alpha = jax.nn.softmax(jnp.stack(lses, axis=0), axis=0)
    o = jnp.sum(alpha[..., None] * jnp.stack(outs, axis=0), axis=0)
    return o.reshape(bsz, seq, ATTN_OUT_WIDTH).astype(xn.dtype) @ w_o


def hier_moe(x, w_grp, b_grp, w_exp, b_exp, w_gate, w_up, w_down):
    f32 = jnp.float32
    bsz, seq, d = x.shape
    t = x.reshape(bsz * seq, d)
    n_tok = t.shape[0]
    grp_logits = (t @ w_grp).astype(f32) + b_grp.astype(f32)
    grp_prob = jax.nn.softmax(grp_logits, axis=-1)
    grp = jnp.argmax(grp_logits, axis=-1)
    p_grp = jnp.take_along_axis(grp_prob, grp[:, None], axis=-1)
    exp_logits = ((t @ w_exp).astype(f32) + b_exp.astype(f32)).reshape(
        n_tok, N_EXPERT_GROUPS, EXPERTS_PER_GROUP)
    sel = jnp.take_along_axis(exp_logits, grp[:, None, None], axis=1)[:, 0]
    top_v, top_i = lax.top_k(sel, TOP_K_INNER)
    top_w = jax.nn.softmax(top_v, axis=-1)
    within = jnp.sum(jax.nn.one_hot(top_i, EXPERTS_PER_GROUP, dtype=f32) * top_w[..., None], axis=1)
    gate = (jax.nn.one_hot(grp, N_EXPERT_GROUPS, dtype=f32)[:, :, None]
            * within[:, None, :] * p_grp[:, :, None]).reshape(n_tok, N_EXPERTS).astype(x.dtype)
    y = jnp.zeros_like(t)
    for e in range(N_EXPERTS):
        hdn = jax.nn.silu(t @ w_gate[e]) * (t @ w_up[e])
        y = y + gate[:, e:e + 1] * (hdn @ w_down[e])
    return y.reshape(bsz, seq, d)


def setup_inputs(seed: int = 0) -> dict:
    key = jax.random.key(seed)
    ks = iter(jax.random.split(key, 64))
    f32 = jnp.float32
    D = D_MODEL

    def nrm(shape, scale):
        return jax.random.normal(next(ks), shape, f32) * scale

    def unif(shape, lo, hi):
        return jax.random.uniform(next(ks), shape, f32, minval=lo, maxval=hi)

    NA, NB = N_A_LAYERS, N_B_LAYERS
    F = EXPERT_HIDDEN
    return {
        "x": nrm((BATCH, SEQ, D), 1.0),
        "norm_mix_g": 1.0 + nrm((DEPTH, D), 0.02),
        "norm_ffn_g": 1.0 + nrm((DEPTH, D), 0.02),
        "rwkv_mu": unif((NA, N_SHIFT_MIX, D), 0.0, 1.0),
        "rwkv_w_r": nrm((NA, D, D), D ** -0.5),
        "rwkv_w_k": nrm((NA, D, D), D ** -0.5),
        "rwkv_w_v": nrm((NA, D, D), D ** -0.5),
        "rwkv_w0": unif((NA, D), -4.0, 1.0),
        "rwkv_w1": nrm((NA, D, DECAY_LORA), D ** -0.5),
        "rwkv_w2": nrm((NA, DECAY_LORA, D), 0.1 * DECAY_LORA ** -0.5),
        "rwkv_a0": nrm((NA, D), 0.1),
        "rwkv_a1": nrm((NA, D, AAA_LORA), D ** -0.5),
        "rwkv_a2": nrm((NA, AAA_LORA, D), 0.1 * AAA_LORA ** -0.5),
        "rwkv_g1": nrm((NA, D, GATE_LORA), D ** -0.5),
        "rwkv_g2": nrm((NA, GATE_LORA, D), GATE_LORA ** -0.5),
        "rwkv_k_k": 0.85 + nrm((NA, D), 0.02),
        "rwkv_k_a": 1.0 + nrm((NA, D), 0.02),
        "rwkv_r_k": nrm((NA, RWKV_HEADS, RWKV_HEAD), 0.1),
        "rwkv_ln_w": 1.0 + nrm((NA, D), 0.02),
        "rwkv_ln_b": nrm((NA, D), 0.02),
        "rwkv_w_o": nrm((NA, D, D), D ** -0.5),
        "kv_norm_g": 1.0 + nrm((D,), 0.02),
        "w_kv": nrm((D, 2 * Q_WIDTH), D ** -0.5),
        "attn_w_q": nrm((NB, D, Q_WIDTH), D ** -0.5),
        "attn_w_o": nrm((NB, ATTN_OUT_WIDTH, D), ATTN_OUT_WIDTH ** -0.5),
        "moe_w_grp": nrm((DEPTH, D, N_EXPERT_GROUPS), D ** -0.5),
        "moe_b_grp": nrm((DEPTH, N_EXPERT_GROUPS), 0.01),
        "moe_w_exp": nrm((DEPTH, D, N_EXPERTS), D ** -0.5),
        "moe_b_exp": nrm((DEPTH, N_EXPERTS), 0.01),
        "moe_w_gate": nrm((DEPTH, N_EXPERTS, D, F), D ** -0.5),
        "moe_w_up": nrm((DEPTH, N_EXPERTS, D, F), D ** -0.5),
        "moe_w_down": nrm((DEPTH, N_EXPERTS, F, D), F ** -0.5),
        "final_norm_g": 1.0 + nrm((D,), 0.02),
    }


def reference(x, norm_mix_g, norm_ffn_g, rwkv_mu, rwkv_w_r, rwkv_w_k, rwkv_w_v,
              rwkv_w0, rwkv_w1, rwkv_w2, rwkv_a0, rwkv_a1, rwkv_a2, rwkv_g1, rwkv_g2,
              rwkv_k_k, rwkv_k_a, rwkv_r_k, rwkv_ln_w, rwkv_ln_b, rwkv_w_o,
              kv_norm_g, w_kv, attn_w_q, attn_w_o,
              moe_w_grp, moe_b_grp, moe_w_exp, moe_b_exp, moe_w_gate, moe_w_up, moe_w_down,
              final_norm_g):
    bsz, seq, _ = x.shape
    h = x
    k_sh = None
    v_sh = None
    for layer in range(DEPTH):
        xn = rmsnorm(h, norm_mix_g[layer])
        if layer < N_A_LAYERS:
            i = layer
            h = h + rwkv7_time_mix(xn, rwkv_mu[i], rwkv_w_r[i], rwkv_w_k[i], rwkv_w_v[i],
                                   rwkv_w0[i], rwkv_w1[i], rwkv_w2[i],
                                   rwkv_a0[i], rwkv_a1[i], rwkv_a2[i],
                                   rwkv_g1[i], rwkv_g2[i], rwkv_k_k[i], rwkv_k_a[i],
                                   rwkv_r_k[i], rwkv_ln_w[i], rwkv_ln_b[i], rwkv_w_o[i])
        else:
            i = layer - N_A_LAYERS
            h = h + dilated_attention(xn, k_sh, v_sh, attn_w_q[i], attn_w_o[i])
        h = h + hier_moe(rmsnorm(h, norm_ffn_g[layer]), moe_w_grp[layer], moe_b_grp[layer],
                         moe_w_exp[layer], moe_b_exp[layer], moe_w_gate[layer],
                         moe_w_up[layer], moe_w_down[layer])
        if layer == N_A_LAYERS - 1:
            kv = rmsnorm(h, kv_norm_g) @ w_kv
            k_sh = kv[..., :Q_WIDTH].reshape(bsz, seq, N_DIL_GROUPS, DIL_HEADS, DIL_HEAD_DIM)
            v_sh = kv[..., Q_WIDTH:].reshape(bsz, seq, N_DIL_GROUPS, DIL_HEADS, DIL_HEAD_DIM)
    return rmsnorm(h, final_norm_g)
```

```python
import functools
import math

import jax
import jax.numpy as jnp
from jax import lax
from jax.experimental import pallas as pl
from jax.experimental.pallas import tpu as pltpu

F32 = jnp.float32
BF16 = jnp.bfloat16
HIGHEST = lax.Precision.HIGHEST

NORM_EPS = 1e-6
RWKV_HEAD = 64
GN_EPS = RWKV_HEAD * 1e-5
DIL_PATTERNS = ((128, 1), (512, 4), (2048, 16))
DIL_HEADS = 8
DIL_HEAD_DIM = 64
N_EXPERT_GROUPS = 4
EXPERTS_PER_GROUP = 4
LANES = 128
VMEM_LIMIT = 56 * 1024 * 1024

CHUNK = 64
REC_ROWS = 256
PRE_ROWS = 256
PROJ_ROWS = 512
MOE_BLOCK = 1024
FFN_ROWS = 256
ATT_BLOCK = 128


def _cparams(*sem):
    return pltpu.CompilerParams(dimension_semantics=sem, vmem_limit_bytes=VMEM_LIMIT)


def _dot(a, b):
    return jnp.dot(a, b, preferred_element_type=F32)


def _dot_nt(a, b):
    return lax.dot_general(a, b, (((1,), (1,)), ((), ())), preferred_element_type=F32)


def _dot_tn(a, b, precision=None):
    return lax.dot_general(a, b, (((0,), (0,)), ((), ())), preferred_element_type=F32, precision=precision)


def _rms(x):
    return x * lax.rsqrt(jnp.mean(x * x, axis=-1, keepdims=True) + NORM_EPS)


def _sigmoid(z):
    return 1.0 / (1.0 + jnp.exp(-z))


def _const_spec(shape):
    nd = len(shape)
    return pl.BlockSpec(shape, lambda *_: (0,) * nd)


def _rwkv_pre_kernel(x_ref, gmix_ref, mu_ref, w0_ref, a0_ref, wr_ref, wk_ref, wv_ref,
                     w1_ref, w2_ref, a1_ref, a2_ref, g1_ref, g2_ref,
                     r_ref, k_ref, v_ref, a_ref, lw_ref, g_ref, prev_ref):
    @pl.when(pl.program_id(1) == 0)
    def _():
        prev_ref[...] = jnp.zeros_like(prev_ref)

    x = x_ref[0]
    rows = x.shape[0]
    xn = _rms(x) * gmix_ref[...]
    row = lax.broadcasted_iota(jnp.int32, xn.shape, 0)
    shifted = jnp.where(row == 0, prev_ref[7:8, :], pltpu.roll(xn, 1, 0))
    prev_ref[...] = xn[rows - 8:, :]
    xx = shifted - xn

    def mix(i):
        return (xn + xx * mu_ref[i:i + 1, :]).astype(BF16)

    r_ref[0] = _dot(mix(0), wr_ref[...]).astype(BF16)
    k_ref[0] = _dot(mix(2), wk_ref[...]).astype(BF16)
    v_ref[0] = _dot(mix(3), wv_ref[...]).astype(BF16)
    u = w0_ref[...] + _dot(jnp.tanh(_dot(mix(1), w1_ref[...])).astype(BF16), w2_ref[...])
    w_log = -(jnp.maximum(-u, 0.0) + jnp.log(1.0 + jnp.exp(-jnp.abs(u)))) - 0.5
    lw_ref[0] = -jnp.exp(w_log)
    a_ref[0] = _sigmoid(a0_ref[...] + _dot(_dot(mix(4), a1_ref[...]).astype(BF16), a2_ref[...])).astype(BF16)
    g_ref[0] = _dot(_sigmoid(_dot(mix(5), g1_ref[...])).astype(BF16), g2_ref[...]).astype(BF16)


def _pad_cols(w, n):
    return jnp.pad(w, ((0, 0), (0, n - w.shape[1])))


def _pad_rows(w, n):
    return jnp.pad(w, ((0, n - w.shape[0]), (0, 0)))


def _rwkv_pre(x, gmix, mu, w0, a0, w_r, w_k, w_v, w1, w2, a1, a2, g1, g2):
    bsz, seq, d = x.shape
    lw = -(-w1.shape[1] // LANES) * LANES
    la = -(-a1.shape[1] // LANES) * LANES
    lg = -(-g1.shape[1] // LANES) * LANES
    mu8 = jnp.pad(mu, ((0, 8 - mu.shape[0]), (0, 0)))
    args = (x, gmix.reshape(1, d), mu8, w0.reshape(1, d), a0.reshape(1, d),
            w_r.astype(BF16), w_k.astype(BF16), w_v.astype(BF16),
            _pad_cols(w1, lw).astype(BF16), _pad_rows(w2, lw).astype(BF16),
            _pad_cols(a1, la).astype(BF16), _pad_rows(a2, la).astype(BF16),
            _pad_cols(g1, lg).astype(BF16), _pad_rows(g2, lg).astype(BF16))
    tile = pl.BlockSpec((1, PRE_ROWS, d), lambda b, s: (b, s, 0))
    in_specs = [tile] + [_const_spec(a.shape) for a in args[1:]]
    out_dt = (BF16, BF16, BF16, BF16, F32, BF16)
    return pl.pallas_call(
        _rwkv_pre_kernel,
        grid=(bsz, seq // PRE_ROWS),
        in_specs=in_specs,
        out_specs=[tile] * 6,
        out_shape=[jax.ShapeDtypeStruct((bsz, seq, d), t) for t in out_dt],
        scratch_shapes=[pltpu.VMEM((8, d), F32)],
        compiler_params=_cparams("arbitrary", "arbitrary"),
        name="rwkv_pre",
    )(*args)


def _unit_lower_inverse(n_mat, eye):
    inv = eye + n_mat
    power = n_mat
    steps = int(math.log2(CHUNK)) - 1
    for _ in range(steps):
        pb = power.astype(BF16)
        power = _dot(pb, pb)
        inv = inv + _dot(inv.astype(BF16), power.astype(BF16))
    return inv


def _rwkv_rec_kernel(r_ref, k_ref, v_ref, a_ref, lw_ref, g_ref, kk_ref, ka_ref, rk_ref, lnw_ref, lnb_ref,
                     o_ref, state_ref):
    @pl.when(pl.program_id(2) == 0)
    def _():
        state_ref[...] = jnp.zeros_like(state_ref)

    c = CHUNK
    n_chunks = r_ref.shape[1] // c
    heads = r_ref.shape[2] // RWKV_HEAD
    ri = lax.broadcasted_iota(jnp.int32, (c, c), 0)
    ci = lax.broadcasted_iota(jnp.int32, (c, c), 1)
    strict = ri > ci
    incl = ri >= ci
    tri = incl.astype(F32)
    eye = (ri == ci).astype(F32)

    for h in range(heads):
        hs = slice(h * RWKV_HEAD, (h + 1) * RWKV_HEAD)
        kkp, kap, rkp = kk_ref[:, hs], ka_ref[:, hs], rk_ref[:, hs]
        lnw, lnb = lnw_ref[:, hs], lnb_ref[:, hs]
        pre = []
        for j in range(n_chunks):
            rs = slice(j * c, (j + 1) * c)
            r_ = r_ref[0, rs, hs].astype(F32)
            k_ = k_ref[0, rs, hs].astype(F32)
            v_ = v_ref[0, rs, hs].astype(F32)
            a_ = a_ref[0, rs, hs].astype(F32)
            lw_ = lw_ref[0, rs, hs]
            kk = k_ * kkp
            kk = kk / jnp.maximum(jnp.sqrt(jnp.sum(kk * kk, axis=-1, keepdims=True)), 1e-12)
            k2 = k_ * (1.0 + (a_ - 1.0) * kap)
            p_ = -(kk * a_)
            cum = jnp.dot(tri, lw_, precision=HIGHEST, preferred_element_type=F32)
            tot = cum[c - 1:c, :]
            e_neg = jnp.exp(-cum)
            e_tot = jnp.exp(tot - cum)
            q_t = kk * jnp.exp(cum - lw_)
            r_t = r_ * jnp.exp(cum)
            qr = jnp.concatenate([q_t, r_t], axis=0).astype(BF16)
            kp = jnp.concatenate([k2 * e_neg, p_ * e_neg], axis=0).astype(BF16)
            amat = _dot_nt(qr, kp)
            a_qk = jnp.where(strict, amat[:c, :c], 0.0)
            a_qp = jnp.where(strict, amat[:c, c:], 0.0)
            a_rk = jnp.where(incl, amat[c:, :c], 0.0)
            a_rp = jnp.where(incl, amat[c:, c:], 0.0)
            vb = v_.astype(BF16)
            av = _dot(jnp.concatenate([a_qk, a_rk], axis=0).astype(BF16), vb)
            inv = _unit_lower_inverse(a_qp, eye)
            wu = _dot(inv.astype(BF16), jnp.concatenate([q_t, av[:c]], axis=1).astype(BF16))
            wub = wu.astype(BF16)
            ry = jnp.concatenate([r_t, av[c:]], axis=1) + _dot(a_rp.astype(BF16), wub)
            lhs = jnp.concatenate([p_ * e_tot, k2 * e_tot], axis=0).astype(BF16)
            zeros = jnp.zeros((c, RWKV_HEAD), BF16)
            rhs = jnp.concatenate([wub, jnp.concatenate([zeros, vb], axis=1)], axis=0)
            mg = _dot_tn(lhs, rhs)
            bonus = jnp.sum(r_ * k2 * rkp, axis=-1, keepdims=True) * v_
            gate = g_ref[0, rs, hs].astype(F32)
            pre.append((ry, mg, jnp.exp(tot), bonus, gate))

        state = state_ref[h]
        for j in range(n_chunks):
            ry, mg, gam, bonus, gate = pre[j]
            rs = slice(j * c, (j + 1) * c)
            sb = state.astype(BF16)
            y = _dot(ry[:, :RWKV_HEAD].astype(BF16), sb) + ry[:, RWKV_HEAD:]
            gam_col = jnp.sum(eye * gam, axis=-1, keepdims=True)
            state = gam_col * state + _dot(mg[:, :RWKV_HEAD].astype(BF16), sb) + mg[:, RWKV_HEAD:]
            mean = jnp.mean(y, axis=-1, keepdims=True)
            yc = y - mean
            var = jnp.mean(yc * yc, axis=-1, keepdims=True)
            yn = yc * lax.rsqrt(var + GN_EPS) * lnw + lnb
            o_ref[0, rs, hs] = ((yn + bonus) * gate).astype(BF16)
        state_ref[h] = state


def _rwkv_rec(r, k, v, a, lw, g, k_k, k_a, r_k, ln_w, ln_b):
    bsz, seq, d = r.shape
    hw = 2 * RWKV_HEAD
    tile = pl.BlockSpec((1, REC_ROWS, hw), lambda b, h, s: (b, s, h))
    par = pl.BlockSpec((1, hw), lambda b, h, s: (0, h))
    params = [p.reshape(1, d).astype(F32) for p in (k_k, k_a, r_k, ln_w, ln_b)]
    return pl.pallas_call(
        _rwkv_rec_kernel,
        grid=(bsz, d // hw, seq // REC_ROWS),
        in_specs=[tile] * 6 + [par] * 5,
        out_specs=tile,
        out_shape=jax.ShapeDtypeStruct((bsz, seq, d), BF16),
        scratch_shapes=[pltpu.VMEM((hw // RWKV_HEAD, RWKV_HEAD, RWKV_HEAD), F32)],
        compiler_params=_cparams("arbitrary", "arbitrary", "arbitrary"),
        name="rwkv_rec",
    )(r, k, v, a, lw, g, *params)


def _proj_res_kernel(a_ref, w_ref, res_ref, o_ref):
    o_ref[...] = res_ref[...] + _dot(a_ref[...], w_ref[...])


def _proj_res(a, w, res):
    n, kdim = a.shape
    d = w.shape[1]
    return pl.pallas_call(
        _proj_res_kernel,
        grid=(n // PROJ_ROWS,),
        in_specs=[pl.BlockSpec((PROJ_ROWS, kdim), lambda i: (i, 0)), _const_spec(w.shape),
                  pl.BlockSpec((PROJ_ROWS, d), lambda i: (i, 0))],
        out_specs=pl.BlockSpec((PROJ_ROWS, d), lambda i: (i, 0)),
        out_shape=jax.ShapeDtypeStruct((n, d), F32),
        compiler_params=_cparams("arbitrary"),
        name="proj_res",
    )(a, w.astype(BF16), res)


def _moe_route_kernel(h_ref, g_ref, wr_ref, br_ref, xs_ref, gs_ref, dest_ref, cnt_ref):
    nb = h_ref.shape[0]
    nbp = xs_ref.shape[0]
    t = _rms(h_ref[...]) * g_ref[...]
    logits = jnp.dot(t, wr_ref[...], precision=HIGHEST, preferred_element_type=F32) + br_ref[...]
    lane = lax.broadcasted_iota(jnp.int32, logits.shape, 1).astype(F32)
    neg = jnp.float32(-jnp.inf)
    big = jnp.float32(LANES)
    is_grp = lane < N_EXPERT_GROUPS
    gl = jnp.where(is_grp, logits, neg)
    gmax = jnp.max(gl, axis=-1, keepdims=True)
    grp = jnp.min(jnp.where(gl == gmax, lane, big), axis=-1, keepdims=True)
    p_grp = 1.0 / jnp.sum(jnp.where(is_grp, jnp.exp(gl - gmax), 0.0), axis=-1, keepdims=True)
    e_lo = N_EXPERT_GROUPS + grp * EXPERTS_PER_GROUP
    in_grp = (lane >= e_lo) & (lane < e_lo + EXPERTS_PER_GROUP)
    el = jnp.where(in_grp, logits, neg)
    v1 = jnp.max(el, axis=-1, keepdims=True)
    i1 = jnp.min(jnp.where(el == v1, lane, big), axis=-1, keepdims=True)
    el2 = jnp.where(lane == i1, neg, el)
    v2 = jnp.max(el2, axis=-1, keepdims=True)
    i2 = jnp.min(jnp.where(el2 == v2, lane, big), axis=-1, keepdims=True)
    e2 = jnp.exp(v2 - v1)
    w1 = 1.0 / (1.0 + e2)
    w2 = e2 / (1.0 + e2)
    gates = jnp.where(lane == i1, w1, jnp.where(lane == i2, w2, 0.0)) * p_grp

    onehot = (lane == grp).astype(BF16)
    ri = lax.broadcasted_iota(jnp.int32, (nb, nb), 0)
    ci = lax.broadcasted_iota(jnp.int32, (nb, nb), 1)
    rank = _dot((ri > ci).astype(BF16), onehot)
    counts = jnp.sum(onehot.astype(F32), axis=0, keepdims=True)
    padded = jnp.ceil(counts / FFN_ROWS) * FFN_ROWS
    lane1 = lax.broadcasted_iota(jnp.int32, (1, LANES), 1)
    offs = jnp.zeros((1, LANES), F32)
    for gidx in range(1, N_EXPERT_GROUPS):
        prev = jnp.sum(jnp.where(lane1 < gidx, padded, 0.0), axis=-1, keepdims=True)
        offs = jnp.where(lane1 == gidx, prev, offs)
    dest = jnp.sum(onehot.astype(F32) * (rank + offs), axis=-1, keepdims=True)
    slot = lax.broadcasted_iota(jnp.int32, (nb, nbp), 1).astype(F32)
    perm_t = (dest == slot).astype(BF16)
    xs_ref[...] = _dot_tn(perm_t, t.astype(BF16)).astype(BF16)
    g_hi = gates.astype(BF16)
    g_lo = (gates - g_hi.astype(F32)).astype(BF16)
    moved = _dot_tn(perm_t, jnp.concatenate([g_hi, g_lo], axis=1))
    gs_ref[...] = moved[:, :LANES] + moved[:, LANES:]
    dest_ref[...] = dest
    cnt_ref[0] = jnp.where(lane1 < N_EXPERT_GROUPS, counts, 0.0)


def _moe_ffn_kernel(tile_ref, grp_ref, nvalid_ref, xs_ref, gs_ref, wg_ref, wu_ref, wd_ref, ys_ref):
    i = pl.program_id(0)

    @pl.when(i < nvalid_ref[0])
    def _():
        x = xs_ref[...]
        gates = gs_ref[...]
        lane = lax.broadcasted_iota(jnp.int32, gates.shape, 1)
        base = N_EXPERT_GROUPS + grp_ref[i] * EXPERTS_PER_GROUP
        acc = jnp.zeros(ys_ref.shape, F32)
        for e in range(EXPERTS_PER_GROUP):
            ge = jnp.sum(jnp.where(lane == base + e, gates, 0.0), axis=-1, keepdims=True)
            gate_act = _dot(x, wg_ref[e])
            hdn = gate_act * _sigmoid(gate_act) * _dot(x, wu_ref[e])
            acc = acc + ge * _dot(hdn.astype(BF16), wd_ref[e])
        ys_ref[...] = acc.astype(BF16)

    @pl.when(i >= nvalid_ref[0])
    def _():
        ys_ref[...] = jnp.zeros_like(ys_ref)


def _moe_merge_kernel(h_ref, dest_ref, ys_ref, o_ref):
    nb, nbp = h_ref.shape[0], ys_ref.shape[0]
    slot = lax.broadcasted_iota(jnp.int32, (nb, nbp), 1).astype(F32)
    perm_t = (dest_ref[...] == slot).astype(BF16)
    o_ref[...] = h_ref[...] + _dot(perm_t, ys_ref[...])


def _moe_merge_norm_kernel(h_ref, dest_ref, ys_ref, g_ref, o_ref):
    nb, nbp = h_ref.shape[0], ys_ref.shape[0]
    slot = lax.broadcasted_iota(jnp.int32, (nb, nbp), 1).astype(F32)
    perm_t = (dest_ref[...] == slot).astype(BF16)
    o_ref[...] = _rms(h_ref[...] + _dot(perm_t, ys_ref[...])) * g_ref[...]


def _hier_moe_residual(h, norm_g, w_grp, b_grp, w_exp, b_exp, w_gate, w_up, w_down, final_g=None):
    n, d = h.shape
    nb = MOE_BLOCK
    nblk = n // nb
    nbp = nb + N_EXPERT_GROUPS * FFN_ROWS
    tiles_per_blk = nbp // FFN_ROWS
    n_exp = N_EXPERT_GROUPS * EXPERTS_PER_GROUP
    w_router = jnp.pad(jnp.concatenate([w_grp, w_exp], axis=1), ((0, 0), (0, LANES - N_EXPERT_GROUPS - n_exp)))
    b_router = jnp.pad(jnp.concatenate([b_grp, b_exp]), (0, LANES - N_EXPERT_GROUPS - n_exp)).reshape(1, LANES)

    xs, gs, dest, counts = pl.pallas_call(
        _moe_route_kernel,
        grid=(nblk,),
        in_specs=[pl.BlockSpec((nb, d), lambda i: (i, 0)), _const_spec((1, d)),
                  _const_spec((d, LANES)), _const_spec((1, LANES))],
        out_specs=[pl.BlockSpec((nbp, d), lambda i: (i, 0)), pl.BlockSpec((nbp, LANES), lambda i: (i, 0)),
                   pl.BlockSpec((nb, 1), lambda i: (i, 0)), pl.BlockSpec((1, 1, LANES), lambda i: (i, 0, 0))],
        out_shape=[jax.ShapeDtypeStruct((nblk * nbp, d), BF16), jax.ShapeDtypeStruct((nblk * nbp, LANES), F32),
                   jax.ShapeDtypeStruct((n, 1), F32), jax.ShapeDtypeStruct((nblk, 1, LANES), F32)],
        compiler_params=_cparams("arbitrary"),
        name="moe_route",
    )(h, norm_g.reshape(1, d), w_router, b_router)

    cnt = counts[:, 0, :N_EXPERT_GROUPS].astype(jnp.int32)
    tiles = (cnt + FFN_ROWS - 1) // FFN_ROWS
    first = jnp.cumsum(tiles, axis=1) - tiles + (jnp.arange(nblk, dtype=jnp.int32) * tiles_per_blk)[:, None]
    order_tiles = tiles.T.reshape(-1)
    order_first = first.T.reshape(-1)
    order_grp = jnp.repeat(jnp.arange(N_EXPERT_GROUPS, dtype=jnp.int32), nblk)
    ends = jnp.cumsum(order_tiles)
    n_valid = ends[-1]
    max_tiles = nblk * tiles_per_blk
    step = jnp.arange(max_tiles, dtype=jnp.int32)
    seg = jnp.minimum(jnp.sum((ends[None, :] <= step[:, None]).astype(jnp.int32), axis=1), order_tiles.shape[0] - 1)
    within = step - (ends[seg] - order_tiles[seg])
    used_idx = order_first[seg] + within
    unused = (step % tiles_per_blk) >= jnp.sum(tiles, axis=1)[step // tiles_per_blk]
    unused_rank = jnp.cumsum(unused.astype(jnp.int32)) - 1
    want = jnp.maximum(step - n_valid, 0)
    unused_idx = jnp.sum(jnp.where(unused[None, :] & (unused_rank[None, :] == want[:, None]), step[None, :], 0), axis=1)
    tile_idx = jnp.where(step < n_valid, used_idx, unused_idx).astype(jnp.int32)
    last_valid = jnp.maximum(n_valid - 1, 0)
    tile_grp = jnp.where(step < n_valid, order_grp[seg], order_grp[seg][last_valid]).astype(jnp.int32)

    def wspec(shape):
        return pl.BlockSpec((EXPERTS_PER_GROUP,) + shape, lambda i, tile, grp, nv: (grp[i], 0, 0))

    f = w_gate.shape[-1]
    ys = pl.pallas_call(
        _moe_ffn_kernel,
        grid_spec=pltpu.PrefetchScalarGridSpec(
            num_scalar_prefetch=3,
            grid=(max_tiles,),
            in_specs=[pl.BlockSpec((FFN_ROWS, d), lambda i, tile, grp, nv: (tile[i], 0)),
                      pl.BlockSpec((FFN_ROWS, LANES), lambda i, tile, grp, nv: (tile[i], 0)),
                      wspec((d, f)), wspec((d, f)), wspec((f, d))],
            out_specs=pl.BlockSpec((FFN_ROWS, d), lambda i, tile, grp, nv: (tile[i], 0)),
        ),
        out_shape=jax.ShapeDtypeStruct((nblk * nbp, d), BF16),
        compiler_params=_cparams("arbitrary"),
        name="moe_ffn",
    )(tile_idx, tile_grp, n_valid.reshape(1).astype(jnp.int32), xs, gs,
      w_gate.astype(BF16), w_up.astype(BF16), w_down.astype(BF16))

    blk = pl.BlockSpec((nb, d), lambda i: (i, 0))
    in_specs = [blk, pl.BlockSpec((nb, 1), lambda i: (i, 0)), pl.BlockSpec((nbp, d), lambda i: (i, 0))]
    args = [h, dest, ys]
    body = _moe_merge_kernel
    if final_g is not None:
        in_specs.append(_const_spec((1, d)))
        args.append(final_g.reshape(1, d))
        body = _moe_merge_norm_kernel
    return pl.pallas_call(
        body,
        grid=(nblk,),
        in_specs=in_specs,
        out_specs=blk,
        out_shape=jax.ShapeDtypeStruct((n, d), F32),
        compiler_params=_cparams("arbitrary"),
        name="moe_merge",
    )(*args)


def _qkv_kernel(h_ref, gq_ref, gkv_ref, wq_ref, wk_ref, wv_ref, q_ref, k_ref, v_ref):
    n = _rms(h_ref[...])
    xq = (n * gq_ref[...]).astype(BF16)
    xkv = (n * gkv_ref[...]).astype(BF16)
    q_ref[...] = _dot(xq, wq_ref[...]).astype(BF16)
    k_ref[...] = _dot(xkv, wk_ref[...]).astype(BF16)
    v_ref[...] = _dot(xkv, wv_ref[...]).astype(BF16)


def _qkv_proj(h, g_q, g_kv, w_q, w_kv):
    n, d = h.shape
    qw = w_q.shape[1]
    scale = 1.0 / math.sqrt(DIL_HEAD_DIM)
    rows = pl.BlockSpec((PROJ_ROWS, d), lambda i: (i, 0))
    out = pl.BlockSpec((PROJ_ROWS, qw), lambda i: (i, 0))
    return pl.pallas_call(
        _qkv_kernel,
        grid=(n // PROJ_ROWS,),
        in_specs=[rows, _const_spec((1, d)), _const_spec((1, d))] + [_const_spec((d, qw))] * 3,
        out_specs=[out] * 3,
        out_shape=[jax.ShapeDtypeStruct((n, qw), BF16)] * 3,
        compiler_params=_cparams("arbitrary"),
        name="qkv_proj",
    )(h, g_q.reshape(1, d), g_kv.reshape(1, d), (w_q * scale).astype(BF16),
      w_kv[:, :qw].astype(BF16), w_kv[:, qw:].astype(BF16))


def _attn_kernel(q_ref, kc_ref, kp_ref, vc_ref, vp_ref, o_ref, lse_ref):
    n = pl.program_id(2)
    blk = q_ref.shape[1]
    ri = lax.broadcasted_iota(jnp.int32, (blk, blk), 0)
    ci = lax.broadcasted_iota(jnp.int32, (blk, blk), 1)
    prev_ok = (ci >= ri) & (n > 0)
    cur_ok = ci <= ri
    lowest = jnp.finfo(F32).min
    for h in range(DIL_HEADS):
        hs = slice(h * DIL_HEAD_DIM, (h + 1) * DIL_HEAD_DIM)
        q = q_ref[0, :, hs]
        s_p = jnp.where(prev_ok, _dot_nt(q, kp_ref[0, :, hs]), lowest)
        s_c = jnp.where(cur_ok, _dot_nt(q, kc_ref[0, :, hs]), lowest)
        m = jnp.maximum(jnp.max(s_p, axis=-1, keepdims=True), jnp.max(s_c, axis=-1, keepdims=True))
        p_p = jnp.exp(s_p - m)
        p_c = jnp.exp(s_c - m)
        l = jnp.sum(p_p, axis=-1, keepdims=True) + jnp.sum(p_c, axis=-1, keepdims=True)
        acc = _dot(p_p.astype(BF16), vp_ref[0, :, hs]) + _dot(p_c.astype(BF16), vc_ref[0, :, hs])
        o_ref[0, :, hs] = (acc / l).astype(BF16)
        lse_ref[0, :, hs] = jnp.broadcast_to(m + jnp.log(l), (blk, DIL_HEAD_DIM))


def _attn_group(q, k, v, gi, dilation, bsz, seq):
    width = q.shape[1]
    gw = DIL_HEADS * DIL_HEAD_DIM
    per_row = width // gw
    sub = seq // dilation
    nblk = sub // ATT_BLOCK
    view = (bsz, sub, dilation * width)
    qv, kv_, vv = q.reshape(view), k.reshape(view), v.reshape(view)
    cur = pl.BlockSpec((1, ATT_BLOCK, gw), lambda b, r, n: (b, n, r * per_row + gi))
    prev = pl.BlockSpec((1, ATT_BLOCK, gw), lambda b, r, n: (b, jnp.maximum(n - 1, 0), r * per_row + gi))
    out = pl.BlockSpec((1, ATT_BLOCK, gw), lambda b, r, n: (b, n, r))
    o, lse = pl.pallas_call(
        _attn_kernel,
        grid=(bsz, dilation, nblk),
        in_specs=[cur, cur, prev, cur, prev],
        out_specs=[out, out],
        out_shape=[jax.ShapeDtypeStruct((bsz, sub, dilation * gw), BF16),
                   jax.ShapeDtypeStruct((bsz, sub, dilation * gw), F32)],
        compiler_params=_cparams("arbitrary", "arbitrary", "arbitrary"),
        name=f"attn_group{gi}",
    )(qv, kv_, kv_, vv, vv)
    return o.reshape(bsz * seq, gw), lse.reshape(bsz * seq, gw)


def _attn_merge_kernel(o0_ref, o1_ref, o2_ref, l0_ref, l1_ref, l2_ref, w_ref, res_ref, out_ref):
    l0, l1, l2 = l0_ref[...], l1_ref[...], l2_ref[...]
    m = jnp.maximum(jnp.maximum(l0, l1), l2)
    e0, e1, e2 = jnp.exp(l0 - m), jnp.exp(l1 - m), jnp.exp(l2 - m)
    mix = (e0 * o0_ref[...].astype(F32) + e1 * o1_ref[...].astype(F32) + e2 * o2_ref[...].astype(F32)) / (e0 + e1 + e2)
    out_ref[...] = res_ref[...] + _dot(mix.astype(BF16), w_ref[...])


def _attn_merge(outs, lses, w_o, res):
    n, gw = outs[0].shape
    d = w_o.shape[1]
    rows = pl.BlockSpec((PROJ_ROWS, gw), lambda i: (i, 0))
    full = pl.BlockSpec((PROJ_ROWS, d), lambda i: (i, 0))
    return pl.pallas_call(
        _attn_merge_kernel,
        grid=(n // PROJ_ROWS,),
        in_specs=[rows] * 6 + [_const_spec(w_o.shape), full],
        out_specs=full,
        out_shape=jax.ShapeDtypeStruct((n, d), F32),
        compiler_params=_cparams("arbitrary"),
        name="attn_merge",
    )(*outs, *lses, w_o.astype(BF16), res)


def kernel(x, norm_mix_g, norm_ffn_g, rwkv_mu, rwkv_w_r, rwkv_w_k, rwkv_w_v, rwkv_w0, rwkv_w1, rwkv_w2,
           rwkv_a0, rwkv_a1, rwkv_a2, rwkv_g1, rwkv_g2, rwkv_k_k, rwkv_k_a, rwkv_r_k, rwkv_ln_w, rwkv_ln_b,
           rwkv_w_o, kv_norm_g, w_kv, attn_w_q, attn_w_o, moe_w_grp, moe_b_grp, moe_w_exp, moe_b_exp,
           moe_w_gate, moe_w_up, moe_w_down, final_norm_g):
    bsz, seq, d = x.shape
    n = bsz * seq
    depth = norm_mix_g.shape[0]
    n_rwkv = rwkv_mu.shape[0]
    h = x.reshape(n, d)
    q = k_sh = v_sh = None
    for layer in range(depth):
        if layer < n_rwkv:
            i = layer
            r, k, v, a, lw, g = _rwkv_pre(h.reshape(bsz, seq, d), norm_mix_g[layer], rwkv_mu[i], rwkv_w0[i], rwkv_a0[i],
                                          rwkv_w_r[i], rwkv_w_k[i], rwkv_w_v[i], rwkv_w1[i], rwkv_w2[i],
                                          rwkv_a1[i], rwkv_a2[i], rwkv_g1[i], rwkv_g2[i])
            y = _rwkv_rec(r, k, v, a, lw, g, rwkv_k_k[i], rwkv_k_a[i], rwkv_r_k[i], rwkv_ln_w[i], rwkv_ln_b[i])
            h = _proj_res(y.reshape(n, d), rwkv_w_o[i], h)
        else:
            i = layer - n_rwkv
            if i == 0:
                q, k_sh, v_sh = _qkv_proj(h, norm_mix_g[layer], kv_norm_g, attn_w_q[i], w_kv)
            else:
                q = _qkv_proj(h, norm_mix_g[layer], kv_norm_g, attn_w_q[i], w_kv)[0]
            outs, lses = [], []
            for gi, (window, dilation) in enumerate(DIL_PATTERNS):
                assert window // dilation == ATT_BLOCK and seq % window == 0
                o, lse = _attn_group(q, k_sh, v_sh, gi, dilation, bsz, seq)
                outs.append(o)
                lses.append(lse)
            h = _attn_merge(outs, lses, attn_w_o[i], h)
        last = layer == depth - 1
        h = _hier_moe_residual(h, norm_ffn_g[layer], moe_w_grp[layer], moe_b_grp[layer], moe_w_exp[layer],
                               moe_b_exp[layer], moe_w_gate[layer], moe_w_up[layer], moe_w_down[layer],
                               final_g=final_norm_g if last else None)
    return h.reshape(bsz, seq, d)
```

```python
import functools
import math

import jax
import jax.numpy as jnp
from jax import lax
from jax.experimental import pallas as pl
from jax.experimental.pallas import tpu as pltpu

F32 = jnp.float32
BF16 = jnp.bfloat16
HIGHEST = lax.Precision.HIGHEST

NORM_EPS = 1e-6
RWKV_HEAD = 64
GN_EPS = RWKV_HEAD * 1e-5
DIL_PATTERNS = ((128, 1), (512, 4), (2048, 16))
DIL_HEADS = 8
DIL_HEAD_DIM = 64
N_EXPERT_GROUPS = 4
EXPERTS_PER_GROUP = 4
LANES = 128
VMEM_LIMIT = 56 * 1024 * 1024

CHUNK = 64
REC_ROWS = 512
PRE_ROWS = 256
PROJ_ROWS = 512
MOE_BLOCK = 1024
FFN_ROWS = 256
ATT_BLOCK = 128


def _cparams(*sem):
    return pltpu.CompilerParams(dimension_semantics=sem, vmem_limit_bytes=VMEM_LIMIT)


def _dot(a, b):
    return jnp.dot(a, b, preferred_element_type=F32)


def _dot_nt(a, b):
    return lax.dot_general(a, b, (((1,), (1,)), ((), ())), preferred_element_type=F32)


def _dot_tn(a, b, precision=None):
    return lax.dot_general(a, b, (((0,), (0,)), ((), ())), preferred_element_type=F32, precision=precision)


def _rms(x):
    return x * lax.rsqrt(jnp.mean(x * x, axis=-1, keepdims=True) + NORM_EPS)


def _sigmoid(z):
    return 1.0 / (1.0 + jnp.exp(-z))


def _const_spec(shape):
    nd = len(shape)
    return pl.BlockSpec(shape, lambda *_: (0,) * nd)


def _rwkv_pre_kernel(x_ref, gmix_ref, mu_ref, w0_ref, a0_ref, wr_ref, wk_ref, wv_ref,
                     w1_ref, w2_ref, a1_ref, a2_ref, g1_ref, g2_ref,
                     r_ref, k_ref, v_ref, a_ref, lw_ref, g_ref, prev_ref):
    @pl.when(pl.program_id(1) == 0)
    def _():
        prev_ref[...] = jnp.zeros_like(prev_ref)

    x = x_ref[0]
    rows = x.shape[0]
    xn = _rms(x) * gmix_ref[...]
    row = lax.broadcasted_iota(jnp.int32, xn.shape, 0)
    shifted = jnp.where(row == 0, prev_ref[7:8, :], pltpu.roll(xn, 1, 0))
    prev_ref[...] = xn[rows - 8:, :]
    xx = shifted - xn

    def mix(i):
        return (xn + xx * mu_ref[i:i + 1, :]).astype(BF16)

    r_ref[0] = _dot(mix(0), wr_ref[...]).astype(BF16)
    k_ref[0] = _dot(mix(2), wk_ref[...]).astype(BF16)
    v_ref[0] = _dot(mix(3), wv_ref[...]).astype(BF16)
    u = w0_ref[...] + _dot(jnp.tanh(_dot(mix(1), w1_ref[...])).astype(BF16), w2_ref[...])
    w_log = -(jnp.maximum(-u, 0.0) + jnp.log(1.0 + jnp.exp(-jnp.abs(u)))) - 0.5
    lw_ref[0] = -jnp.exp(w_log)
    a_ref[0] = _sigmoid(a0_ref[...] + _dot(_dot(mix(4), a1_ref[...]).astype(BF16), a2_ref[...])).astype(BF16)
    g_ref[0] = _dot(_sigmoid(_dot(mix(5), g1_ref[...])).astype(BF16), g2_ref[...]).astype(BF16)


def _pad_cols(w, n):
    return jnp.pad(w, ((0, 0), (0, n - w.shape[1])))


def _pad_rows(w, n):
    return jnp.pad(w, ((0, n - w.shape[0]), (0, 0)))


def _rwkv_pre(x, gmix, mu, w0, a0, w_r, w_k, w_v, w1, w2, a1, a2, g1, g2):
    bsz, seq, d = x.shape
    lw = -(-w1.shape[1] // LANES) * LANES
    la = -(-a1.shape[1] // LANES) * LANES
    lg = -(-g1.shape[1] // LANES) * LANES
    mu8 = jnp.pad(mu, ((0, 8 - mu.shape[0]), (0, 0)))
    args = (x, gmix.reshape(1, d), mu8, w0.reshape(1, d), a0.reshape(1, d),
            w_r.astype(BF16), w_k.astype(BF16), w_v.astype(BF16),
            _pad_cols(w1, lw).astype(BF16), _pad_rows(w2, lw).astype(BF16),
            _pad_cols(a1, la).astype(BF16), _pad_rows(a2, la).astype(BF16),
            _pad_cols(g1, lg).astype(BF16), _pad_rows(g2, lg).astype(BF16))
    tile = pl.BlockSpec((1, PRE_ROWS, d), lambda b, s: (b, s, 0))
    in_specs = [tile] + [_const_spec(a.shape) for a in args[1:]]
    out_dt = (BF16, BF16, BF16, BF16, F32, BF16)
    return pl.pallas_call(
        _rwkv_pre_kernel,
        grid=(bsz, seq // PRE_ROWS),
        in_specs=in_specs,
        out_specs=[tile] * 6,
        out_shape=[jax.ShapeDtypeStruct((bsz, seq, d), t) for t in out_dt],
        scratch_shapes=[pltpu.VMEM((8, d), F32)],
        compiler_params=_cparams("arbitrary", "arbitrary"),
        name="rwkv_pre",
    )(*args)


def _rwkv_rec_kernel(r_ref, k_ref, v_ref, a_ref, lw_ref, g_ref, kk_ref, ka_ref, rk_ref, lnw_ref, lnb_ref,
                     o_ref, state_ref):
    @pl.when(pl.program_id(2) == 0)
    def _():
        state_ref[...] = jnp.zeros_like(state_ref)

    c = CHUNK
    hd = RWKV_HEAD
    w = 2 * hd
    n = r_ref.shape[1] // c
    lane = lax.broadcasted_iota(jnp.int32, (c, w), 1)
    head0 = lane < hd
    ri = lax.broadcasted_iota(jnp.int32, (w, w), 0)
    ci = lax.broadcasted_iota(jnp.int32, (w, w), 1)
    same = (ri < c) == (ci < c)
    strict = same & (ri > ci)
    incl = same & (ri >= ci)
    eye = (ri == ci).astype(F32)
    tri = (lax.broadcasted_iota(jnp.int32, (c, c), 0) >= lax.broadcasted_iota(jnp.int32, (c, c), 1)).astype(BF16)
    kkp, kap, rkp, lnw, lnb = kk_ref[...], ka_ref[...], rk_ref[...], lnw_ref[...], lnb_ref[...]

    def head_sum(x):
        s0 = jnp.sum(jnp.where(head0, x, 0.0), axis=-1, keepdims=True)
        s1 = jnp.sum(jnp.where(head0, 0.0, x), axis=-1, keepdims=True)
        return jnp.where(head0, s0, s1)

    def stack(x):
        return jnp.concatenate([jnp.where(head0, x, 0.0), jnp.where(head0, 0.0, x)], axis=0)

    js = range(n)
    rows = [slice(j * c, (j + 1) * c) for j in js]
    r_ = [r_ref[0, rows[j], :].astype(F32) for j in js]
    k_ = [k_ref[0, rows[j], :].astype(F32) for j in js]
    v_ = [v_ref[0, rows[j], :].astype(F32) for j in js]
    a_ = [a_ref[0, rows[j], :].astype(F32) for j in js]
    lw_ = [lw_ref[0, rows[j], :] for j in js]

    def cumsum(x):
        hi = x.astype(BF16)
        lo = (x - hi.astype(F32)).astype(BF16)
        both = _dot(tri, jnp.concatenate([hi, lo], axis=1))
        return both[:, :w] + both[:, w:]

    cum = [cumsum(lw_[j]) for j in js]
    kk = [k_[j] * kkp for j in js]
    kk = [kk[j] / jnp.maximum(jnp.sqrt(head_sum(kk[j] * kk[j])), 1e-12) for j in js]
    k2 = [k_[j] * (1.0 + (a_[j] - 1.0) * kap) for j in js]
    p_ = [-(kk[j] * a_[j]) for j in js]
    tot = [cum[j][c - 1:c, :] for j in js]
    e_neg = [jnp.exp(-cum[j]) for j in js]
    e_tot = [jnp.exp(tot[j] - cum[j]) for j in js]
    rt_s = [stack(r_[j] * jnp.exp(cum[j])) for j in js]
    qt_s = [stack(kk[j] * jnp.exp(cum[j] - lw_[j])).astype(BF16) for j in js]
    kh_s = [stack(k2[j] * e_neg[j]).astype(BF16) for j in js]
    ph_s = [stack(p_[j] * e_neg[j]).astype(BF16) for j in js]
    kt_s = [stack(k2[j] * e_tot[j]).astype(BF16) for j in js]
    pt_s = [stack(p_[j] * e_tot[j]).astype(BF16) for j in js]
    v_s = [stack(v_[j]).astype(BF16) for j in js]
    gam_col = [jnp.sum(eye * jnp.exp(tot[j]), axis=-1, keepdims=True) for j in js]

    amat = [_dot_nt(jnp.concatenate([qt_s[j], rt_s[j].astype(BF16)], axis=0),
                    jnp.concatenate([kh_s[j], ph_s[j]], axis=0)) for j in js]
    a_qp = [jnp.where(strict, amat[j][:w, w:], 0.0) for j in js]
    a_rp = [jnp.where(incl, amat[j][w:, w:], 0.0).astype(BF16) for j in js]
    a_k = [jnp.concatenate([jnp.where(strict, amat[j][:w, :w], 0.0),
                            jnp.where(incl, amat[j][w:, :w], 0.0)], axis=0).astype(BF16) for j in js]
    av = [_dot(a_k[j], v_s[j]) for j in js]

    inv = [eye + a_qp[j] for j in js]
    power = [a_qp[j].astype(BF16) for j in js]
    for _ in range(int(math.log2(c)) - 1):
        sq = [_dot(power[j], power[j]) for j in js]
        power = [sq[j].astype(BF16) for j in js]
        inv = [inv[j] + _dot(inv[j].astype(BF16), power[j]) for j in js]

    wu = [_dot(inv[j].astype(BF16), jnp.concatenate([qt_s[j], av[j][:w].astype(BF16)], axis=1)).astype(BF16)
          for j in js]
    ry = [_dot(a_rp[j], wu[j]) for j in js]
    rw = [(rt_s[j] + ry[j][:, :w]).astype(BF16) for j in js]
    yv = [av[j][w:] + ry[j][:, w:] for j in js]
    zeros = jnp.zeros((w, w), BF16)
    mg = [_dot_tn(jnp.concatenate([pt_s[j], kt_s[j]], axis=0),
                  jnp.concatenate([wu[j], jnp.concatenate([zeros, v_s[j]], axis=1)], axis=0)) for j in js]
    bonus = [head_sum(r_[j] * k2[j] * rkp) * v_[j] for j in js]

    state = state_ref[...]
    ys = []
    for j in js:
        sb = state.astype(BF16)
        y_s = _dot(rw[j], sb) + yv[j]
        ys.append(y_s[:c] + y_s[c:])
        state = gam_col[j] * state + _dot(mg[j][:, :w].astype(BF16), sb) + mg[j][:, w:]
    state_ref[...] = state

    for j in js:
        y = ys[j]
        yc = y - head_sum(y) * (1.0 / hd)
        var = head_sum(yc * yc) * (1.0 / hd)
        yn = yc * lax.rsqrt(var + GN_EPS) * lnw + lnb
        o_ref[0, rows[j], :] = ((yn + bonus[j]) * g_ref[0, rows[j], :].astype(F32)).astype(BF16)


def _rwkv_rec(r, k, v, a, lw, g, k_k, k_a, r_k, ln_w, ln_b):
    bsz, seq, d = r.shape
    hw = 2 * RWKV_HEAD
    tile = pl.BlockSpec((1, REC_ROWS, hw), lambda b, h, s: (b, s, h))
    par = pl.BlockSpec((1, hw), lambda b, h, s: (0, h))
    params = [p.reshape(1, d).astype(F32) for p in (k_k, k_a, r_k, ln_w, ln_b)]
    return pl.pallas_call(
        _rwkv_rec_kernel,
        grid=(bsz, d // hw, seq // REC_ROWS),
        in_specs=[tile] * 6 + [par] * 5,
        out_specs=tile,
        out_shape=jax.ShapeDtypeStruct((bsz, seq, d), BF16),
        scratch_shapes=[pltpu.VMEM((hw, hw), F32)],
        compiler_params=_cparams("arbitrary", "arbitrary", "arbitrary"),
        name="rwkv_rec",
    )(r, k, v, a, lw, g, *params)


def _proj_res_kernel(a_ref, w_ref, res_ref, o_ref):
    o_ref[...] = res_ref[...] + _dot(a_ref[...], w_ref[...])


def _proj_res(a, w, res):
    n, kdim = a.shape
    d = w.shape[1]
    return pl.pallas_call(
        _proj_res_kernel,
        grid=(n // PROJ_ROWS,),
        in_specs=[pl.BlockSpec((PROJ_ROWS, kdim), lambda i: (i, 0)), _const_spec(w.shape),
                  pl.BlockSpec((PROJ_ROWS, d), lambda i: (i, 0))],
        out_specs=pl.BlockSpec((PROJ_ROWS, d), lambda i: (i, 0)),
        out_shape=jax.ShapeDtypeStruct((n, d), F32),
        compiler_params=_cparams("arbitrary"),
        name="proj_res",
    )(a, w.astype(BF16), res)


def _moe_route_kernel(h_ref, g_ref, wr_ref, br_ref, xs_ref, gs_ref, dest_ref, cnt_ref):
    nb = h_ref.shape[0]
    nbp = xs_ref.shape[0]
    t = _rms(h_ref[...]) * g_ref[...]
    logits = jnp.dot(t, wr_ref[...], precision=HIGHEST, preferred_element_type=F32) + br_ref[...]
    lane = lax.broadcasted_iota(jnp.int32, logits.shape, 1).astype(F32)
    neg = jnp.float32(-jnp.inf)
    big = jnp.float32(LANES)
    is_grp = lane < N_EXPERT_GROUPS
    gl = jnp.where(is_grp, logits, neg)
    gmax = jnp.max(gl, axis=-1, keepdims=True)
    grp = jnp.min(jnp.where(gl == gmax, lane, big), axis=-1, keepdims=True)
    p_grp = 1.0 / jnp.sum(jnp.where(is_grp, jnp.exp(gl - gmax), 0.0), axis=-1, keepdims=True)
    e_lo = N_EXPERT_GROUPS + grp * EXPERTS_PER_GROUP
    in_grp = (lane >= e_lo) & (lane < e_lo + EXPERTS_PER_GROUP)
    el = jnp.where(in_grp, logits, neg)
    v1 = jnp.max(el, axis=-1, keepdims=True)
    i1 = jnp.min(jnp.where(el == v1, lane, big), axis=-1, keepdims=True)
    el2 = jnp.where(lane == i1, neg, el)
    v2 = jnp.max(el2, axis=-1, keepdims=True)
    i2 = jnp.min(jnp.where(el2 == v2, lane, big), axis=-1, keepdims=True)
    e2 = jnp.exp(v2 - v1)
    w1 = 1.0 / (1.0 + e2)
    w2 = e2 / (1.0 + e2)
    gates = jnp.where(lane == i1, w1, jnp.where(lane == i2, w2, 0.0)) * p_grp

    onehot = (lane == grp).astype(BF16)
    ri = lax.broadcasted_iota(jnp.int32, (nb, nb), 0)
    ci = lax.broadcasted_iota(jnp.int32, (nb, nb), 1)
    rank = _dot((ri > ci).astype(BF16), onehot)
    counts = jnp.sum(onehot.astype(F32), axis=0, keepdims=True)
    padded = jnp.ceil(counts / FFN_ROWS) * FFN_ROWS
    lane1 = lax.broadcasted_iota(jnp.int32, (1, LANES), 1)
    offs = jnp.zeros((1, LANES), F32)
    for gidx in range(1, N_EXPERT_GROUPS):
        prev = jnp.sum(jnp.where(lane1 < gidx, padded, 0.0), axis=-1, keepdims=True)
        offs = jnp.where(lane1 == gidx, prev, offs)
    dest = jnp.sum(onehot.astype(F32) * (rank + offs), axis=-1, keepdims=True)
    slot = lax.broadcasted_iota(jnp.int32, (nb, nbp), 1).astype(F32)
    perm_t = (dest == slot).astype(BF16)
    xs_ref[...] = _dot_tn(perm_t, t.astype(BF16)).astype(BF16)
    g_hi = gates.astype(BF16)
    g_lo = (gates - g_hi.astype(F32)).astype(BF16)
    moved = _dot_tn(perm_t, jnp.concatenate([g_hi, g_lo], axis=1))
    gs_ref[...] = moved[:, :LANES] + moved[:, LANES:]
    dest_ref[...] = dest
    cnt_ref[0] = jnp.where(lane1 < N_EXPERT_GROUPS, counts, 0.0)


def _moe_ffn_kernel(tile_ref, grp_ref, nvalid_ref, xs_ref, gs_ref, wg_ref, wu_ref, wd_ref, ys_ref):
    i = pl.program_id(0)

    @pl.when(i < nvalid_ref[0])
    def _():
        x = xs_ref[...]
        gates = gs_ref[...]
        lane = lax.broadcasted_iota(jnp.int32, gates.shape, 1)
        base = N_EXPERT_GROUPS + grp_ref[i] * EXPERTS_PER_GROUP
        acc = jnp.zeros(ys_ref.shape, F32)
        for e in range(EXPERTS_PER_GROUP):
            ge = jnp.sum(jnp.where(lane == base + e, gates, 0.0), axis=-1, keepdims=True)
            gate_act = _dot(x, wg_ref[e])
            hdn = gate_act * _sigmoid(gate_act) * _dot(x, wu_ref[e])
            acc = acc + ge * _dot(hdn.astype(BF16), wd_ref[e])
        ys_ref[...] = acc.astype(BF16)

    @pl.when(i >= nvalid_ref[0])
    def _():
        ys_ref[...] = jnp.zeros_like(ys_ref)


def _moe_merge_kernel(h_ref, dest_ref, ys_ref, o_ref):
    nb, nbp = h_ref.shape[0], ys_ref.shape[0]
    slot = lax.broadcasted_iota(jnp.int32, (nb, nbp), 1).astype(F32)
    perm_t = (dest_ref[...] == slot).astype(BF16)
    o_ref[...] = h_ref[...] + _dot(perm_t, ys_ref[...])


def _moe_merge_norm_kernel(h_ref, dest_ref, ys_ref, g_ref, o_ref):
    nb, nbp = h_ref.shape[0], ys_ref.shape[0]
    slot = lax.broadcasted_iota(jnp.int32, (nb, nbp), 1).astype(F32)
    perm_t = (dest_ref[...] == slot).astype(BF16)
    o_ref[...] = _rms(h_ref[...] + _dot(perm_t, ys_ref[...])) * g_ref[...]


def _hier_moe_residual(h, norm_g, w_grp, b_grp, w_exp, b_exp, w_gate, w_up, w_down, final_g=None):
    n, d = h.shape
    nb = MOE_BLOCK
    nblk = n // nb
    nbp = nb + N_EXPERT_GROUPS * FFN_ROWS
    tiles_per_blk = nbp // FFN_ROWS
    n_exp = N_EXPERT_GROUPS * EXPERTS_PER_GROUP
    w_router = jnp.pad(jnp.concatenate([w_grp, w_exp], axis=1), ((0, 0), (0, LANES - N_EXPERT_GROUPS - n_exp)))
    b_router = jnp.pad(jnp.concatenate([b_grp, b_exp]), (0, LANES - N_EXPERT_GROUPS - n_exp)).reshape(1, LANES)

    xs, gs, dest, counts = pl.pallas_call(
        _moe_route_kernel,
        grid=(nblk,),
        in_specs=[pl.BlockSpec((nb, d), lambda i: (i, 0)), _const_spec((1, d)),
                  _const_spec((d, LANES)), _const_spec((1, LANES))],
        out_specs=[pl.BlockSpec((nbp, d), lambda i: (i, 0)), pl.BlockSpec((nbp, LANES), lambda i: (i, 0)),
                   pl.BlockSpec((nb, 1), lambda i: (i, 0)), pl.BlockSpec((1, 1, LANES), lambda i: (i, 0, 0))],
        out_shape=[jax.ShapeDtypeStruct((nblk * nbp, d), BF16), jax.ShapeDtypeStruct((nblk * nbp, LANES), F32),
                   jax.ShapeDtypeStruct((n, 1), F32), jax.ShapeDtypeStruct((nblk, 1, LANES), F32)],
        compiler_params=_cparams("arbitrary"),
        name="moe_route",
    )(h, norm_g.reshape(1, d), w_router, b_router)

    cnt = counts[:, 0, :N_EXPERT_GROUPS].astype(jnp.int32)
    tiles = (cnt + FFN_ROWS - 1) // FFN_ROWS
    first = jnp.cumsum(tiles, axis=1) - tiles + (jnp.arange(nblk, dtype=jnp.int32) * tiles_per_blk)[:, None]
    order_tiles = tiles.T.reshape(-1)
    order_first = first.T.reshape(-1)
    order_grp = jnp.repeat(jnp.arange(N_EXPERT_GROUPS, dtype=jnp.int32), nblk)
    ends = jnp.cumsum(order_tiles)
    n_valid = ends[-1]
    max_tiles = nblk * tiles_per_blk
    step = jnp.arange(max_tiles, dtype=jnp.int32)
    seg = jnp.minimum(jnp.sum((ends[None, :] <= step[:, None]).astype(jnp.int32), axis=1), order_tiles.shape[0] - 1)
    within = step - (ends[seg] - order_tiles[seg])
    used_idx = order_first[seg] + within
    unused = (step % tiles_per_blk) >= jnp.sum(tiles, axis=1)[step // tiles_per_blk]
    unused_rank = jnp.cumsum(unused.astype(jnp.int32)) - 1
    want = jnp.maximum(step - n_valid, 0)
    unused_idx = jnp.sum(jnp.where(unused[None, :] & (unused_rank[None, :] == want[:, None]), step[None, :], 0), axis=1)
    tile_idx = jnp.where(step < n_valid, used_idx, unused_idx).astype(jnp.int32)
    last_valid = jnp.maximum(n_valid - 1, 0)
    tile_grp = jnp.where(step < n_valid, order_grp[seg], order_grp[seg][last_valid]).astype(jnp.int32)

    def wspec(shape):
        return pl.BlockSpec((EXPERTS_PER_GROUP,) + shape, lambda i, tile, grp, nv: (grp[i], 0, 0))

    f = w_gate.shape[-1]
    ys = pl.pallas_call(
        _moe_ffn_kernel,
        grid_spec=pltpu.PrefetchScalarGridSpec(
            num_scalar_prefetch=3,
            grid=(max_tiles,),
            in_specs=[pl.BlockSpec((FFN_ROWS, d), lambda i, tile, grp, nv: (tile[i], 0)),
                      pl.BlockSpec((FFN_ROWS, LANES), lambda i, tile, grp, nv: (tile[i], 0)),
                      wspec((d, f)), wspec((d, f)), wspec((f, d))],
            out_specs=pl.BlockSpec((FFN_ROWS, d), lambda i, tile, grp, nv: (tile[i], 0)),
        ),
        out_shape=jax.ShapeDtypeStruct((nblk * nbp, d), BF16),
        compiler_params=_cparams("arbitrary"),
        name="moe_ffn",
    )(tile_idx, tile_grp, n_valid.reshape(1).astype(jnp.int32), xs, gs,
      w_gate.astype(BF16), w_up.astype(BF16), w_down.astype(BF16))

    blk = pl.BlockSpec((nb, d), lambda i: (i, 0))
    in_specs = [blk, pl.BlockSpec((nb, 1), lambda i: (i, 0)), pl.BlockSpec((nbp, d), lambda i: (i, 0))]
    args = [h, dest, ys]
    body = _moe_merge_kernel
    if final_g is not None:
        in_specs.append(_const_spec((1, d)))
        args.append(final_g.reshape(1, d))
        body = _moe_merge_norm_kernel
    return pl.pallas_call(
        body,
        grid=(nblk,),
        in_specs=in_specs,
        out_specs=blk,
        out_shape=jax.ShapeDtypeStruct((n, d), F32),
        compiler_params=_cparams("arbitrary"),
        name="moe_merge",
    )(*args)


def _qkv_kernel(h_ref, gq_ref, gkv_ref, wq_ref, wk_ref, wv_ref, q_ref, k_ref, v_ref):
    n = _rms(h_ref[...])
    xq = (n * gq_ref[...]).astype(BF16)
    xkv = (n * gkv_ref[...]).astype(BF16)
    q_ref[...] = _dot(xq, wq_ref[...]).astype(BF16)
    k_ref[...] = _dot(xkv, wk_ref[...]).astype(BF16)
    v_ref[...] = _dot(xkv, wv_ref[...]).astype(BF16)


def _qkv_proj(h, g_q, g_kv, w_q, w_kv):
    n, d = h.shape
    qw = w_q.shape[1]
    scale = 1.0 / math.sqrt(DIL_HEAD_DIM)
    rows = pl.BlockSpec((PROJ_ROWS, d), lambda i: (i, 0))
    out = pl.BlockSpec((PROJ_ROWS, qw), lambda i: (i, 0))
    return pl.pallas_call(
        _qkv_kernel,
        grid=(n // PROJ_ROWS,),
        in_specs=[rows, _const_spec((1, d)), _const_spec((1, d))] + [_const_spec((d, qw))] * 3,
        out_specs=[out] * 3,
        out_shape=[jax.ShapeDtypeStruct((n, qw), BF16)] * 3,
        compiler_params=_cparams("arbitrary"),
        name="qkv_proj",
    )(h, g_q.reshape(1, d), g_kv.reshape(1, d), (w_q * scale).astype(BF16),
      w_kv[:, :qw].astype(BF16), w_kv[:, qw:].astype(BF16))


def _attn_kernel(q_ref, kc_ref, kp_ref, vc_ref, vp_ref, o_ref, lse_ref):
    n = pl.program_id(2)
    blk = q_ref.shape[1]
    ri = lax.broadcasted_iota(jnp.int32, (blk, blk), 0)
    ci = lax.broadcasted_iota(jnp.int32, (blk, blk), 1)
    prev_ok = (ci >= ri) & (n > 0)
    cur_ok = ci <= ri
    lowest = jnp.finfo(F32).min
    for h in range(DIL_HEADS):
        hs = slice(h * DIL_HEAD_DIM, (h + 1) * DIL_HEAD_DIM)
        q = q_ref[0, :, hs]
        s_p = jnp.where(prev_ok, _dot_nt(q, kp_ref[0, :, hs]), lowest)
        s_c = jnp.where(cur_ok, _dot_nt(q, kc_ref[0, :, hs]), lowest)
        m = jnp.maximum(jnp.max(s_p, axis=-1, keepdims=True), jnp.max(s_c, axis=-1, keepdims=True))
        p_p = jnp.exp(s_p - m)
        p_c = jnp.exp(s_c - m)
        l = jnp.sum(p_p, axis=-1, keepdims=True) + jnp.sum(p_c, axis=-1, keepdims=True)
        acc = _dot(p_p.astype(BF16), vp_ref[0, :, hs]) + _dot(p_c.astype(BF16), vc_ref[0, :, hs])
        o_ref[0, :, hs] = (acc / l).astype(BF16)
        lse_ref[0, :, hs] = jnp.broadcast_to(m + jnp.log(l), (blk, DIL_HEAD_DIM))


def _attn_group(q, k, v, gi, dilation, bsz, seq):
    width = q.shape[1]
    gw = DIL_HEADS * DIL_HEAD_DIM
    per_row = width // gw
    sub = seq // dilation
    nblk = sub // ATT_BLOCK
    view = (bsz, sub, dilation * width)
    qv, kv_, vv = q.reshape(view), k.reshape(view), v.reshape(view)
    cur = pl.BlockSpec((1, ATT_BLOCK, gw), lambda b, r, n: (b, n, r * per_row + gi))
    prev = pl.BlockSpec((1, ATT_BLOCK, gw), lambda b, r, n: (b, jnp.maximum(n - 1, 0), r * per_row + gi))
    out = pl.BlockSpec((1, ATT_BLOCK, gw), lambda b, r, n: (b, n, r))
    o, lse = pl.pallas_call(
        _attn_kernel,
        grid=(bsz, dilation, nblk),
        in_specs=[cur, cur, prev, cur, prev],
        out_specs=[out, out],
        out_shape=[jax.ShapeDtypeStruct((bsz, sub, dilation * gw), BF16),
                   jax.ShapeDtypeStruct((bsz, sub, dilation * gw), F32)],
        compiler_params=_cparams("arbitrary", "arbitrary", "arbitrary"),
        name=f"attn_group{gi}",
    )(qv, kv_, kv_, vv, vv)
    return o.reshape(bsz * seq, gw), lse.reshape(bsz * seq, gw)


def _attn_merge_kernel(o0_ref, o1_ref, o2_ref, l0_ref, l1_ref, l2_ref, w_ref, res_ref, out_ref):
    l0, l1, l2 = l0_ref[...], l1_ref[...], l2_ref[...]
    m = jnp.maximum(jnp.maximum(l0, l1), l2)
    e0, e1, e2 = jnp.exp(l0 - m), jnp.exp(l1 - m), jnp.exp(l2 - m)
    mix = (e0 * o0_ref[...].astype(F32) + e1 * o1_ref[...].astype(F32) + e2 * o2_ref[...].astype(F32)) / (e0 + e1 + e2)
    out_ref[...] = res_ref[...] + _dot(mix.astype(BF16), w_ref[...])


def _attn_merge(outs, lses, w_o, res):
    n, gw = outs[0].shape
    d = w_o.shape[1]
    rows = pl.BlockSpec((PROJ_ROWS, gw), lambda i: (i, 0))
    full = pl.BlockSpec((PROJ_ROWS, d), lambda i: (i, 0))
    return pl.pallas_call(
        _attn_merge_kernel,
        grid=(n // PROJ_ROWS,),
        in_specs=[rows] * 6 + [_const_spec(w_o.shape), full],
        out_specs=full,
        out_shape=jax.ShapeDtypeStruct((n, d), F32),
        compiler_params=_cparams("arbitrary"),
        name="attn_merge",
    )(*outs, *lses, w_o.astype(BF16), res)


def kernel(x, norm_mix_g, norm_ffn_g, rwkv_mu, rwkv_w_r, rwkv_w_k, rwkv_w_v, rwkv_w0, rwkv_w1, rwkv_w2,
           rwkv_a0, rwkv_a1, rwkv_a2, rwkv_g1, rwkv_g2, rwkv_k_k, rwkv_k_a, rwkv_r_k, rwkv_ln_w, rwkv_ln_b,
           rwkv_w_o, kv_norm_g, w_kv, attn_w_q, attn_w_o, moe_w_grp, moe_b_grp, moe_w_exp, moe_b_exp,
           moe_w_gate, moe_w_up, moe_w_down, final_norm_g):
    bsz, seq, d = x.shape
    n = bsz * seq
    depth = norm_mix_g.shape[0]
    n_rwkv = rwkv_mu.shape[0]
    h = x.reshape(n, d)
    q = k_sh = v_sh = None
    for layer in range(depth):
        if layer < n_rwkv:
            i = layer
            r, k, v, a, lw, g = _rwkv_pre(h.reshape(bsz, seq, d), norm_mix_g[layer], rwkv_mu[i], rwkv_w0[i], rwkv_a0[i],
                                          rwkv_w_r[i], rwkv_w_k[i], rwkv_w_v[i], rwkv_w1[i], rwkv_w2[i],
                                          rwkv_a1[i], rwkv_a2[i], rwkv_g1[i], rwkv_g2[i])
            y = _rwkv_rec(r, k, v, a, lw, g, rwkv_k_k[i], rwkv_k_a[i], rwkv_r_k[i], rwkv_ln_w[i], rwkv_ln_b[i])
            h = _proj_res(y.reshape(n, d), rwkv_w_o[i], h)
        else:
            i = layer - n_rwkv
            if i == 0:
                q, k_sh, v_sh = _qkv_proj(h, norm_mix_g[layer], kv_norm_g, attn_w_q[i], w_kv)
            else:
                q = _qkv_proj(h, norm_mix_g[layer], kv_norm_g, attn_w_q[i], w_kv)[0]
            outs, lses = [], []
            for gi, (window, dilation) in enumerate(DIL_PATTERNS):
                assert window // dilation == ATT_BLOCK and seq % window == 0
                o, lse = _attn_group(q, k_sh, v_sh, gi, dilation, bsz, seq)
                outs.append(o)
                lses.append(lse)
            h = _attn_merge(outs, lses, attn_w_o[i], h)
        last = layer == depth - 1
        h = _hier_moe_residual(h, norm_ffn_g[layer], moe_w_grp[layer], moe_b_grp[layer], moe_w_exp[layer],
                               moe_b_exp[layer], moe_w_gate[layer], moe_w_up[layer], moe_w_down[layer],
                               final_g=final_norm_g if last else None)
    return h.reshape(bsz, seq, d)
```

```python
import functools
import math

import jax
import jax.numpy as jnp
from jax import lax
from jax.experimental import pallas as pl
from jax.experimental.pallas import tpu as pltpu

F32 = jnp.float32
BF16 = jnp.bfloat16
HIGHEST = lax.Precision.HIGHEST

NORM_EPS = 1e-6
RWKV_HEAD = 64
GN_EPS = RWKV_HEAD * 1e-5
DIL_PATTERNS = ((128, 1), (512, 4), (2048, 16))
DIL_HEADS = 8
DIL_HEAD_DIM = 64
N_EXPERT_GROUPS = 4
EXPERTS_PER_GROUP = 4
LANES = 128
VMEM_LIMIT = 56 * 1024 * 1024

CHUNK = 64
REC_ROWS = 512
PRE_ROWS = 256
PROJ_ROWS = 512
MOE_BLOCK = 1024
FFN_ROWS = 256
ATT_BLOCK = 128


def _cparams(*sem):
    return pltpu.CompilerParams(dimension_semantics=sem, vmem_limit_bytes=VMEM_LIMIT)


def _dot(a, b):
    return jnp.dot(a, b, preferred_element_type=F32)


def _dot_nt(a, b):
    return lax.dot_general(a, b, (((1,), (1,)), ((), ())), preferred_element_type=F32)


def _dot_tn(a, b, precision=None):
    return lax.dot_general(a, b, (((0,), (0,)), ((), ())), preferred_element_type=F32, precision=precision)


def _rms(x):
    return x * lax.rsqrt(jnp.mean(x * x, axis=-1, keepdims=True) + NORM_EPS)


def _sigmoid(z):
    return 1.0 / (1.0 + jnp.exp(-z))


def _const_spec(shape):
    nd = len(shape)
    return pl.BlockSpec(shape, lambda *_: (0,) * nd)


def _rwkv_pre_kernel(x_ref, gmix_ref, mu_ref, w0_ref, a0_ref, wr_ref, wk_ref, wv_ref,
                     w1_ref, w2_ref, a1_ref, a2_ref, g1_ref, g2_ref,
                     r_ref, k_ref, v_ref, a_ref, lw_ref, g_ref, prev_ref):
    @pl.when(pl.program_id(1) == 0)
    def _():
        prev_ref[...] = jnp.zeros_like(prev_ref)

    x = x_ref[0]
    rows = x.shape[0]
    xn = _rms(x) * gmix_ref[...]
    row = lax.broadcasted_iota(jnp.int32, xn.shape, 0)
    shifted = jnp.where(row == 0, prev_ref[7:8, :], pltpu.roll(xn, 1, 0))
    prev_ref[...] = xn[rows - 8:, :]
    xx = shifted - xn

    def mix(i):
        return (xn + xx * mu_ref[i:i + 1, :]).astype(BF16)

    r_ref[0] = _dot(mix(0), wr_ref[...]).astype(BF16)
    k_ref[0] = _dot(mix(2), wk_ref[...]).astype(BF16)
    v_ref[0] = _dot(mix(3), wv_ref[...]).astype(BF16)
    u = w0_ref[...] + _dot(jnp.tanh(_dot(mix(1), w1_ref[...])).astype(BF16), w2_ref[...])
    w_log = -(jnp.maximum(-u, 0.0) + jnp.log(1.0 + jnp.exp(-jnp.abs(u)))) - 0.5
    lw_ref[0] = -jnp.exp(w_log)
    a_ref[0] = _sigmoid(a0_ref[...] + _dot(_dot(mix(4), a1_ref[...]).astype(BF16), a2_ref[...])).astype(BF16)
    g_ref[0] = _dot(_sigmoid(_dot(mix(5), g1_ref[...])).astype(BF16), g2_ref[...]).astype(BF16)


def _pad_cols(w, n):
    return jnp.pad(w, ((0, 0), (0, n - w.shape[1])))


def _pad_rows(w, n):
    return jnp.pad(w, ((0, n - w.shape[0]), (0, 0)))


def _rwkv_pre(x, gmix, mu, w0, a0, w_r, w_k, w_v, w1, w2, a1, a2, g1, g2):
    bsz, seq, d = x.shape
    lw = -(-w1.shape[1] // LANES) * LANES
    la = -(-a1.shape[1] // LANES) * LANES
    lg = -(-g1.shape[1] // LANES) * LANES
    mu8 = jnp.pad(mu, ((0, 8 - mu.shape[0]), (0, 0)))
    args = (x, gmix.reshape(1, d), mu8, w0.reshape(1, d), a0.reshape(1, d),
            w_r.astype(BF16), w_k.astype(BF16), w_v.astype(BF16),
            _pad_cols(w1, lw).astype(BF16), _pad_rows(w2, lw).astype(BF16),
            _pad_cols(a1, la).astype(BF16), _pad_rows(a2, la).astype(BF16),
            _pad_cols(g1, lg).astype(BF16), _pad_rows(g2, lg).astype(BF16))
    tile = pl.BlockSpec((1, PRE_ROWS, d), lambda b, s: (b, s, 0))
    in_specs = [tile] + [_const_spec(a.shape) for a in args[1:]]
    out_dt = (BF16, BF16, BF16, BF16, F32, BF16)
    return pl.pallas_call(
        _rwkv_pre_kernel,
        grid=(bsz, seq // PRE_ROWS),
        in_specs=in_specs,
        out_specs=[tile] * 6,
        out_shape=[jax.ShapeDtypeStruct((bsz, seq, d), t) for t in out_dt],
        scratch_shapes=[pltpu.VMEM((8, d), F32)],
        compiler_params=_cparams("arbitrary", "arbitrary"),
        name="rwkv_pre",
    )(*args)


def _rwkv_rec_kernel(r_ref, k_ref, v_ref, a_ref, lw_ref, g_ref, kk_ref, ka_ref, rk_ref, lnw_ref, lnb_ref,
                     o_ref, state_ref):
    @pl.when(pl.program_id(2) == 0)
    def _():
        state_ref[...] = jnp.zeros_like(state_ref)

    c = CHUNK
    hd = RWKV_HEAD
    w = 2 * hd
    n = r_ref.shape[1] // c
    lane = lax.broadcasted_iota(jnp.int32, (c, w), 1)
    head0 = lane < hd
    ri = lax.broadcasted_iota(jnp.int32, (w, w), 0)
    ci = lax.broadcasted_iota(jnp.int32, (w, w), 1)
    same = (ri < c) == (ci < c)
    strict = same & (ri > ci)
    incl = same & (ri >= ci)
    eye = (ri == ci).astype(F32)
    tri = (lax.broadcasted_iota(jnp.int32, (c, c), 0) >= lax.broadcasted_iota(jnp.int32, (c, c), 1)).astype(BF16)
    kkp, kap, rkp, lnw, lnb = kk_ref[...], ka_ref[...], rk_ref[...], lnw_ref[...], lnb_ref[...]

    def head_sum(x):
        s0 = jnp.sum(jnp.where(head0, x, 0.0), axis=-1, keepdims=True)
        s1 = jnp.sum(jnp.where(head0, 0.0, x), axis=-1, keepdims=True)
        return jnp.where(head0, s0, s1)

    def stack(x):
        return jnp.concatenate([jnp.where(head0, x, 0.0), jnp.where(head0, 0.0, x)], axis=0)

    js = range(n)
    rows = [slice(j * c, (j + 1) * c) for j in js]
    r_ = [r_ref[0, rows[j], :].astype(F32) for j in js]
    k_ = [k_ref[0, rows[j], :].astype(F32) for j in js]
    v_ = [v_ref[0, rows[j], :].astype(F32) for j in js]
    a_ = [a_ref[0, rows[j], :].astype(F32) for j in js]
    lw_ = [lw_ref[0, rows[j], :] for j in js]

    def cumsum(x):
        hi = x.astype(BF16)
        lo = (x - hi.astype(F32)).astype(BF16)
        both = _dot(tri, jnp.concatenate([hi, lo], axis=1))
        return both[:, :w] + both[:, w:]

    cum = [cumsum(lw_[j]) for j in js]
    kk = [k_[j] * kkp for j in js]
    kk = [kk[j] / jnp.maximum(jnp.sqrt(head_sum(kk[j] * kk[j])), 1e-12) for j in js]
    k2 = [k_[j] * (1.0 + (a_[j] - 1.0) * kap) for j in js]
    p_ = [-(kk[j] * a_[j]) for j in js]
    tot = [cum[j][c - 1:c, :] for j in js]
    e_neg = [jnp.exp(-cum[j]) for j in js]
    e_tot = [jnp.exp(tot[j] - cum[j]) for j in js]
    rt_s = [stack(r_[j] * jnp.exp(cum[j])) for j in js]
    qt_s = [stack(kk[j] * jnp.exp(cum[j] - lw_[j])).astype(BF16) for j in js]
    kh_s = [stack(k2[j] * e_neg[j]).astype(BF16) for j in js]
    ph_s = [stack(p_[j] * e_neg[j]).astype(BF16) for j in js]
    kt_s = [stack(k2[j] * e_tot[j]).astype(BF16) for j in js]
    pt_s = [stack(p_[j] * e_tot[j]).astype(BF16) for j in js]
    v_s = [stack(v_[j]).astype(BF16) for j in js]
    gam_col = [jnp.sum(eye * jnp.exp(tot[j]), axis=-1, keepdims=True) for j in js]

    amat = [_dot_nt(jnp.concatenate([qt_s[j], rt_s[j].astype(BF16)], axis=0),
                    jnp.concatenate([kh_s[j], ph_s[j]], axis=0)) for j in js]
    a_qp = [jnp.where(strict, amat[j][:w, w:], 0.0) for j in js]
    a_rp = [jnp.where(incl, amat[j][w:, w:], 0.0).astype(BF16) for j in js]
    a_k = [jnp.concatenate([jnp.where(strict, amat[j][:w, :w], 0.0),
                            jnp.where(incl, amat[j][w:, :w], 0.0)], axis=0).astype(BF16) for j in js]
    av = [_dot(a_k[j], v_s[j]) for j in js]

    inv = [eye + a_qp[j] for j in js]
    power = [a_qp[j].astype(BF16) for j in js]
    for _ in range(int(math.log2(c)) - 1):
        sq = [_dot(power[j], power[j]) for j in js]
        power = [sq[j].astype(BF16) for j in js]
        inv = [inv[j] + _dot(inv[j].astype(BF16), power[j]) for j in js]

    wu = [_dot(inv[j].astype(BF16), jnp.concatenate([qt_s[j], av[j][:w].astype(BF16)], axis=1)).astype(BF16)
          for j in js]
    ry = [_dot(a_rp[j], wu[j]) for j in js]
    rw = [(rt_s[j] + ry[j][:, :w]).astype(BF16) for j in js]
    yv = [av[j][w:] + ry[j][:, w:] for j in js]
    zeros = jnp.zeros((w, w), BF16)
    mg = [_dot_tn(jnp.concatenate([pt_s[j], kt_s[j]], axis=0),
                  jnp.concatenate([wu[j], jnp.concatenate([zeros, v_s[j]], axis=1)], axis=0)) for j in js]
    bonus = [head_sum(r_[j] * k2[j] * rkp) * v_[j] for j in js]

    state = state_ref[...]
    ys = []
    for j in js:
        sb = state.astype(BF16)
        y_s = _dot(rw[j], sb) + yv[j]
        ys.append(y_s[:c] + y_s[c:])
        state = gam_col[j] * state + _dot(mg[j][:, :w].astype(BF16), sb) + mg[j][:, w:]
    state_ref[...] = state

    for j in js:
        y = ys[j]
        yc = y - head_sum(y) * (1.0 / hd)
        var = head_sum(yc * yc) * (1.0 / hd)
        yn = yc * lax.rsqrt(var + GN_EPS) * lnw + lnb
        o_ref[0, rows[j], :] = ((yn + bonus[j]) * g_ref[0, rows[j], :].astype(F32)).astype(BF16)


def _rwkv_rec(r, k, v, a, lw, g, k_k, k_a, r_k, ln_w, ln_b):
    bsz, seq, d = r.shape
    hw = 2 * RWKV_HEAD
    tile = pl.BlockSpec((1, REC_ROWS, hw), lambda b, h, s: (b, s, h))
    par = pl.BlockSpec((1, hw), lambda b, h, s: (0, h))
    params = [p.reshape(1, d).astype(F32) for p in (k_k, k_a, r_k, ln_w, ln_b)]
    return pl.pallas_call(
        _rwkv_rec_kernel,
        grid=(bsz, d // hw, seq // REC_ROWS),
        in_specs=[tile] * 6 + [par] * 5,
        out_specs=tile,
        out_shape=jax.ShapeDtypeStruct((bsz, seq, d), BF16),
        scratch_shapes=[pltpu.VMEM((hw, hw), F32)],
        compiler_params=_cparams("arbitrary", "arbitrary", "arbitrary"),
        name="rwkv_rec",
    )(r, k, v, a, lw, g, *params)


def _proj_res_kernel(a_ref, w_ref, res_ref, o_ref):
    o_ref[...] = res_ref[...] + _dot(a_ref[...], w_ref[...])


def _proj_res(a, w, res):
    n, kdim = a.shape
    d = w.shape[1]
    return pl.pallas_call(
        _proj_res_kernel,
        grid=(n // PROJ_ROWS,),
        in_specs=[pl.BlockSpec((PROJ_ROWS, kdim), lambda i: (i, 0)), _const_spec(w.shape),
                  pl.BlockSpec((PROJ_ROWS, d), lambda i: (i, 0))],
        out_specs=pl.BlockSpec((PROJ_ROWS, d), lambda i: (i, 0)),
        out_shape=jax.ShapeDtypeStruct((n, d), F32),
        compiler_params=_cparams("arbitrary"),
        name="proj_res",
    )(a, w.astype(BF16), res)


def _moe_route_kernel(h_ref, g_ref, wr_ref, br_ref, xs_ref, gs_ref, dest_ref, cnt_ref):
    nb = h_ref.shape[0]
    nbp = xs_ref.shape[0]
    t = _rms(h_ref[...]) * g_ref[...]
    logits = jnp.dot(t, wr_ref[...], precision=HIGHEST, preferred_element_type=F32) + br_ref[...]
    lane = lax.broadcasted_iota(jnp.int32, logits.shape, 1).astype(F32)
    neg = jnp.float32(-jnp.inf)
    big = jnp.float32(LANES)
    is_grp = lane < N_EXPERT_GROUPS
    gl = jnp.where(is_grp, logits, neg)
    gmax = jnp.max(gl, axis=-1, keepdims=True)
    grp = jnp.min(jnp.where(gl == gmax, lane, big), axis=-1, keepdims=True)
    p_grp = 1.0 / jnp.sum(jnp.where(is_grp, jnp.exp(gl - gmax), 0.0), axis=-1, keepdims=True)
    e_lo = N_EXPERT_GROUPS + grp * EXPERTS_PER_GROUP
    in_grp = (lane >= e_lo) & (lane < e_lo + EXPERTS_PER_GROUP)
    el = jnp.where(in_grp, logits, neg)
    v1 = jnp.max(el, axis=-1, keepdims=True)
    i1 = jnp.min(jnp.where(el == v1, lane, big), axis=-1, keepdims=True)
    el2 = jnp.where(lane == i1, neg, el)
    v2 = jnp.max(el2, axis=-1, keepdims=True)
    i2 = jnp.min(jnp.where(el2 == v2, lane, big), axis=-1, keepdims=True)
    e2 = jnp.exp(v2 - v1)
    w1 = 1.0 / (1.0 + e2)
    w2 = e2 / (1.0 + e2)
    gates = jnp.where(lane == i1, w1, jnp.where(lane == i2, w2, 0.0)) * p_grp

    onehot = (lane == grp).astype(BF16)
    ri = lax.broadcasted_iota(jnp.int32, (nb, nb), 0)
    ci = lax.broadcasted_iota(jnp.int32, (nb, nb), 1)
    rank = _dot((ri > ci).astype(BF16), onehot)
    counts = jnp.sum(onehot.astype(F32), axis=0, keepdims=True)
    padded = jnp.ceil(counts / FFN_ROWS) * FFN_ROWS
    lane1 = lax.broadcasted_iota(jnp.int32, (1, LANES), 1)
    offs = jnp.zeros((1, LANES), F32)
    for gidx in range(1, N_EXPERT_GROUPS):
        prev = jnp.sum(jnp.where(lane1 < gidx, padded, 0.0), axis=-1, keepdims=True)
        offs = jnp.where(lane1 == gidx, prev, offs)
    dest = jnp.sum(onehot.astype(F32) * (rank + offs), axis=-1, keepdims=True)
    slot = lax.broadcasted_iota(jnp.int32, (nb, nbp), 1).astype(F32)
    perm_t = (dest == slot).astype(BF16)
    xs_ref[...] = _dot_tn(perm_t, t.astype(BF16)).astype(BF16)
    g_hi = gates.astype(BF16)
    g_lo = (gates - g_hi.astype(F32)).astype(BF16)
    moved = _dot_tn(perm_t, jnp.concatenate([g_hi, g_lo], axis=1))
    gs_ref[...] = moved[:, :LANES] + moved[:, LANES:]
    dest_ref[...] = dest
    cnt_ref[0] = jnp.where(lane1 < N_EXPERT_GROUPS, counts, 0.0)


def _moe_ffn_kernel(tile_ref, grp_ref, nvalid_ref, xs_ref, gs_ref, wg_ref, wu_ref, wd_ref, ys_ref):
    i = pl.program_id(0)

    @pl.when(i < nvalid_ref[0])
    def _():
        x = xs_ref[...]
        gates = gs_ref[...]
        lane = lax.broadcasted_iota(jnp.int32, gates.shape, 1)
        base = N_EXPERT_GROUPS + grp_ref[i] * EXPERTS_PER_GROUP
        acc = jnp.zeros(ys_ref.shape, F32)
        for e in range(EXPERTS_PER_GROUP):
            ge = jnp.sum(jnp.where(lane == base + e, gates, 0.0), axis=-1, keepdims=True)
            gate_act = _dot(x, wg_ref[e])
            hdn = gate_act * _sigmoid(gate_act) * _dot(x, wu_ref[e])
            acc = acc + ge * _dot(hdn.astype(BF16), wd_ref[e])
        ys_ref[...] = acc.astype(BF16)

    @pl.when(i >= nvalid_ref[0])
    def _():
        ys_ref[...] = jnp.zeros_like(ys_ref)


def _moe_merge_kernel(h_ref, dest_ref, ys_ref, o_ref):
    nb, nbp = h_ref.shape[0], ys_ref.shape[0]
    slot = lax.broadcasted_iota(jnp.int32, (nb, nbp), 1).astype(F32)
    perm_t = (dest_ref[...] == slot).astype(BF16)
    o_ref[...] = h_ref[...] + _dot(perm_t, ys_ref[...])


def _moe_merge_norm_kernel(h_ref, dest_ref, ys_ref, g_ref, o_ref):
    nb, nbp = h_ref.shape[0], ys_ref.shape[0]
    slot = lax.broadcasted_iota(jnp.int32, (nb, nbp), 1).astype(F32)
    perm_t = (dest_ref[...] == slot).astype(BF16)
    o_ref[...] = _rms(h_ref[...] + _dot(perm_t, ys_ref[...])) * g_ref[...]


def _hier_moe_residual(h, norm_g, w_grp, b_grp, w_exp, b_exp, w_gate, w_up, w_down, final_g=None):
    n, d = h.shape
    nb = MOE_BLOCK
    nblk = n // nb
    nbp = nb + N_EXPERT_GROUPS * FFN_ROWS
    tiles_per_blk = nbp // FFN_ROWS
    n_exp = N_EXPERT_GROUPS * EXPERTS_PER_GROUP
    w_router = jnp.pad(jnp.concatenate([w_grp, w_exp], axis=1), ((0, 0), (0, LANES - N_EXPERT_GROUPS - n_exp)))
    b_router = jnp.pad(jnp.concatenate([b_grp, b_exp]), (0, LANES - N_EXPERT_GROUPS - n_exp)).reshape(1, LANES)

    xs, gs, dest, counts = pl.pallas_call(
        _moe_route_kernel,
        grid=(nblk,),
        in_specs=[pl.BlockSpec((nb, d), lambda i: (i, 0)), _const_spec((1, d)),
                  _const_spec((d, LANES)), _const_spec((1, LANES))],
        out_specs=[pl.BlockSpec((nbp, d), lambda i: (i, 0)), pl.BlockSpec((nbp, LANES), lambda i: (i, 0)),
                   pl.BlockSpec((nb, 1), lambda i: (i, 0)), pl.BlockSpec((1, 1, LANES), lambda i: (i, 0, 0))],
        out_shape=[jax.ShapeDtypeStruct((nblk * nbp, d), BF16), jax.ShapeDtypeStruct((nblk * nbp, LANES), F32),
                   jax.ShapeDtypeStruct((n, 1), F32), jax.ShapeDtypeStruct((nblk, 1, LANES), F32)],
        compiler_params=_cparams("arbitrary"),
        name="moe_route",
    )(h, norm_g.reshape(1, d), w_router, b_router)

    cnt = counts[:, 0, :N_EXPERT_GROUPS].astype(jnp.int32)
    tiles = (cnt + FFN_ROWS - 1) // FFN_ROWS
    first = jnp.cumsum(tiles, axis=1) - tiles + (jnp.arange(nblk, dtype=jnp.int32) * tiles_per_blk)[:, None]
    order_tiles = tiles.T.reshape(-1)
    order_first = first.T.reshape(-1)
    order_grp = jnp.repeat(jnp.arange(N_EXPERT_GROUPS, dtype=jnp.int32), nblk)
    ends = jnp.cumsum(order_tiles)
    n_valid = ends[-1]
    max_tiles = nblk * tiles_per_blk
    step = jnp.arange(max_tiles, dtype=jnp.int32)
    seg = jnp.minimum(jnp.sum((ends[None, :] <= step[:, None]).astype(jnp.int32), axis=1), order_tiles.shape[0] - 1)
    within = step - (ends[seg] - order_tiles[seg])
    used_idx = order_first[seg] + within
    unused = (step % tiles_per_blk) >= jnp.sum(tiles, axis=1)[step // tiles_per_blk]
    unused_rank = jnp.cumsum(unused.astype(jnp.int32)) - 1
    want = jnp.maximum(step - n_valid, 0)
    unused_idx = jnp.sum(jnp.where(unused[None, :] & (unused_rank[None, :] == want[:, None]), step[None, :], 0), axis=1)
    tile_idx = jnp.where(step < n_valid, used_idx, unused_idx).astype(jnp.int32)
    last_valid = jnp.maximum(n_valid - 1, 0)
    tile_grp = jnp.where(step < n_valid, order_grp[seg], order_grp[seg][last_valid]).astype(jnp.int32)

    def wspec(shape):
        return pl.BlockSpec((EXPERTS_PER_GROUP,) + shape, lambda i, tile, grp, nv: (grp[i], 0, 0))

    f = w_gate.shape[-1]
    ys = pl.pallas_call(
        _moe_ffn_kernel,
        grid_spec=pltpu.PrefetchScalarGridSpec(
            num_scalar_prefetch=3,
            grid=(max_tiles,),
            in_specs=[pl.BlockSpec((FFN_ROWS, d), lambda i, tile, grp, nv: (tile[i], 0)),
                      pl.BlockSpec((FFN_ROWS, LANES), lambda i, tile, grp, nv: (tile[i], 0)),
                      wspec((d, f)), wspec((d, f)), wspec((f, d))],
            out_specs=pl.BlockSpec((FFN_ROWS, d), lambda i, tile, grp, nv: (tile[i], 0)),
        ),
        out_shape=jax.ShapeDtypeStruct((nblk * nbp, d), BF16),
        compiler_params=_cparams("arbitrary"),
        name="moe_ffn",
    )(tile_idx, tile_grp, n_valid.reshape(1).astype(jnp.int32), xs, gs,
      w_gate.astype(BF16), w_up.astype(BF16), w_down.astype(BF16))

    blk = pl.BlockSpec((nb, d), lambda i: (i, 0))
    in_specs = [blk, pl.BlockSpec((nb, 1), lambda i: (i, 0)), pl.BlockSpec((nbp, d), lambda i: (i, 0))]
    args = [h, dest, ys]
    body = _moe_merge_kernel
    if final_g is not None:
        in_specs.append(_const_spec((1, d)))
        args.append(final_g.reshape(1, d))
        body = _moe_merge_norm_kernel
    return pl.pallas_call(
        body,
        grid=(nblk,),
        in_specs=in_specs,
        out_specs=blk,
        out_shape=jax.ShapeDtypeStruct((n, d), F32),
        compiler_params=_cparams("arbitrary"),
        name="moe_merge",
    )(*args)


GROUP_WIDTH = DIL_HEADS * DIL_HEAD_DIM
N_STRIDED = 3 * sum(1 for _, dil in DIL_PATTERNS if dil > 1)


def _qkv_kernel(h_ref, gq_ref, gkv_ref, wq_ref, wk_ref, wv_ref, *refs):
    outs, scratch = refs[:-1], refs[-1]
    n = _rms(h_ref[...])
    xq = (n * gq_ref[...]).astype(BF16)
    xkv = (n * gkv_ref[...]).astype(BF16)
    gw = GROUP_WIDTH
    slot = 0
    for gi, (_, dil) in enumerate(DIL_PATTERNS):
        cols = slice(gi * gw, (gi + 1) * gw)
        for t, (x, w_ref) in enumerate(((xq, wq_ref), (xkv, wk_ref), (xkv, wv_ref))):
            res = _dot(x, w_ref[:, cols])
            out_ref = outs[3 * gi + t]
            if dil == 1:
                out_ref[...] = res.astype(BF16)
                continue
            sub = res.shape[0] // dil
            for c in range(gw // LANES):
                scratch[slot, c] = res[:, c * LANES:(c + 1) * LANES]
            for r in range(dil):
                for c in range(gw // LANES):
                    lo = r * gw + c * LANES
                    out_ref[:, lo:lo + LANES] = scratch[slot, c, pl.ds(r, sub, stride=dil), :].astype(BF16)
            slot += 1


def _qkv_proj(h, g_q, g_kv, w_q, w_kv):
    n, d = h.shape
    qw = w_q.shape[1]
    gw = GROUP_WIDTH
    scale = 1.0 / math.sqrt(DIL_HEAD_DIM)
    rows = pl.BlockSpec((PROJ_ROWS, d), lambda i: (i, 0))
    out_specs, out_shape = [], []
    for _, dil in DIL_PATTERNS:
        for _ in range(3):
            out_specs.append(pl.BlockSpec((PROJ_ROWS // dil, dil * gw), lambda i: (i, 0)))
            out_shape.append(jax.ShapeDtypeStruct((n // dil, dil * gw), BF16))
    return pl.pallas_call(
        _qkv_kernel,
        grid=(n // PROJ_ROWS,),
        in_specs=[rows, _const_spec((1, d)), _const_spec((1, d))] + [_const_spec((d, qw))] * 3,
        out_specs=out_specs,
        out_shape=out_shape,
        scratch_shapes=[pltpu.VMEM((N_STRIDED, gw // LANES, PROJ_ROWS, LANES), F32)],
        compiler_params=_cparams("arbitrary"),
        name="qkv_proj",
    )(h, g_q.reshape(1, d), g_kv.reshape(1, d), (w_q * scale).astype(BF16),
      w_kv[:, :qw].astype(BF16), w_kv[:, qw:].astype(BF16))


def _attn_kernel(q_ref, kc_ref, kp_ref, vc_ref, vp_ref, o_ref, lse_ref):
    n = pl.program_id(2)
    blk = q_ref.shape[1]
    pw = 2 * DIL_HEAD_DIM
    pairs = range(DIL_HEADS // 2)
    ri = lax.broadcasted_iota(jnp.int32, (2 * blk, blk), 0)
    ci = lax.broadcasted_iota(jnp.int32, (2 * blk, blk), 1)
    qi = jnp.where(ri >= blk, ri - blk, ri)
    prev_ok = (ci >= qi) & (n > 0)
    cur_ok = ci <= qi
    head0 = lax.broadcasted_iota(jnp.int32, (blk, pw), 1) < DIL_HEAD_DIM
    lowest = jnp.finfo(F32).min
    lanes = [slice(p * pw, (p + 1) * pw) for p in pairs]
    q = [q_ref[0, :, lanes[p]] for p in pairs]
    zero = jnp.zeros((blk, pw), BF16)
    qs = [jnp.concatenate([jnp.where(head0, q[p], zero), jnp.where(head0, zero, q[p])], axis=0) for p in pairs]
    s_p = [jnp.where(prev_ok, _dot_nt(qs[p], kp_ref[0, :, lanes[p]]), lowest) for p in pairs]
    s_c = [jnp.where(cur_ok, _dot_nt(qs[p], kc_ref[0, :, lanes[p]]), lowest) for p in pairs]
    m = [jnp.maximum(jnp.max(s_p[p], axis=-1, keepdims=True), jnp.max(s_c[p], axis=-1, keepdims=True)) for p in pairs]
    e_p = [jnp.exp(s_p[p] - m[p]) for p in pairs]
    e_c = [jnp.exp(s_c[p] - m[p]) for p in pairs]
    l = [jnp.sum(e_p[p], axis=-1, keepdims=True) + jnp.sum(e_c[p], axis=-1, keepdims=True) for p in pairs]
    pv = [_dot(e_p[p].astype(BF16), vp_ref[0, :, lanes[p]]) + _dot(e_c[p].astype(BF16), vc_ref[0, :, lanes[p]])
          for p in pairs]
    lane_t = lax.broadcasted_iota(jnp.int32, (blk, LANES), 1)
    tile = jnp.zeros((blk, LANES), F32)
    for p in pairs:
        o = pv[p] / l[p]
        o_ref[0, :, lanes[p]] = jnp.where(head0, o[:blk], o[blk:]).astype(BF16)
        lse = m[p] + jnp.log(l[p])
        tile = jnp.where(lane_t == 2 * p, lse[:blk], jnp.where(lane_t == 2 * p + 1, lse[blk:], tile))
    lse_ref[0] = tile


def _attn_group(q, k, v, gi, dilation, bsz, seq):
    gw = GROUP_WIDTH
    sub = seq // dilation
    nblk = sub // ATT_BLOCK
    view = (bsz, sub, dilation * gw)
    qv, kv_, vv = q.reshape(view), k.reshape(view), v.reshape(view)
    cur = pl.BlockSpec((1, ATT_BLOCK, gw), lambda b, r, n: (b, n, r))
    prev = pl.BlockSpec((1, ATT_BLOCK, gw), lambda b, r, n: (b, jnp.maximum(n - 1, 0), r))
    lse_spec = pl.BlockSpec((1, ATT_BLOCK, LANES), lambda b, r, n: (b, n, r))
    o, lse = pl.pallas_call(
        _attn_kernel,
        grid=(bsz, dilation, nblk),
        in_specs=[cur, cur, prev, cur, prev],
        out_specs=[cur, lse_spec],
        out_shape=[jax.ShapeDtypeStruct(view, BF16),
                   jax.ShapeDtypeStruct((bsz, sub, dilation * LANES), F32)],
        compiler_params=_cparams("arbitrary", "arbitrary", "arbitrary"),
        name=f"attn_group{gi}",
    )(qv, kv_, kv_, vv, vv)
    return o.reshape(bsz * sub, dilation * gw), lse.reshape(bsz * sub, dilation * LANES)


def _attn_merge_kernel(o0_ref, o1_ref, o2_ref, l0_ref, l1_ref, l2_ref, w_ref, res_ref, out_ref, *scratch):
    rows = res_ref.shape[0]
    gw = GROUP_WIDTH
    outs, lses = [], []
    slot = 0
    for (_, dil), o_ref, l_ref in zip(DIL_PATTERNS, (o0_ref, o1_ref, o2_ref), (l0_ref, l1_ref, l2_ref)):
        if dil == 1:
            outs.append(o_ref[...].astype(F32))
            lses.append(l_ref[...])
            continue
        o_scr, l_scr = scratch[2 * slot], scratch[2 * slot + 1]
        sub = rows // dil
        for r in range(dil):
            for c in range(gw // LANES):
                lo = r * gw + c * LANES
                o_scr[c, pl.ds(r, sub, stride=dil), :] = o_ref[:, lo:lo + LANES].astype(F32)
            l_scr[pl.ds(r, sub, stride=dil), :] = l_ref[:, r * LANES:(r + 1) * LANES]
        outs.append(jnp.concatenate([o_scr[c] for c in range(gw // LANES)], axis=1))
        lses.append(l_scr[...])
        slot += 1
    m = jnp.maximum(jnp.maximum(lses[0], lses[1]), lses[2])
    e = [jnp.exp(l - m) for l in lses]
    denom = e[0] + e[1] + e[2]
    expand = (lax.broadcasted_iota(jnp.int32, (LANES, gw), 1) // DIL_HEAD_DIM
              == lax.broadcasted_iota(jnp.int32, (LANES, gw), 0)).astype(BF16)
    mix = jnp.zeros((rows, gw), F32)
    for g in range(len(outs)):
        alpha = e[g] / denom
        hi = alpha.astype(BF16)
        lo = (alpha - hi.astype(F32)).astype(BF16)
        mix = mix + (_dot(hi, expand) + _dot(lo, expand)) * outs[g]
    out_ref[...] = res_ref[...] + _dot(mix.astype(BF16), w_ref[...])


def _attn_merge(outs, lses, w_o, res):
    n, d = res.shape
    gw = GROUP_WIDTH
    o_specs = [pl.BlockSpec((PROJ_ROWS // dil, dil * gw), lambda i: (i, 0)) for _, dil in DIL_PATTERNS]
    l_specs = [pl.BlockSpec((PROJ_ROWS // dil, dil * LANES), lambda i: (i, 0)) for _, dil in DIL_PATTERNS]
    full = pl.BlockSpec((PROJ_ROWS, d), lambda i: (i, 0))
    scratch = []
    for _, dil in DIL_PATTERNS:
        if dil > 1:
            scratch += [pltpu.VMEM((gw // LANES, PROJ_ROWS, LANES), F32), pltpu.VMEM((PROJ_ROWS, LANES), F32)]
    return pl.pallas_call(
        _attn_merge_kernel,
        grid=(n // PROJ_ROWS,),
        in_specs=o_specs + l_specs + [_const_spec(w_o.shape), full],
        out_specs=full,
        out_shape=jax.ShapeDtypeStruct((n, d), F32),
        scratch_shapes=scratch,
        compiler_params=_cparams("arbitrary"),
        name="attn_merge",
    )(*outs, *lses, w_o.astype(BF16), res)


def kernel(x, norm_mix_g, norm_ffn_g, rwkv_mu, rwkv_w_r, rwkv_w_k, rwkv_w_v, rwkv_w0, rwkv_w1, rwkv_w2,
           rwkv_a0, rwkv_a1, rwkv_a2, rwkv_g1, rwkv_g2, rwkv_k_k, rwkv_k_a, rwkv_r_k, rwkv_ln_w, rwkv_ln_b,
           rwkv_w_o, kv_norm_g, w_kv, attn_w_q, attn_w_o, moe_w_grp, moe_b_grp, moe_w_exp, moe_b_exp,
           moe_w_gate, moe_w_up, moe_w_down, final_norm_g):
    bsz, seq, d = x.shape
    n = bsz * seq
    depth = norm_mix_g.shape[0]
    n_rwkv = rwkv_mu.shape[0]
    h = x.reshape(n, d)
    q = k_sh = v_sh = None
    for layer in range(depth):
        if layer < n_rwkv:
            i = layer
            r, k, v, a, lw, g = _rwkv_pre(h.reshape(bsz, seq, d), norm_mix_g[layer], rwkv_mu[i], rwkv_w0[i], rwkv_a0[i],
                                          rwkv_w_r[i], rwkv_w_k[i], rwkv_w_v[i], rwkv_w1[i], rwkv_w2[i],
                                          rwkv_a1[i], rwkv_a2[i], rwkv_g1[i], rwkv_g2[i])
            y = _rwkv_rec(r, k, v, a, lw, g, rwkv_k_k[i], rwkv_k_a[i], rwkv_r_k[i], rwkv_ln_w[i], rwkv_ln_b[i])
            h = _proj_res(y.reshape(n, d), rwkv_w_o[i], h)
        else:
            i = layer - n_rwkv
            qkv = _qkv_proj(h, norm_mix_g[layer], kv_norm_g, attn_w_q[i], w_kv)
            q = qkv[0::3]
            if i == 0:
                k_sh, v_sh = qkv[1::3], qkv[2::3]
            outs, lses = [], []
            for gi, (window, dilation) in enumerate(DIL_PATTERNS):
                assert window // dilation == ATT_BLOCK and seq % window == 0
                o, lse = _attn_group(q[gi], k_sh[gi], v_sh[gi], gi, dilation, bsz, seq)
                outs.append(o)
                lses.append(lse)
            h = _attn_merge(outs, lses, attn_w_o[i], h)
        last = layer == depth - 1
        h = _hier_moe_residual(h, norm_ffn_g[layer], moe_w_grp[layer], moe_b_grp[layer], moe_w_exp[layer],
                               moe_b_exp[layer], moe_w_gate[layer], moe_w_up[layer], moe_w_down[layer],
                               final_g=final_norm_g if last else None)
    return h.reshape(bsz, seq, d)
```

```python
import functools
import math

import jax
import jax.numpy as jnp
from jax import lax
from jax.experimental import pallas as pl
from jax.experimental.pallas import tpu as pltpu

F32 = jnp.float32
BF16 = jnp.bfloat16
HIGHEST = lax.Precision.HIGHEST

NORM_EPS = 1e-6
RWKV_HEAD = 64
GN_EPS = RWKV_HEAD * 1e-5
DIL_PATTERNS = ((128, 1), (512, 4), (2048, 16))
DIL_HEADS = 8
DIL_HEAD_DIM = 64
N_EXPERT_GROUPS = 4
EXPERTS_PER_GROUP = 4
LANES = 128
VMEM_LIMIT = 56 * 1024 * 1024

CHUNK = 64
REC_ROWS = 512
PRE_ROWS = 256
PROJ_ROWS = 512
MOE_BLOCK = 1024
FFN_ROWS = 128
FFN_TILES = 4
ATT_BLOCK = 128


def _cparams(*sem):
    return pltpu.CompilerParams(dimension_semantics=sem, vmem_limit_bytes=VMEM_LIMIT)


def _dot(a, b):
    return jnp.dot(a, b, preferred_element_type=F32)


def _dot_nt(a, b):
    return lax.dot_general(a, b, (((1,), (1,)), ((), ())), preferred_element_type=F32)


def _dot_tn(a, b, precision=None):
    return lax.dot_general(a, b, (((0,), (0,)), ((), ())), preferred_element_type=F32, precision=precision)


def _rms(x):
    return x * lax.rsqrt(jnp.mean(x * x, axis=-1, keepdims=True) + NORM_EPS)


def _sigmoid(z):
    return 1.0 / (1.0 + jnp.exp(-z))


def _const_spec(shape):
    nd = len(shape)
    return pl.BlockSpec(shape, lambda *_: (0,) * nd)


def _rwkv_pre_kernel(x_ref, gmix_ref, mu_ref, w0_ref, a0_ref, wr_ref, wk_ref, wv_ref,
                     w1_ref, w2_ref, a1_ref, a2_ref, g1_ref, g2_ref,
                     r_ref, k_ref, v_ref, a_ref, lw_ref, g_ref, prev_ref):
    @pl.when(pl.program_id(1) == 0)
    def _():
        prev_ref[...] = jnp.zeros_like(prev_ref)

    x = x_ref[0]
    rows = x.shape[0]
    xn = _rms(x) * gmix_ref[...]
    row = lax.broadcasted_iota(jnp.int32, xn.shape, 0)
    shifted = jnp.where(row == 0, prev_ref[7:8, :], pltpu.roll(xn, 1, 0))
    prev_ref[...] = xn[rows - 8:, :]
    xx = shifted - xn

    def mix(i):
        return (xn + xx * mu_ref[i:i + 1, :]).astype(BF16)

    r_ref[0] = _dot(mix(0), wr_ref[...]).astype(BF16)
    k_ref[0] = _dot(mix(2), wk_ref[...]).astype(BF16)
    v_ref[0] = _dot(mix(3), wv_ref[...]).astype(BF16)
    u = w0_ref[...] + _dot(jnp.tanh(_dot(mix(1), w1_ref[...])).astype(BF16), w2_ref[...])
    w_log = -(jnp.maximum(-u, 0.0) + jnp.log(1.0 + jnp.exp(-jnp.abs(u)))) - 0.5
    lw_ref[0] = -jnp.exp(w_log)
    a_ref[0] = _sigmoid(a0_ref[...] + _dot(_dot(mix(4), a1_ref[...]).astype(BF16), a2_ref[...])).astype(BF16)
    g_ref[0] = _dot(_sigmoid(_dot(mix(5), g1_ref[...])).astype(BF16), g2_ref[...]).astype(BF16)


def _pad_cols(w, n):
    return jnp.pad(w, ((0, 0), (0, n - w.shape[1])))


def _pad_rows(w, n):
    return jnp.pad(w, ((0, n - w.shape[0]), (0, 0)))


def _rwkv_pre(x, gmix, mu, w0, a0, w_r, w_k, w_v, w1, w2, a1, a2, g1, g2):
    bsz, seq, d = x.shape
    lw = -(-w1.shape[1] // LANES) * LANES
    la = -(-a1.shape[1] // LANES) * LANES
    lg = -(-g1.shape[1] // LANES) * LANES
    mu8 = jnp.pad(mu, ((0, 8 - mu.shape[0]), (0, 0)))
    args = (x, gmix.reshape(1, d), mu8, w0.reshape(1, d), a0.reshape(1, d),
            w_r.astype(BF16), w_k.astype(BF16), w_v.astype(BF16),
            _pad_cols(w1, lw).astype(BF16), _pad_rows(w2, lw).astype(BF16),
            _pad_cols(a1, la).astype(BF16), _pad_rows(a2, la).astype(BF16),
            _pad_cols(g1, lg).astype(BF16), _pad_rows(g2, lg).astype(BF16))
    tile = pl.BlockSpec((1, PRE_ROWS, d), lambda b, s: (b, s, 0))
    in_specs = [tile] + [_const_spec(a.shape) for a in args[1:]]
    out_dt = (BF16, BF16, BF16, BF16, F32, BF16)
    return pl.pallas_call(
        _rwkv_pre_kernel,
        grid=(bsz, seq // PRE_ROWS),
        in_specs=in_specs,
        out_specs=[tile] * 6,
        out_shape=[jax.ShapeDtypeStruct((bsz, seq, d), t) for t in out_dt],
        scratch_shapes=[pltpu.VMEM((8, d), F32)],
        compiler_params=_cparams("arbitrary", "arbitrary"),
        name="rwkv_pre",
    )(*args)


def _rwkv_rec_kernel(r_ref, k_ref, v_ref, a_ref, lw_ref, g_ref, kk_ref, ka_ref, rk_ref, lnw_ref, lnb_ref,
                     o_ref, state_ref):
    @pl.when(pl.program_id(2) == 0)
    def _():
        state_ref[...] = jnp.zeros_like(state_ref)

    c = CHUNK
    hd = RWKV_HEAD
    w = 2 * hd
    n = r_ref.shape[1] // c
    lane = lax.broadcasted_iota(jnp.int32, (c, w), 1)
    head0 = lane < hd
    ri = lax.broadcasted_iota(jnp.int32, (w, w), 0)
    ci = lax.broadcasted_iota(jnp.int32, (w, w), 1)
    same = (ri < c) == (ci < c)
    strict = same & (ri > ci)
    incl = same & (ri >= ci)
    eye = (ri == ci).astype(F32)
    tri = (lax.broadcasted_iota(jnp.int32, (c, c), 0) >= lax.broadcasted_iota(jnp.int32, (c, c), 1)).astype(BF16)
    kkp, kap, rkp, lnw, lnb = kk_ref[...], ka_ref[...], rk_ref[...], lnw_ref[...], lnb_ref[...]

    def head_sum(x):
        s0 = jnp.sum(jnp.where(head0, x, 0.0), axis=-1, keepdims=True)
        s1 = jnp.sum(jnp.where(head0, 0.0, x), axis=-1, keepdims=True)
        return jnp.where(head0, s0, s1)

    def stack(x):
        return jnp.concatenate([jnp.where(head0, x, 0.0), jnp.where(head0, 0.0, x)], axis=0)

    js = range(n)
    rows = [slice(j * c, (j + 1) * c) for j in js]
    r_ = [r_ref[0, rows[j], :].astype(F32) for j in js]
    k_ = [k_ref[0, rows[j], :].astype(F32) for j in js]
    v_ = [v_ref[0, rows[j], :].astype(F32) for j in js]
    a_ = [a_ref[0, rows[j], :].astype(F32) for j in js]
    lw_ = [lw_ref[0, rows[j], :] for j in js]

    def cumsum(x):
        hi = x.astype(BF16)
        lo = (x - hi.astype(F32)).astype(BF16)
        both = _dot(tri, jnp.concatenate([hi, lo], axis=1))
        return both[:, :w] + both[:, w:]

    cum = [cumsum(lw_[j]) for j in js]
    kk = [k_[j] * kkp for j in js]
    kk = [kk[j] / jnp.maximum(jnp.sqrt(head_sum(kk[j] * kk[j])), 1e-12) for j in js]
    k2 = [k_[j] * (1.0 + (a_[j] - 1.0) * kap) for j in js]
    p_ = [-(kk[j] * a_[j]) for j in js]
    tot = [cum[j][c - 1:c, :] for j in js]
    e_neg = [jnp.exp(-cum[j]) for j in js]
    e_tot = [jnp.exp(tot[j] - cum[j]) for j in js]
    rt_s = [stack(r_[j] * jnp.exp(cum[j])) for j in js]
    qt_s = [stack(kk[j] * jnp.exp(cum[j] - lw_[j])).astype(BF16) for j in js]
    kh_s = [stack(k2[j] * e_neg[j]).astype(BF16) for j in js]
    ph_s = [stack(p_[j] * e_neg[j]).astype(BF16) for j in js]
    kt_s = [stack(k2[j] * e_tot[j]).astype(BF16) for j in js]
    pt_s = [stack(p_[j] * e_tot[j]).astype(BF16) for j in js]
    v_s = [stack(v_[j]).astype(BF16) for j in js]
    gam_col = [jnp.sum(eye * jnp.exp(tot[j]), axis=-1, keepdims=True) for j in js]

    amat = [_dot_nt(jnp.concatenate([qt_s[j], rt_s[j].astype(BF16)], axis=0),
                    jnp.concatenate([kh_s[j], ph_s[j]], axis=0)) for j in js]
    a_qp = [jnp.where(strict, amat[j][:w, w:], 0.0) for j in js]
    a_rp = [jnp.where(incl, amat[j][w:, w:], 0.0).astype(BF16) for j in js]
    a_k = [jnp.concatenate([jnp.where(strict, amat[j][:w, :w], 0.0),
                            jnp.where(incl, amat[j][w:, :w], 0.0)], axis=0).astype(BF16) for j in js]
    av = [_dot(a_k[j], v_s[j]) for j in js]

    inv = [eye + a_qp[j] for j in js]
    power = [a_qp[j].astype(BF16) for j in js]
    for _ in range(int(math.log2(c)) - 1):
        sq = [_dot(power[j], power[j]) for j in js]
        power = [sq[j].astype(BF16) for j in js]
        inv = [inv[j] + _dot(inv[j].astype(BF16), power[j]) for j in js]

    wu = [_dot(inv[j].astype(BF16), jnp.concatenate([qt_s[j], av[j][:w].astype(BF16)], axis=1)).astype(BF16)
          for j in js]
    ry = [_dot(a_rp[j], wu[j]) for j in js]
    rw = [(rt_s[j] + ry[j][:, :w]).astype(BF16) for j in js]
    yv = [av[j][w:] + ry[j][:, w:] for j in js]
    zeros = jnp.zeros((w, w), BF16)
    mg = [_dot_tn(jnp.concatenate([pt_s[j], kt_s[j]], axis=0),
                  jnp.concatenate([wu[j], jnp.concatenate([zeros, v_s[j]], axis=1)], axis=0)) for j in js]
    bonus = [head_sum(r_[j] * k2[j] * rkp) * v_[j] for j in js]

    state = state_ref[...]
    ys = []
    for j in js:
        sb = state.astype(BF16)
        y_s = _dot(rw[j], sb) + yv[j]
        ys.append(y_s[:c] + y_s[c:])
        state = gam_col[j] * state + _dot(mg[j][:, :w].astype(BF16), sb) + mg[j][:, w:]
    state_ref[...] = state

    for j in js:
        y = ys[j]
        yc = y - head_sum(y) * (1.0 / hd)
        var = head_sum(yc * yc) * (1.0 / hd)
        yn = yc * lax.rsqrt(var + GN_EPS) * lnw + lnb
        o_ref[0, rows[j], :] = ((yn + bonus[j]) * g_ref[0, rows[j], :].astype(F32)).astype(BF16)


def _rwkv_rec(r, k, v, a, lw, g, k_k, k_a, r_k, ln_w, ln_b):
    bsz, seq, d = r.shape
    hw = 2 * RWKV_HEAD
    tile = pl.BlockSpec((1, REC_ROWS, hw), lambda b, h, s: (b, s, h))
    par = pl.BlockSpec((1, hw), lambda b, h, s: (0, h))
    params = [p.reshape(1, d).astype(F32) for p in (k_k, k_a, r_k, ln_w, ln_b)]
    return pl.pallas_call(
        _rwkv_rec_kernel,
        grid=(bsz, d // hw, seq // REC_ROWS),
        in_specs=[tile] * 6 + [par] * 5,
        out_specs=tile,
        out_shape=jax.ShapeDtypeStruct((bsz, seq, d), BF16),
        scratch_shapes=[pltpu.VMEM((hw, hw), F32)],
        compiler_params=_cparams("arbitrary", "arbitrary", "arbitrary"),
        name="rwkv_rec",
    )(r, k, v, a, lw, g, *params)


def _proj_res_kernel(a_ref, w_ref, res_ref, o_ref):
    o_ref[...] = res_ref[...] + _dot(a_ref[...], w_ref[...])


def _proj_res(a, w, res):
    n, kdim = a.shape
    d = w.shape[1]
    return pl.pallas_call(
        _proj_res_kernel,
        grid=(n // PROJ_ROWS,),
        in_specs=[pl.BlockSpec((PROJ_ROWS, kdim), lambda i: (i, 0)), _const_spec(w.shape),
                  pl.BlockSpec((PROJ_ROWS, d), lambda i: (i, 0))],
        out_specs=pl.BlockSpec((PROJ_ROWS, d), lambda i: (i, 0)),
        out_shape=jax.ShapeDtypeStruct((n, d), F32),
        compiler_params=_cparams("arbitrary"),
        name="proj_res",
    )(a, w.astype(BF16), res)


def _moe_route_kernel(h_ref, g_ref, wr_ref, br_ref, xs_ref, gs_ref, dest_ref, cnt_ref):
    nb = h_ref.shape[0]
    nbp = xs_ref.shape[0]
    t = _rms(h_ref[...]) * g_ref[...]
    logits = jnp.dot(t, wr_ref[...], precision=HIGHEST, preferred_element_type=F32) + br_ref[...]
    lane = lax.broadcasted_iota(jnp.int32, logits.shape, 1).astype(F32)
    neg = jnp.float32(-jnp.inf)
    big = jnp.float32(LANES)
    is_grp = lane < N_EXPERT_GROUPS
    gl = jnp.where(is_grp, logits, neg)
    gmax = jnp.max(gl, axis=-1, keepdims=True)
    grp = jnp.min(jnp.where(gl == gmax, lane, big), axis=-1, keepdims=True)
    p_grp = 1.0 / jnp.sum(jnp.where(is_grp, jnp.exp(gl - gmax), 0.0), axis=-1, keepdims=True)
    e_lo = N_EXPERT_GROUPS + grp * EXPERTS_PER_GROUP
    in_grp = (lane >= e_lo) & (lane < e_lo + EXPERTS_PER_GROUP)
    el = jnp.where(in_grp, logits, neg)
    v1 = jnp.max(el, axis=-1, keepdims=True)
    i1 = jnp.min(jnp.where(el == v1, lane, big), axis=-1, keepdims=True)
    el2 = jnp.where(lane == i1, neg, el)
    v2 = jnp.max(el2, axis=-1, keepdims=True)
    i2 = jnp.min(jnp.where(el2 == v2, lane, big), axis=-1, keepdims=True)
    e2 = jnp.exp(v2 - v1)
    w1 = 1.0 / (1.0 + e2)
    w2 = e2 / (1.0 + e2)
    gates = jnp.where(lane == i1, w1, jnp.where(lane == i2, w2, 0.0)) * p_grp

    onehot = (lane == grp).astype(BF16)
    ri = lax.broadcasted_iota(jnp.int32, (nb, nb), 0)
    ci = lax.broadcasted_iota(jnp.int32, (nb, nb), 1)
    rank = _dot((ri > ci).astype(BF16), onehot)
    counts = jnp.sum(onehot.astype(F32), axis=0, keepdims=True)
    padded = jnp.ceil(counts / FFN_ROWS) * FFN_ROWS
    lane1 = lax.broadcasted_iota(jnp.int32, (1, LANES), 1)
    offs = jnp.zeros((1, LANES), F32)
    for gidx in range(1, N_EXPERT_GROUPS):
        prev = jnp.sum(jnp.where(lane1 < gidx, padded, 0.0), axis=-1, keepdims=True)
        offs = jnp.where(lane1 == gidx, prev, offs)
    dest = jnp.sum(onehot.astype(F32) * (rank + offs), axis=-1, keepdims=True)
    slot = lax.broadcasted_iota(jnp.int32, (nb, nbp), 1).astype(F32)
    perm_t = (dest == slot).astype(BF16)
    xs_ref[...] = _dot_tn(perm_t, t.astype(BF16)).astype(BF16)
    g_hi = gates.astype(BF16)
    g_lo = (gates - g_hi.astype(F32)).astype(BF16)
    moved = _dot_tn(perm_t, jnp.concatenate([g_hi, g_lo], axis=1))
    gs_ref[...] = moved[:, :LANES] + moved[:, LANES:]
    dest_ref[...] = dest
    cnt_ref[0] = jnp.where(lane1 < N_EXPERT_GROUPS, counts, 0.0)


def _moe_ffn_kernel(tile_ref, grp_ref, real_ref, *refs):
    nt = FFN_TILES
    xs_refs, gs_refs = refs[:nt], refs[nt:2 * nt]
    wg_ref, wu_ref, wd_ref, ys_ref = refs[2 * nt:]
    i = pl.program_id(0)

    @pl.when(real_ref[i] > 0)
    def _():
        x = jnp.concatenate([r[...] for r in xs_refs], axis=0)
        gates = jnp.concatenate([r[...] for r in gs_refs], axis=0)
        lane = lax.broadcasted_iota(jnp.int32, gates.shape, 1)
        base = N_EXPERT_GROUPS + grp_ref[i] * EXPERTS_PER_GROUP
        acc = jnp.zeros(ys_ref.shape, F32)
        for e in range(EXPERTS_PER_GROUP):
            ge = jnp.sum(jnp.where(lane == base + e, gates, 0.0), axis=-1, keepdims=True)
            gate_act = _dot(x, wg_ref[e])
            hdn = gate_act * _sigmoid(gate_act) * _dot(x, wu_ref[e])
            acc = acc + ge * _dot(hdn.astype(BF16), wd_ref[e])
        ys_ref[...] = acc.astype(BF16)

    @pl.when(real_ref[i] == 0)
    def _():
        ys_ref[...] = jnp.zeros_like(ys_ref)


def _moe_merge_kernel(slot_ref, h_ref, dest_ref, *refs):
    ys_refs, o_ref = refs[:-1], refs[-1]
    ys = jnp.concatenate([r[...] for r in ys_refs], axis=0)
    nb, nbp = h_ref.shape[0], ys.shape[0]
    slot = lax.broadcasted_iota(jnp.int32, (nb, nbp), 1).astype(F32)
    perm_t = (dest_ref[...] == slot).astype(BF16)
    o_ref[...] = h_ref[...] + _dot(perm_t, ys)


def _moe_merge_norm_kernel(slot_ref, h_ref, dest_ref, *refs):
    ys_refs, g_ref, o_ref = refs[:-2], refs[-2], refs[-1]
    ys = jnp.concatenate([r[...] for r in ys_refs], axis=0)
    nb, nbp = h_ref.shape[0], ys.shape[0]
    slot = lax.broadcasted_iota(jnp.int32, (nb, nbp), 1).astype(F32)
    perm_t = (dest_ref[...] == slot).astype(BF16)
    o_ref[...] = _rms(h_ref[...] + _dot(perm_t, ys)) * g_ref[...]


def _ffn_slots(n_tiles):
    return -(-(n_tiles + N_EXPERT_GROUPS * (FFN_TILES - 1)) // FFN_TILES) * FFN_TILES


def _ffn_schedule(counts, nblk, tiles_per_blk):
    i32 = jnp.int32
    tiles = (counts + FFN_ROWS - 1) // FFN_ROWS
    first_rel = jnp.cumsum(tiles, axis=1) - tiles
    used_blk = jnp.sum(tiles, axis=1)
    grp_pad = (jnp.sum(tiles, axis=0) + FFN_TILES - 1) // FFN_TILES * FFN_TILES
    grp_start = jnp.cumsum(grp_pad) - grp_pad
    before = jnp.cumsum(tiles, axis=0) - tiles
    n_tiles = nblk * tiles_per_blk
    t = jnp.arange(n_tiles, dtype=i32)
    b, k = t // tiles_per_blk, t % tiles_per_blk
    used = k < used_blk[b]
    grp_t = jnp.minimum(jnp.sum(((first_rel + tiles)[b] <= k[:, None]).astype(i32), axis=1), N_EXPERT_GROUPS - 1)
    pos = grp_start[grp_t] + before[b, grp_t] + (k - first_rel[b, grp_t])
    s = jnp.arange(_ffn_slots(n_tiles), dtype=i32)
    hit = used[None, :] & (pos[None, :] == s[:, None])
    slot_has = jnp.sum(hit.astype(i32), axis=1) > 0
    free_rank = jnp.cumsum((~slot_has).astype(i32)) - 1
    unused_rank = jnp.cumsum((~used).astype(i32)) - 1
    fill = (~used)[None, :] & (unused_rank[None, :] == free_rank[:, None])
    placed = jnp.where(slot_has[:, None], hit, fill)
    slot_tile = jnp.sum(jnp.where(placed, t[None, :], 0), axis=1).astype(i32)
    tile_slot = jnp.sum(jnp.where(placed, s[:, None], 0), axis=0).astype(i32)
    slot_grp = jnp.sum(jnp.where(hit, grp_t[None, :], 0), axis=1).reshape(-1, FFN_TILES)
    step_real = jnp.sum(slot_has.astype(i32).reshape(-1, FFN_TILES), axis=1).astype(i32)
    n_real_steps = jnp.sum(grp_pad) // FFN_TILES
    last_grp = slot_grp[jnp.maximum(n_real_steps - 1, 0), 0]
    step_grp = jnp.where(step_real > 0, slot_grp[:, 0], last_grp).astype(i32)
    return slot_tile, tile_slot, step_grp, step_real


def _hier_moe_residual(h, norm_g, w_grp, b_grp, w_exp, b_exp, w_gate, w_up, w_down, final_g=None):
    n, d = h.shape
    nb = MOE_BLOCK
    nblk = n // nb
    nbp = nb + N_EXPERT_GROUPS * FFN_ROWS
    tiles_per_blk = nbp // FFN_ROWS
    n_exp = N_EXPERT_GROUPS * EXPERTS_PER_GROUP
    w_router = jnp.pad(jnp.concatenate([w_grp, w_exp], axis=1), ((0, 0), (0, LANES - N_EXPERT_GROUPS - n_exp)))
    b_router = jnp.pad(jnp.concatenate([b_grp, b_exp]), (0, LANES - N_EXPERT_GROUPS - n_exp)).reshape(1, LANES)

    xs, gs, dest, counts = pl.pallas_call(
        _moe_route_kernel,
        grid=(nblk,),
        in_specs=[pl.BlockSpec((nb, d), lambda i: (i, 0)), _const_spec((1, d)),
                  _const_spec((d, LANES)), _const_spec((1, LANES))],
        out_specs=[pl.BlockSpec((nbp, d), lambda i: (i, 0)), pl.BlockSpec((nbp, LANES), lambda i: (i, 0)),
                   pl.BlockSpec((nb, 1), lambda i: (i, 0)), pl.BlockSpec((1, 1, LANES), lambda i: (i, 0, 0))],
        out_shape=[jax.ShapeDtypeStruct((nblk * nbp, d), BF16), jax.ShapeDtypeStruct((nblk * nbp, LANES), F32),
                   jax.ShapeDtypeStruct((n, 1), F32), jax.ShapeDtypeStruct((nblk, 1, LANES), F32)],
        compiler_params=_cparams("arbitrary"),
        name="moe_route",
    )(h, norm_g.reshape(1, d), w_router, b_router)

    n_tiles = nblk * tiles_per_blk
    n_slots = _ffn_slots(n_tiles)
    slot_tile, tile_slot, step_grp, step_real = _ffn_schedule(
        counts[:, 0, :N_EXPERT_GROUPS].astype(jnp.int32), nblk, tiles_per_blk)

    def tile_spec(width, j):
        return pl.BlockSpec((FFN_ROWS, width), lambda i, tile, grp, real: (tile[i * FFN_TILES + j], 0))

    def wspec(shape):
        return pl.BlockSpec((EXPERTS_PER_GROUP,) + shape, lambda i, tile, grp, real: (grp[i], 0, 0))

    f = w_gate.shape[-1]
    ys = pl.pallas_call(
        _moe_ffn_kernel,
        grid_spec=pltpu.PrefetchScalarGridSpec(
            num_scalar_prefetch=3,
            grid=(n_slots // FFN_TILES,),
            in_specs=[tile_spec(d, j) for j in range(FFN_TILES)] + [tile_spec(LANES, j) for j in range(FFN_TILES)]
            + [wspec((d, f)), wspec((d, f)), wspec((f, d))],
            out_specs=pl.BlockSpec((FFN_TILES * FFN_ROWS, d), lambda i, tile, grp, real: (i, 0)),
        ),
        out_shape=jax.ShapeDtypeStruct((n_slots * FFN_ROWS, d), BF16),
        compiler_params=_cparams("arbitrary"),
        name="moe_ffn",
    )(slot_tile, step_grp, step_real, *([xs] * FFN_TILES), *([gs] * FFN_TILES),
      w_gate.astype(BF16), w_up.astype(BF16), w_down.astype(BF16))

    blk = pl.BlockSpec((nb, d), lambda i, slot: (i, 0))
    in_specs = [blk, pl.BlockSpec((nb, 1), lambda i, slot: (i, 0))]
    in_specs += [pl.BlockSpec((FFN_ROWS, d), functools.partial(lambda j, i, slot: (slot[i * tiles_per_blk + j], 0), j))
                 for j in range(tiles_per_blk)]
    args = [h, dest] + [ys] * tiles_per_blk
    body = _moe_merge_kernel
    if final_g is not None:
        in_specs.append(pl.BlockSpec((1, d), lambda i, slot: (0, 0)))
        args.append(final_g.reshape(1, d))
        body = _moe_merge_norm_kernel
    return pl.pallas_call(
        body,
        grid_spec=pltpu.PrefetchScalarGridSpec(num_scalar_prefetch=1, grid=(nblk,), in_specs=in_specs, out_specs=blk),
        out_shape=jax.ShapeDtypeStruct((n, d), F32),
        compiler_params=_cparams("arbitrary"),
        name="moe_merge",
    )(tile_slot, *args)


GROUP_WIDTH = DIL_HEADS * DIL_HEAD_DIM
N_STRIDED = 3 * sum(1 for _, dil in DIL_PATTERNS if dil > 1)


def _qkv_kernel(h_ref, gq_ref, gkv_ref, wq_ref, wk_ref, wv_ref, *refs):
    outs, scratch = refs[:-1], refs[-1]
    n = _rms(h_ref[...])
    xq = (n * gq_ref[...]).astype(BF16)
    xkv = (n * gkv_ref[...]).astype(BF16)
    gw = GROUP_WIDTH
    slot = 0
    for gi, (_, dil) in enumerate(DIL_PATTERNS):
        cols = slice(gi * gw, (gi + 1) * gw)
        for t, (x, w_ref) in enumerate(((xq, wq_ref), (xkv, wk_ref), (xkv, wv_ref))):
            res = _dot(x, w_ref[:, cols])
            out_ref = outs[3 * gi + t]
            if dil == 1:
                out_ref[...] = res.astype(BF16)
                continue
            sub = res.shape[0] // dil
            for c in range(gw // LANES):
                scratch[slot, c] = res[:, c * LANES:(c + 1) * LANES]
            for r in range(dil):
                for c in range(gw // LANES):
                    lo = r * gw + c * LANES
                    out_ref[:, lo:lo + LANES] = scratch[slot, c, pl.ds(r, sub, stride=dil), :].astype(BF16)
            slot += 1


def _qkv_proj(h, g_q, g_kv, w_q, w_kv):
    n, d = h.shape
    qw = w_q.shape[1]
    gw = GROUP_WIDTH
    scale = 1.0 / math.sqrt(DIL_HEAD_DIM)
    rows = pl.BlockSpec((PROJ_ROWS, d), lambda i: (i, 0))
    out_specs, out_shape = [], []
    for _, dil in DIL_PATTERNS:
        for _ in range(3):
            out_specs.append(pl.BlockSpec((PROJ_ROWS // dil, dil * gw), lambda i: (i, 0)))
            out_shape.append(jax.ShapeDtypeStruct((n // dil, dil * gw), BF16))
    return pl.pallas_call(
        _qkv_kernel,
        grid=(n // PROJ_ROWS,),
        in_specs=[rows, _const_spec((1, d)), _const_spec((1, d))] + [_const_spec((d, qw))] * 3,
        out_specs=out_specs,
        out_shape=out_shape,
        scratch_shapes=[pltpu.VMEM((N_STRIDED, gw // LANES, PROJ_ROWS, LANES), F32)],
        compiler_params=_cparams("arbitrary"),
        name="qkv_proj",
    )(h, g_q.reshape(1, d), g_kv.reshape(1, d), (w_q * scale).astype(BF16),
      w_kv[:, :qw].astype(BF16), w_kv[:, qw:].astype(BF16))


def _attn_kernel(q_ref, kc_ref, kp_ref, vc_ref, vp_ref, o_ref, lse_ref):
    n = pl.program_id(2)
    blk = q_ref.shape[1]
    pw = 2 * DIL_HEAD_DIM
    pairs = range(DIL_HEADS // 2)
    ri = lax.broadcasted_iota(jnp.int32, (2 * blk, blk), 0)
    ci = lax.broadcasted_iota(jnp.int32, (2 * blk, blk), 1)
    qi = jnp.where(ri >= blk, ri - blk, ri)
    prev_ok = (ci >= qi) & (n > 0)
    cur_ok = ci <= qi
    head0 = lax.broadcasted_iota(jnp.int32, (blk, pw), 1) < DIL_HEAD_DIM
    lowest = jnp.finfo(F32).min
    lanes = [slice(p * pw, (p + 1) * pw) for p in pairs]
    q = [q_ref[0, :, lanes[p]] for p in pairs]
    zero = jnp.zeros((blk, pw), BF16)
    qs = [jnp.concatenate([jnp.where(head0, q[p], zero), jnp.where(head0, zero, q[p])], axis=0) for p in pairs]
    s_p = [jnp.where(prev_ok, _dot_nt(qs[p], kp_ref[0, :, lanes[p]]), lowest) for p in pairs]
    s_c = [jnp.where(cur_ok, _dot_nt(qs[p], kc_ref[0, :, lanes[p]]), lowest) for p in pairs]
    m = [jnp.maximum(jnp.max(s_p[p], axis=-1, keepdims=True), jnp.max(s_c[p], axis=-1, keepdims=True)) for p in pairs]
    e_p = [jnp.exp(s_p[p] - m[p]) for p in pairs]
    e_c = [jnp.exp(s_c[p] - m[p]) for p in pairs]
    l = [jnp.sum(e_p[p], axis=-1, keepdims=True) + jnp.sum(e_c[p], axis=-1, keepdims=True) for p in pairs]
    pv = [_dot(e_p[p].astype(BF16), vp_ref[0, :, lanes[p]]) + _dot(e_c[p].astype(BF16), vc_ref[0, :, lanes[p]])
          for p in pairs]
    lane_t = lax.broadcasted_iota(jnp.int32, (blk, LANES), 1)
    tile = jnp.zeros((blk, LANES), F32)
    for p in pairs:
        o = pv[p] / l[p]
        o_ref[0, :, lanes[p]] = jnp.where(head0, o[:blk], o[blk:]).astype(BF16)
        lse = m[p] + jnp.log(l[p])
        tile = jnp.where(lane_t == 2 * p, lse[:blk], jnp.where(lane_t == 2 * p + 1, lse[blk:], tile))
    lse_ref[0] = tile


def _attn_group(q, k, v, gi, dilation, bsz, seq):
    gw = GROUP_WIDTH
    sub = seq // dilation
    nblk = sub // ATT_BLOCK
    view = (bsz, sub, dilation * gw)
    qv, kv_, vv = q.reshape(view), k.reshape(view), v.reshape(view)
    cur = pl.BlockSpec((1, ATT_BLOCK, gw), lambda b, r, n: (b, n, r))
    prev = pl.BlockSpec((1, ATT_BLOCK, gw), lambda b, r, n: (b, jnp.maximum(n - 1, 0), r))
    lse_spec = pl.BlockSpec((1, ATT_BLOCK, LANES), lambda b, r, n: (b, n, r))
    o, lse = pl.pallas_call(
        _attn_kernel,
        grid=(bsz, dilation, nblk),
        in_specs=[cur, cur, prev, cur, prev],
        out_specs=[cur, lse_spec],
        out_shape=[jax.ShapeDtypeStruct(view, BF16),
                   jax.ShapeDtypeStruct((bsz, sub, dilation * LANES), F32)],
        compiler_params=_cparams("arbitrary", "arbitrary", "arbitrary"),
        name=f"attn_group{gi}",
    )(qv, kv_, kv_, vv, vv)
    return o.reshape(bsz * sub, dilation * gw), lse.reshape(bsz * sub, dilation * LANES)


def _attn_merge_kernel(o0_ref, o1_ref, o2_ref, l0_ref, l1_ref, l2_ref, w_ref, res_ref, out_ref, *scratch):
    rows = res_ref.shape[0]
    gw = GROUP_WIDTH
    outs, lses = [], []
    slot = 0
    for (_, dil), o_ref, l_ref in zip(DIL_PATTERNS, (o0_ref, o1_ref, o2_ref), (l0_ref, l1_ref, l2_ref)):
        if dil == 1:
            outs.append(o_ref[...].astype(F32))
            lses.append(l_ref[...])
            continue
        o_scr, l_scr = scratch[2 * slot], scratch[2 * slot + 1]
        sub = rows // dil
        for r in range(dil):
            for c in range(gw // LANES):
                lo = r * gw + c * LANES
                o_scr[c, pl.ds(r, sub, stride=dil), :] = o_ref[:, lo:lo + LANES].astype(F32)
            l_scr[pl.ds(r, sub, stride=dil), :] = l_ref[:, r * LANES:(r + 1) * LANES]
        outs.append(jnp.concatenate([o_scr[c] for c in range(gw // LANES)], axis=1))
        lses.append(l_scr[...])
        slot += 1
    m = jnp.maximum(jnp.maximum(lses[0], lses[1]), lses[2])
    e = [jnp.exp(l - m) for l in lses]
    denom = e[0] + e[1] + e[2]
    expand = (lax.broadcasted_iota(jnp.int32, (LANES, gw), 1) // DIL_HEAD_DIM
              == lax.broadcasted_iota(jnp.int32, (LANES, gw), 0)).astype(BF16)
    mix = jnp.zeros((rows, gw), F32)
    for g in range(len(outs)):
        alpha = e[g] / denom
        hi = alpha.astype(BF16)
        lo = (alpha - hi.astype(F32)).astype(BF16)
        mix = mix + (_dot(hi, expand) + _dot(lo, expand)) * outs[g]
    out_ref[...] = res_ref[...] + _dot(mix.astype(BF16), w_ref[...])


def _attn_merge(outs, lses, w_o, res):
    n, d = res.shape
    gw = GROUP_WIDTH
    o_specs = [pl.BlockSpec((PROJ_ROWS // dil, dil * gw), lambda i: (i, 0)) for _, dil in DIL_PATTERNS]
    l_specs = [pl.BlockSpec((PROJ_ROWS // dil, dil * LANES), lambda i: (i, 0)) for _, dil in DIL_PATTERNS]
    full = pl.BlockSpec((PROJ_ROWS, d), lambda i: (i, 0))
    scratch = []
    for _, dil in DIL_PATTERNS:
        if dil > 1:
            scratch += [pltpu.VMEM((gw // LANES, PROJ_ROWS, LANES), F32), pltpu.VMEM((PROJ_ROWS, LANES), F32)]
    return pl.pallas_call(
        _attn_merge_kernel,
        grid=(n // PROJ_ROWS,),
        in_specs=o_specs + l_specs + [_const_spec(w_o.shape), full],
        out_specs=full,
        out_shape=jax.ShapeDtypeStruct((n, d), F32),
        scratch_shapes=scratch,
        compiler_params=_cparams("arbitrary"),
        name="attn_merge",
    )(*outs, *lses, w_o.astype(BF16), res)


def kernel(x, norm_mix_g, norm_ffn_g, rwkv_mu, rwkv_w_r, rwkv_w_k, rwkv_w_v, rwkv_w0, rwkv_w1, rwkv_w2,
           rwkv_a0, rwkv_a1, rwkv_a2, rwkv_g1, rwkv_g2, rwkv_k_k, rwkv_k_a, rwkv_r_k, rwkv_ln_w, rwkv_ln_b,
           rwkv_w_o, kv_norm_g, w_kv, attn_w_q, attn_w_o, moe_w_grp, moe_b_grp, moe_w_exp, moe_b_exp,
           moe_w_gate, moe_w_up, moe_w_down, final_norm_g):
    bsz, seq, d = x.shape
    n = bsz * seq
    depth = norm_mix_g.shape[0]
    n_rwkv = rwkv_mu.shape[0]
    h = x.reshape(n, d)
    q = k_sh = v_sh = None
    for layer in range(depth):
        if layer < n_rwkv:
            i = layer
            r, k, v, a, lw, g = _rwkv_pre(h.reshape(bsz, seq, d), norm_mix_g[layer], rwkv_mu[i], rwkv_w0[i], rwkv_a0[i],
                                          rwkv_w_r[i], rwkv_w_k[i], rwkv_w_v[i], rwkv_w1[i], rwkv_w2[i],
                                          rwkv_a1[i], rwkv_a2[i], rwkv_g1[i], rwkv_g2[i])
            y = _rwkv_rec(r, k, v, a, lw, g, rwkv_k_k[i], rwkv_k_a[i], rwkv_r_k[i], rwkv_ln_w[i], rwkv_ln_b[i])
            h = _proj_res(y.reshape(n, d), rwkv_w_o[i], h)
        else:
            i = layer - n_rwkv
            qkv = _qkv_proj(h, norm_mix_g[layer], kv_norm_g, attn_w_q[i], w_kv)
            q = qkv[0::3]
            if i == 0:
                k_sh, v_sh = qkv[1::3], qkv[2::3]
            outs, lses = [], []
            for gi, (window, dilation) in enumerate(DIL_PATTERNS):
                assert window // dilation == ATT_BLOCK and seq % window == 0
                o, lse = _attn_group(q[gi], k_sh[gi], v_sh[gi], gi, dilation, bsz, seq)
                outs.append(o)
                lses.append(lse)
            h = _attn_merge(outs, lses, attn_w_o[i], h)
        last = layer == depth - 1
        h = _hier_moe_residual(h, norm_ffn_g[layer], moe_w_grp[layer], moe_b_grp[layer], moe_w_exp[layer],
                               moe_b_exp[layer], moe_w_gate[layer], moe_w_up[layer], moe_w_down[layer],
                               final_g=final_norm_g if last else None)
    return h.reshape(bsz, seq, d)
```

```python
import functools
import math

import jax
import jax.numpy as jnp
from jax import lax
from jax.experimental import pallas as pl
from jax.experimental.pallas import tpu as pltpu

F32 = jnp.float32
BF16 = jnp.bfloat16
HIGHEST = lax.Precision.HIGHEST

NORM_EPS = 1e-6
RWKV_HEAD = 64
GN_EPS = RWKV_HEAD * 1e-5
DIL_PATTERNS = ((128, 1), (512, 4), (2048, 16))
DIL_HEADS = 8
DIL_HEAD_DIM = 64
N_EXPERT_GROUPS = 4
EXPERTS_PER_GROUP = 4
LANES = 128
VMEM_LIMIT = 56 * 1024 * 1024

CHUNK = 64
REC_ROWS = 2048
PRE_ROWS = 256
PROJ_ROWS = 512
MOE_BLOCK = 1024
FFN_ROWS = 128
FFN_TILES = 4
ATT_BLOCK = 128


def _cparams(*sem):
    return pltpu.CompilerParams(dimension_semantics=sem, vmem_limit_bytes=VMEM_LIMIT)


def _dot(a, b):
    return jnp.dot(a, b, preferred_element_type=F32)


def _dot_nt(a, b):
    return lax.dot_general(a, b, (((1,), (1,)), ((), ())), preferred_element_type=F32)


def _dot_tn(a, b, precision=None):
    return lax.dot_general(a, b, (((0,), (0,)), ((), ())), preferred_element_type=F32, precision=precision)


def _rms(x):
    return x * lax.rsqrt(jnp.mean(x * x, axis=-1, keepdims=True) + NORM_EPS)


def _sigmoid(z):
    return 1.0 / (1.0 + jnp.exp(-z))


def _const_spec(shape):
    nd = len(shape)
    return pl.BlockSpec(shape, lambda *_: (0,) * nd)


def _rwkv_pre_kernel(x_ref, gmix_ref, mu_ref, w0_ref, a0_ref, wr_ref, wk_ref, wv_ref,
                     w1_ref, w2_ref, a1_ref, a2_ref, g1_ref, g2_ref,
                     r_ref, k_ref, v_ref, a_ref, lw_ref, g_ref, prev_ref):
    @pl.when(pl.program_id(1) == 0)
    def _():
        prev_ref[...] = jnp.zeros_like(prev_ref)

    x = x_ref[0]
    rows = x.shape[0]
    xn = _rms(x) * gmix_ref[...]
    row = lax.broadcasted_iota(jnp.int32, xn.shape, 0)
    shifted = jnp.where(row == 0, prev_ref[7:8, :], pltpu.roll(xn, 1, 0))
    prev_ref[...] = xn[rows - 8:, :]
    xx = shifted - xn

    def mix(i):
        return (xn + xx * mu_ref[i:i + 1, :]).astype(BF16)

    r_ref[0] = _dot(mix(0), wr_ref[...]).astype(BF16)
    k_ref[0] = _dot(mix(2), wk_ref[...]).astype(BF16)
    v_ref[0] = _dot(mix(3), wv_ref[...]).astype(BF16)
    u = w0_ref[...] + _dot(jnp.tanh(_dot(mix(1), w1_ref[...])).astype(BF16), w2_ref[...])
    w_log = -(jnp.maximum(-u, 0.0) + jnp.log(1.0 + jnp.exp(-jnp.abs(u)))) - 0.5
    lw_ref[0] = -jnp.exp(w_log)
    a_ref[0] = _sigmoid(a0_ref[...] + _dot(_dot(mix(4), a1_ref[...]).astype(BF16), a2_ref[...])).astype(BF16)
    g_ref[0] = _dot(_sigmoid(_dot(mix(5), g1_ref[...])).astype(BF16), g2_ref[...]).astype(BF16)


def _pad_cols(w, n):
    return jnp.pad(w, ((0, 0), (0, n - w.shape[1])))


def _pad_rows(w, n):
    return jnp.pad(w, ((0, n - w.shape[0]), (0, 0)))


def _rwkv_pre(x, gmix, mu, w0, a0, w_r, w_k, w_v, w1, w2, a1, a2, g1, g2):
    bsz, seq, d = x.shape
    lw = -(-w1.shape[1] // LANES) * LANES
    la = -(-a1.shape[1] // LANES) * LANES
    lg = -(-g1.shape[1] // LANES) * LANES
    mu8 = jnp.pad(mu, ((0, 8 - mu.shape[0]), (0, 0)))
    args = (x, gmix.reshape(1, d), mu8, w0.reshape(1, d), a0.reshape(1, d),
            w_r.astype(BF16), w_k.astype(BF16), w_v.astype(BF16),
            _pad_cols(w1, lw).astype(BF16), _pad_rows(w2, lw).astype(BF16),
            _pad_cols(a1, la).astype(BF16), _pad_rows(a2, la).astype(BF16),
            _pad_cols(g1, lg).astype(BF16), _pad_rows(g2, lg).astype(BF16))
    tile = pl.BlockSpec((1, PRE_ROWS, d), lambda b, s: (b, s, 0))
    in_specs = [tile] + [_const_spec(a.shape) for a in args[1:]]
    out_dt = (BF16, BF16, BF16, BF16, F32, BF16)
    return pl.pallas_call(
        _rwkv_pre_kernel,
        grid=(bsz, seq // PRE_ROWS),
        in_specs=in_specs,
        out_specs=[tile] * 6,
        out_shape=[jax.ShapeDtypeStruct((bsz, seq, d), t) for t in out_dt],
        scratch_shapes=[pltpu.VMEM((8, d), F32)],
        compiler_params=_cparams("arbitrary", "arbitrary"),
        name="rwkv_pre",
    )(*args)


def _rwkv_rec_kernel(r_ref, k_ref, v_ref, a_ref, lw_ref, g_ref, kk_ref, ka_ref, rk_ref, lnw_ref, lnb_ref,
                     o_ref, state_ref):
    @pl.when(pl.program_id(2) == 0)
    def _():
        state_ref[...] = jnp.zeros_like(state_ref)

    c = CHUNK
    hd = RWKV_HEAD
    w = 2 * hd
    assert c == hd
    n = r_ref.shape[1] // c
    lane = lax.broadcasted_iota(jnp.int32, (c, w), 1)
    row = lax.broadcasted_iota(jnp.int32, (c, w), 0)
    head0 = lane < hd
    col = jnp.where(head0, lane, lane - hd)
    strict = row > col
    incl = row >= col
    eye = (row == col).astype(F32)
    tri = (lax.broadcasted_iota(jnp.int32, (c, c), 0) >= lax.broadcasted_iota(jnp.int32, (c, c), 1)).astype(BF16)
    kkp, kap, rkp, lnw, lnb = kk_ref[...], ka_ref[...], rk_ref[...], lnw_ref[...], lnb_ref[...]

    def head_sum(x):
        s0 = jnp.sum(jnp.where(head0, x, 0.0), axis=-1, keepdims=True)
        s1 = jnp.sum(jnp.where(head0, 0.0, x), axis=-1, keepdims=True)
        return jnp.where(head0, s0, s1)

    def stack(x):
        return jnp.concatenate([jnp.where(head0, x, 0.0), jnp.where(head0, 0.0, x)], axis=0)

    def cumsum(x):
        hi = x.astype(BF16)
        lo = (x - hi.astype(F32)).astype(BF16)
        both = _dot(tri, jnp.concatenate([hi, lo], axis=1))
        return both[:, :w] + both[:, w:]

    zeros = jnp.zeros((c, w), BF16)
    carried = [state_ref[...]]

    def chunk(j):
        rs = slice(j * c, (j + 1) * c)
        r_ = r_ref[0, rs, :].astype(F32)
        k_ = k_ref[0, rs, :].astype(F32)
        v_ = v_ref[0, rs, :].astype(F32)
        a_ = a_ref[0, rs, :].astype(F32)
        lw_ = lw_ref[0, rs, :]
        cum = cumsum(lw_)
        yield
        kk = k_ * kkp
        kk = kk * lax.rsqrt(jnp.maximum(head_sum(kk * kk), 1e-24))
        k2 = k_ * (1.0 + (a_ - 1.0) * kap)
        p_ = -(kk * a_)
        tot = cum[c - 1:c, :]
        e_neg = jnp.exp(-cum)
        e_tot = jnp.exp(tot - cum)
        rt = r_ * jnp.exp(cum)
        qt = kk * jnp.exp(cum - lw_)
        vb = v_.astype(BF16)
        amat = _dot_nt(jnp.concatenate([qt, rt], axis=0).astype(BF16),
                       jnp.concatenate([stack(k2 * e_neg), stack(p_ * e_neg)], axis=0).astype(BF16))
        yield
        a_qp = jnp.where(strict, amat[:c, w:], 0.0)
        a_rp = jnp.where(incl, amat[c:, w:], 0.0).astype(BF16)
        a_k = jnp.concatenate([jnp.where(strict, amat[:c, :w], 0.0),
                               jnp.where(incl, amat[c:, :w], 0.0)], axis=0).astype(BF16)
        av = _dot(a_k, stack(vb))
        inv = eye + a_qp
        power = a_qp.astype(BF16)
        power = _dot(power, stack(power))
        yield
        steps = int(math.log2(c)) - 1
        for i in range(steps):
            pb = power.astype(BF16)
            if i + 1 < steps:
                both = _dot(jnp.concatenate([inv.astype(BF16), pb], axis=0), stack(pb))
                inv = inv + both[:c]
                power = both[c:]
            else:
                inv = inv + _dot(inv.astype(BF16), stack(pb))
            yield
        wu = _dot(inv.astype(BF16),
                  jnp.concatenate([stack(qt.astype(BF16)), stack(av[:c].astype(BF16))], axis=1))
        wub = wu.astype(BF16)
        yield
        ry = _dot(a_rp, jnp.concatenate([stack(wub[:, :w]), stack(wub[:, w:])], axis=1))
        mg = _dot_tn(jnp.concatenate([p_ * e_tot, k2 * e_tot], axis=0).astype(BF16),
                     jnp.concatenate([wub, jnp.concatenate([zeros, vb], axis=1)], axis=0))
        yield
        rw = (rt + ry[:, :w]).astype(BF16)
        m_w = jnp.where(head0, mg[:c, :w], mg[c:, :w]).astype(BF16)
        g_w = jnp.where(head0, mg[:c, w:], mg[c:, w:])
        gam = head_sum(eye * jnp.exp(tot))
        state = carried[0]
        both = _dot(jnp.concatenate([rw, m_w], axis=0), stack(state.astype(BF16)))
        y = both[:c] + av[c:] + ry[:, w:]
        carried[0] = gam * state + both[c:] + g_w
        yield
        yc = y - head_sum(y) * (1.0 / hd)
        var = head_sum(yc * yc) * (1.0 / hd)
        yn = yc * lax.rsqrt(var + GN_EPS) * lnw + lnb
        bonus = head_sum(r_ * k2 * rkp) * v_
        o_ref[0, rs, :] = ((yn + bonus) * g_ref[0, rs, :].astype(F32)).astype(BF16)

    live, started = [], 0
    while started < n or live:
        if started < n:
            live.append(chunk(started))
            started += 1
        for gen in list(live):
            if next(gen, "done") == "done":
                live.remove(gen)
    state_ref[...] = carried[0]


def _rwkv_rec(r, k, v, a, lw, g, k_k, k_a, r_k, ln_w, ln_b):
    bsz, seq, d = r.shape
    hw = 2 * RWKV_HEAD
    tile = pl.BlockSpec((1, REC_ROWS, hw), lambda b, h, s: (b, s, h))
    par = pl.BlockSpec((1, hw), lambda b, h, s: (0, h))
    params = [p.reshape(1, d).astype(F32) for p in (k_k, k_a, r_k, ln_w, ln_b)]
    return pl.pallas_call(
        _rwkv_rec_kernel,
        grid=(bsz, d // hw, seq // REC_ROWS),
        in_specs=[tile] * 6 + [par] * 5,
        out_specs=tile,
        out_shape=jax.ShapeDtypeStruct((bsz, seq, d), BF16),
        scratch_shapes=[pltpu.VMEM((CHUNK, hw), F32)],
        compiler_params=_cparams("arbitrary", "arbitrary", "arbitrary"),
        name="rwkv_rec",
    )(r, k, v, a, lw, g, *params)


def _proj_res_kernel(a_ref, w_ref, res_ref, o_ref):
    o_ref[...] = res_ref[...] + _dot(a_ref[...], w_ref[...])


def _proj_res(a, w, res):
    n, kdim = a.shape
    d = w.shape[1]
    return pl.pallas_call(
        _proj_res_kernel,
        grid=(n // PROJ_ROWS,),
        in_specs=[pl.BlockSpec((PROJ_ROWS, kdim), lambda i: (i, 0)), _const_spec(w.shape),
                  pl.BlockSpec((PROJ_ROWS, d), lambda i: (i, 0))],
        out_specs=pl.BlockSpec((PROJ_ROWS, d), lambda i: (i, 0)),
        out_shape=jax.ShapeDtypeStruct((n, d), F32),
        compiler_params=_cparams("arbitrary"),
        name="proj_res",
    )(a, w.astype(BF16), res)


def _moe_route_kernel(h_ref, g_ref, wr_ref, br_ref, xs_ref, gs_ref, dest_ref, cnt_ref):
    nb = h_ref.shape[0]
    nbp = xs_ref.shape[0]
    t = _rms(h_ref[...]) * g_ref[...]
    logits = jnp.dot(t, wr_ref[...], precision=HIGHEST, preferred_element_type=F32) + br_ref[...]
    lane = lax.broadcasted_iota(jnp.int32, logits.shape, 1).astype(F32)
    neg = jnp.float32(-jnp.inf)
    big = jnp.float32(LANES)
    is_grp = lane < N_EXPERT_GROUPS
    gl = jnp.where(is_grp, logits, neg)
    gmax = jnp.max(gl, axis=-1, keepdims=True)
    grp = jnp.min(jnp.where(gl == gmax, lane, big), axis=-1, keepdims=True)
    p_grp = 1.0 / jnp.sum(jnp.where(is_grp, jnp.exp(gl - gmax), 0.0), axis=-1, keepdims=True)
    e_lo = N_EXPERT_GROUPS + grp * EXPERTS_PER_GROUP
    in_grp = (lane >= e_lo) & (lane < e_lo + EXPERTS_PER_GROUP)
    el = jnp.where(in_grp, logits, neg)
    v1 = jnp.max(el, axis=-1, keepdims=True)
    i1 = jnp.min(jnp.where(el == v1, lane, big), axis=-1, keepdims=True)
    el2 = jnp.where(lane == i1, neg, el)
    v2 = jnp.max(el2, axis=-1, keepdims=True)
    i2 = jnp.min(jnp.where(el2 == v2, lane, big), axis=-1, keepdims=True)
    e2 = jnp.exp(v2 - v1)
    w1 = 1.0 / (1.0 + e2)
    w2 = e2 / (1.0 + e2)
    gates = jnp.where(lane == i1, w1, jnp.where(lane == i2, w2, 0.0)) * p_grp

    onehot = (lane == grp).astype(BF16)
    ri = lax.broadcasted_iota(jnp.int32, (nb, nb), 0)
    ci = lax.broadcasted_iota(jnp.int32, (nb, nb), 1)
    rank = _dot((ri > ci).astype(BF16), onehot)
    counts = jnp.sum(onehot.astype(F32), axis=0, keepdims=True)
    padded = jnp.ceil(counts / FFN_ROWS) * FFN_ROWS
    lane1 = lax.broadcasted_iota(jnp.int32, (1, LANES), 1)
    offs = jnp.zeros((1, LANES), F32)
    for gidx in range(1, N_EXPERT_GROUPS):
        prev = jnp.sum(jnp.where(lane1 < gidx, padded, 0.0), axis=-1, keepdims=True)
        offs = jnp.where(lane1 == gidx, prev, offs)
    dest = jnp.sum(onehot.astype(F32) * (rank + offs), axis=-1, keepdims=True)
    slot = lax.broadcasted_iota(jnp.int32, (nb, nbp), 1).astype(F32)
    perm_t = (dest == slot).astype(BF16)
    xs_ref[...] = _dot_tn(perm_t, t.astype(BF16)).astype(BF16)
    g_hi = gates.astype(BF16)
    g_lo = (gates - g_hi.astype(F32)).astype(BF16)
    moved = _dot_tn(perm_t, jnp.concatenate([g_hi, g_lo], axis=1))
    gs_ref[...] = moved[:, :LANES] + moved[:, LANES:]
    dest_ref[...] = dest
    cnt_ref[0] = jnp.where(lane1 < N_EXPERT_GROUPS, counts, 0.0)


def _moe_ffn_kernel(tile_ref, grp_ref, real_ref, *refs):
    nt = FFN_TILES
    xs_refs, gs_refs = refs[:nt], refs[nt:2 * nt]
    wg_ref, wu_ref, wd_ref, ys_ref = refs[2 * nt:]
    i = pl.program_id(0)

    @pl.when(real_ref[i] > 0)
    def _():
        x = jnp.concatenate([r[...] for r in xs_refs], axis=0)
        gates = jnp.concatenate([r[...] for r in gs_refs], axis=0)
        lane = lax.broadcasted_iota(jnp.int32, gates.shape, 1)
        base = N_EXPERT_GROUPS + grp_ref[i] * EXPERTS_PER_GROUP
        acc = jnp.zeros(ys_ref.shape, F32)
        for e in range(EXPERTS_PER_GROUP):
            ge = jnp.sum(jnp.where(lane == base + e, gates, 0.0), axis=-1, keepdims=True)
            gate_act = _dot(x, wg_ref[e])
            hdn = gate_act * _sigmoid(gate_act) * _dot(x, wu_ref[e])
            acc = acc + ge * _dot(hdn.astype(BF16), wd_ref[e])
        ys_ref[...] = acc.astype(BF16)

    @pl.when(real_ref[i] == 0)
    def _():
        ys_ref[...] = jnp.zeros_like(ys_ref)


def _moe_merge_kernel(slot_ref, h_ref, dest_ref, *refs):
    ys_refs, o_ref = refs[:-1], refs[-1]
    ys = jnp.concatenate([r[...] for r in ys_refs], axis=0)
    nb, nbp = h_ref.shape[0], ys.shape[0]
    slot = lax.broadcasted_iota(jnp.int32, (nb, nbp), 1).astype(F32)
    perm_t = (dest_ref[...] == slot).astype(BF16)
    o_ref[...] = h_ref[...] + _dot(perm_t, ys)


def _moe_merge_norm_kernel(slot_ref, h_ref, dest_ref, *refs):
    ys_refs, g_ref, o_ref = refs[:-2], refs[-2], refs[-1]
    ys = jnp.concatenate([r[...] for r in ys_refs], axis=0)
    nb, nbp = h_ref.shape[0], ys.shape[0]
    slot = lax.broadcasted_iota(jnp.int32, (nb, nbp), 1).astype(F32)
    perm_t = (dest_ref[...] == slot).astype(BF16)
    o_ref[...] = _rms(h_ref[...] + _dot(perm_t, ys)) * g_ref[...]


def _ffn_slots(n_tiles):
    return -(-(n_tiles + N_EXPERT_GROUPS * (FFN_TILES - 1)) // FFN_TILES) * FFN_TILES


def _ffn_schedule(counts, nblk, tiles_per_blk):
    i32 = jnp.int32
    tiles = (counts + FFN_ROWS - 1) // FFN_ROWS
    first_rel = jnp.cumsum(tiles, axis=1) - tiles
    used_blk = jnp.sum(tiles, axis=1)
    grp_pad = (jnp.sum(tiles, axis=0) + FFN_TILES - 1) // FFN_TILES * FFN_TILES
    grp_start = jnp.cumsum(grp_pad) - grp_pad
    before = jnp.cumsum(tiles, axis=0) - tiles
    n_tiles = nblk * tiles_per_blk
    t = jnp.arange(n_tiles, dtype=i32)
    b, k = t // tiles_per_blk, t % tiles_per_blk
    used = k < used_blk[b]
    grp_t = jnp.minimum(jnp.sum(((first_rel + tiles)[b] <= k[:, None]).astype(i32), axis=1), N_EXPERT_GROUPS - 1)
    pos = grp_start[grp_t] + before[b, grp_t] + (k - first_rel[b, grp_t])
    s = jnp.arange(_ffn_slots(n_tiles), dtype=i32)
    hit = used[None, :] & (pos[None, :] == s[:, None])
    slot_has = jnp.sum(hit.astype(i32), axis=1) > 0
    free_rank = jnp.cumsum((~slot_has).astype(i32)) - 1
    unused_rank = jnp.cumsum((~used).astype(i32)) - 1
    fill = (~used)[None, :] & (unused_rank[None, :] == free_rank[:, None])
    placed = jnp.where(slot_has[:, None], hit, fill)
    slot_tile = jnp.sum(jnp.where(placed, t[None, :], 0), axis=1).astype(i32)
    tile_slot = jnp.sum(jnp.where(placed, s[:, None], 0), axis=0).astype(i32)
    slot_grp = jnp.sum(jnp.where(hit, grp_t[None, :], 0), axis=1).reshape(-1, FFN_TILES)
    step_real = jnp.sum(slot_has.astype(i32).reshape(-1, FFN_TILES), axis=1).astype(i32)
    n_real_steps = jnp.sum(grp_pad) // FFN_TILES
    last_grp = slot_grp[jnp.maximum(n_real_steps - 1, 0), 0]
    step_grp = jnp.where(step_real > 0, slot_grp[:, 0], last_grp).astype(i32)
    return slot_tile, tile_slot, step_grp, step_real


def _hier_moe_residual(h, norm_g, w_grp, b_grp, w_exp, b_exp, w_gate, w_up, w_down, final_g=None):
    n, d = h.shape
    nb = MOE_BLOCK
    nblk = n // nb
    nbp = nb + N_EXPERT_GROUPS * FFN_ROWS
    tiles_per_blk = nbp // FFN_ROWS
    n_exp = N_EXPERT_GROUPS * EXPERTS_PER_GROUP
    w_router = jnp.pad(jnp.concatenate([w_grp, w_exp], axis=1), ((0, 0), (0, LANES - N_EXPERT_GROUPS - n_exp)))
    b_router = jnp.pad(jnp.concatenate([b_grp, b_exp]), (0, LANES - N_EXPERT_GROUPS - n_exp)).reshape(1, LANES)

    xs, gs, dest, counts = pl.pallas_call(
        _moe_route_kernel,
        grid=(nblk,),
        in_specs=[pl.BlockSpec((nb, d), lambda i: (i, 0)), _const_spec((1, d)),
                  _const_spec((d, LANES)), _const_spec((1, LANES))],
        out_specs=[pl.BlockSpec((nbp, d), lambda i: (i, 0)), pl.BlockSpec((nbp, LANES), lambda i: (i, 0)),
                   pl.BlockSpec((nb, 1), lambda i: (i, 0)), pl.BlockSpec((1, 1, LANES), lambda i: (i, 0, 0))],
        out_shape=[jax.ShapeDtypeStruct((nblk * nbp, d), BF16), jax.ShapeDtypeStruct((nblk * nbp, LANES), F32),
                   jax.ShapeDtypeStruct((n, 1), F32), jax.ShapeDtypeStruct((nblk, 1, LANES), F32)],
        compiler_params=_cparams("arbitrary"),
        name="moe_route",
    )(h, norm_g.reshape(1, d), w_router, b_router)

    n_tiles = nblk * tiles_per_blk
    n_slots = _ffn_slots(n_tiles)
    slot_tile, tile_slot, step_grp, step_real = _ffn_schedule(
        counts[:, 0, :N_EXPERT_GROUPS].astype(jnp.int32), nblk, tiles_per_blk)

    def tile_spec(width, j):
        return pl.BlockSpec((FFN_ROWS, width), lambda i, tile, grp, real: (tile[i * FFN_TILES + j], 0))

    def wspec(shape):
        return pl.BlockSpec((EXPERTS_PER_GROUP,) + shape, lambda i, tile, grp, real: (grp[i], 0, 0))

    f = w_gate.shape[-1]
    ys = pl.pallas_call(
        _moe_ffn_kernel,
        grid_spec=pltpu.PrefetchScalarGridSpec(
            num_scalar_prefetch=3,
            grid=(n_slots // FFN_TILES,),
            in_specs=[tile_spec(d, j) for j in range(FFN_TILES)] + [tile_spec(LANES, j) for j in range(FFN_TILES)]
            + [wspec((d, f)), wspec((d, f)), wspec((f, d))],
            out_specs=pl.BlockSpec((FFN_TILES * FFN_ROWS, d), lambda i, tile, grp, real: (i, 0)),
        ),
        out_shape=jax.ShapeDtypeStruct((n_slots * FFN_ROWS, d), BF16),
        compiler_params=_cparams("arbitrary"),
        name="moe_ffn",
    )(slot_tile, step_grp, step_real, *([xs] * FFN_TILES), *([gs] * FFN_TILES),
      w_gate.astype(BF16), w_up.astype(BF16), w_down.astype(BF16))

    blk = pl.BlockSpec((nb, d), lambda i, slot: (i, 0))
    in_specs = [blk, pl.BlockSpec((nb, 1), lambda i, slot: (i, 0))]
    in_specs += [pl.BlockSpec((FFN_ROWS, d), functools.partial(lambda j, i, slot: (slot[i * tiles_per_blk + j], 0), j))
                 for j in range(tiles_per_blk)]
    args = [h, dest] + [ys] * tiles_per_blk
    body = _moe_merge_kernel
    if final_g is not None:
        in_specs.append(pl.BlockSpec((1, d), lambda i, slot: (0, 0)))
        args.append(final_g.reshape(1, d))
        body = _moe_merge_norm_kernel
    return pl.pallas_call(
        body,
        grid_spec=pltpu.PrefetchScalarGridSpec(num_scalar_prefetch=1, grid=(nblk,), in_specs=in_specs, out_specs=blk),
        out_shape=jax.ShapeDtypeStruct((n, d), F32),
        compiler_params=_cparams("arbitrary"),
        name="moe_merge",
    )(tile_slot, *args)


GROUP_WIDTH = DIL_HEADS * DIL_HEAD_DIM
N_STRIDED = 3 * sum(1 for _, dil in DIL_PATTERNS if dil > 1)


def _qkv_kernel(h_ref, gq_ref, gkv_ref, wq_ref, wk_ref, wv_ref, *refs):
    outs, scratch = refs[:-1], refs[-1]
    n = _rms(h_ref[...])
    xq = (n * gq_ref[...]).astype(BF16)
    xkv = (n * gkv_ref[...]).astype(BF16)
    gw = GROUP_WIDTH
    slot = 0
    for gi, (_, dil) in enumerate(DIL_PATTERNS):
        cols = slice(gi * gw, (gi + 1) * gw)
        for t, (x, w_ref) in enumerate(((xq, wq_ref), (xkv, wk_ref), (xkv, wv_ref))):
            res = _dot(x, w_ref[:, cols])
            out_ref = outs[3 * gi + t]
            if dil == 1:
                out_ref[...] = res.astype(BF16)
                continue
            sub = res.shape[0] // dil
            for c in range(gw // LANES):
                scratch[slot, c] = res[:, c * LANES:(c + 1) * LANES]
            for r in range(dil):
                for c in range(gw // LANES):
                    lo = r * gw + c * LANES
                    out_ref[:, lo:lo + LANES] = scratch[slot, c, pl.ds(r, sub, stride=dil), :].astype(BF16)
            slot += 1


def _qkv_proj(h, g_q, g_kv, w_q, w_kv):
    n, d = h.shape
    qw = w_q.shape[1]
    gw = GROUP_WIDTH
    scale = 1.0 / math.sqrt(DIL_HEAD_DIM)
    rows = pl.BlockSpec((PROJ_ROWS, d), lambda i: (i, 0))
    out_specs, out_shape = [], []
    for _, dil in DIL_PATTERNS:
        for _ in range(3):
            out_specs.append(pl.BlockSpec((PROJ_ROWS // dil, dil * gw), lambda i: (i, 0)))
            out_shape.append(jax.ShapeDtypeStruct((n // dil, dil * gw), BF16))
    return pl.pallas_call(
        _qkv_kernel,
        grid=(n // PROJ_ROWS,),
        in_specs=[rows, _const_spec((1, d)), _const_spec((1, d))] + [_const_spec((d, qw))] * 3,
        out_specs=out_specs,
        out_shape=out_shape,
        scratch_shapes=[pltpu.VMEM((N_STRIDED, gw // LANES, PROJ_ROWS, LANES), F32)],
        compiler_params=_cparams("arbitrary"),
        name="qkv_proj",
    )(h, g_q.reshape(1, d), g_kv.reshape(1, d), (w_q * scale).astype(BF16),
      w_kv[:, :qw].astype(BF16), w_kv[:, qw:].astype(BF16))


def _attn_kernel(q_ref, kc_ref, kp_ref, vc_ref, vp_ref, o_ref, lse_ref):
    n = pl.program_id(2)
    blk = q_ref.shape[1]
    pw = 2 * DIL_HEAD_DIM
    pairs = range(DIL_HEADS // 2)
    ri = lax.broadcasted_iota(jnp.int32, (2 * blk, blk), 0)
    ci = lax.broadcasted_iota(jnp.int32, (2 * blk, blk), 1)
    qi = jnp.where(ri >= blk, ri - blk, ri)
    prev_ok = (ci >= qi) & (n > 0)
    cur_ok = ci <= qi
    head0 = lax.broadcasted_iota(jnp.int32, (blk, pw), 1) < DIL_HEAD_DIM
    lowest = jnp.finfo(F32).min
    lanes = [slice(p * pw, (p + 1) * pw) for p in pairs]
    q = [q_ref[0, :, lanes[p]] for p in pairs]
    zero = jnp.zeros((blk, pw), BF16)
    qs = [jnp.concatenate([jnp.where(head0, q[p], zero), jnp.where(head0, zero, q[p])], axis=0) for p in pairs]
    s_p = [jnp.where(prev_ok, _dot_nt(qs[p], kp_ref[0, :, lanes[p]]), lowest) for p in pairs]
    s_c = [jnp.where(cur_ok, _dot_nt(qs[p], kc_ref[0, :, lanes[p]]), lowest) for p in pairs]
    m = [jnp.maximum(jnp.max(s_p[p], axis=-1, keepdims=True), jnp.max(s_c[p], axis=-1, keepdims=True)) for p in pairs]
    e_p = [jnp.exp(s_p[p] - m[p]) for p in pairs]
    e_c = [jnp.exp(s_c[p] - m[p]) for p in pairs]
    l = [jnp.sum(e_p[p], axis=-1, keepdims=True) + jnp.sum(e_c[p], axis=-1, keepdims=True) for p in pairs]
    pv = [_dot(e_p[p].astype(BF16), vp_ref[0, :, lanes[p]]) + _dot(e_c[p].astype(BF16), vc_ref[0, :, lanes[p]])
          for p in pairs]
    lane_t = lax.broadcasted_iota(jnp.int32, (blk, LANES), 1)
    tile = jnp.zeros((blk, LANES), F32)
    for p in pairs:
        o = pv[p] / l[p]
        o_ref[0, :, lanes[p]] = jnp.where(head0, o[:blk], o[blk:]).astype(BF16)
        lse = m[p] + jnp.log(l[p])
        tile = jnp.where(lane_t == 2 * p, lse[:blk], jnp.where(lane_t == 2 * p + 1, lse[blk:], tile))
    lse_ref[0] = tile


def _attn_group(q, k, v, gi, dilation, bsz, seq):
    gw = GROUP_WIDTH
    sub = seq // dilation
    nblk = sub // ATT_BLOCK
    view = (bsz, sub, dilation * gw)
    qv, kv_, vv = q.reshape(view), k.reshape(view), v.reshape(view)
    cur = pl.BlockSpec((1, ATT_BLOCK, gw), lambda b, r, n: (b, n, r))
    prev = pl.BlockSpec((1, ATT_BLOCK, gw), lambda b, r, n: (b, jnp.maximum(n - 1, 0), r))
    lse_spec = pl.BlockSpec((1, ATT_BLOCK, LANES), lambda b, r, n: (b, n, r))
    o, lse = pl.pallas_call(
        _attn_kernel,
        grid=(bsz, dilation, nblk),
        in_specs=[cur, cur, prev, cur, prev],
        out_specs=[cur, lse_spec],
        out_shape=[jax.ShapeDtypeStruct(view, BF16),
                   jax.ShapeDtypeStruct((bsz, sub, dilation * LANES), F32)],
        compiler_params=_cparams("arbitrary", "arbitrary", "arbitrary"),
        name=f"attn_group{gi}",
    )(qv, kv_, kv_, vv, vv)
    return o.reshape(bsz * sub, dilation * gw), lse.reshape(bsz * sub, dilation * LANES)


def _attn_merge_kernel(o0_ref, o1_ref, o2_ref, l0_ref, l1_ref, l2_ref, w_ref, res_ref, out_ref, *scratch):
    rows = res_ref.shape[0]
    gw = GROUP_WIDTH
    outs, lses = [], []
    slot = 0
    for (_, dil), o_ref, l_ref in zip(DIL_PATTERNS, (o0_ref, o1_ref, o2_ref), (l0_ref, l1_ref, l2_ref)):
        if dil == 1:
            outs.append(o_ref[...].astype(F32))
            lses.append(l_ref[...])
            continue
        o_scr, l_scr = scratch[2 * slot], scratch[2 * slot + 1]
        sub = rows // dil
        for r in range(dil):
            for c in range(gw // LANES):
                lo = r * gw + c * LANES
                o_scr[c, pl.ds(r, sub, stride=dil), :] = o_ref[:, lo:lo + LANES].astype(F32)
            l_scr[pl.ds(r, sub, stride=dil), :] = l_ref[:, r * LANES:(r + 1) * LANES]
        outs.append(jnp.concatenate([o_scr[c] for c in range(gw // LANES)], axis=1))
        lses.append(l_scr[...])
        slot += 1
    m = jnp.maximum(jnp.maximum(lses[0], lses[1]), lses[2])
    e = [jnp.exp(l - m) for l in lses]
    denom = e[0] + e[1] + e[2]
    expand = (lax.broadcasted_iota(jnp.int32, (LANES, gw), 1) // DIL_HEAD_DIM
              == lax.broadcasted_iota(jnp.int32, (LANES, gw), 0)).astype(BF16)
    mix = jnp.zeros((rows, gw), F32)
    for g in range(len(outs)):
        alpha = e[g] / denom
        hi = alpha.astype(BF16)
        lo = (alpha - hi.astype(F32)).astype(BF16)
        mix = mix + (_dot(hi, expand) + _dot(lo, expand)) * outs[g]
    out_ref[...] = res_ref[...] + _dot(mix.astype(BF16), w_ref[...])


def _attn_merge(outs, lses, w_o, res):
    n, d = res.shape
    gw = GROUP_WIDTH
    o_specs = [pl.BlockSpec((PROJ_ROWS // dil, dil * gw), lambda i: (i, 0)) for _, dil in DIL_PATTERNS]
    l_specs = [pl.BlockSpec((PROJ_ROWS // dil, dil * LANES), lambda i: (i, 0)) for _, dil in DIL_PATTERNS]
    full = pl.BlockSpec((PROJ_ROWS, d), lambda i: (i, 0))
    scratch = []
    for _, dil in DIL_PATTERNS:
        if dil > 1:
            scratch += [pltpu.VMEM((gw // LANES, PROJ_ROWS, LANES), F32), pltpu.VMEM((PROJ_ROWS, LANES), F32)]
    return pl.pallas_call(
        _attn_merge_kernel,
        grid=(n // PROJ_ROWS,),
        in_specs=o_specs + l_specs + [_const_spec(w_o.shape), full],
        out_specs=full,
        out_shape=jax.ShapeDtypeStruct((n, d), F32),
        scratch_shapes=scratch,
        compiler_params=_cparams("arbitrary"),
        name="attn_merge",
    )(*outs, *lses, w_o.astype(BF16), res)


def kernel(x, norm_mix_g, norm_ffn_g, rwkv_mu, rwkv_w_r, rwkv_w_k, rwkv_w_v, rwkv_w0, rwkv_w1, rwkv_w2,
           rwkv_a0, rwkv_a1, rwkv_a2, rwkv_g1, rwkv_g2, rwkv_k_k, rwkv_k_a, rwkv_r_k, rwkv_ln_w, rwkv_ln_b,
           rwkv_w_o, kv_norm_g, w_kv, attn_w_q, attn_w_o, moe_w_grp, moe_b_grp, moe_w_exp, moe_b_exp,
           moe_w_gate, moe_w_up, moe_w_down, final_norm_g):
    bsz, seq, d = x.shape
    n = bsz * seq
    depth = norm_mix_g.shape[0]
    n_rwkv = rwkv_mu.shape[0]
    h = x.reshape(n, d)
    q = k_sh = v_sh = None
    for layer in range(depth):
        if layer < n_rwkv:
            i = layer
            r, k, v, a, lw, g = _rwkv_pre(h.reshape(bsz, seq, d), norm_mix_g[layer], rwkv_mu[i], rwkv_w0[i], rwkv_a0[i],
                                          rwkv_w_r[i], rwkv_w_k[i], rwkv_w_v[i], rwkv_w1[i], rwkv_w2[i],
                                          rwkv_a1[i], rwkv_a2[i], rwkv_g1[i], rwkv_g2[i])
            y = _rwkv_rec(r, k, v, a, lw, g, rwkv_k_k[i], rwkv_k_a[i], rwkv_r_k[i], rwkv_ln_w[i], rwkv_ln_b[i])
            h = _proj_res(y.reshape(n, d), rwkv_w_o[i], h)
        else:
            i = layer - n_rwkv
            qkv = _qkv_proj(h, norm_mix_g[layer], kv_norm_g, attn_w_q[i], w_kv)
            q = qkv[0::3]
            if i == 0:
                k_sh, v_sh = qkv[1::3], qkv[2::3]
            outs, lses = [], []
            for gi, (window, dilation) in enumerate(DIL_PATTERNS):
                assert window // dilation == ATT_BLOCK and seq % window == 0
                o, lse = _attn_group(q[gi], k_sh[gi], v_sh[gi], gi, dilation, bsz, seq)
                outs.append(o)
                lses.append(lse)
            h = _attn_merge(outs, lses, attn_w_o[i], h)
        last = layer == depth - 1
        h = _hier_moe_residual(h, norm_ffn_g[layer], moe_w_grp[layer], moe_b_grp[layer], moe_w_exp[layer],
                               moe_b_exp[layer], moe_w_gate[layer], moe_w_up[layer], moe_w_down[layer],
                               final_g=final_norm_g if last else None)
    return h.reshape(bsz, seq, d)
```

```python
import functools
import math

import jax
import jax.numpy as jnp
from jax import lax
from jax.experimental import pallas as pl
from jax.experimental.pallas import tpu as pltpu

F32 = jnp.float32
BF16 = jnp.bfloat16
HIGHEST = lax.Precision.HIGHEST

NORM_EPS = 1e-6
RWKV_HEAD = 64
GN_EPS = RWKV_HEAD * 1e-5
DIL_PATTERNS = ((128, 1), (512, 4), (2048, 16))
DIL_HEADS = 8
DIL_HEAD_DIM = 64
N_EXPERT_GROUPS = 4
EXPERTS_PER_GROUP = 4
LANES = 128
VMEM_LIMIT = 56 * 1024 * 1024

CHUNK = 64
REC_ROWS = 2048
PRE_ROWS = 256
PROJ_ROWS = 512
MOE_BLOCK = 1024
FFN_ROWS = 128
FFN_TILES = 4
ATT_BLOCK = 128
ATT_WINDOWS = 4


def _cparams(*sem):
    return pltpu.CompilerParams(dimension_semantics=sem, vmem_limit_bytes=VMEM_LIMIT)


def _dot(a, b):
    return jnp.dot(a, b, preferred_element_type=F32)


def _dot_nt(a, b):
    return lax.dot_general(a, b, (((1,), (1,)), ((), ())), preferred_element_type=F32)


def _dot_tn(a, b, precision=None):
    return lax.dot_general(a, b, (((0,), (0,)), ((), ())), preferred_element_type=F32, precision=precision)


def _rms(x):
    return x * lax.rsqrt(jnp.mean(x * x, axis=-1, keepdims=True) + NORM_EPS)


def _sigmoid(z):
    return 1.0 / (1.0 + jnp.exp(-z))


def _const_spec(shape):
    nd = len(shape)
    return pl.BlockSpec(shape, lambda *_: (0,) * nd)


def _rwkv_pre_kernel(x_ref, gmix_ref, mu_ref, w0_ref, a0_ref, wr_ref, wk_ref, wv_ref,
                     w1_ref, w2_ref, a1_ref, a2_ref, g1_ref, g2_ref,
                     r_ref, k_ref, v_ref, a_ref, lw_ref, g_ref, prev_ref):
    @pl.when(pl.program_id(1) == 0)
    def _():
        prev_ref[...] = jnp.zeros_like(prev_ref)

    x = x_ref[0]
    rows = x.shape[0]
    xn = _rms(x) * gmix_ref[...]
    row = lax.broadcasted_iota(jnp.int32, xn.shape, 0)
    shifted = jnp.where(row == 0, prev_ref[7:8, :], pltpu.roll(xn, 1, 0))
    prev_ref[...] = xn[rows - 8:, :]
    xx = shifted - xn

    def mix(i):
        return (xn + xx * mu_ref[i:i + 1, :]).astype(BF16)

    r_ref[0] = _dot(mix(0), wr_ref[...]).astype(BF16)
    k_ref[0] = _dot(mix(2), wk_ref[...]).astype(BF16)
    v_ref[0] = _dot(mix(3), wv_ref[...]).astype(BF16)
    u = w0_ref[...] + _dot(jnp.tanh(_dot(mix(1), w1_ref[...])).astype(BF16), w2_ref[...])
    w_log = -(jnp.maximum(-u, 0.0) + jnp.log(1.0 + jnp.exp(-jnp.abs(u)))) - 0.5
    lw_ref[0] = -jnp.exp(w_log)
    a_ref[0] = _sigmoid(a0_ref[...] + _dot(_dot(mix(4), a1_ref[...]).astype(BF16), a2_ref[...])).astype(BF16)
    g_ref[0] = _dot(_sigmoid(_dot(mix(5), g1_ref[...])).astype(BF16), g2_ref[...]).astype(BF16)


def _pad_cols(w, n):
    return jnp.pad(w, ((0, 0), (0, n - w.shape[1])))


def _pad_rows(w, n):
    return jnp.pad(w, ((0, n - w.shape[0]), (0, 0)))


def _rwkv_pre(x, gmix, mu, w0, a0, w_r, w_k, w_v, w1, w2, a1, a2, g1, g2):
    bsz, seq, d = x.shape
    lw = -(-w1.shape[1] // LANES) * LANES
    la = -(-a1.shape[1] // LANES) * LANES
    lg = -(-g1.shape[1] // LANES) * LANES
    mu8 = jnp.pad(mu, ((0, 8 - mu.shape[0]), (0, 0)))
    args = (x, gmix.reshape(1, d), mu8, w0.reshape(1, d), a0.reshape(1, d),
            w_r.astype(BF16), w_k.astype(BF16), w_v.astype(BF16),
            _pad_cols(w1, lw).astype(BF16), _pad_rows(w2, lw).astype(BF16),
            _pad_cols(a1, la).astype(BF16), _pad_rows(a2, la).astype(BF16),
            _pad_cols(g1, lg).astype(BF16), _pad_rows(g2, lg).astype(BF16))
    tile = pl.BlockSpec((1, PRE_ROWS, d), lambda b, s: (b, s, 0))
    in_specs = [tile] + [_const_spec(a.shape) for a in args[1:]]
    out_dt = (BF16, BF16, BF16, BF16, F32, BF16)
    return pl.pallas_call(
        _rwkv_pre_kernel,
        grid=(bsz, seq // PRE_ROWS),
        in_specs=in_specs,
        out_specs=[tile] * 6,
        out_shape=[jax.ShapeDtypeStruct((bsz, seq, d), t) for t in out_dt],
        scratch_shapes=[pltpu.VMEM((8, d), F32)],
        compiler_params=_cparams("arbitrary", "arbitrary"),
        name="rwkv_pre",
    )(*args)


def _rwkv_rec_kernel(r_ref, k_ref, v_ref, a_ref, lw_ref, g_ref, kk_ref, ka_ref, rk_ref, lnw_ref, lnb_ref,
                     o_ref, state_ref):
    @pl.when(pl.program_id(2) == 0)
    def _():
        state_ref[...] = jnp.zeros_like(state_ref)

    c = CHUNK
    hd = RWKV_HEAD
    w = 2 * hd
    assert c == hd
    n = r_ref.shape[1] // c
    lane = lax.broadcasted_iota(jnp.int32, (c, w), 1)
    row = lax.broadcasted_iota(jnp.int32, (c, w), 0)
    head0 = lane < hd
    col = jnp.where(head0, lane, lane - hd)
    strict = row > col
    incl = row >= col
    eye = (row == col).astype(F32)
    tri = (lax.broadcasted_iota(jnp.int32, (c, c), 0) >= lax.broadcasted_iota(jnp.int32, (c, c), 1)).astype(BF16)
    kkp, kap, rkp, lnw, lnb = kk_ref[...], ka_ref[...], rk_ref[...], lnw_ref[...], lnb_ref[...]

    def head_sum(x):
        s0 = jnp.sum(jnp.where(head0, x, 0.0), axis=-1, keepdims=True)
        s1 = jnp.sum(jnp.where(head0, 0.0, x), axis=-1, keepdims=True)
        return jnp.where(head0, s0, s1)

    def stack(x):
        return jnp.concatenate([jnp.where(head0, x, 0.0), jnp.where(head0, 0.0, x)], axis=0)

    def cumsum(x):
        hi = x.astype(BF16)
        lo = (x - hi.astype(F32)).astype(BF16)
        both = _dot(tri, jnp.concatenate([hi, lo], axis=1))
        return both[:, :w] + both[:, w:]

    zeros = jnp.zeros((c, w), BF16)
    carried = [state_ref[...]]

    def chunk(j):
        rs = slice(j * c, (j + 1) * c)
        r_ = r_ref[0, rs, :].astype(F32)
        k_ = k_ref[0, rs, :].astype(F32)
        v_ = v_ref[0, rs, :].astype(F32)
        a_ = a_ref[0, rs, :].astype(F32)
        lw_ = lw_ref[0, rs, :]
        cum = cumsum(lw_)
        yield
        kk = k_ * kkp
        kk = kk * lax.rsqrt(jnp.maximum(head_sum(kk * kk), 1e-24))
        k2 = k_ * (1.0 + (a_ - 1.0) * kap)
        p_ = -(kk * a_)
        tot = cum[c - 1:c, :]
        e_neg = jnp.exp(-cum)
        e_tot = jnp.exp(tot - cum)
        rt = r_ * jnp.exp(cum)
        qt = kk * jnp.exp(cum - lw_)
        vb = v_.astype(BF16)
        amat = _dot_nt(jnp.concatenate([qt, rt], axis=0).astype(BF16),
                       jnp.concatenate([stack(k2 * e_neg), stack(p_ * e_neg)], axis=0).astype(BF16))
        yield
        a_qp = jnp.where(strict, amat[:c, w:], 0.0)
        a_rp = jnp.where(incl, amat[c:, w:], 0.0).astype(BF16)
        a_k = jnp.concatenate([jnp.where(strict, amat[:c, :w], 0.0),
                               jnp.where(incl, amat[c:, :w], 0.0)], axis=0).astype(BF16)
        av = _dot(a_k, stack(vb))
        inv = eye + a_qp
        power = a_qp.astype(BF16)
        power = _dot(power, stack(power))
        yield
        steps = int(math.log2(c)) - 1
        for i in range(steps):
            pb = power.astype(BF16)
            if i + 1 < steps:
                both = _dot(jnp.concatenate([inv.astype(BF16), pb], axis=0), stack(pb))
                inv = inv + both[:c]
                power = both[c:]
            else:
                inv = inv + _dot(inv.astype(BF16), stack(pb))
            yield
        wu = _dot(inv.astype(BF16),
                  jnp.concatenate([stack(qt.astype(BF16)), stack(av[:c].astype(BF16))], axis=1))
        wub = wu.astype(BF16)
        yield
        ry = _dot(a_rp, jnp.concatenate([stack(wub[:, :w]), stack(wub[:, w:])], axis=1))
        mg = _dot_tn(jnp.concatenate([p_ * e_tot, k2 * e_tot], axis=0).astype(BF16),
                     jnp.concatenate([wub, jnp.concatenate([zeros, vb], axis=1)], axis=0))
        yield
        rw = (rt + ry[:, :w]).astype(BF16)
        m_w = jnp.where(head0, mg[:c, :w], mg[c:, :w]).astype(BF16)
        g_w = jnp.where(head0, mg[:c, w:], mg[c:, w:])
        gam = head_sum(eye * jnp.exp(tot))
        state = carried[0]
        both = _dot(jnp.concatenate([rw, m_w], axis=0), stack(state.astype(BF16)))
        y = both[:c] + av[c:] + ry[:, w:]
        carried[0] = gam * state + both[c:] + g_w
        yield
        yc = y - head_sum(y) * (1.0 / hd)
        var = head_sum(yc * yc) * (1.0 / hd)
        yn = yc * lax.rsqrt(var + GN_EPS) * lnw + lnb
        bonus = head_sum(r_ * k2 * rkp) * v_
        o_ref[0, rs, :] = ((yn + bonus) * g_ref[0, rs, :].astype(F32)).astype(BF16)

    live, started = [], 0
    while started < n or live:
        if started < n:
            live.append(chunk(started))
            started += 1
        for gen in list(live):
            if next(gen, "done") == "done":
                live.remove(gen)
    state_ref[...] = carried[0]


def _rwkv_rec(r, k, v, a, lw, g, k_k, k_a, r_k, ln_w, ln_b):
    bsz, seq, d = r.shape
    hw = 2 * RWKV_HEAD
    tile = pl.BlockSpec((1, REC_ROWS, hw), lambda b, h, s: (b, s, h))
    par = pl.BlockSpec((1, hw), lambda b, h, s: (0, h))
    params = [p.reshape(1, d).astype(F32) for p in (k_k, k_a, r_k, ln_w, ln_b)]
    return pl.pallas_call(
        _rwkv_rec_kernel,
        grid=(bsz, d // hw, seq // REC_ROWS),
        in_specs=[tile] * 6 + [par] * 5,
        out_specs=tile,
        out_shape=jax.ShapeDtypeStruct((bsz, seq, d), BF16),
        scratch_shapes=[pltpu.VMEM((CHUNK, hw), F32)],
        compiler_params=_cparams("arbitrary", "arbitrary", "arbitrary"),
        name="rwkv_rec",
    )(r, k, v, a, lw, g, *params)


def _proj_res_kernel(a_ref, w_ref, res_ref, o_ref):
    o_ref[...] = res_ref[...] + _dot(a_ref[...], w_ref[...])


def _proj_res(a, w, res):
    n, kdim = a.shape
    d = w.shape[1]
    return pl.pallas_call(
        _proj_res_kernel,
        grid=(n // PROJ_ROWS,),
        in_specs=[pl.BlockSpec((PROJ_ROWS, kdim), lambda i: (i, 0)), _const_spec(w.shape),
                  pl.BlockSpec((PROJ_ROWS, d), lambda i: (i, 0))],
        out_specs=pl.BlockSpec((PROJ_ROWS, d), lambda i: (i, 0)),
        out_shape=jax.ShapeDtypeStruct((n, d), F32),
        compiler_params=_cparams("arbitrary"),
        name="proj_res",
    )(a, w.astype(BF16), res)


def _moe_route_kernel(h_ref, g_ref, wr_ref, br_ref, xs_ref, gs_ref, dest_ref, cnt_ref):
    nb = h_ref.shape[0]
    nbp = xs_ref.shape[0]
    t = _rms(h_ref[...]) * g_ref[...]
    logits = jnp.dot(t, wr_ref[...], precision=HIGHEST, preferred_element_type=F32) + br_ref[...]
    lane = lax.broadcasted_iota(jnp.int32, logits.shape, 1).astype(F32)
    neg = jnp.float32(-jnp.inf)
    big = jnp.float32(LANES)
    is_grp = lane < N_EXPERT_GROUPS
    gl = jnp.where(is_grp, logits, neg)
    gmax = jnp.max(gl, axis=-1, keepdims=True)
    grp = jnp.min(jnp.where(gl == gmax, lane, big), axis=-1, keepdims=True)
    p_grp = 1.0 / jnp.sum(jnp.where(is_grp, jnp.exp(gl - gmax), 0.0), axis=-1, keepdims=True)
    e_lo = N_EXPERT_GROUPS + grp * EXPERTS_PER_GROUP
    in_grp = (lane >= e_lo) & (lane < e_lo + EXPERTS_PER_GROUP)
    el = jnp.where(in_grp, logits, neg)
    v1 = jnp.max(el, axis=-1, keepdims=True)
    i1 = jnp.min(jnp.where(el == v1, lane, big), axis=-1, keepdims=True)
    el2 = jnp.where(lane == i1, neg, el)
    v2 = jnp.max(el2, axis=-1, keepdims=True)
    i2 = jnp.min(jnp.where(el2 == v2, lane, big), axis=-1, keepdims=True)
    e2 = jnp.exp(v2 - v1)
    w1 = 1.0 / (1.0 + e2)
    w2 = e2 / (1.0 + e2)
    gates = jnp.where(lane == i1, w1, jnp.where(lane == i2, w2, 0.0)) * p_grp

    onehot = (lane == grp).astype(BF16)
    ri = lax.broadcasted_iota(jnp.int32, (nb, nb), 0)
    ci = lax.broadcasted_iota(jnp.int32, (nb, nb), 1)
    rank = _dot((ri > ci).astype(BF16), onehot)
    counts = jnp.sum(onehot.astype(F32), axis=0, keepdims=True)
    padded = jnp.ceil(counts / FFN_ROWS) * FFN_ROWS
    lane1 = lax.broadcasted_iota(jnp.int32, (1, LANES), 1)
    offs = jnp.zeros((1, LANES), F32)
    for gidx in range(1, N_EXPERT_GROUPS):
        prev = jnp.sum(jnp.where(lane1 < gidx, padded, 0.0), axis=-1, keepdims=True)
        offs = jnp.where(lane1 == gidx, prev, offs)
    dest = jnp.sum(onehot.astype(F32) * (rank + offs), axis=-1, keepdims=True)
    slot = lax.broadcasted_iota(jnp.int32, (nb, nbp), 1).astype(F32)
    perm_t = (dest == slot).astype(BF16)
    xs_ref[...] = _dot_tn(perm_t, t.astype(BF16)).astype(BF16)
    g_hi = gates.astype(BF16)
    g_lo = (gates - g_hi.astype(F32)).astype(BF16)
    moved = _dot_tn(perm_t, jnp.concatenate([g_hi, g_lo], axis=1))
    gs_ref[...] = moved[:, :LANES] + moved[:, LANES:]
    dest_ref[...] = dest
    cnt_ref[0] = jnp.where(lane1 < N_EXPERT_GROUPS, counts, 0.0)


def _moe_ffn_kernel(tile_ref, grp_ref, real_ref, *refs):
    nt = FFN_TILES
    xs_refs, gs_refs = refs[:nt], refs[nt:2 * nt]
    wg_ref, wu_ref, wd_ref, ys_ref = refs[2 * nt:]
    i = pl.program_id(0)

    @pl.when(real_ref[i] > 0)
    def _():
        x = jnp.concatenate([r[...] for r in xs_refs], axis=0)
        gates = jnp.concatenate([r[...] for r in gs_refs], axis=0)
        lane = lax.broadcasted_iota(jnp.int32, gates.shape, 1)
        base = N_EXPERT_GROUPS + grp_ref[i] * EXPERTS_PER_GROUP
        acc = jnp.zeros(ys_ref.shape, F32)
        for e in range(EXPERTS_PER_GROUP):
            ge = jnp.sum(jnp.where(lane == base + e, gates, 0.0), axis=-1, keepdims=True)
            gate_act = _dot(x, wg_ref[e])
            hdn = gate_act * _sigmoid(gate_act) * _dot(x, wu_ref[e])
            acc = acc + ge * _dot(hdn.astype(BF16), wd_ref[e])
        ys_ref[...] = acc.astype(BF16)

    @pl.when(real_ref[i] == 0)
    def _():
        ys_ref[...] = jnp.zeros_like(ys_ref)


def _moe_merge_kernel(slot_ref, h_ref, dest_ref, *refs):
    ys_refs, o_ref = refs[:-1], refs[-1]
    ys = jnp.concatenate([r[...] for r in ys_refs], axis=0)
    nb, nbp = h_ref.shape[0], ys.shape[0]
    slot = lax.broadcasted_iota(jnp.int32, (nb, nbp), 1).astype(F32)
    perm_t = (dest_ref[...] == slot).astype(BF16)
    o_ref[...] = h_ref[...] + _dot(perm_t, ys)


def _moe_merge_norm_kernel(slot_ref, h_ref, dest_ref, *refs):
    ys_refs, g_ref, o_ref = refs[:-2], refs[-2], refs[-1]
    ys = jnp.concatenate([r[...] for r in ys_refs], axis=0)
    nb, nbp = h_ref.shape[0], ys.shape[0]
    slot = lax.broadcasted_iota(jnp.int32, (nb, nbp), 1).astype(F32)
    perm_t = (dest_ref[...] == slot).astype(BF16)
    o_ref[...] = _rms(h_ref[...] + _dot(perm_t, ys)) * g_ref[...]


def _ffn_slots(n_tiles):
    return -(-(n_tiles + N_EXPERT_GROUPS * (FFN_TILES - 1)) // FFN_TILES) * FFN_TILES


def _ffn_schedule(counts, nblk, tiles_per_blk):
    i32 = jnp.int32
    tiles = (counts + FFN_ROWS - 1) // FFN_ROWS
    first_rel = jnp.cumsum(tiles, axis=1) - tiles
    used_blk = jnp.sum(tiles, axis=1)
    grp_pad = (jnp.sum(tiles, axis=0) + FFN_TILES - 1) // FFN_TILES * FFN_TILES
    grp_start = jnp.cumsum(grp_pad) - grp_pad
    before = jnp.cumsum(tiles, axis=0) - tiles
    n_tiles = nblk * tiles_per_blk
    t = jnp.arange(n_tiles, dtype=i32)
    b, k = t // tiles_per_blk, t % tiles_per_blk
    used = k < used_blk[b]
    grp_t = jnp.minimum(jnp.sum(((first_rel + tiles)[b] <= k[:, None]).astype(i32), axis=1), N_EXPERT_GROUPS - 1)
    pos = grp_start[grp_t] + before[b, grp_t] + (k - first_rel[b, grp_t])
    s = jnp.arange(_ffn_slots(n_tiles), dtype=i32)
    hit = used[None, :] & (pos[None, :] == s[:, None])
    slot_has = jnp.sum(hit.astype(i32), axis=1) > 0
    free_rank = jnp.cumsum((~slot_has).astype(i32)) - 1
    unused_rank = jnp.cumsum((~used).astype(i32)) - 1
    fill = (~used)[None, :] & (unused_rank[None, :] == free_rank[:, None])
    placed = jnp.where(slot_has[:, None], hit, fill)
    slot_tile = jnp.sum(jnp.where(placed, t[None, :], 0), axis=1).astype(i32)
    tile_slot = jnp.sum(jnp.where(placed, s[:, None], 0), axis=0).astype(i32)
    slot_grp = jnp.sum(jnp.where(hit, grp_t[None, :], 0), axis=1).reshape(-1, FFN_TILES)
    step_real = jnp.sum(slot_has.astype(i32).reshape(-1, FFN_TILES), axis=1).astype(i32)
    n_real_steps = jnp.sum(grp_pad) // FFN_TILES
    last_grp = slot_grp[jnp.maximum(n_real_steps - 1, 0), 0]
    step_grp = jnp.where(step_real > 0, slot_grp[:, 0], last_grp).astype(i32)
    return slot_tile, tile_slot, step_grp, step_real


def _hier_moe_residual(h, norm_g, w_grp, b_grp, w_exp, b_exp, w_gate, w_up, w_down, final_g=None):
    n, d = h.shape
    nb = MOE_BLOCK
    nblk = n // nb
    nbp = nb + N_EXPERT_GROUPS * FFN_ROWS
    tiles_per_blk = nbp // FFN_ROWS
    n_exp = N_EXPERT_GROUPS * EXPERTS_PER_GROUP
    w_router = jnp.pad(jnp.concatenate([w_grp, w_exp], axis=1), ((0, 0), (0, LANES - N_EXPERT_GROUPS - n_exp)))
    b_router = jnp.pad(jnp.concatenate([b_grp, b_exp]), (0, LANES - N_EXPERT_GROUPS - n_exp)).reshape(1, LANES)

    xs, gs, dest, counts = pl.pallas_call(
        _moe_route_kernel,
        grid=(nblk,),
        in_specs=[pl.BlockSpec((nb, d), lambda i: (i, 0)), _const_spec((1, d)),
                  _const_spec((d, LANES)), _const_spec((1, LANES))],
        out_specs=[pl.BlockSpec((nbp, d), lambda i: (i, 0)), pl.BlockSpec((nbp, LANES), lambda i: (i, 0)),
                   pl.BlockSpec((nb, 1), lambda i: (i, 0)), pl.BlockSpec((1, 1, LANES), lambda i: (i, 0, 0))],
        out_shape=[jax.ShapeDtypeStruct((nblk * nbp, d), BF16), jax.ShapeDtypeStruct((nblk * nbp, LANES), F32),
                   jax.ShapeDtypeStruct((n, 1), F32), jax.ShapeDtypeStruct((nblk, 1, LANES), F32)],
        compiler_params=_cparams("arbitrary"),
        name="moe_route",
    )(h, norm_g.reshape(1, d), w_router, b_router)

    n_tiles = nblk * tiles_per_blk
    n_slots = _ffn_slots(n_tiles)
    slot_tile, tile_slot, step_grp, step_real = _ffn_schedule(
        counts[:, 0, :N_EXPERT_GROUPS].astype(jnp.int32), nblk, tiles_per_blk)

    def tile_spec(width, j):
        return pl.BlockSpec((FFN_ROWS, width), lambda i, tile, grp, real: (tile[i * FFN_TILES + j], 0))

    def wspec(shape):
        return pl.BlockSpec((EXPERTS_PER_GROUP,) + shape, lambda i, tile, grp, real: (grp[i], 0, 0))

    f = w_gate.shape[-1]
    ys = pl.pallas_call(
        _moe_ffn_kernel,
        grid_spec=pltpu.PrefetchScalarGridSpec(
            num_scalar_prefetch=3,
            grid=(n_slots // FFN_TILES,),
            in_specs=[tile_spec(d, j) for j in range(FFN_TILES)] + [tile_spec(LANES, j) for j in range(FFN_TILES)]
            + [wspec((d, f)), wspec((d, f)), wspec((f, d))],
            out_specs=pl.BlockSpec((FFN_TILES * FFN_ROWS, d), lambda i, tile, grp, real: (i, 0)),
        ),
        out_shape=jax.ShapeDtypeStruct((n_slots * FFN_ROWS, d), BF16),
        compiler_params=_cparams("arbitrary"),
        name="moe_ffn",
    )(slot_tile, step_grp, step_real, *([xs] * FFN_TILES), *([gs] * FFN_TILES),
      w_gate.astype(BF16), w_up.astype(BF16), w_down.astype(BF16))

    blk = pl.BlockSpec((nb, d), lambda i, slot: (i, 0))
    in_specs = [blk, pl.BlockSpec((nb, 1), lambda i, slot: (i, 0))]
    in_specs += [pl.BlockSpec((FFN_ROWS, d), functools.partial(lambda j, i, slot: (slot[i * tiles_per_blk + j], 0), j))
                 for j in range(tiles_per_blk)]
    args = [h, dest] + [ys] * tiles_per_blk
    body = _moe_merge_kernel
    if final_g is not None:
        in_specs.append(pl.BlockSpec((1, d), lambda i, slot: (0, 0)))
        args.append(final_g.reshape(1, d))
        body = _moe_merge_norm_kernel
    return pl.pallas_call(
        body,
        grid_spec=pltpu.PrefetchScalarGridSpec(num_scalar_prefetch=1, grid=(nblk,), in_specs=in_specs, out_specs=blk),
        out_shape=jax.ShapeDtypeStruct((n, d), F32),
        compiler_params=_cparams("arbitrary"),
        name="moe_merge",
    )(tile_slot, *args)


GROUP_WIDTH = DIL_HEADS * DIL_HEAD_DIM
N_STRIDED = 3 * sum(1 for _, dil in DIL_PATTERNS if dil > 1)


def _qkv_kernel(h_ref, gq_ref, gkv_ref, wq_ref, wk_ref, wv_ref, *refs):
    outs, scratch = refs[:-1], refs[-1]
    n = _rms(h_ref[...])
    xq = (n * gq_ref[...]).astype(BF16)
    xkv = (n * gkv_ref[...]).astype(BF16)
    gw = GROUP_WIDTH
    slot = 0
    for gi, (_, dil) in enumerate(DIL_PATTERNS):
        cols = slice(gi * gw, (gi + 1) * gw)
        for t, (x, w_ref) in enumerate(((xq, wq_ref), (xkv, wk_ref), (xkv, wv_ref))):
            res = _dot(x, w_ref[:, cols])
            out_ref = outs[3 * gi + t]
            if dil == 1:
                out_ref[...] = res.astype(BF16)
                continue
            sub = res.shape[0] // dil
            for c in range(gw // LANES):
                scratch[slot, c] = res[:, c * LANES:(c + 1) * LANES]
            for r in range(dil):
                for c in range(gw // LANES):
                    lo = r * gw + c * LANES
                    out_ref[:, lo:lo + LANES] = scratch[slot, c, pl.ds(r, sub, stride=dil), :].astype(BF16)
            slot += 1


def _qkv_proj(h, g_q, g_kv, w_q, w_kv):
    n, d = h.shape
    qw = w_q.shape[1]
    gw = GROUP_WIDTH
    scale = math.log2(math.e) / math.sqrt(DIL_HEAD_DIM)
    rows = pl.BlockSpec((PROJ_ROWS, d), lambda i: (i, 0))
    out_specs, out_shape = [], []
    for _, dil in DIL_PATTERNS:
        for _ in range(3):
            out_specs.append(pl.BlockSpec((PROJ_ROWS // dil, dil * gw), lambda i: (i, 0)))
            out_shape.append(jax.ShapeDtypeStruct((n // dil, dil * gw), BF16))
    return pl.pallas_call(
        _qkv_kernel,
        grid=(n // PROJ_ROWS,),
        in_specs=[rows, _const_spec((1, d)), _const_spec((1, d))] + [_const_spec((d, qw))] * 3,
        out_specs=out_specs,
        out_shape=out_shape,
        scratch_shapes=[pltpu.VMEM((N_STRIDED, gw // LANES, PROJ_ROWS, LANES), F32)],
        compiler_params=_cparams("arbitrary"),
        name="qkv_proj",
    )(h, g_q.reshape(1, d), g_kv.reshape(1, d), (w_q * scale).astype(BF16),
      w_kv[:, :qw].astype(BF16), w_kv[:, qw:].astype(BF16))


def _attn_kernel(q_ref, kc_ref, kp_ref, vc_ref, vp_ref, o_ref, lse_ref):
    n = pl.program_id(2)
    blk = ATT_BLOCK
    pw = 2 * DIL_HEAD_DIM
    ri = lax.broadcasted_iota(jnp.int32, (blk, blk), 0)
    ci = lax.broadcasted_iota(jnp.int32, (blk, blk), 1)
    lowest = jnp.finfo(F32).min
    cap_prev = jnp.where(ci >= ri, jnp.inf, lowest).astype(F32)
    cap_first = jnp.where(n > 0, cap_prev, lowest)
    cap_own = jnp.where(ci <= ri, jnp.inf, lowest).astype(F32)
    head0 = lax.broadcasted_iota(jnp.int32, (blk, pw), 1) < DIL_HEAD_DIM
    zero = jnp.zeros((blk, pw), BF16)
    lane_t = lax.broadcasted_iota(jnp.int32, (blk, LANES), 1)

    def stack(x):
        return jnp.concatenate([jnp.where(head0, x, zero), jnp.where(head0, zero, x)], axis=0)

    streams = [(i, p) for i in range(q_ref.shape[1] // blk) for p in range(DIL_HEADS // 2)]

    def rows(i):
        return slice(i * blk, (i + 1) * blk)

    def lanes(p):
        return slice(p * pw, (p + 1) * pw)

    stacked = {}

    def kv_blocks(cur_ref, prev_ref, i, p):
        def one(w):
            name = (id(cur_ref), w, p)
            if name not in stacked:
                stacked[name] = stack(prev_ref[0, :, lanes(p)] if w < 0 else cur_ref[0, rows(w), lanes(p)])
            return stacked[name]
        return jnp.concatenate([one(i - 1), one(i)], axis=0)

    s = []
    for i, p in streams:
        raw = _dot_nt(q_ref[0, rows(i), lanes(p)], kv_blocks(kc_ref, kp_ref, i, p))
        cap = cap_first if i == 0 else cap_prev
        s.append([jnp.minimum(raw[:, :blk], cap), jnp.minimum(raw[:, blk:2 * blk], cap),
                  jnp.minimum(raw[:, 2 * blk:3 * blk], cap_own), jnp.minimum(raw[:, 3 * blk:], cap_own)])
    m_a = [jnp.max(jnp.maximum(x[0], x[2]), axis=-1, keepdims=True) for x in s]
    m_b = [jnp.max(jnp.maximum(x[1], x[3]), axis=-1, keepdims=True) for x in s]
    e = [[jnp.exp2(x[0] - ma), jnp.exp2(x[1] - mb), jnp.exp2(x[2] - ma), jnp.exp2(x[3] - mb)]
         for x, ma, mb in zip(s, m_a, m_b)]
    pb = [jnp.concatenate([y.astype(BF16) for y in x], axis=1) for x in e]
    blk_head = (lax.broadcasted_iota(jnp.int32, (4 * blk, LANES), 0) // blk) % 2
    sel_lane = lax.broadcasted_iota(jnp.int32, (4 * blk, LANES), 1) - DIL_HEADS
    sel = [(sel_lane == 2 * p + blk_head).astype(BF16) for p in range(DIL_HEADS // 2)]
    pv = [_dot(pb[idx], jnp.concatenate([kv_blocks(vc_ref, vp_ref, i, p), sel[p]], axis=1))
          for idx, (i, p) in enumerate(streams)]
    tiles = {}
    for idx, (i, p) in enumerate(streams):
        o_ref[0, rows(i), lanes(p)] = pv[idx][:, :pw].astype(BF16)
        tile = tiles.get(i, jnp.zeros((blk, LANES), F32)) + pv[idx][:, pw:]
        tiles[i] = jnp.where(lane_t == 2 * p, m_a[idx], jnp.where(lane_t == 2 * p + 1, m_b[idx], tile))
    for i, tile in tiles.items():
        lse_ref[0, rows(i), :] = tile


def _attn_group(q, k, v, gi, dilation, bsz, seq):
    gw = GROUP_WIDTH
    sub = seq // dilation
    windows = min(ATT_WINDOWS, sub // ATT_BLOCK)
    step_rows = windows * ATT_BLOCK
    assert sub % step_rows == 0
    view = (bsz, sub, dilation * gw)
    qv, kv_, vv = q.reshape(view), k.reshape(view), v.reshape(view)
    cur = pl.BlockSpec((1, step_rows, gw), lambda b, r, n: (b, n, r))
    prev = pl.BlockSpec((1, ATT_BLOCK, gw), lambda b, r, n: (b, jnp.maximum(n * windows - 1, 0), r))
    lse_spec = pl.BlockSpec((1, step_rows, LANES), lambda b, r, n: (b, n, r))
    o, lse = pl.pallas_call(
        _attn_kernel,
        grid=(bsz, dilation, sub // step_rows),
        in_specs=[cur, cur, prev, cur, prev],
        out_specs=[cur, lse_spec],
        out_shape=[jax.ShapeDtypeStruct(view, BF16),
                   jax.ShapeDtypeStruct((bsz, sub, dilation * LANES), F32)],
        compiler_params=_cparams("arbitrary", "arbitrary", "arbitrary"),
        name=f"attn_group{gi}",
    )(qv, kv_, kv_, vv, vv)
    return o.reshape(bsz * sub, dilation * gw), lse.reshape(bsz * sub, dilation * LANES)


def _attn_merge_kernel(o0_ref, o1_ref, o2_ref, l0_ref, l1_ref, l2_ref, w_ref, res_ref, out_ref, *scratch):
    rows = res_ref.shape[0]
    gw = GROUP_WIDTH
    outs, lses = [], []
    slot = 0
    for (_, dil), o_ref, l_ref in zip(DIL_PATTERNS, (o0_ref, o1_ref, o2_ref), (l0_ref, l1_ref, l2_ref)):
        if dil == 1:
            outs.append(o_ref[...].astype(F32))
            lses.append(l_ref[...])
            continue
        o_scr, l_scr = scratch[2 * slot], scratch[2 * slot + 1]
        sub = rows // dil
        for r in range(dil):
            for c in range(gw // LANES):
                lo = r * gw + c * LANES
                o_scr[c, pl.ds(r, sub, stride=dil), :] = o_ref[:, lo:lo + LANES].astype(F32)
            l_scr[pl.ds(r, sub, stride=dil), :] = l_ref[:, r * LANES:(r + 1) * LANES]
        outs.append(jnp.concatenate([o_scr[c] for c in range(gw // LANES)], axis=1))
        lses.append(l_scr[...])
        slot += 1
    m = jnp.maximum(jnp.maximum(lses[0], lses[1]), lses[2])
    e = [jnp.exp2(l - m) for l in lses]
    sums = [pltpu.roll(l, LANES - DIL_HEADS, 1) for l in lses]
    denom = e[0] * sums[0] + e[1] * sums[1] + e[2] * sums[2]
    is_head = lax.broadcasted_iota(jnp.int32, (rows, LANES), 1) < DIL_HEADS
    expand = (lax.broadcasted_iota(jnp.int32, (LANES, gw), 1) // DIL_HEAD_DIM
              == lax.broadcasted_iota(jnp.int32, (LANES, gw), 0)).astype(BF16)
    mix = jnp.zeros((rows, gw), F32)
    for g in range(len(outs)):
        alpha = jnp.where(is_head, e[g] / denom, 0.0)
        hi = alpha.astype(BF16)
        lo = (alpha - hi.astype(F32)).astype(BF16)
        mix = mix + (_dot(hi, expand) + _dot(lo, expand)) * outs[g]
    out_ref[...] = res_ref[...] + _dot(mix.astype(BF16), w_ref[...])


def _attn_merge(outs, lses, w_o, res):
    n, d = res.shape
    gw = GROUP_WIDTH
    o_specs = [pl.BlockSpec((PROJ_ROWS // dil, dil * gw), lambda i: (i, 0)) for _, dil in DIL_PATTERNS]
    l_specs = [pl.BlockSpec((PROJ_ROWS // dil, dil * LANES), lambda i: (i, 0)) for _, dil in DIL_PATTERNS]
    full = pl.BlockSpec((PROJ_ROWS, d), lambda i: (i, 0))
    scratch = []
    for _, dil in DIL_PATTERNS:
        if dil > 1:
            scratch += [pltpu.VMEM((gw // LANES, PROJ_ROWS, LANES), F32), pltpu.VMEM((PROJ_ROWS, LANES), F32)]
    return pl.pallas_call(
        _attn_merge_kernel,
        grid=(n // PROJ_ROWS,),
        in_specs=o_specs + l_specs + [_const_spec(w_o.shape), full],
        out_specs=full,
        out_shape=jax.ShapeDtypeStruct((n, d), F32),
        scratch_shapes=scratch,
        compiler_params=_cparams("arbitrary"),
        name="attn_merge",
    )(*outs, *lses, w_o.astype(BF16), res)


def kernel(x, norm_mix_g, norm_ffn_g, rwkv_mu, rwkv_w_r, rwkv_w_k, rwkv_w_v, rwkv_w0, rwkv_w1, rwkv_w2,
           rwkv_a0, rwkv_a1, rwkv_a2, rwkv_g1, rwkv_g2, rwkv_k_k, rwkv_k_a, rwkv_r_k, rwkv_ln_w, rwkv_ln_b,
           rwkv_w_o, kv_norm_g, w_kv, attn_w_q, attn_w_o, moe_w_grp, moe_b_grp, moe_w_exp, moe_b_exp,
           moe_w_gate, moe_w_up, moe_w_down, final_norm_g):
    bsz, seq, d = x.shape
    n = bsz * seq
    depth = norm_mix_g.shape[0]
    n_rwkv = rwkv_mu.shape[0]
    h = x.reshape(n, d)
    q = k_sh = v_sh = None
    for layer in range(depth):
        if layer < n_rwkv:
            i = layer
            r, k, v, a, lw, g = _rwkv_pre(h.reshape(bsz, seq, d), norm_mix_g[layer], rwkv_mu[i], rwkv_w0[i], rwkv_a0[i],
                                          rwkv_w_r[i], rwkv_w_k[i], rwkv_w_v[i], rwkv_w1[i], rwkv_w2[i],
                                          rwkv_a1[i], rwkv_a2[i], rwkv_g1[i], rwkv_g2[i])
            y = _rwkv_rec(r, k, v, a, lw, g, rwkv_k_k[i], rwkv_k_a[i], rwkv_r_k[i], rwkv_ln_w[i], rwkv_ln_b[i])
            h = _proj_res(y.reshape(n, d), rwkv_w_o[i], h)
        else:
            i = layer - n_rwkv
            qkv = _qkv_proj(h, norm_mix_g[layer], kv_norm_g, attn_w_q[i], w_kv)
            q = qkv[0::3]
            if i == 0:
                k_sh, v_sh = qkv[1::3], qkv[2::3]
            outs, lses = [], []
            for gi, (window, dilation) in enumerate(DIL_PATTERNS):
                assert window // dilation == ATT_BLOCK and seq % window == 0
                o, lse = _attn_group(q[gi], k_sh[gi], v_sh[gi], gi, dilation, bsz, seq)
                outs.append(o)
                lses.append(lse)
            h = _attn_merge(outs, lses, attn_w_o[i], h)
        last = layer == depth - 1
        h = _hier_moe_residual(h, norm_ffn_g[layer], moe_w_grp[layer], moe_b_grp[layer], moe_w_exp[layer],
                               moe_b_exp[layer], moe_w_gate[layer], moe_w_up[layer], moe_w_down[layer],
                               final_g=final_norm_g if last else None)
    return h.reshape(bsz, seq, d)
```

```python
import functools
import math

import jax
import jax.numpy as jnp
from jax import lax
from jax.experimental import pallas as pl
from jax.experimental.pallas import tpu as pltpu

F32 = jnp.float32
BF16 = jnp.bfloat16
HIGHEST = lax.Precision.HIGHEST

NORM_EPS = 1e-6
RWKV_HEAD = 64
GN_EPS = RWKV_HEAD * 1e-5
DIL_PATTERNS = ((128, 1), (512, 4), (2048, 16))
DIL_HEADS = 8
DIL_HEAD_DIM = 64
N_EXPERT_GROUPS = 4
EXPERTS_PER_GROUP = 4
LANES = 128
VMEM_LIMIT = 56 * 1024 * 1024

CHUNK = 64
REC_ROWS = 2048
PRE_ROWS = 512
PROJ_ROWS = 512
MOE_BLOCK = 1024
FFN_ROWS = 128
FFN_TILES = 4
ATT_BLOCK = 128
ATT_WINDOWS = 4


def _cparams(*sem):
    return pltpu.CompilerParams(dimension_semantics=sem, vmem_limit_bytes=VMEM_LIMIT)


def _dot(a, b):
    return jnp.dot(a, b, preferred_element_type=F32)


def _dot_nt(a, b):
    return lax.dot_general(a, b, (((1,), (1,)), ((), ())), preferred_element_type=F32)


def _dot_tn(a, b, precision=None):
    return lax.dot_general(a, b, (((0,), (0,)), ((), ())), preferred_element_type=F32, precision=precision)


def _rms(x):
    return x * lax.rsqrt(jnp.mean(x * x, axis=-1, keepdims=True) + NORM_EPS)


def _sigmoid(z):
    return 1.0 / (1.0 + jnp.exp(-z))


def _const_spec(shape):
    nd = len(shape)
    return pl.BlockSpec(shape, lambda *_: (0,) * nd)


def _rwkv_pre_kernel(x_ref, gmix_ref, mu_ref, w0_ref, a0_ref, wr_ref, wk_ref, wv_ref,
                     w1_ref, w2_ref, a1_ref, a2_ref, g1_ref, g2_ref,
                     r_ref, k_ref, v_ref, a_ref, lw_ref, g_ref, prev_ref):
    @pl.when(pl.program_id(1) == 0)
    def _():
        prev_ref[...] = jnp.zeros_like(prev_ref)

    x = x_ref[0]
    rows = x.shape[0]
    xn = _rms(x) * gmix_ref[...]
    row = lax.broadcasted_iota(jnp.int32, xn.shape, 0)
    shifted = jnp.where(row == 0, prev_ref[7:8, :], pltpu.roll(xn, 1, 0))
    prev_ref[...] = xn[rows - 8:, :]
    xx = shifted - xn

    def mix(i):
        return (xn + xx * mu_ref[i:i + 1, :]).astype(BF16)

    r_ref[0] = _dot(mix(0), wr_ref[...]).astype(BF16)
    k_ref[0] = _dot(mix(2), wk_ref[...]).astype(BF16)
    v_ref[0] = _dot(mix(3), wv_ref[...]).astype(BF16)
    u = w0_ref[...] + _dot(jnp.tanh(_dot(mix(1), w1_ref[...])).astype(BF16), w2_ref[...])
    w_log = -(jnp.maximum(-u, 0.0) + jnp.log(1.0 + jnp.exp(-jnp.abs(u)))) - 0.5
    lw_ref[0] = -jnp.exp(w_log)
    a_ref[0] = _sigmoid(a0_ref[...] + _dot(_dot(mix(4), a1_ref[...]).astype(BF16), a2_ref[...])).astype(BF16)
    g_ref[0] = _dot(_sigmoid(_dot(mix(5), g1_ref[...])).astype(BF16), g2_ref[...]).astype(BF16)


def _pad_cols(w, n):
    return jnp.pad(w, ((0, 0), (0, n - w.shape[1])))


def _pad_rows(w, n):
    return jnp.pad(w, ((0, n - w.shape[0]), (0, 0)))


def _rwkv_pre(x, gmix, mu, w0, a0, w_r, w_k, w_v, w1, w2, a1, a2, g1, g2):
    bsz, seq, d = x.shape
    lw = -(-w1.shape[1] // LANES) * LANES
    la = -(-a1.shape[1] // LANES) * LANES
    lg = -(-g1.shape[1] // LANES) * LANES
    mu8 = jnp.pad(mu, ((0, 8 - mu.shape[0]), (0, 0)))
    args = (x, gmix.reshape(1, d), mu8, w0.reshape(1, d), a0.reshape(1, d),
            w_r.astype(BF16), w_k.astype(BF16), w_v.astype(BF16),
            _pad_cols(w1, lw).astype(BF16), _pad_rows(w2, lw).astype(BF16),
            _pad_cols(a1, la).astype(BF16), _pad_rows(a2, la).astype(BF16),
            _pad_cols(g1, lg).astype(BF16), _pad_rows(g2, lg).astype(BF16))
    tile = pl.BlockSpec((1, PRE_ROWS, d), lambda b, s: (b, s, 0))
    in_specs = [tile] + [_const_spec(a.shape) for a in args[1:]]
    out_dt = (BF16, BF16, BF16, BF16, F32, BF16)
    return pl.pallas_call(
        _rwkv_pre_kernel,
        grid=(bsz, seq // PRE_ROWS),
        in_specs=in_specs,
        out_specs=[tile] * 6,
        out_shape=[jax.ShapeDtypeStruct((bsz, seq, d), t) for t in out_dt],
        scratch_shapes=[pltpu.VMEM((8, d), F32)],
        compiler_params=_cparams("arbitrary", "arbitrary"),
        name="rwkv_pre",
    )(*args)


def _rwkv_rec_kernel(r_ref, k_ref, v_ref, a_ref, lw_ref, g_ref, kk_ref, ka_ref, rk_ref, lnw_ref, lnb_ref,
                     o_ref, state_ref):
    @pl.when(pl.program_id(2) == 0)
    def _():
        state_ref[...] = jnp.zeros_like(state_ref)

    c = CHUNK
    hd = RWKV_HEAD
    w = 2 * hd
    assert c == hd
    n = r_ref.shape[1] // c
    lane = lax.broadcasted_iota(jnp.int32, (c, w), 1)
    row = lax.broadcasted_iota(jnp.int32, (c, w), 0)
    head0 = lane < hd
    col = jnp.where(head0, lane, lane - hd)
    strict = row > col
    incl = row >= col
    eye = (row == col).astype(F32)
    tri = (lax.broadcasted_iota(jnp.int32, (c, c), 0) >= lax.broadcasted_iota(jnp.int32, (c, c), 1)).astype(BF16)
    kkp, kap, rkp, lnw, lnb = kk_ref[...], ka_ref[...], rk_ref[...], lnw_ref[...], lnb_ref[...]

    def head_sum(x):
        s0 = jnp.sum(jnp.where(head0, x, 0.0), axis=-1, keepdims=True)
        s1 = jnp.sum(jnp.where(head0, 0.0, x), axis=-1, keepdims=True)
        return jnp.where(head0, s0, s1)

    def stack(x):
        return jnp.concatenate([jnp.where(head0, x, 0.0), jnp.where(head0, 0.0, x)], axis=0)

    def cumsum(x):
        hi = x.astype(BF16)
        lo = (x - hi.astype(F32)).astype(BF16)
        both = _dot(tri, jnp.concatenate([hi, lo], axis=1))
        return both[:, :w] + both[:, w:]

    zeros = jnp.zeros((c, w), BF16)
    carried = [state_ref[...]]

    def chunk(j):
        rs = slice(j * c, (j + 1) * c)
        r_ = r_ref[0, rs, :].astype(F32)
        k_ = k_ref[0, rs, :].astype(F32)
        v_ = v_ref[0, rs, :].astype(F32)
        a_ = a_ref[0, rs, :].astype(F32)
        lw_ = lw_ref[0, rs, :]
        cum = cumsum(lw_)
        yield
        kk = k_ * kkp
        kk = kk * lax.rsqrt(jnp.maximum(head_sum(kk * kk), 1e-24))
        k2 = k_ * (1.0 + (a_ - 1.0) * kap)
        p_ = -(kk * a_)
        tot = cum[c - 1:c, :]
        e_neg = jnp.exp(-cum)
        e_tot = jnp.exp(tot - cum)
        rt = r_ * jnp.exp(cum)
        qt = kk * jnp.exp(cum - lw_)
        vb = v_.astype(BF16)
        amat = _dot_nt(jnp.concatenate([qt, rt], axis=0).astype(BF16),
                       jnp.concatenate([stack(k2 * e_neg), stack(p_ * e_neg)], axis=0).astype(BF16))
        yield
        a_qp = jnp.where(strict, amat[:c, w:], 0.0)
        a_rp = jnp.where(incl, amat[c:, w:], 0.0).astype(BF16)
        a_k = jnp.concatenate([jnp.where(strict, amat[:c, :w], 0.0),
                               jnp.where(incl, amat[c:, :w], 0.0)], axis=0).astype(BF16)
        av = _dot(a_k, stack(vb))
        inv = eye + a_qp
        power = a_qp.astype(BF16)
        power = _dot(power, stack(power))
        yield
        steps = int(math.log2(c)) - 1
        for i in range(steps):
            pb = power.astype(BF16)
            if i + 1 < steps:
                both = _dot(jnp.concatenate([inv.astype(BF16), pb], axis=0), stack(pb))
                inv = inv + both[:c]
                power = both[c:]
            else:
                inv = inv + _dot(inv.astype(BF16), stack(pb))
            yield
        wu = _dot(inv.astype(BF16),
                  jnp.concatenate([stack(qt.astype(BF16)), stack(av[:c].astype(BF16))], axis=1))
        wub = wu.astype(BF16)
        yield
        ry = _dot(a_rp, jnp.concatenate([stack(wub[:, :w]), stack(wub[:, w:])], axis=1))
        mg = _dot_tn(jnp.concatenate([p_ * e_tot, k2 * e_tot], axis=0).astype(BF16),
                     jnp.concatenate([wub, jnp.concatenate([zeros, vb], axis=1)], axis=0))
        yield
        rw = (rt + ry[:, :w]).astype(BF16)
        m_w = jnp.where(head0, mg[:c, :w], mg[c:, :w]).astype(BF16)
        g_w = jnp.where(head0, mg[:c, w:], mg[c:, w:])
        gam = head_sum(eye * jnp.exp(tot))
        state = carried[0]
        both = _dot(jnp.concatenate([rw, m_w], axis=0), stack(state.astype(BF16)))
        y = both[:c] + av[c:] + ry[:, w:]
        carried[0] = gam * state + both[c:] + g_w
        yield
        yc = y - head_sum(y) * (1.0 / hd)
        var = head_sum(yc * yc) * (1.0 / hd)
        yn = yc * lax.rsqrt(var + GN_EPS) * lnw + lnb
        bonus = head_sum(r_ * k2 * rkp) * v_
        o_ref[0, rs, :] = ((yn + bonus) * g_ref[0, rs, :].astype(F32)).astype(BF16)

    live, started = [], 0
    while started < n or live:
        if started < n:
            live.append(chunk(started))
            started += 1
        for gen in list(live):
            if next(gen, "done") == "done":
                live.remove(gen)
    state_ref[...] = carried[0]


def _rwkv_rec(r, k, v, a, lw, g, k_k, k_a, r_k, ln_w, ln_b):
    bsz, seq, d = r.shape
    hw = 2 * RWKV_HEAD
    tile = pl.BlockSpec((1, REC_ROWS, hw), lambda b, h, s: (b, s, h))
    par = pl.BlockSpec((1, hw), lambda b, h, s: (0, h))
    params = [p.reshape(1, d).astype(F32) for p in (k_k, k_a, r_k, ln_w, ln_b)]
    return pl.pallas_call(
        _rwkv_rec_kernel,
        grid=(bsz, d // hw, seq // REC_ROWS),
        in_specs=[tile] * 6 + [par] * 5,
        out_specs=tile,
        out_shape=jax.ShapeDtypeStruct((bsz, seq, d), BF16),
        scratch_shapes=[pltpu.VMEM((CHUNK, hw), F32)],
        compiler_params=_cparams("arbitrary", "arbitrary", "arbitrary"),
        name="rwkv_rec",
    )(r, k, v, a, lw, g, *params)


def _proj_res_kernel(a_ref, w_ref, res_ref, o_ref):
    o_ref[...] = res_ref[...] + _dot(a_ref[...], w_ref[...])


def _proj_res(a, w, res):
    n, kdim = a.shape
    d = w.shape[1]
    return pl.pallas_call(
        _proj_res_kernel,
        grid=(n // PROJ_ROWS,),
        in_specs=[pl.BlockSpec((PROJ_ROWS, kdim), lambda i: (i, 0)), _const_spec(w.shape),
                  pl.BlockSpec((PROJ_ROWS, d), lambda i: (i, 0))],
        out_specs=pl.BlockSpec((PROJ_ROWS, d), lambda i: (i, 0)),
        out_shape=jax.ShapeDtypeStruct((n, d), F32),
        compiler_params=_cparams("arbitrary"),
        name="proj_res",
    )(a, w.astype(BF16), res)


def _moe_route_kernel(h_ref, g_ref, wr_ref, br_ref, xs_ref, gs_ref, dest_ref, cnt_ref):
    nb = h_ref.shape[0]
    nbp = xs_ref.shape[0]
    tb = (_rms(h_ref[...]) * g_ref[...]).astype(BF16)
    logits = _dot(tb, wr_ref[...]) + br_ref[...]
    lane = lax.broadcasted_iota(jnp.int32, logits.shape, 1).astype(F32)
    neg = jnp.float32(-jnp.inf)
    big = jnp.float32(LANES)
    is_grp = lane < N_EXPERT_GROUPS
    gl = jnp.where(is_grp, logits, neg)
    gmax = jnp.max(gl, axis=-1, keepdims=True)
    grp = jnp.min(jnp.where(gl == gmax, lane, big), axis=-1, keepdims=True)
    p_grp = 1.0 / jnp.sum(jnp.where(is_grp, jnp.exp(gl - gmax), 0.0), axis=-1, keepdims=True)
    e_lo = N_EXPERT_GROUPS + grp * EXPERTS_PER_GROUP
    in_grp = (lane >= e_lo) & (lane < e_lo + EXPERTS_PER_GROUP)
    el = jnp.where(in_grp, logits, neg)
    v1 = jnp.max(el, axis=-1, keepdims=True)
    i1 = jnp.min(jnp.where(el == v1, lane, big), axis=-1, keepdims=True)
    el2 = jnp.where(lane == i1, neg, el)
    v2 = jnp.max(el2, axis=-1, keepdims=True)
    i2 = jnp.min(jnp.where(el2 == v2, lane, big), axis=-1, keepdims=True)
    e2 = jnp.exp(v2 - v1)
    w1 = 1.0 / (1.0 + e2)
    w2 = e2 / (1.0 + e2)
    gates = jnp.where(lane == i1, w1, jnp.where(lane == i2, w2, 0.0)) * p_grp

    onehot = (lane == grp).astype(BF16)
    ri = lax.broadcasted_iota(jnp.int32, (nb, nb), 0)
    ci = lax.broadcasted_iota(jnp.int32, (nb, nb), 1)
    rank = _dot((ri > ci).astype(BF16), onehot)
    counts = jnp.sum(onehot.astype(F32), axis=0, keepdims=True)
    padded = jnp.ceil(counts / FFN_ROWS) * FFN_ROWS
    lane1 = lax.broadcasted_iota(jnp.int32, (1, LANES), 1)
    offs = jnp.zeros((1, LANES), F32)
    for gidx in range(1, N_EXPERT_GROUPS):
        prev = jnp.sum(jnp.where(lane1 < gidx, padded, 0.0), axis=-1, keepdims=True)
        offs = jnp.where(lane1 == gidx, prev, offs)
    dest = jnp.sum(onehot.astype(F32) * (rank + offs), axis=-1, keepdims=True)
    slot = lax.broadcasted_iota(jnp.int32, (nb, nbp), 1).astype(F32)
    perm_t = (dest == slot).astype(BF16)
    xs_ref[...] = _dot_tn(perm_t, tb).astype(BF16)
    g_hi = gates.astype(BF16)
    g_lo = (gates - g_hi.astype(F32)).astype(BF16)
    moved = _dot_tn(perm_t, jnp.concatenate([g_hi, g_lo], axis=1))
    gs_ref[...] = moved[:, :LANES] + moved[:, LANES:]
    dest_ref[...] = dest
    cnt_ref[0] = jnp.where(lane1 < N_EXPERT_GROUPS, counts, 0.0)


def _moe_ffn_kernel(tile_ref, grp_ref, real_ref, *refs):
    nt = FFN_TILES
    xs_refs, gs_refs = refs[:nt], refs[nt:2 * nt]
    wg_ref, wu_ref, wd_ref, ys_ref = refs[2 * nt:]
    i = pl.program_id(0)

    @pl.when(real_ref[i] > 0)
    def _():
        x = jnp.concatenate([r[...] for r in xs_refs], axis=0)
        gates = jnp.concatenate([r[...] for r in gs_refs], axis=0)
        lane = lax.broadcasted_iota(jnp.int32, gates.shape, 1)
        base = N_EXPERT_GROUPS + grp_ref[i] * EXPERTS_PER_GROUP
        acc = jnp.zeros(ys_ref.shape, F32)
        for e in range(EXPERTS_PER_GROUP):
            ge = jnp.sum(jnp.where(lane == base + e, gates, 0.0), axis=-1, keepdims=True)
            gate_act = _dot(x, wg_ref[e])
            hdn = gate_act * _sigmoid(gate_act) * _dot(x, wu_ref[e])
            acc = acc + ge * _dot(hdn.astype(BF16), wd_ref[e])
        ys_ref[...] = acc.astype(BF16)

    @pl.when(real_ref[i] == 0)
    def _():
        ys_ref[...] = jnp.zeros_like(ys_ref)


def _moe_merge_kernel(slot_ref, h_ref, dest_ref, *refs):
    ys_refs, o_ref = refs[:-1], refs[-1]
    ys = jnp.concatenate([r[...] for r in ys_refs], axis=0)
    nb, nbp = h_ref.shape[0], ys.shape[0]
    slot = lax.broadcasted_iota(jnp.int32, (nb, nbp), 1).astype(F32)
    perm_t = (dest_ref[...] == slot).astype(BF16)
    o_ref[...] = h_ref[...] + _dot(perm_t, ys)


def _moe_merge_norm_kernel(slot_ref, h_ref, dest_ref, *refs):
    ys_refs, g_ref, o_ref = refs[:-2], refs[-2], refs[-1]
    ys = jnp.concatenate([r[...] for r in ys_refs], axis=0)
    nb, nbp = h_ref.shape[0], ys.shape[0]
    slot = lax.broadcasted_iota(jnp.int32, (nb, nbp), 1).astype(F32)
    perm_t = (dest_ref[...] == slot).astype(BF16)
    o_ref[...] = _rms(h_ref[...] + _dot(perm_t, ys)) * g_ref[...]


def _ffn_slots(n_tiles):
    return -(-(n_tiles + N_EXPERT_GROUPS * (FFN_TILES - 1)) // FFN_TILES) * FFN_TILES


def _ffn_schedule(counts, nblk, tiles_per_blk):
    i32 = jnp.int32
    tiles = (counts + FFN_ROWS - 1) // FFN_ROWS
    first_rel = jnp.cumsum(tiles, axis=1) - tiles
    used_blk = jnp.sum(tiles, axis=1)
    grp_pad = (jnp.sum(tiles, axis=0) + FFN_TILES - 1) // FFN_TILES * FFN_TILES
    grp_start = jnp.cumsum(grp_pad) - grp_pad
    before = jnp.cumsum(tiles, axis=0) - tiles
    n_tiles = nblk * tiles_per_blk
    t = jnp.arange(n_tiles, dtype=i32)
    b, k = t // tiles_per_blk, t % tiles_per_blk
    used = k < used_blk[b]
    grp_t = jnp.minimum(jnp.sum(((first_rel + tiles)[b] <= k[:, None]).astype(i32), axis=1), N_EXPERT_GROUPS - 1)
    pos = grp_start[grp_t] + before[b, grp_t] + (k - first_rel[b, grp_t])
    s = jnp.arange(_ffn_slots(n_tiles), dtype=i32)
    hit = used[None, :] & (pos[None, :] == s[:, None])
    slot_has = jnp.sum(hit.astype(i32), axis=1) > 0
    free_rank = jnp.cumsum((~slot_has).astype(i32)) - 1
    unused_rank = jnp.cumsum((~used).astype(i32)) - 1
    fill = (~used)[None, :] & (unused_rank[None, :] == free_rank[:, None])
    placed = jnp.where(slot_has[:, None], hit, fill)
    slot_tile = jnp.sum(jnp.where(placed, t[None, :], 0), axis=1).astype(i32)
    tile_slot = jnp.sum(jnp.where(placed, s[:, None], 0), axis=0).astype(i32)
    slot_grp = jnp.sum(jnp.where(hit, grp_t[None, :], 0), axis=1).reshape(-1, FFN_TILES)
    step_real = jnp.sum(slot_has.astype(i32).reshape(-1, FFN_TILES), axis=1).astype(i32)
    n_real_steps = jnp.sum(grp_pad) // FFN_TILES
    last_grp = slot_grp[jnp.maximum(n_real_steps - 1, 0), 0]
    step_grp = jnp.where(step_real > 0, slot_grp[:, 0], last_grp).astype(i32)
    return slot_tile, tile_slot, step_grp, step_real


def _hier_moe_residual(h, norm_g, w_grp, b_grp, w_exp, b_exp, layer, w_gate, w_up, w_down, final_g=None):
    n, d = h.shape
    nb = MOE_BLOCK
    nblk = n // nb
    nbp = nb + N_EXPERT_GROUPS * FFN_ROWS
    tiles_per_blk = nbp // FFN_ROWS
    n_exp = N_EXPERT_GROUPS * EXPERTS_PER_GROUP
    w_router = jnp.pad(jnp.concatenate([w_grp, w_exp], axis=1), ((0, 0), (0, LANES - N_EXPERT_GROUPS - n_exp)))
    b_router = jnp.pad(jnp.concatenate([b_grp, b_exp]), (0, LANES - N_EXPERT_GROUPS - n_exp)).reshape(1, LANES)

    xs, gs, dest, counts = pl.pallas_call(
        _moe_route_kernel,
        grid=(nblk,),
        in_specs=[pl.BlockSpec((nb, d), lambda i: (i, 0)), _const_spec((1, d)),
                  _const_spec((d, LANES)), _const_spec((1, LANES))],
        out_specs=[pl.BlockSpec((nbp, d), lambda i: (i, 0)), pl.BlockSpec((nbp, LANES), lambda i: (i, 0)),
                   pl.BlockSpec((nb, 1), lambda i: (i, 0)), pl.BlockSpec((1, 1, LANES), lambda i: (i, 0, 0))],
        out_shape=[jax.ShapeDtypeStruct((nblk * nbp, d), BF16), jax.ShapeDtypeStruct((nblk * nbp, LANES), F32),
                   jax.ShapeDtypeStruct((n, 1), F32), jax.ShapeDtypeStruct((nblk, 1, LANES), F32)],
        compiler_params=_cparams("arbitrary"),
        name="moe_route",
    )(h, norm_g.reshape(1, d), w_router.astype(BF16), b_router)

    n_tiles = nblk * tiles_per_blk
    n_slots = _ffn_slots(n_tiles)
    slot_tile, tile_slot, step_grp, step_real = _ffn_schedule(
        counts[:, 0, :N_EXPERT_GROUPS].astype(jnp.int32), nblk, tiles_per_blk)

    def tile_spec(width, j):
        return pl.BlockSpec((FFN_ROWS, width), lambda i, tile, grp, real: (tile[i * FFN_TILES + j], 0))

    def wspec(shape):
        return pl.BlockSpec((None, EXPERTS_PER_GROUP) + shape, lambda i, tile, grp, real: (layer, grp[i], 0, 0))

    f = w_gate.shape[-1]
    ys = pl.pallas_call(
        _moe_ffn_kernel,
        grid_spec=pltpu.PrefetchScalarGridSpec(
            num_scalar_prefetch=3,
            grid=(n_slots // FFN_TILES,),
            in_specs=[tile_spec(d, j) for j in range(FFN_TILES)] + [tile_spec(LANES, j) for j in range(FFN_TILES)]
            + [wspec((d, f)), wspec((d, f)), wspec((f, d))],
            out_specs=pl.BlockSpec((FFN_TILES * FFN_ROWS, d), lambda i, tile, grp, real: (i, 0)),
        ),
        out_shape=jax.ShapeDtypeStruct((n_slots * FFN_ROWS, d), BF16),
        compiler_params=_cparams("arbitrary"),
        name="moe_ffn",
    )(slot_tile, step_grp, step_real, *([xs] * FFN_TILES), *([gs] * FFN_TILES),
      w_gate, w_up, w_down)

    blk = pl.BlockSpec((nb, d), lambda i, slot: (i, 0))
    in_specs = [blk, pl.BlockSpec((nb, 1), lambda i, slot: (i, 0))]
    in_specs += [pl.BlockSpec((FFN_ROWS, d), functools.partial(lambda j, i, slot: (slot[i * tiles_per_blk + j], 0), j))
                 for j in range(tiles_per_blk)]
    args = [h, dest] + [ys] * tiles_per_blk
    body = _moe_merge_kernel
    if final_g is not None:
        in_specs.append(pl.BlockSpec((1, d), lambda i, slot: (0, 0)))
        args.append(final_g.reshape(1, d))
        body = _moe_merge_norm_kernel
    return pl.pallas_call(
        body,
        grid_spec=pltpu.PrefetchScalarGridSpec(num_scalar_prefetch=1, grid=(nblk,), in_specs=in_specs, out_specs=blk),
        out_shape=jax.ShapeDtypeStruct((n, d), F32),
        compiler_params=_cparams("arbitrary"),
        name="moe_merge",
    )(tile_slot, *args)


GROUP_WIDTH = DIL_HEADS * DIL_HEAD_DIM
N_STRIDED = 3 * sum(1 for _, dil in DIL_PATTERNS if dil > 1)


def _qkv_kernel(h_ref, gq_ref, gkv_ref, wq_ref, wk_ref, wv_ref, *refs):
    outs, scratch = refs[:-1], refs[-1]
    n = _rms(h_ref[...])
    xq = (n * gq_ref[...]).astype(BF16)
    xkv = (n * gkv_ref[...]).astype(BF16)
    gw = GROUP_WIDTH
    slot = 0
    for gi, (_, dil) in enumerate(DIL_PATTERNS):
        cols = slice(gi * gw, (gi + 1) * gw)
        for t, (x, w_ref) in enumerate(((xq, wq_ref), (xkv, wk_ref), (xkv, wv_ref))):
            res = _dot(x, w_ref[:, cols])
            out_ref = outs[3 * gi + t]
            if dil == 1:
                out_ref[...] = res.astype(BF16)
                continue
            sub = res.shape[0] // dil
            for c in range(gw // LANES):
                scratch[slot, c] = res[:, c * LANES:(c + 1) * LANES]
            for r in range(dil):
                for c in range(gw // LANES):
                    lo = r * gw + c * LANES
                    out_ref[:, lo:lo + LANES] = scratch[slot, c, pl.ds(r, sub, stride=dil), :].astype(BF16)
            slot += 1


def _qkv_proj(h, g_q, g_kv, w_q, w_kv):
    n, d = h.shape
    qw = w_q.shape[1]
    gw = GROUP_WIDTH
    scale = math.log2(math.e) / math.sqrt(DIL_HEAD_DIM)
    rows = pl.BlockSpec((PROJ_ROWS, d), lambda i: (i, 0))
    out_specs, out_shape = [], []
    for _, dil in DIL_PATTERNS:
        for _ in range(3):
            out_specs.append(pl.BlockSpec((PROJ_ROWS // dil, dil * gw), lambda i: (i, 0)))
            out_shape.append(jax.ShapeDtypeStruct((n // dil, dil * gw), BF16))
    return pl.pallas_call(
        _qkv_kernel,
        grid=(n // PROJ_ROWS,),
        in_specs=[rows, _const_spec((1, d)), _const_spec((1, d))] + [_const_spec((d, qw))] * 3,
        out_specs=out_specs,
        out_shape=out_shape,
        scratch_shapes=[pltpu.VMEM((N_STRIDED, gw // LANES, PROJ_ROWS, LANES), F32)],
        compiler_params=_cparams("arbitrary"),
        name="qkv_proj",
    )(h, g_q.reshape(1, d), g_kv.reshape(1, d), (w_q * scale).astype(BF16),
      w_kv[:, :qw].astype(BF16), w_kv[:, qw:].astype(BF16))


def _attn_kernel(q_ref, kc_ref, kp_ref, vc_ref, vp_ref, o_ref, lse_ref):
    n = pl.program_id(2)
    blk = ATT_BLOCK
    pw = 2 * DIL_HEAD_DIM
    ri = lax.broadcasted_iota(jnp.int32, (blk, blk), 0)
    ci = lax.broadcasted_iota(jnp.int32, (blk, blk), 1)
    lowest = jnp.finfo(F32).min
    cap_prev = jnp.where(ci >= ri, jnp.inf, lowest).astype(F32)
    cap_first = jnp.where(n > 0, cap_prev, lowest)
    cap_own = jnp.where(ci <= ri, jnp.inf, lowest).astype(F32)
    head0 = lax.broadcasted_iota(jnp.int32, (blk, pw), 1) < DIL_HEAD_DIM
    zero = jnp.zeros((blk, pw), BF16)
    lane_t = lax.broadcasted_iota(jnp.int32, (blk, LANES), 1)

    def stack(x):
        return jnp.concatenate([jnp.where(head0, x, zero), jnp.where(head0, zero, x)], axis=0)

    streams = [(i, p) for i in range(q_ref.shape[1] // blk) for p in range(DIL_HEADS // 2)]

    def rows(i):
        return slice(i * blk, (i + 1) * blk)

    def lanes(p):
        return slice(p * pw, (p + 1) * pw)

    stacked = {}

    def kv_blocks(cur_ref, prev_ref, i, p):
        def one(w):
            name = (id(cur_ref), w, p)
            if name not in stacked:
                stacked[name] = stack(prev_ref[0, :, lanes(p)] if w < 0 else cur_ref[0, rows(w), lanes(p)])
            return stacked[name]
        return jnp.concatenate([one(i - 1), one(i)], axis=0)

    s = []
    for i, p in streams:
        raw = _dot_nt(q_ref[0, rows(i), lanes(p)], kv_blocks(kc_ref, kp_ref, i, p))
        cap = cap_first if i == 0 else cap_prev
        s.append([jnp.minimum(raw[:, :blk], cap), jnp.minimum(raw[:, blk:2 * blk], cap),
                  jnp.minimum(raw[:, 2 * blk:3 * blk], cap_own), jnp.minimum(raw[:, 3 * blk:], cap_own)])
    m_a = [jnp.max(jnp.maximum(x[0], x[2]), axis=-1, keepdims=True) for x in s]
    m_b = [jnp.max(jnp.maximum(x[1], x[3]), axis=-1, keepdims=True) for x in s]
    e = [[jnp.exp2(x[0] - ma), jnp.exp2(x[1] - mb), jnp.exp2(x[2] - ma), jnp.exp2(x[3] - mb)]
         for x, ma, mb in zip(s, m_a, m_b)]
    pb = [jnp.concatenate([y.astype(BF16) for y in x], axis=1) for x in e]
    blk_head = (lax.broadcasted_iota(jnp.int32, (4 * blk, LANES), 0) // blk) % 2
    sel_lane = lax.broadcasted_iota(jnp.int32, (4 * blk, LANES), 1) - DIL_HEADS
    sel = [(sel_lane == 2 * p + blk_head).astype(BF16) for p in range(DIL_HEADS // 2)]
    pv = [_dot(pb[idx], jnp.concatenate([kv_blocks(vc_ref, vp_ref, i, p), sel[p]], axis=1))
          for idx, (i, p) in enumerate(streams)]
    tiles = {}
    for idx, (i, p) in enumerate(streams):
        o_ref[0, rows(i), lanes(p)] = pv[idx][:, :pw].astype(BF16)
        tile = tiles.get(i, jnp.zeros((blk, LANES), F32)) + pv[idx][:, pw:]
        tiles[i] = jnp.where(lane_t == 2 * p, m_a[idx], jnp.where(lane_t == 2 * p + 1, m_b[idx], tile))
    for i, tile in tiles.items():
        lse_ref[0, rows(i), :] = tile


def _attn_group(q, k, v, gi, dilation, bsz, seq):
    gw = GROUP_WIDTH
    sub = seq // dilation
    windows = min(ATT_WINDOWS, sub // ATT_BLOCK)
    step_rows = windows * ATT_BLOCK
    assert sub % step_rows == 0
    view = (bsz, sub, dilation * gw)
    qv, kv_, vv = q.reshape(view), k.reshape(view), v.reshape(view)
    cur = pl.BlockSpec((1, step_rows, gw), lambda b, r, n: (b, n, r))
    prev = pl.BlockSpec((1, ATT_BLOCK, gw), lambda b, r, n: (b, jnp.maximum(n * windows - 1, 0), r))
    lse_spec = pl.BlockSpec((1, step_rows, LANES), lambda b, r, n: (b, n, r))
    o, lse = pl.pallas_call(
        _attn_kernel,
        grid=(bsz, dilation, sub // step_rows),
        in_specs=[cur, cur, prev, cur, prev],
        out_specs=[cur, lse_spec],
        out_shape=[jax.ShapeDtypeStruct(view, BF16),
                   jax.ShapeDtypeStruct((bsz, sub, dilation * LANES), F32)],
        compiler_params=_cparams("arbitrary", "arbitrary", "arbitrary"),
        name=f"attn_group{gi}",
    )(qv, kv_, kv_, vv, vv)
    return o.reshape(bsz * sub, dilation * gw), lse.reshape(bsz * sub, dilation * LANES)


def _attn_merge_kernel(o0_ref, o1_ref, o2_ref, l0_ref, l1_ref, l2_ref, w_ref, res_ref, out_ref, *scratch):
    rows = res_ref.shape[0]
    gw = GROUP_WIDTH
    outs, lses = [], []
    slot = 0
    for (_, dil), o_ref, l_ref in zip(DIL_PATTERNS, (o0_ref, o1_ref, o2_ref), (l0_ref, l1_ref, l2_ref)):
        if dil == 1:
            outs.append(o_ref[...].astype(F32))
            lses.append(l_ref[...])
            continue
        o_scr, l_scr = scratch[2 * slot], scratch[2 * slot + 1]
        sub = rows // dil
        for r in range(dil):
            for c in range(gw // LANES):
                lo = r * gw + c * LANES
                o_scr[c, pl.ds(r, sub, stride=dil), :] = o_ref[:, lo:lo + LANES].astype(F32)
            l_scr[pl.ds(r, sub, stride=dil), :] = l_ref[:, r * LANES:(r + 1) * LANES]
        outs.append(jnp.concatenate([o_scr[c] for c in range(gw // LANES)], axis=1))
        lses.append(l_scr[...])
        slot += 1
    m = jnp.maximum(jnp.maximum(lses[0], lses[1]), lses[2])
    e = [jnp.exp2(l - m) for l in lses]
    sums = [pltpu.roll(l, LANES - DIL_HEADS, 1) for l in lses]
    denom = e[0] * sums[0] + e[1] * sums[1] + e[2] * sums[2]
    is_head = lax.broadcasted_iota(jnp.int32, (rows, LANES), 1) < DIL_HEADS
    expand = (lax.broadcasted_iota(jnp.int32, (LANES, gw), 1) // DIL_HEAD_DIM
              == lax.broadcasted_iota(jnp.int32, (LANES, gw), 0)).astype(BF16)
    mix = jnp.zeros((rows, gw), F32)
    for g in range(len(outs)):
        alpha = jnp.where(is_head, e[g] / denom, 0.0)
        hi = alpha.astype(BF16)
        lo = (alpha - hi.astype(F32)).astype(BF16)
        mix = mix + (_dot(hi, expand) + _dot(lo, expand)) * outs[g]
    out_ref[...] = res_ref[...] + _dot(mix.astype(BF16), w_ref[...])


def _attn_merge(outs, lses, w_o, res):
    n, d = res.shape
    gw = GROUP_WIDTH
    o_specs = [pl.BlockSpec((PROJ_ROWS // dil, dil * gw), lambda i: (i, 0)) for _, dil in DIL_PATTERNS]
    l_specs = [pl.BlockSpec((PROJ_ROWS // dil, dil * LANES), lambda i: (i, 0)) for _, dil in DIL_PATTERNS]
    full = pl.BlockSpec((PROJ_ROWS, d), lambda i: (i, 0))
    scratch = []
    for _, dil in DIL_PATTERNS:
        if dil > 1:
            scratch += [pltpu.VMEM((gw // LANES, PROJ_ROWS, LANES), F32), pltpu.VMEM((PROJ_ROWS, LANES), F32)]
    return pl.pallas_call(
        _attn_merge_kernel,
        grid=(n // PROJ_ROWS,),
        in_specs=o_specs + l_specs + [_const_spec(w_o.shape), full],
        out_specs=full,
        out_shape=jax.ShapeDtypeStruct((n, d), F32),
        scratch_shapes=scratch,
        compiler_params=_cparams("arbitrary"),
        name="attn_merge",
    )(*outs, *lses, w_o.astype(BF16), res)


def kernel(x, norm_mix_g, norm_ffn_g, rwkv_mu, rwkv_w_r, rwkv_w_k, rwkv_w_v, rwkv_w0, rwkv_w1, rwkv_w2,
           rwkv_a0, rwkv_a1, rwkv_a2, rwkv_g1, rwkv_g2, rwkv_k_k, rwkv_k_a, rwkv_r_k, rwkv_ln_w, rwkv_ln_b,
           rwkv_w_o, kv_norm_g, w_kv, attn_w_q, attn_w_o, moe_w_grp, moe_b_grp, moe_w_exp, moe_b_exp,
           moe_w_gate, moe_w_up, moe_w_down, final_norm_g):
    bsz, seq, d = x.shape
    n = bsz * seq
    depth = norm_mix_g.shape[0]
    n_rwkv = rwkv_mu.shape[0]
    h = x.reshape(n, d)
    expert_w = (moe_w_gate.astype(BF16), moe_w_up.astype(BF16), moe_w_down.astype(BF16))
    q = k_sh = v_sh = None
    for layer in range(depth):
        if layer < n_rwkv:
            i = layer
            r, k, v, a, lw, g = _rwkv_pre(h.reshape(bsz, seq, d), norm_mix_g[layer], rwkv_mu[i], rwkv_w0[i], rwkv_a0[i],
                                          rwkv_w_r[i], rwkv_w_k[i], rwkv_w_v[i], rwkv_w1[i], rwkv_w2[i],
                                          rwkv_a1[i], rwkv_a2[i], rwkv_g1[i], rwkv_g2[i])
            y = _rwkv_rec(r, k, v, a, lw, g, rwkv_k_k[i], rwkv_k_a[i], rwkv_r_k[i], rwkv_ln_w[i], rwkv_ln_b[i])
            h = _proj_res(y.reshape(n, d), rwkv_w_o[i], h)
        else:
            i = layer - n_rwkv
            qkv = _qkv_proj(h, norm_mix_g[layer], kv_norm_g, attn_w_q[i], w_kv)
            q = qkv[0::3]
            if i == 0:
                k_sh, v_sh = qkv[1::3], qkv[2::3]
            outs, lses = [], []
            for gi, (window, dilation) in enumerate(DIL_PATTERNS):
                assert window // dilation == ATT_BLOCK and seq % window == 0
                o, lse = _attn_group(q[gi], k_sh[gi], v_sh[gi], gi, dilation, bsz, seq)
                outs.append(o)
                lses.append(lse)
            h = _attn_merge(outs, lses, attn_w_o[i], h)
        last = layer == depth - 1
        h = _hier_moe_residual(h, norm_ffn_g[layer], moe_w_grp[layer], moe_b_grp[layer], moe_w_exp[layer],
                               moe_b_exp[layer], layer, *expert_w, final_g=final_norm_g if last else None)
    return h.reshape(bsz, seq, d)
```

```python
import functools
import math

import jax
import jax.numpy as jnp
from jax import lax
from jax.experimental import pallas as pl
from jax.experimental.pallas import tpu as pltpu

F32 = jnp.float32
BF16 = jnp.bfloat16
HIGHEST = lax.Precision.HIGHEST

NORM_EPS = 1e-6
RWKV_HEAD = 64
GN_EPS = RWKV_HEAD * 1e-5
DIL_PATTERNS = ((128, 1), (512, 4), (2048, 16))
DIL_HEADS = 8
DIL_HEAD_DIM = 64
N_EXPERT_GROUPS = 4
EXPERTS_PER_GROUP = 4
LANES = 128
VMEM_LIMIT = 56 * 1024 * 1024

CHUNK = 64
REC_ROWS = 2048
REC_PAIRS = 2
PRE_ROWS = 512
PROJ_ROWS = 512
MOE_BLOCK = 1024
FFN_ROWS = 64
FFN_TILES = 8
ATT_BLOCK = 128
ATT_WINDOWS = 4


def _cparams(*sem):
    return pltpu.CompilerParams(dimension_semantics=sem, vmem_limit_bytes=VMEM_LIMIT)


def _dot(a, b):
    return jnp.dot(a, b, preferred_element_type=F32)


def _dot_nt(a, b):
    return lax.dot_general(a, b, (((1,), (1,)), ((), ())), preferred_element_type=F32)


def _dot_tn(a, b, precision=None):
    return lax.dot_general(a, b, (((0,), (0,)), ((), ())), preferred_element_type=F32, precision=precision)


def _rms(x):
    return x * lax.rsqrt(jnp.mean(x * x, axis=-1, keepdims=True) + NORM_EPS)


def _sigmoid(z):
    return 1.0 / (1.0 + jnp.exp(-z))


def _const_spec(shape):
    nd = len(shape)
    return pl.BlockSpec(shape, lambda *_: (0,) * nd)


def _rwkv_pre_kernel(x_ref, gmix_ref, mu_ref, w0_ref, a0_ref, wr_ref, wk_ref, wv_ref,
                     w1_ref, w2_ref, a1_ref, a2_ref, g1_ref, g2_ref,
                     r_ref, k_ref, v_ref, a_ref, lw_ref, g_ref, prev_ref):
    @pl.when(pl.program_id(1) == 0)
    def _():
        prev_ref[...] = jnp.zeros_like(prev_ref)

    x = x_ref[0]
    rows = x.shape[0]
    xn = _rms(x) * gmix_ref[...]
    row = lax.broadcasted_iota(jnp.int32, xn.shape, 0)
    shifted = jnp.where(row == 0, prev_ref[7:8, :], pltpu.roll(xn, 1, 0))
    prev_ref[...] = xn[rows - 8:, :]
    xx = shifted - xn

    def mix(i):
        return (xn + xx * mu_ref[i:i + 1, :]).astype(BF16)

    r_ref[0] = _dot(mix(0), wr_ref[...]).astype(BF16)
    k_ref[0] = _dot(mix(2), wk_ref[...]).astype(BF16)
    v_ref[0] = _dot(mix(3), wv_ref[...]).astype(BF16)
    u = w0_ref[...] + _dot(jnp.tanh(_dot(mix(1), w1_ref[...])).astype(BF16), w2_ref[...])
    w_log = -(jnp.maximum(-u, 0.0) + jnp.log(1.0 + jnp.exp(-jnp.abs(u)))) - 0.5
    lw_ref[0] = -jnp.exp(w_log)
    a_ref[0] = _sigmoid(a0_ref[...] + _dot(_dot(mix(4), a1_ref[...]).astype(BF16), a2_ref[...])).astype(BF16)
    g_ref[0] = _dot(_sigmoid(_dot(mix(5), g1_ref[...])).astype(BF16), g2_ref[...]).astype(BF16)


def _pad_cols(w, n):
    return jnp.pad(w, ((0, 0), (0, n - w.shape[1])))


def _pad_rows(w, n):
    return jnp.pad(w, ((0, n - w.shape[0]), (0, 0)))


def _rwkv_pre(x, gmix, mu, w0, a0, w_r, w_k, w_v, w1, w2, a1, a2, g1, g2):
    bsz, seq, d = x.shape
    lw = -(-w1.shape[1] // LANES) * LANES
    la = -(-a1.shape[1] // LANES) * LANES
    lg = -(-g1.shape[1] // LANES) * LANES
    mu8 = jnp.pad(mu, ((0, 8 - mu.shape[0]), (0, 0)))
    args = (x, gmix.reshape(1, d), mu8, w0.reshape(1, d), a0.reshape(1, d),
            w_r.astype(BF16), w_k.astype(BF16), w_v.astype(BF16),
            _pad_cols(w1, lw).astype(BF16), _pad_rows(w2, lw).astype(BF16),
            _pad_cols(a1, la).astype(BF16), _pad_rows(a2, la).astype(BF16),
            _pad_cols(g1, lg).astype(BF16), _pad_rows(g2, lg).astype(BF16))
    tile = pl.BlockSpec((1, PRE_ROWS, d), lambda b, s: (b, s, 0))
    in_specs = [tile] + [_const_spec(a.shape) for a in args[1:]]
    out_dt = (BF16, BF16, BF16, BF16, F32, BF16)
    return pl.pallas_call(
        _rwkv_pre_kernel,
        grid=(bsz, seq // PRE_ROWS),
        in_specs=in_specs,
        out_specs=[tile] * 6,
        out_shape=[jax.ShapeDtypeStruct((bsz, seq, d), t) for t in out_dt],
        scratch_shapes=[pltpu.VMEM((8, d), F32)],
        compiler_params=_cparams("arbitrary", "arbitrary"),
        name="rwkv_pre",
    )(*args)


def _rwkv_rec_kernel(r_ref, k_ref, v_ref, a_ref, lw_ref, g_ref, kk_ref, ka_ref, rk_ref, lnw_ref, lnb_ref,
                     o_ref, state_ref):
    @pl.when(pl.program_id(2) == 0)
    def _():
        state_ref[...] = jnp.zeros_like(state_ref)

    c = CHUNK
    hd = RWKV_HEAD
    w = 2 * hd
    assert c == hd
    n = r_ref.shape[1] // c
    lane = lax.broadcasted_iota(jnp.int32, (c, w), 1)
    row = lax.broadcasted_iota(jnp.int32, (c, w), 0)
    head0 = lane < hd
    col = jnp.where(head0, lane, lane - hd)
    strict = row > col
    incl = row >= col
    eye = (row == col).astype(F32)
    tri = (lax.broadcasted_iota(jnp.int32, (c, c), 0) >= lax.broadcasted_iota(jnp.int32, (c, c), 1)).astype(BF16)
    pairs = r_ref.shape[2] // w

    def head_sum(x):
        s0 = jnp.sum(jnp.where(head0, x, 0.0), axis=-1, keepdims=True)
        s1 = jnp.sum(jnp.where(head0, 0.0, x), axis=-1, keepdims=True)
        return jnp.where(head0, s0, s1)

    def stack(x):
        return jnp.concatenate([jnp.where(head0, x, 0.0), jnp.where(head0, 0.0, x)], axis=0)

    def cumsum(x):
        hi = x.astype(BF16)
        lo = (x - hi.astype(F32)).astype(BF16)
        both = _dot(tri, jnp.concatenate([hi, lo], axis=1))
        return both[:, :w] + both[:, w:]

    zeros = jnp.zeros((c, w), BF16)
    carried = [state_ref[p] for p in range(pairs)]

    def chunk(j, p):
        rs = slice(j * c, (j + 1) * c)
        ls = slice(p * w, (p + 1) * w)
        kkp, kap, rkp, lnw, lnb = kk_ref[:, ls], ka_ref[:, ls], rk_ref[:, ls], lnw_ref[:, ls], lnb_ref[:, ls]
        r_ = r_ref[0, rs, ls].astype(F32)
        k_ = k_ref[0, rs, ls].astype(F32)
        v_ = v_ref[0, rs, ls].astype(F32)
        a_ = a_ref[0, rs, ls].astype(F32)
        lw_ = lw_ref[0, rs, ls]
        cum = cumsum(lw_)
        yield
        kk = k_ * kkp
        kk = kk * lax.rsqrt(jnp.maximum(head_sum(kk * kk), 1e-24))
        k2 = k_ * (1.0 + (a_ - 1.0) * kap)
        p_ = -(kk * a_)
        tot = cum[c - 1:c, :]
        e_neg = jnp.exp(-cum)
        e_tot = jnp.exp(tot - cum)
        rt = r_ * jnp.exp(cum)
        qt = kk * jnp.exp(cum - lw_)
        vb = v_.astype(BF16)
        amat = _dot_nt(jnp.concatenate([qt, rt], axis=0).astype(BF16),
                       jnp.concatenate([stack(k2 * e_neg), stack(p_ * e_neg)], axis=0).astype(BF16))
        yield
        a_qp = jnp.where(strict, amat[:c, w:], 0.0)
        a_rp = jnp.where(incl, amat[c:, w:], 0.0).astype(BF16)
        a_k = jnp.concatenate([jnp.where(strict, amat[:c, :w], 0.0),
                               jnp.where(incl, amat[c:, :w], 0.0)], axis=0).astype(BF16)
        av = _dot(a_k, stack(vb))
        inv = eye + a_qp
        power = a_qp.astype(BF16)
        power = _dot(power, stack(power))
        yield
        steps = int(math.log2(c)) - 1
        for i in range(steps):
            pb = power.astype(BF16)
            if i + 1 < steps:
                both = _dot(jnp.concatenate([inv.astype(BF16), pb], axis=0), stack(pb))
                inv = inv + both[:c]
                power = both[c:]
            else:
                inv = inv + _dot(inv.astype(BF16), stack(pb))
            yield
        wu = _dot(inv.astype(BF16),
                  jnp.concatenate([stack(qt.astype(BF16)), stack(av[:c].astype(BF16))], axis=1))
        wub = wu.astype(BF16)
        yield
        ry = _dot(a_rp, jnp.concatenate([stack(wub[:, :w]), stack(wub[:, w:])], axis=1))
        mg = _dot_tn(jnp.concatenate([p_ * e_tot, k2 * e_tot], axis=0).astype(BF16),
                     jnp.concatenate([wub, jnp.concatenate([zeros, vb], axis=1)], axis=0))
        yield
        rw = (rt + ry[:, :w]).astype(BF16)
        m_w = jnp.where(head0, mg[:c, :w], mg[c:, :w]).astype(BF16)
        g_w = jnp.where(head0, mg[:c, w:], mg[c:, w:])
        gam = head_sum(eye * jnp.exp(tot))
        state = carried[p]
        both = _dot(jnp.concatenate([rw, m_w], axis=0), stack(state.astype(BF16)))
        y = both[:c] + av[c:] + ry[:, w:]
        carried[p] = gam * state + both[c:] + g_w
        yield
        yc = y - head_sum(y) * (1.0 / hd)
        var = head_sum(yc * yc) * (1.0 / hd)
        yn = yc * lax.rsqrt(var + GN_EPS) * lnw + lnb
        bonus = head_sum(r_ * k2 * rkp) * v_
        o_ref[0, rs, ls] = ((yn + bonus) * g_ref[0, rs, ls].astype(F32)).astype(BF16)

    live, started = [], 0
    while started < n or live:
        if started < n:
            live.extend(chunk(started, p) for p in range(pairs))
            started += 1
        for gen in list(live):
            if next(gen, "done") == "done":
                live.remove(gen)
    for p in range(pairs):
        state_ref[p] = carried[p]


def _rwkv_rec(r, k, v, a, lw, g, k_k, k_a, r_k, ln_w, ln_b):
    bsz, seq, d = r.shape
    pair_w = 2 * RWKV_HEAD
    hw = REC_PAIRS * pair_w
    tile = pl.BlockSpec((1, REC_ROWS, hw), lambda b, h, s: (b, s, h))
    par = pl.BlockSpec((1, hw), lambda b, h, s: (0, h))
    params = [p.reshape(1, d).astype(F32) for p in (k_k, k_a, r_k, ln_w, ln_b)]
    return pl.pallas_call(
        _rwkv_rec_kernel,
        grid=(bsz, d // hw, seq // REC_ROWS),
        in_specs=[tile] * 6 + [par] * 5,
        out_specs=tile,
        out_shape=jax.ShapeDtypeStruct((bsz, seq, d), BF16),
        scratch_shapes=[pltpu.VMEM((REC_PAIRS, CHUNK, pair_w), F32)],
        compiler_params=_cparams("arbitrary", "arbitrary", "arbitrary"),
        name="rwkv_rec",
    )(r, k, v, a, lw, g, *params)


def _proj_res_kernel(a_ref, w_ref, res_ref, o_ref):
    o_ref[...] = res_ref[...] + _dot(a_ref[...], w_ref[...])


def _proj_res(a, w, res):
    n, kdim = a.shape
    d = w.shape[1]
    return pl.pallas_call(
        _proj_res_kernel,
        grid=(n // PROJ_ROWS,),
        in_specs=[pl.BlockSpec((PROJ_ROWS, kdim), lambda i: (i, 0)), _const_spec(w.shape),
                  pl.BlockSpec((PROJ_ROWS, d), lambda i: (i, 0))],
        out_specs=pl.BlockSpec((PROJ_ROWS, d), lambda i: (i, 0)),
        out_shape=jax.ShapeDtypeStruct((n, d), F32),
        compiler_params=_cparams("arbitrary"),
        name="proj_res",
    )(a, w.astype(BF16), res)


def _moe_route_kernel(h_ref, g_ref, wr_ref, br_ref, xs_ref, gs_ref, dest_ref, cnt_ref):
    nb = h_ref.shape[0]
    nbp = xs_ref.shape[0]
    tb = (_rms(h_ref[...]) * g_ref[...]).astype(BF16)
    logits = _dot(tb, wr_ref[...]) + br_ref[...]
    lane = lax.broadcasted_iota(jnp.int32, logits.shape, 1).astype(F32)
    neg = jnp.float32(-jnp.inf)
    big = jnp.float32(LANES)
    is_grp = lane < N_EXPERT_GROUPS
    gl = jnp.where(is_grp, logits, neg)
    gmax = jnp.max(gl, axis=-1, keepdims=True)
    grp = jnp.min(jnp.where(gl == gmax, lane, big), axis=-1, keepdims=True)
    p_grp = 1.0 / jnp.sum(jnp.where(is_grp, jnp.exp(gl - gmax), 0.0), axis=-1, keepdims=True)
    e_lo = N_EXPERT_GROUPS + grp * EXPERTS_PER_GROUP
    in_grp = (lane >= e_lo) & (lane < e_lo + EXPERTS_PER_GROUP)
    el = jnp.where(in_grp, logits, neg)
    v1 = jnp.max(el, axis=-1, keepdims=True)
    i1 = jnp.min(jnp.where(el == v1, lane, big), axis=-1, keepdims=True)
    el2 = jnp.where(lane == i1, neg, el)
    v2 = jnp.max(el2, axis=-1, keepdims=True)
    i2 = jnp.min(jnp.where(el2 == v2, lane, big), axis=-1, keepdims=True)
    e2 = jnp.exp(v2 - v1)
    w1 = 1.0 / (1.0 + e2)
    w2 = e2 / (1.0 + e2)
    gates = jnp.where(lane == i1, w1, jnp.where(lane == i2, w2, 0.0)) * p_grp

    onehot = (lane == grp).astype(BF16)
    ri = lax.broadcasted_iota(jnp.int32, (nb, nb), 0)
    ci = lax.broadcasted_iota(jnp.int32, (nb, nb), 1)
    rank = _dot((ri > ci).astype(BF16), onehot)
    counts = jnp.sum(onehot.astype(F32), axis=0, keepdims=True)
    padded = jnp.ceil(counts / FFN_ROWS) * FFN_ROWS
    lane1 = lax.broadcasted_iota(jnp.int32, (1, LANES), 1)
    offs = jnp.zeros((1, LANES), F32)
    for gidx in range(1, N_EXPERT_GROUPS):
        prev = jnp.sum(jnp.where(lane1 < gidx, padded, 0.0), axis=-1, keepdims=True)
        offs = jnp.where(lane1 == gidx, prev, offs)
    dest = jnp.sum(onehot.astype(F32) * (rank + offs), axis=-1, keepdims=True)
    slot = lax.broadcasted_iota(jnp.int32, (nb, nbp), 1).astype(F32)
    perm_t = (dest == slot).astype(BF16)
    xs_ref[...] = _dot_tn(perm_t, tb).astype(BF16)
    g_hi = gates.astype(BF16)
    g_lo = (gates - g_hi.astype(F32)).astype(BF16)
    moved = _dot_tn(perm_t, jnp.concatenate([g_hi, g_lo], axis=1))
    gs_ref[...] = moved[:, :LANES] + moved[:, LANES:]
    dest_ref[...] = dest
    cnt_ref[0] = jnp.where(lane1 < N_EXPERT_GROUPS, counts, 0.0)


def _moe_ffn_kernel(tile_ref, grp_ref, real_ref, *refs):
    nt = FFN_TILES
    xs_refs, gs_refs = refs[:nt], refs[nt:2 * nt]
    wg_ref, wu_ref, wd_ref, ys_ref = refs[2 * nt:]
    i = pl.program_id(0)

    @pl.when(real_ref[i] > 0)
    def _():
        x = jnp.concatenate([r[...] for r in xs_refs], axis=0)
        gates = jnp.concatenate([r[...] for r in gs_refs], axis=0)
        lane = lax.broadcasted_iota(jnp.int32, gates.shape, 1)
        base = N_EXPERT_GROUPS + grp_ref[i] * EXPERTS_PER_GROUP
        acc = jnp.zeros(ys_ref.shape, F32)
        for e in range(EXPERTS_PER_GROUP):
            ge = jnp.sum(jnp.where(lane == base + e, gates, 0.0), axis=-1, keepdims=True)
            gate_act = _dot(x, wg_ref[e])
            hdn = gate_act * _sigmoid(gate_act) * _dot(x, wu_ref[e])
            acc = acc + ge * _dot(hdn.astype(BF16), wd_ref[e])
        ys_ref[...] = acc.astype(BF16)

    @pl.when(real_ref[i] == 0)
    def _():
        ys_ref[...] = jnp.zeros_like(ys_ref)


def _moe_merge_kernel(slot_ref, h_ref, dest_ref, *refs):
    ys_refs, o_ref = refs[:-1], refs[-1]
    ys = jnp.concatenate([r[...] for r in ys_refs], axis=0)
    nb, nbp = h_ref.shape[0], ys.shape[0]
    slot = lax.broadcasted_iota(jnp.int32, (nb, nbp), 1).astype(F32)
    perm_t = (dest_ref[...] == slot).astype(BF16)
    o_ref[...] = h_ref[...] + _dot(perm_t, ys)


def _moe_merge_norm_kernel(slot_ref, h_ref, dest_ref, *refs):
    ys_refs, g_ref, o_ref = refs[:-2], refs[-2], refs[-1]
    ys = jnp.concatenate([r[...] for r in ys_refs], axis=0)
    nb, nbp = h_ref.shape[0], ys.shape[0]
    slot = lax.broadcasted_iota(jnp.int32, (nb, nbp), 1).astype(F32)
    perm_t = (dest_ref[...] == slot).astype(BF16)
    o_ref[...] = _rms(h_ref[...] + _dot(perm_t, ys)) * g_ref[...]


def _ffn_slots(n_tiles):
    return -(-(n_tiles + N_EXPERT_GROUPS * (FFN_TILES - 1)) // FFN_TILES) * FFN_TILES


def _ffn_schedule(counts, nblk, tiles_per_blk):
    i32 = jnp.int32
    tiles = (counts + FFN_ROWS - 1) // FFN_ROWS
    first_rel = jnp.cumsum(tiles, axis=1) - tiles
    used_blk = jnp.sum(tiles, axis=1)
    grp_pad = (jnp.sum(tiles, axis=0) + FFN_TILES - 1) // FFN_TILES * FFN_TILES
    grp_start = jnp.cumsum(grp_pad) - grp_pad
    before = jnp.cumsum(tiles, axis=0) - tiles
    n_tiles = nblk * tiles_per_blk
    t = jnp.arange(n_tiles, dtype=i32)
    b, k = t // tiles_per_blk, t % tiles_per_blk
    used = k < used_blk[b]
    grp_t = jnp.minimum(jnp.sum(((first_rel + tiles)[b] <= k[:, None]).astype(i32), axis=1), N_EXPERT_GROUPS - 1)
    pos = grp_start[grp_t] + before[b, grp_t] + (k - first_rel[b, grp_t])
    s = jnp.arange(_ffn_slots(n_tiles), dtype=i32)
    hit = used[None, :] & (pos[None, :] == s[:, None])
    slot_has = jnp.sum(hit.astype(i32), axis=1) > 0
    free_rank = jnp.cumsum((~slot_has).astype(i32)) - 1
    unused_rank = jnp.cumsum((~used).astype(i32)) - 1
    fill = (~used)[None, :] & (unused_rank[None, :] == free_rank[:, None])
    placed = jnp.where(slot_has[:, None], hit, fill)
    slot_tile = jnp.sum(jnp.where(placed, t[None, :], 0), axis=1).astype(i32)
    tile_slot = jnp.sum(jnp.where(placed, s[:, None], 0), axis=0).astype(i32)
    slot_grp = jnp.sum(jnp.where(hit, grp_t[None, :], 0), axis=1).reshape(-1, FFN_TILES)
    step_real = jnp.sum(slot_has.astype(i32).reshape(-1, FFN_TILES), axis=1).astype(i32)
    n_real_steps = jnp.sum(grp_pad) // FFN_TILES
    last_grp = slot_grp[jnp.maximum(n_real_steps - 1, 0), 0]
    step_grp = jnp.where(step_real > 0, slot_grp[:, 0], last_grp).astype(i32)
    return slot_tile, tile_slot, step_grp, step_real


def _hier_moe_residual(h, norm_g, w_grp, b_grp, w_exp, b_exp, layer, w_gate, w_up, w_down, final_g=None):
    n, d = h.shape
    nb = MOE_BLOCK
    nblk = n // nb
    nbp = nb + N_EXPERT_GROUPS * FFN_ROWS
    tiles_per_blk = nbp // FFN_ROWS
    n_exp = N_EXPERT_GROUPS * EXPERTS_PER_GROUP
    w_router = jnp.pad(jnp.concatenate([w_grp, w_exp], axis=1), ((0, 0), (0, LANES - N_EXPERT_GROUPS - n_exp)))
    b_router = jnp.pad(jnp.concatenate([b_grp, b_exp]), (0, LANES - N_EXPERT_GROUPS - n_exp)).reshape(1, LANES)

    xs, gs, dest, counts = pl.pallas_call(
        _moe_route_kernel,
        grid=(nblk,),
        in_specs=[pl.BlockSpec((nb, d), lambda i: (i, 0)), _const_spec((1, d)),
                  _const_spec((d, LANES)), _const_spec((1, LANES))],
        out_specs=[pl.BlockSpec((nbp, d), lambda i: (i, 0)), pl.BlockSpec((nbp, LANES), lambda i: (i, 0)),
                   pl.BlockSpec((nb, 1), lambda i: (i, 0)), pl.BlockSpec((1, 1, LANES), lambda i: (i, 0, 0))],
        out_shape=[jax.ShapeDtypeStruct((nblk * nbp, d), BF16), jax.ShapeDtypeStruct((nblk * nbp, LANES), F32),
                   jax.ShapeDtypeStruct((n, 1), F32), jax.ShapeDtypeStruct((nblk, 1, LANES), F32)],
        compiler_params=_cparams("arbitrary"),
        name="moe_route",
    )(h, norm_g.reshape(1, d), w_router.astype(BF16), b_router)

    n_tiles = nblk * tiles_per_blk
    n_slots = _ffn_slots(n_tiles)
    slot_tile, tile_slot, step_grp, step_real = _ffn_schedule(
        counts[:, 0, :N_EXPERT_GROUPS].astype(jnp.int32), nblk, tiles_per_blk)

    def tile_spec(width, j):
        return pl.BlockSpec((FFN_ROWS, width), lambda i, tile, grp, real: (tile[i * FFN_TILES + j], 0))

    def wspec(shape):
        return pl.BlockSpec((None, EXPERTS_PER_GROUP) + shape, lambda i, tile, grp, real: (layer, grp[i], 0, 0))

    f = w_gate.shape[-1]
    ys = pl.pallas_call(
        _moe_ffn_kernel,
        grid_spec=pltpu.PrefetchScalarGridSpec(
            num_scalar_prefetch=3,
            grid=(n_slots // FFN_TILES,),
            in_specs=[tile_spec(d, j) for j in range(FFN_TILES)] + [tile_spec(LANES, j) for j in range(FFN_TILES)]
            + [wspec((d, f)), wspec((d, f)), wspec((f, d))],
            out_specs=pl.BlockSpec((FFN_TILES * FFN_ROWS, d), lambda i, tile, grp, real: (i, 0)),
        ),
        out_shape=jax.ShapeDtypeStruct((n_slots * FFN_ROWS, d), BF16),
        compiler_params=_cparams("arbitrary"),
        name="moe_ffn",
    )(slot_tile, step_grp, step_real, *([xs] * FFN_TILES), *([gs] * FFN_TILES),
      w_gate, w_up, w_down)

    blk = pl.BlockSpec((nb, d), lambda i, slot: (i, 0))
    in_specs = [blk, pl.BlockSpec((nb, 1), lambda i, slot: (i, 0))]
    in_specs += [pl.BlockSpec((FFN_ROWS, d), functools.partial(lambda j, i, slot: (slot[i * tiles_per_blk + j], 0), j))
                 for j in range(tiles_per_blk)]
    args = [h, dest] + [ys] * tiles_per_blk
    body = _moe_merge_kernel
    if final_g is not None:
        in_specs.append(pl.BlockSpec((1, d), lambda i, slot: (0, 0)))
        args.append(final_g.reshape(1, d))
        body = _moe_merge_norm_kernel
    return pl.pallas_call(
        body,
        grid_spec=pltpu.PrefetchScalarGridSpec(num_scalar_prefetch=1, grid=(nblk,), in_specs=in_specs, out_specs=blk),
        out_shape=jax.ShapeDtypeStruct((n, d), F32),
        compiler_params=_cparams("arbitrary"),
        name="moe_merge",
    )(tile_slot, *args)


GROUP_WIDTH = DIL_HEADS * DIL_HEAD_DIM
N_STRIDED = 3 * sum(1 for _, dil in DIL_PATTERNS if dil > 1)


def _qkv_kernel(h_ref, gq_ref, gkv_ref, wq_ref, wk_ref, wv_ref, *refs):
    outs, scratch = refs[:-1], refs[-1]
    n = _rms(h_ref[...])
    xq = (n * gq_ref[...]).astype(BF16)
    xkv = (n * gkv_ref[...]).astype(BF16)
    gw = GROUP_WIDTH
    slot = 0
    for gi, (_, dil) in enumerate(DIL_PATTERNS):
        cols = slice(gi * gw, (gi + 1) * gw)
        for t, (x, w_ref) in enumerate(((xq, wq_ref), (xkv, wk_ref), (xkv, wv_ref))):
            res = _dot(x, w_ref[:, cols])
            out_ref = outs[3 * gi + t]
            if dil == 1:
                out_ref[...] = res.astype(BF16)
                continue
            sub = res.shape[0] // dil
            for c in range(gw // LANES):
                scratch[slot, c] = res[:, c * LANES:(c + 1) * LANES]
            for r in range(dil):
                for c in range(gw // LANES):
                    lo = r * gw + c * LANES
                    out_ref[:, lo:lo + LANES] = scratch[slot, c, pl.ds(r, sub, stride=dil), :].astype(BF16)
            slot += 1


def _qkv_proj(h, g_q, g_kv, w_q, w_kv):
    n, d = h.shape
    qw = w_q.shape[1]
    gw = GROUP_WIDTH
    scale = math.log2(math.e) / math.sqrt(DIL_HEAD_DIM)
    rows = pl.BlockSpec((PROJ_ROWS, d), lambda i: (i, 0))
    out_specs, out_shape = [], []
    for _, dil in DIL_PATTERNS:
        for _ in range(3):
            out_specs.append(pl.BlockSpec((PROJ_ROWS // dil, dil * gw), lambda i: (i, 0)))
            out_shape.append(jax.ShapeDtypeStruct((n // dil, dil * gw), BF16))
    return pl.pallas_call(
        _qkv_kernel,
        grid=(n // PROJ_ROWS,),
        in_specs=[rows, _const_spec((1, d)), _const_spec((1, d))] + [_const_spec((d, qw))] * 3,
        out_specs=out_specs,
        out_shape=out_shape,
        scratch_shapes=[pltpu.VMEM((N_STRIDED, gw // LANES, PROJ_ROWS, LANES), F32)],
        compiler_params=_cparams("arbitrary"),
        name="qkv_proj",
    )(h, g_q.reshape(1, d), g_kv.reshape(1, d), (w_q * scale).astype(BF16),
      w_kv[:, :qw].astype(BF16), w_kv[:, qw:].astype(BF16))


def _attn_kernel(q_ref, kc_ref, kp_ref, vc_ref, vp_ref, o_ref, lse_ref):
    n = pl.program_id(2)
    blk = ATT_BLOCK
    pw = 2 * DIL_HEAD_DIM
    ri = lax.broadcasted_iota(jnp.int32, (blk, blk), 0)
    ci = lax.broadcasted_iota(jnp.int32, (blk, blk), 1)
    lowest = jnp.finfo(F32).min
    cap_prev = jnp.where(ci >= ri, jnp.inf, lowest).astype(F32)
    cap_first = jnp.where(n > 0, cap_prev, lowest)
    cap_own = jnp.where(ci <= ri, jnp.inf, lowest).astype(F32)
    head0 = lax.broadcasted_iota(jnp.int32, (blk, pw), 1) < DIL_HEAD_DIM
    zero = jnp.zeros((blk, pw), BF16)
    lane_t = lax.broadcasted_iota(jnp.int32, (blk, LANES), 1)

    def stack(x):
        return jnp.concatenate([jnp.where(head0, x, zero), jnp.where(head0, zero, x)], axis=0)

    streams = [(i, p) for i in range(q_ref.shape[1] // blk) for p in range(DIL_HEADS // 2)]

    def rows(i):
        return slice(i * blk, (i + 1) * blk)

    def lanes(p):
        return slice(p * pw, (p + 1) * pw)

    stacked = {}

    def kv_blocks(cur_ref, prev_ref, i, p):
        def one(w):
            name = (id(cur_ref), w, p)
            if name not in stacked:
                stacked[name] = stack(prev_ref[0, :, lanes(p)] if w < 0 else cur_ref[0, rows(w), lanes(p)])
            return stacked[name]
        return jnp.concatenate([one(i - 1), one(i)], axis=0)

    s = []
    for i, p in streams:
        raw = _dot_nt(q_ref[0, rows(i), lanes(p)], kv_blocks(kc_ref, kp_ref, i, p))
        cap = cap_first if i == 0 else cap_prev
        s.append([jnp.minimum(raw[:, :blk], cap), jnp.minimum(raw[:, blk:2 * blk], cap),
                  jnp.minimum(raw[:, 2 * blk:3 * blk], cap_own), jnp.minimum(raw[:, 3 * blk:], cap_own)])
    m_a = [jnp.max(jnp.maximum(x[0], x[2]), axis=-1, keepdims=True) for x in s]
    m_b = [jnp.max(jnp.maximum(x[1], x[3]), axis=-1, keepdims=True) for x in s]
    e = [[jnp.exp2(x[0] - ma), jnp.exp2(x[1] - mb), jnp.exp2(x[2] - ma), jnp.exp2(x[3] - mb)]
         for x, ma, mb in zip(s, m_a, m_b)]
    pb = [jnp.concatenate([y.astype(BF16) for y in x], axis=1) for x in e]
    blk_head = (lax.broadcasted_iota(jnp.int32, (4 * blk, LANES), 0) // blk) % 2
    sel_lane = lax.broadcasted_iota(jnp.int32, (4 * blk, LANES), 1) - DIL_HEADS
    sel = [(sel_lane == 2 * p + blk_head).astype(BF16) for p in range(DIL_HEADS // 2)]
    pv = [_dot(pb[idx], jnp.concatenate([kv_blocks(vc_ref, vp_ref, i, p), sel[p]], axis=1))
          for idx, (i, p) in enumerate(streams)]
    tiles = {}
    for idx, (i, p) in enumerate(streams):
        o_ref[0, rows(i), lanes(p)] = pv[idx][:, :pw].astype(BF16)
        tile = tiles.get(i, jnp.zeros((blk, LANES), F32)) + pv[idx][:, pw:]
        tiles[i] = jnp.where(lane_t == 2 * p, m_a[idx], jnp.where(lane_t == 2 * p + 1, m_b[idx], tile))
    for i, tile in tiles.items():
        lse_ref[0, rows(i), :] = tile


def _attn_group(q, k, v, gi, dilation, bsz, seq):
    gw = GROUP_WIDTH
    sub = seq // dilation
    windows = min(ATT_WINDOWS, sub // ATT_BLOCK)
    step_rows = windows * ATT_BLOCK
    assert sub % step_rows == 0
    view = (bsz, sub, dilation * gw)
    qv, kv_, vv = q.reshape(view), k.reshape(view), v.reshape(view)
    cur = pl.BlockSpec((1, step_rows, gw), lambda b, r, n: (b, n, r))
    prev = pl.BlockSpec((1, ATT_BLOCK, gw), lambda b, r, n: (b, jnp.maximum(n * windows - 1, 0), r))
    lse_spec = pl.BlockSpec((1, step_rows, LANES), lambda b, r, n: (b, n, r))
    o, lse = pl.pallas_call(
        _attn_kernel,
        grid=(bsz, dilation, sub // step_rows),
        in_specs=[cur, cur, prev, cur, prev],
        out_specs=[cur, lse_spec],
        out_shape=[jax.ShapeDtypeStruct(view, BF16),
                   jax.ShapeDtypeStruct((bsz, sub, dilation * LANES), F32)],
        compiler_params=_cparams("arbitrary", "arbitrary", "arbitrary"),
        name=f"attn_group{gi}",
    )(qv, kv_, kv_, vv, vv)
    return o.reshape(bsz * sub, dilation * gw), lse.reshape(bsz * sub, dilation * LANES)


def _attn_merge_kernel(o0_ref, o1_ref, o2_ref, l0_ref, l1_ref, l2_ref, w_ref, res_ref, out_ref, *scratch):
    rows = res_ref.shape[0]
    gw = GROUP_WIDTH
    outs, lses = [], []
    slot = 0
    for (_, dil), o_ref, l_ref in zip(DIL_PATTERNS, (o0_ref, o1_ref, o2_ref), (l0_ref, l1_ref, l2_ref)):
        if dil == 1:
            outs.append(o_ref[...].astype(F32))
            lses.append(l_ref[...])
            continue
        o_scr, l_scr = scratch[2 * slot], scratch[2 * slot + 1]
        sub = rows // dil
        for r in range(dil):
            for c in range(gw // LANES):
                lo = r * gw + c * LANES
                o_scr[c, pl.ds(r, sub, stride=dil), :] = o_ref[:, lo:lo + LANES].astype(F32)
            l_scr[pl.ds(r, sub, stride=dil), :] = l_ref[:, r * LANES:(r + 1) * LANES]
        outs.append(jnp.concatenate([o_scr[c] for c in range(gw // LANES)], axis=1))
        lses.append(l_scr[...])
        slot += 1
    m = jnp.maximum(jnp.maximum(lses[0], lses[1]), lses[2])
    e = [jnp.exp2(l - m) for l in lses]
    sums = [pltpu.roll(l, LANES - DIL_HEADS, 1) for l in lses]
    denom = e[0] * sums[0] + e[1] * sums[1] + e[2] * sums[2]
    is_head = lax.broadcasted_iota(jnp.int32, (rows, LANES), 1) < DIL_HEADS
    expand = (lax.broadcasted_iota(jnp.int32, (LANES, gw), 1) // DIL_HEAD_DIM
              == lax.broadcasted_iota(jnp.int32, (LANES, gw), 0)).astype(BF16)
    mix = jnp.zeros((rows, gw), F32)
    for g in range(len(outs)):
        alpha = jnp.where(is_head, e[g] / denom, 0.0)
        hi = alpha.astype(BF16)
        lo = (alpha - hi.astype(F32)).astype(BF16)
        mix = mix + (_dot(hi, expand) + _dot(lo, expand)) * outs[g]
    out_ref[...] = res_ref[...] + _dot(mix.astype(BF16), w_ref[...])


def _attn_merge(outs, lses, w_o, res):
    n, d = res.shape
    gw = GROUP_WIDTH
    o_specs = [pl.BlockSpec((PROJ_ROWS // dil, dil * gw), lambda i: (i, 0)) for _, dil in DIL_PATTERNS]
    l_specs = [pl.BlockSpec((PROJ_ROWS // dil, dil * LANES), lambda i: (i, 0)) for _, dil in DIL_PATTERNS]
    full = pl.BlockSpec((PROJ_ROWS, d), lambda i: (i, 0))
    scratch = []
    for _, dil in DIL_PATTERNS:
        if dil > 1:
            scratch += [pltpu.VMEM((gw // LANES, PROJ_ROWS, LANES), F32), pltpu.VMEM((PROJ_ROWS, LANES), F32)]
    return pl.pallas_call(
        _attn_merge_kernel,
        grid=(n // PROJ_ROWS,),
        in_specs=o_specs + l_specs + [_const_spec(w_o.shape), full],
        out_specs=full,
        out_shape=jax.ShapeDtypeStruct((n, d), F32),
        scratch_shapes=scratch,
        compiler_params=_cparams("arbitrary"),
        name="attn_merge",
    )(*outs, *lses, w_o.astype(BF16), res)


def kernel(x, norm_mix_g, norm_ffn_g, rwkv_mu, rwkv_w_r, rwkv_w_k, rwkv_w_v, rwkv_w0, rwkv_w1, rwkv_w2,
           rwkv_a0, rwkv_a1, rwkv_a2, rwkv_g1, rwkv_g2, rwkv_k_k, rwkv_k_a, rwkv_r_k, rwkv_ln_w, rwkv_ln_b,
           rwkv_w_o, kv_norm_g, w_kv, attn_w_q, attn_w_o, moe_w_grp, moe_b_grp, moe_w_exp, moe_b_exp,
           moe_w_gate, moe_w_up, moe_w_down, final_norm_g):
    bsz, seq, d = x.shape
    n = bsz * seq
    depth = norm_mix_g.shape[0]
    n_rwkv = rwkv_mu.shape[0]
    h = x.reshape(n, d)
    expert_w = (moe_w_gate.astype(BF16), moe_w_up.astype(BF16), moe_w_down.astype(BF16))
    q = k_sh = v_sh = None
    for layer in range(depth):
        if layer < n_rwkv:
            i = layer
            r, k, v, a, lw, g = _rwkv_pre(h.reshape(bsz, seq, d), norm_mix_g[layer], rwkv_mu[i], rwkv_w0[i], rwkv_a0[i],
                                          rwkv_w_r[i], rwkv_w_k[i], rwkv_w_v[i], rwkv_w1[i], rwkv_w2[i],
                                          rwkv_a1[i], rwkv_a2[i], rwkv_g1[i], rwkv_g2[i])
            y = _rwkv_rec(r, k, v, a, lw, g, rwkv_k_k[i], rwkv_k_a[i], rwkv_r_k[i], rwkv_ln_w[i], rwkv_ln_b[i])
            h = _proj_res(y.reshape(n, d), rwkv_w_o[i], h)
        else:
            i = layer - n_rwkv
            qkv = _qkv_proj(h, norm_mix_g[layer], kv_norm_g, attn_w_q[i], w_kv)
            q = qkv[0::3]
            if i == 0:
                k_sh, v_sh = qkv[1::3], qkv[2::3]
            outs, lses = [], []
            for gi, (window, dilation) in enumerate(DIL_PATTERNS):
                assert window // dilation == ATT_BLOCK and seq % window == 0
                o, lse = _attn_group(q[gi], k_sh[gi], v_sh[gi], gi, dilation, bsz, seq)
                outs.append(o)
                lses.append(lse)
            h = _attn_merge(outs, lses, attn_w_o[i], h)
        last = layer == depth - 1
        h = _hier_moe_residual(h, norm_ffn_g[layer], moe_w_grp[layer], moe_b_grp[layer], moe_w_exp[layer],
                               moe_b_exp[layer], layer, *expert_w, final_g=final_norm_g if last else None)
    return h.reshape(bsz, seq, d)
```

```python
import functools
import math

import jax
import jax.numpy as jnp
from jax import lax
from jax.experimental import pallas as pl
from jax.experimental.pallas import tpu as pltpu

F32 = jnp.float32
BF16 = jnp.bfloat16
HIGHEST = lax.Precision.HIGHEST

NORM_EPS = 1e-6
LOG2_E = math.log2(math.e)
RWKV_HEAD = 64
GN_EPS = RWKV_HEAD * 1e-5
DIL_PATTERNS = ((128, 1), (512, 4), (2048, 16))
DIL_HEADS = 8
DIL_HEAD_DIM = 64
N_EXPERT_GROUPS = 4
EXPERTS_PER_GROUP = 4
LANES = 128
VMEM_LIMIT = 56 * 1024 * 1024

CHUNK = 64
REC_ROWS = 2048
REC_PAIRS = 2
PRE_ROWS = 512
PROJ_ROWS = 512
MOE_BLOCK = 1024
FFN_ROWS = 64
FFN_TILES = 8
ATT_BLOCK = 128
ATT_WINDOWS = 4


def _cparams(*sem):
    return pltpu.CompilerParams(dimension_semantics=sem, vmem_limit_bytes=VMEM_LIMIT)


def _dot(a, b):
    return jnp.dot(a, b, preferred_element_type=F32)


def _dot_nt(a, b):
    return lax.dot_general(a, b, (((1,), (1,)), ((), ())), preferred_element_type=F32)


def _dot_tn(a, b, precision=None):
    return lax.dot_general(a, b, (((0,), (0,)), ((), ())), preferred_element_type=F32, precision=precision)


def _rms(x):
    return x * lax.rsqrt(jnp.mean(x * x, axis=-1, keepdims=True) + NORM_EPS)


def _sigmoid(z):
    return 1.0 / (1.0 + jnp.exp(-z))


def _const_spec(shape):
    nd = len(shape)
    return pl.BlockSpec(shape, lambda *_: (0,) * nd)


def _rwkv_pre_kernel(x_ref, gmix_ref, mu_ref, w0_ref, a0_ref, wr_ref, wk_ref, wv_ref,
                     w1_ref, w2_ref, a1_ref, a2_ref, g1_ref, g2_ref,
                     r_ref, k_ref, v_ref, a_ref, lw_ref, g_ref, prev_ref):
    @pl.when(pl.program_id(1) == 0)
    def _():
        prev_ref[...] = jnp.zeros_like(prev_ref)

    x = x_ref[0]
    rows = x.shape[0]
    xn = _rms(x) * gmix_ref[...]
    row = lax.broadcasted_iota(jnp.int32, xn.shape, 0)
    shifted = jnp.where(row == 0, prev_ref[7:8, :], pltpu.roll(xn, 1, 0))
    prev_ref[...] = xn[rows - 8:, :]
    xx = shifted - xn

    def mix(i):
        return (xn + xx * mu_ref[i:i + 1, :]).astype(BF16)

    r_ref[0] = _dot(mix(0), wr_ref[...]).astype(BF16)
    k_ref[0] = _dot(mix(2), wk_ref[...]).astype(BF16)
    v_ref[0] = _dot(mix(3), wv_ref[...]).astype(BF16)
    u = w0_ref[...] + _dot(jnp.tanh(_dot(mix(1), w1_ref[...])).astype(BF16), w2_ref[...])
    w_log = -(jnp.maximum(-u, 0.0) + jnp.log(1.0 + jnp.exp(-jnp.abs(u)))) - 0.5
    lw_ref[0] = -jnp.exp(w_log) * LOG2_E
    a_ref[0] = _sigmoid(a0_ref[...] + _dot(_dot(mix(4), a1_ref[...]).astype(BF16), a2_ref[...])).astype(BF16)
    g_ref[0] = _dot(_sigmoid(_dot(mix(5), g1_ref[...])).astype(BF16), g2_ref[...]).astype(BF16)


def _pad_cols(w, n):
    return jnp.pad(w, ((0, 0), (0, n - w.shape[1])))


def _pad_rows(w, n):
    return jnp.pad(w, ((0, n - w.shape[0]), (0, 0)))


def _rwkv_pre(x, gmix, mu, w0, a0, w_r, w_k, w_v, w1, w2, a1, a2, g1, g2):
    bsz, seq, d = x.shape
    lw = -(-w1.shape[1] // LANES) * LANES
    la = -(-a1.shape[1] // LANES) * LANES
    lg = -(-g1.shape[1] // LANES) * LANES
    mu8 = jnp.pad(mu, ((0, 8 - mu.shape[0]), (0, 0)))
    args = (x, gmix.reshape(1, d), mu8, w0.reshape(1, d), a0.reshape(1, d),
            w_r.astype(BF16), w_k.astype(BF16), w_v.astype(BF16),
            _pad_cols(w1, lw).astype(BF16), _pad_rows(w2, lw).astype(BF16),
            _pad_cols(a1, la).astype(BF16), _pad_rows(a2, la).astype(BF16),
            _pad_cols(g1, lg).astype(BF16), _pad_rows(g2, lg).astype(BF16))
    tile = pl.BlockSpec((1, PRE_ROWS, d), lambda b, s: (b, s, 0))
    in_specs = [tile] + [_const_spec(a.shape) for a in args[1:]]
    out_dt = (BF16, BF16, BF16, BF16, F32, BF16)
    return pl.pallas_call(
        _rwkv_pre_kernel,
        grid=(bsz, seq // PRE_ROWS),
        in_specs=in_specs,
        out_specs=[tile] * 6,
        out_shape=[jax.ShapeDtypeStruct((bsz, seq, d), t) for t in out_dt],
        scratch_shapes=[pltpu.VMEM((8, d), F32)],
        compiler_params=_cparams("arbitrary", "arbitrary"),
        name="rwkv_pre",
    )(*args)


def _rwkv_rec_kernel(r_ref, k_ref, v_ref, a_ref, lw_ref, g_ref, kk_ref, ka_ref, rk_ref, lnw_ref, lnb_ref,
                     o_ref, state_ref):
    @pl.when(pl.program_id(2) == 0)
    def _():
        state_ref[...] = jnp.zeros_like(state_ref)

    c = CHUNK
    hd = RWKV_HEAD
    w = 2 * hd
    assert c == hd
    n = r_ref.shape[1] // c
    lane = lax.broadcasted_iota(jnp.int32, (c, w), 1)
    row = lax.broadcasted_iota(jnp.int32, (c, w), 0)
    head0 = lane < hd
    col = jnp.where(head0, lane, lane - hd)
    strict = row > col
    incl = row >= col
    eye = (row == col).astype(F32)
    tri = (lax.broadcasted_iota(jnp.int32, (c, c), 0) >= lax.broadcasted_iota(jnp.int32, (c, c), 1)).astype(BF16)
    pairs = r_ref.shape[2] // w

    def head_sum(x):
        s0 = jnp.sum(jnp.where(head0, x, 0.0), axis=-1, keepdims=True)
        s1 = jnp.sum(jnp.where(head0, 0.0, x), axis=-1, keepdims=True)
        return jnp.where(head0, s0, s1)

    def stack(x):
        return jnp.concatenate([jnp.where(head0, x, 0.0), jnp.where(head0, 0.0, x)], axis=0)

    def cumsum(x):
        hi = x.astype(BF16)
        lo = (x - hi.astype(F32)).astype(BF16)
        both = _dot(tri, jnp.concatenate([hi, lo], axis=1))
        return both[:, :w] + both[:, w:]

    zeros = jnp.zeros((c, w), BF16)
    carried = [state_ref[p] for p in range(pairs)]

    def chunk(j, p):
        rs = slice(j * c, (j + 1) * c)
        ls = slice(p * w, (p + 1) * w)
        kkp, kap, rkp, lnw, lnb = kk_ref[:, ls], ka_ref[:, ls], rk_ref[:, ls], lnw_ref[:, ls], lnb_ref[:, ls]
        r_ = r_ref[0, rs, ls].astype(F32)
        k_ = k_ref[0, rs, ls].astype(F32)
        v_ = v_ref[0, rs, ls].astype(F32)
        a_ = a_ref[0, rs, ls].astype(F32)
        lw_ = lw_ref[0, rs, ls]
        cum = cumsum(lw_)
        yield
        kk = k_ * kkp
        kk = kk * lax.rsqrt(jnp.maximum(head_sum(kk * kk), 1e-24))
        k2 = k_ * (1.0 + (a_ - 1.0) * kap)
        p_ = -(kk * a_)
        tot = cum[c - 1:c, :]
        e_neg = jnp.exp2(-cum)
        e_tot = jnp.exp2(tot - cum)
        rt = r_ * jnp.exp2(cum)
        qt = kk * jnp.exp2(cum - lw_)
        vb = v_.astype(BF16)
        amat = _dot_nt(jnp.concatenate([qt, rt], axis=0).astype(BF16),
                       jnp.concatenate([stack((k2 * e_neg).astype(BF16)), stack((p_ * e_neg).astype(BF16))], axis=0))
        yield
        a_qp = jnp.where(strict, amat[:c, w:], 0.0)
        a_rp = jnp.where(incl, amat[c:, w:], 0.0).astype(BF16)
        a_k = jnp.concatenate([jnp.where(strict, amat[:c, :w], 0.0),
                               jnp.where(incl, amat[c:, :w], 0.0)], axis=0).astype(BF16)
        av = _dot(a_k, stack(vb))
        inv = eye + a_qp
        power = a_qp.astype(BF16)
        power = _dot(power, stack(power))
        yield
        steps = int(math.log2(c)) - 1
        for i in range(steps):
            pb = power.astype(BF16)
            if i + 1 < steps:
                both = _dot(jnp.concatenate([inv.astype(BF16), pb], axis=0), stack(pb))
                inv = inv + both[:c]
                power = both[c:]
            else:
                inv = inv + _dot(inv.astype(BF16), stack(pb))
            yield
        wu = _dot(inv.astype(BF16),
                  jnp.concatenate([stack(qt.astype(BF16)), stack(av[:c].astype(BF16))], axis=1))
        wub = wu.astype(BF16)
        yield
        ry = _dot(a_rp, jnp.concatenate([stack(wub[:, :w]), stack(wub[:, w:])], axis=1))
        mg = _dot_tn(jnp.concatenate([p_ * e_tot, k2 * e_tot], axis=0).astype(BF16),
                     jnp.concatenate([wub, jnp.concatenate([zeros, vb], axis=1)], axis=0))
        yield
        rw = (rt + ry[:, :w]).astype(BF16)
        m_w = jnp.where(head0, mg[:c, :w], mg[c:, :w]).astype(BF16)
        g_w = jnp.where(head0, mg[:c, w:], mg[c:, w:])
        gam = head_sum(eye * jnp.exp2(tot))
        state = carried[p]
        both = _dot(jnp.concatenate([rw, m_w], axis=0), stack(state.astype(BF16)))
        y = both[:c] + av[c:] + ry[:, w:]
        carried[p] = gam * state + both[c:] + g_w
        yield
        yc = y - head_sum(y) * (1.0 / hd)
        var = head_sum(yc * yc) * (1.0 / hd)
        yn = yc * lax.rsqrt(var + GN_EPS) * lnw + lnb
        bonus = head_sum(r_ * k2 * rkp) * v_
        o_ref[0, rs, ls] = ((yn + bonus) * g_ref[0, rs, ls].astype(F32)).astype(BF16)

    live, started = [], 0
    while started < n or live:
        if started < n:
            live.extend(chunk(started, p) for p in range(pairs))
            started += 1
        for gen in list(live):
            if next(gen, "done") == "done":
                live.remove(gen)
    for p in range(pairs):
        state_ref[p] = carried[p]


def _rwkv_rec(r, k, v, a, lw, g, k_k, k_a, r_k, ln_w, ln_b):
    bsz, seq, d = r.shape
    pair_w = 2 * RWKV_HEAD
    hw = REC_PAIRS * pair_w
    tile = pl.BlockSpec((1, REC_ROWS, hw), lambda b, h, s: (b, s, h))
    par = pl.BlockSpec((1, hw), lambda b, h, s: (0, h))
    params = [p.reshape(1, d).astype(F32) for p in (k_k, k_a, r_k, ln_w, ln_b)]
    return pl.pallas_call(
        _rwkv_rec_kernel,
        grid=(bsz, d // hw, seq // REC_ROWS),
        in_specs=[tile] * 6 + [par] * 5,
        out_specs=tile,
        out_shape=jax.ShapeDtypeStruct((bsz, seq, d), BF16),
        scratch_shapes=[pltpu.VMEM((REC_PAIRS, CHUNK, pair_w), F32)],
        compiler_params=_cparams("arbitrary", "arbitrary", "arbitrary"),
        name="rwkv_rec",
    )(r, k, v, a, lw, g, *params)


def _moe_route_kernel(h_ref, g_ref, wr_ref, br_ref, xs_ref, gs_ref, dest_ref, cnt_ref):
    _route_block(h_ref[...], g_ref, wr_ref, br_ref, xs_ref, gs_ref, dest_ref, cnt_ref)


def _moe_route_proj_kernel(a_ref, w_ref, res_ref, g_ref, wr_ref, br_ref, h_ref, xs_ref, gs_ref, dest_ref, cnt_ref):
    h = res_ref[...] + _dot(a_ref[...], w_ref[...])
    h_ref[...] = h
    _route_block(h, g_ref, wr_ref, br_ref, xs_ref, gs_ref, dest_ref, cnt_ref)


def _route_block(h, g_ref, wr_ref, br_ref, xs_ref, gs_ref, dest_ref, cnt_ref):
    nb = h.shape[0]
    nbp = xs_ref.shape[0]
    tb = (_rms(h) * g_ref[...]).astype(BF16)
    logits = _dot(tb, wr_ref[...]) + br_ref[...]
    lane = lax.broadcasted_iota(jnp.int32, logits.shape, 1).astype(F32)
    neg = jnp.float32(-jnp.inf)
    big = jnp.float32(LANES)
    is_grp = lane < N_EXPERT_GROUPS
    gl = jnp.where(is_grp, logits, neg)
    gmax = jnp.max(gl, axis=-1, keepdims=True)
    grp = jnp.min(jnp.where(gl == gmax, lane, big), axis=-1, keepdims=True)
    p_grp = 1.0 / jnp.sum(jnp.where(is_grp, jnp.exp(gl - gmax), 0.0), axis=-1, keepdims=True)
    e_lo = N_EXPERT_GROUPS + grp * EXPERTS_PER_GROUP
    in_grp = (lane >= e_lo) & (lane < e_lo + EXPERTS_PER_GROUP)
    el = jnp.where(in_grp, logits, neg)
    v1 = jnp.max(el, axis=-1, keepdims=True)
    i1 = jnp.min(jnp.where(el == v1, lane, big), axis=-1, keepdims=True)
    el2 = jnp.where(lane == i1, neg, el)
    v2 = jnp.max(el2, axis=-1, keepdims=True)
    i2 = jnp.min(jnp.where(el2 == v2, lane, big), axis=-1, keepdims=True)
    e2 = jnp.exp(v2 - v1)
    w1 = 1.0 / (1.0 + e2)
    w2 = e2 / (1.0 + e2)
    gates = jnp.where(lane == i1, w1, jnp.where(lane == i2, w2, 0.0)) * p_grp

    onehot = (lane == grp).astype(BF16)
    ri = lax.broadcasted_iota(jnp.int32, (nb, nb), 0)
    ci = lax.broadcasted_iota(jnp.int32, (nb, nb), 1)
    rank = _dot((ri > ci).astype(BF16), onehot)
    counts = jnp.sum(onehot.astype(F32), axis=0, keepdims=True)
    padded = jnp.ceil(counts / FFN_ROWS) * FFN_ROWS
    lane1 = lax.broadcasted_iota(jnp.int32, (1, LANES), 1)
    offs = jnp.zeros((1, LANES), F32)
    for gidx in range(1, N_EXPERT_GROUPS):
        prev = jnp.sum(jnp.where(lane1 < gidx, padded, 0.0), axis=-1, keepdims=True)
        offs = jnp.where(lane1 == gidx, prev, offs)
    dest = jnp.sum(onehot.astype(F32) * (rank + offs), axis=-1, keepdims=True)
    slot = lax.broadcasted_iota(jnp.int32, (nb, nbp), 1).astype(F32)
    perm_t = (dest == slot).astype(BF16)
    xs_ref[...] = _dot_tn(perm_t, tb).astype(BF16)
    g_hi = gates.astype(BF16)
    g_lo = (gates - g_hi.astype(F32)).astype(BF16)
    moved = _dot_tn(perm_t, jnp.concatenate([g_hi, g_lo], axis=1))
    gs_ref[...] = moved[:, :LANES] + moved[:, LANES:]
    dest_ref[...] = dest
    cnt_ref[0] = jnp.where(lane1 < N_EXPERT_GROUPS, counts, 0.0)


def _moe_ffn_kernel(tile_ref, grp_ref, real_ref, *refs):
    nt = FFN_TILES
    xs_refs, gs_refs = refs[:nt], refs[nt:2 * nt]
    wg_ref, wu_ref, wd_ref, ys_ref = refs[2 * nt:]
    i = pl.program_id(0)

    @pl.when(real_ref[i] > 0)
    def _():
        x = jnp.concatenate([r[...] for r in xs_refs], axis=0)
        gates = jnp.concatenate([r[...] for r in gs_refs], axis=0)
        lane = lax.broadcasted_iota(jnp.int32, gates.shape, 1)
        base = N_EXPERT_GROUPS + grp_ref[i] * EXPERTS_PER_GROUP
        acc = jnp.zeros(ys_ref.shape, F32)
        for e in range(EXPERTS_PER_GROUP):
            ge = jnp.sum(jnp.where(lane == base + e, gates, 0.0), axis=-1, keepdims=True)
            gate_act = _dot(x, wg_ref[e])
            hdn = gate_act * _sigmoid(gate_act) * _dot(x, wu_ref[e])
            acc = acc + ge * _dot(hdn.astype(BF16), wd_ref[e])
        ys_ref[...] = acc.astype(BF16)

    @pl.when(real_ref[i] == 0)
    def _():
        ys_ref[...] = jnp.zeros_like(ys_ref)


def _moe_merge_kernel(slot_ref, h_ref, dest_ref, *refs):
    ys_refs, o_ref = refs[:-1], refs[-1]
    ys = jnp.concatenate([r[...] for r in ys_refs], axis=0)
    nb, nbp = h_ref.shape[0], ys.shape[0]
    slot = lax.broadcasted_iota(jnp.int32, (nb, nbp), 1).astype(F32)
    perm_t = (dest_ref[...] == slot).astype(BF16)
    o_ref[...] = h_ref[...] + _dot(perm_t, ys)


def _moe_merge_norm_kernel(slot_ref, h_ref, dest_ref, *refs):
    ys_refs, g_ref, o_ref = refs[:-2], refs[-2], refs[-1]
    ys = jnp.concatenate([r[...] for r in ys_refs], axis=0)
    nb, nbp = h_ref.shape[0], ys.shape[0]
    slot = lax.broadcasted_iota(jnp.int32, (nb, nbp), 1).astype(F32)
    perm_t = (dest_ref[...] == slot).astype(BF16)
    o_ref[...] = _rms(h_ref[...] + _dot(perm_t, ys)) * g_ref[...]


def _ffn_slots(n_tiles):
    return -(-(n_tiles + N_EXPERT_GROUPS * (FFN_TILES - 1)) // FFN_TILES) * FFN_TILES


def _ffn_schedule(counts, nblk, tiles_per_blk):
    i32 = jnp.int32
    tiles = (counts + FFN_ROWS - 1) // FFN_ROWS
    first_rel = jnp.cumsum(tiles, axis=1) - tiles
    used_blk = jnp.sum(tiles, axis=1)
    grp_pad = (jnp.sum(tiles, axis=0) + FFN_TILES - 1) // FFN_TILES * FFN_TILES
    grp_start = jnp.cumsum(grp_pad) - grp_pad
    before = jnp.cumsum(tiles, axis=0) - tiles
    n_tiles = nblk * tiles_per_blk
    t = jnp.arange(n_tiles, dtype=i32)
    b, k = t // tiles_per_blk, t % tiles_per_blk
    used = k < used_blk[b]
    grp_t = jnp.minimum(jnp.sum(((first_rel + tiles)[b] <= k[:, None]).astype(i32), axis=1), N_EXPERT_GROUPS - 1)
    pos = grp_start[grp_t] + before[b, grp_t] + (k - first_rel[b, grp_t])
    s = jnp.arange(_ffn_slots(n_tiles), dtype=i32)
    hit = used[None, :] & (pos[None, :] == s[:, None])
    slot_has = jnp.sum(hit.astype(i32), axis=1) > 0
    free_rank = jnp.cumsum((~slot_has).astype(i32)) - 1
    unused_rank = jnp.cumsum((~used).astype(i32)) - 1
    fill = (~used)[None, :] & (unused_rank[None, :] == free_rank[:, None])
    placed = jnp.where(slot_has[:, None], hit, fill)
    slot_tile = jnp.sum(jnp.where(placed, t[None, :], 0), axis=1).astype(i32)
    tile_slot = jnp.sum(jnp.where(placed, s[:, None], 0), axis=0).astype(i32)
    slot_grp = jnp.sum(jnp.where(hit, grp_t[None, :], 0), axis=1).reshape(-1, FFN_TILES)
    step_real = jnp.sum(slot_has.astype(i32).reshape(-1, FFN_TILES), axis=1).astype(i32)
    n_real_steps = jnp.sum(grp_pad) // FFN_TILES
    last_grp = slot_grp[jnp.maximum(n_real_steps - 1, 0), 0]
    step_grp = jnp.where(step_real > 0, slot_grp[:, 0], last_grp).astype(i32)
    return slot_tile, tile_slot, step_grp, step_real


def _hier_moe_residual(h, norm_g, w_grp, b_grp, w_exp, b_exp, layer, w_gate, w_up, w_down, final_g=None,
                       pending=None):
    n, d = h.shape
    nb = MOE_BLOCK
    nblk = n // nb
    nbp = nb + N_EXPERT_GROUPS * FFN_ROWS
    tiles_per_blk = nbp // FFN_ROWS
    n_exp = N_EXPERT_GROUPS * EXPERTS_PER_GROUP
    w_router = jnp.pad(jnp.concatenate([w_grp, w_exp], axis=1), ((0, 0), (0, LANES - N_EXPERT_GROUPS - n_exp)))
    b_router = jnp.pad(jnp.concatenate([b_grp, b_exp]), (0, LANES - N_EXPERT_GROUPS - n_exp)).reshape(1, LANES)

    rows = pl.BlockSpec((nb, d), lambda i: (i, 0))
    route_in = [_const_spec((1, d)), _const_spec((d, LANES)), _const_spec((1, LANES))]
    route_args = (norm_g.reshape(1, d), w_router.astype(BF16), b_router)
    route_out = [pl.BlockSpec((nbp, d), lambda i: (i, 0)), pl.BlockSpec((nbp, LANES), lambda i: (i, 0)),
                 pl.BlockSpec((nb, 1), lambda i: (i, 0)), pl.BlockSpec((1, 1, LANES), lambda i: (i, 0, 0))]
    route_shape = [jax.ShapeDtypeStruct((nblk * nbp, d), BF16), jax.ShapeDtypeStruct((nblk * nbp, LANES), F32),
                   jax.ShapeDtypeStruct((n, 1), F32), jax.ShapeDtypeStruct((nblk, 1, LANES), F32)]
    if pending is None:
        xs, gs, dest, counts = pl.pallas_call(
            _moe_route_kernel, grid=(nblk,), in_specs=[rows] + route_in, out_specs=route_out, out_shape=route_shape,
            compiler_params=_cparams("arbitrary"), name="moe_route",
        )(h, *route_args)
    else:
        a, w = pending
        h, xs, gs, dest, counts = pl.pallas_call(
            _moe_route_proj_kernel, grid=(nblk,),
            in_specs=[pl.BlockSpec((nb, a.shape[1]), lambda i: (i, 0)), _const_spec(w.shape), rows] + route_in,
            out_specs=[rows] + route_out, out_shape=[jax.ShapeDtypeStruct((n, d), F32)] + route_shape,
            compiler_params=_cparams("arbitrary"), name="moe_route_proj",
        )(a, w.astype(BF16), h, *route_args)

    n_tiles = nblk * tiles_per_blk
    n_slots = _ffn_slots(n_tiles)
    slot_tile, tile_slot, step_grp, step_real = _ffn_schedule(
        counts[:, 0, :N_EXPERT_GROUPS].astype(jnp.int32), nblk, tiles_per_blk)

    def tile_spec(width, j):
        return pl.BlockSpec((FFN_ROWS, width), lambda i, tile, grp, real: (tile[i * FFN_TILES + j], 0))

    def wspec(shape):
        return pl.BlockSpec((None, EXPERTS_PER_GROUP) + shape, lambda i, tile, grp, real: (layer, grp[i], 0, 0))

    f = w_gate.shape[-1]
    ys = pl.pallas_call(
        _moe_ffn_kernel,
        grid_spec=pltpu.PrefetchScalarGridSpec(
            num_scalar_prefetch=3,
            grid=(n_slots // FFN_TILES,),
            in_specs=[tile_spec(d, j) for j in range(FFN_TILES)] + [tile_spec(LANES, j) for j in range(FFN_TILES)]
            + [wspec((d, f)), wspec((d, f)), wspec((f, d))],
            out_specs=pl.BlockSpec((FFN_TILES * FFN_ROWS, d), lambda i, tile, grp, real: (i, 0)),
        ),
        out_shape=jax.ShapeDtypeStruct((n_slots * FFN_ROWS, d), BF16),
        compiler_params=_cparams("arbitrary"),
        name="moe_ffn",
    )(slot_tile, step_grp, step_real, *([xs] * FFN_TILES), *([gs] * FFN_TILES),
      w_gate, w_up, w_down)

    blk = pl.BlockSpec((nb, d), lambda i, slot: (i, 0))
    in_specs = [blk, pl.BlockSpec((nb, 1), lambda i, slot: (i, 0))]
    in_specs += [pl.BlockSpec((FFN_ROWS, d), functools.partial(lambda j, i, slot: (slot[i * tiles_per_blk + j], 0), j))
                 for j in range(tiles_per_blk)]
    args = [h, dest] + [ys] * tiles_per_blk
    body = _moe_merge_kernel
    if final_g is not None:
        in_specs.append(pl.BlockSpec((1, d), lambda i, slot: (0, 0)))
        args.append(final_g.reshape(1, d))
        body = _moe_merge_norm_kernel
    return pl.pallas_call(
        body,
        grid_spec=pltpu.PrefetchScalarGridSpec(num_scalar_prefetch=1, grid=(nblk,), in_specs=in_specs, out_specs=blk),
        out_shape=jax.ShapeDtypeStruct((n, d), F32),
        compiler_params=_cparams("arbitrary"),
        name="moe_merge",
    )(tile_slot, *args)


GROUP_WIDTH = DIL_HEADS * DIL_HEAD_DIM
N_STRIDED = 3 * sum(1 for _, dil in DIL_PATTERNS if dil > 1)


def _qkv_kernel(h_ref, gq_ref, gkv_ref, wq_ref, wk_ref, wv_ref, *refs):
    outs, scratch = refs[:-1], refs[-1]
    n = _rms(h_ref[...])
    xq = (n * gq_ref[...]).astype(BF16)
    xkv = (n * gkv_ref[...]).astype(BF16)
    gw = GROUP_WIDTH
    slot = 0
    for gi, (_, dil) in enumerate(DIL_PATTERNS):
        cols = slice(gi * gw, (gi + 1) * gw)
        for t, (x, w_ref) in enumerate(((xq, wq_ref), (xkv, wk_ref), (xkv, wv_ref))):
            res = _dot(x, w_ref[:, cols])
            out_ref = outs[3 * gi + t]
            if dil == 1:
                out_ref[...] = res.astype(BF16)
                continue
            sub = res.shape[0] // dil
            for c in range(gw // LANES):
                scratch[slot, c] = res[:, c * LANES:(c + 1) * LANES]
            for r in range(dil):
                for c in range(gw // LANES):
                    lo = r * gw + c * LANES
                    out_ref[:, lo:lo + LANES] = scratch[slot, c, pl.ds(r, sub, stride=dil), :].astype(BF16)
            slot += 1


def _qkv_proj(h, g_q, g_kv, w_q, w_kv):
    n, d = h.shape
    qw = w_q.shape[1]
    gw = GROUP_WIDTH
    scale = math.log2(math.e) / math.sqrt(DIL_HEAD_DIM)
    rows = pl.BlockSpec((PROJ_ROWS, d), lambda i: (i, 0))
    out_specs, out_shape = [], []
    for _, dil in DIL_PATTERNS:
        for _ in range(3):
            out_specs.append(pl.BlockSpec((PROJ_ROWS // dil, dil * gw), lambda i: (i, 0)))
            out_shape.append(jax.ShapeDtypeStruct((n // dil, dil * gw), BF16))
    return pl.pallas_call(
        _qkv_kernel,
        grid=(n // PROJ_ROWS,),
        in_specs=[rows, _const_spec((1, d)), _const_spec((1, d))] + [_const_spec((d, qw))] * 3,
        out_specs=out_specs,
        out_shape=out_shape,
        scratch_shapes=[pltpu.VMEM((N_STRIDED, gw // LANES, PROJ_ROWS, LANES), F32)],
        compiler_params=_cparams("arbitrary"),
        name="qkv_proj",
    )(h, g_q.reshape(1, d), g_kv.reshape(1, d), (w_q * scale).astype(BF16),
      w_kv[:, :qw].astype(BF16), w_kv[:, qw:].astype(BF16))


def _attn_kernel(q_ref, kc_ref, kp_ref, vc_ref, vp_ref, o_ref, lse_ref):
    n = pl.program_id(2)
    blk = ATT_BLOCK
    pw = 2 * DIL_HEAD_DIM
    ri = lax.broadcasted_iota(jnp.int32, (blk, blk), 0)
    ci = lax.broadcasted_iota(jnp.int32, (blk, blk), 1)
    lowest = jnp.finfo(F32).min
    cap_prev = jnp.where(ci >= ri, jnp.inf, lowest).astype(F32)
    cap_first = jnp.where(n > 0, cap_prev, lowest)
    cap_own = jnp.where(ci <= ri, jnp.inf, lowest).astype(F32)
    head0 = lax.broadcasted_iota(jnp.int32, (blk, pw), 1) < DIL_HEAD_DIM
    zero = jnp.zeros((blk, pw), BF16)
    lane_t = lax.broadcasted_iota(jnp.int32, (blk, LANES), 1)

    def stack(x):
        return jnp.concatenate([jnp.where(head0, x, zero), jnp.where(head0, zero, x)], axis=0)

    streams = [(i, p) for i in range(q_ref.shape[1] // blk) for p in range(DIL_HEADS // 2)]

    def rows(i):
        return slice(i * blk, (i + 1) * blk)

    def lanes(p):
        return slice(p * pw, (p + 1) * pw)

    stacked = {}

    def kv_blocks(cur_ref, prev_ref, i, p):
        def one(w):
            name = (id(cur_ref), w, p)
            if name not in stacked:
                stacked[name] = stack(prev_ref[0, :, lanes(p)] if w < 0 else cur_ref[0, rows(w), lanes(p)])
            return stacked[name]
        return jnp.concatenate([one(i - 1), one(i)], axis=0)

    s = []
    for i, p in streams:
        raw = _dot_nt(q_ref[0, rows(i), lanes(p)], kv_blocks(kc_ref, kp_ref, i, p))
        cap = cap_first if i == 0 else cap_prev
        s.append([jnp.minimum(raw[:, :blk], cap), jnp.minimum(raw[:, blk:2 * blk], cap),
                  jnp.minimum(raw[:, 2 * blk:3 * blk], cap_own), jnp.minimum(raw[:, 3 * blk:], cap_own)])
    m_a = [jnp.max(jnp.maximum(x[0], x[2]), axis=-1, keepdims=True) for x in s]
    m_b = [jnp.max(jnp.maximum(x[1], x[3]), axis=-1, keepdims=True) for x in s]
    e = [[jnp.exp2(x[0] - ma), jnp.exp2(x[1] - mb), jnp.exp2(x[2] - ma), jnp.exp2(x[3] - mb)]
         for x, ma, mb in zip(s, m_a, m_b)]
    pb = [jnp.concatenate([y.astype(BF16) for y in x], axis=1) for x in e]
    blk_head = (lax.broadcasted_iota(jnp.int32, (4 * blk, LANES), 0) // blk) % 2
    sel_lane = lax.broadcasted_iota(jnp.int32, (4 * blk, LANES), 1) - DIL_HEADS
    sel = [(sel_lane == 2 * p + blk_head).astype(BF16) for p in range(DIL_HEADS // 2)]
    pv = [_dot(pb[idx], jnp.concatenate([kv_blocks(vc_ref, vp_ref, i, p), sel[p]], axis=1))
          for idx, (i, p) in enumerate(streams)]
    tiles = {}
    for idx, (i, p) in enumerate(streams):
        o_ref[0, rows(i), lanes(p)] = pv[idx][:, :pw].astype(BF16)
        tile = tiles.get(i, jnp.zeros((blk, LANES), F32)) + pv[idx][:, pw:]
        tiles[i] = jnp.where(lane_t == 2 * p, m_a[idx], jnp.where(lane_t == 2 * p + 1, m_b[idx], tile))
    for i, tile in tiles.items():
        lse_ref[0, rows(i), :] = tile


def _attn_group(q, k, v, gi, dilation, bsz, seq):
    gw = GROUP_WIDTH
    sub = seq // dilation
    windows = min(ATT_WINDOWS, sub // ATT_BLOCK)
    step_rows = windows * ATT_BLOCK
    assert sub % step_rows == 0
    view = (bsz, sub, dilation * gw)
    qv, kv_, vv = q.reshape(view), k.reshape(view), v.reshape(view)
    cur = pl.BlockSpec((1, step_rows, gw), lambda b, r, n: (b, n, r))
    prev = pl.BlockSpec((1, ATT_BLOCK, gw), lambda b, r, n: (b, jnp.maximum(n * windows - 1, 0), r))
    lse_spec = pl.BlockSpec((1, step_rows, LANES), lambda b, r, n: (b, n, r))
    o, lse = pl.pallas_call(
        _attn_kernel,
        grid=(bsz, dilation, sub // step_rows),
        in_specs=[cur, cur, prev, cur, prev],
        out_specs=[cur, lse_spec],
        out_shape=[jax.ShapeDtypeStruct(view, BF16),
                   jax.ShapeDtypeStruct((bsz, sub, dilation * LANES), F32)],
        compiler_params=_cparams("arbitrary", "arbitrary", "arbitrary"),
        name=f"attn_group{gi}",
    )(qv, kv_, kv_, vv, vv)
    return o.reshape(bsz * sub, dilation * gw), lse.reshape(bsz * sub, dilation * LANES)


def _attn_merge_kernel(o0_ref, o1_ref, o2_ref, l0_ref, l1_ref, l2_ref, w_ref, res_ref, out_ref, *scratch):
    rows = res_ref.shape[0]
    gw = GROUP_WIDTH
    outs, lses = [], []
    slot = 0
    for (_, dil), o_ref, l_ref in zip(DIL_PATTERNS, (o0_ref, o1_ref, o2_ref), (l0_ref, l1_ref, l2_ref)):
        if dil == 1:
            outs.append(o_ref[...].astype(F32))
            lses.append(l_ref[...])
            continue
        o_scr, l_scr = scratch[2 * slot], scratch[2 * slot + 1]
        sub = rows // dil
        for r in range(dil):
            for c in range(gw // LANES):
                lo = r * gw + c * LANES
                o_scr[c, pl.ds(r, sub, stride=dil), :] = o_ref[:, lo:lo + LANES].astype(F32)
            l_scr[pl.ds(r, sub, stride=dil), :] = l_ref[:, r * LANES:(r + 1) * LANES]
        outs.append(jnp.concatenate([o_scr[c] for c in range(gw // LANES)], axis=1))
        lses.append(l_scr[...])
        slot += 1
    m = jnp.maximum(jnp.maximum(lses[0], lses[1]), lses[2])
    e = [jnp.exp2(l - m) for l in lses]
    sums = [pltpu.roll(l, LANES - DIL_HEADS, 1) for l in lses]
    denom = e[0] * sums[0] + e[1] * sums[1] + e[2] * sums[2]
    is_head = lax.broadcasted_iota(jnp.int32, (rows, LANES), 1) < DIL_HEADS
    expand = (lax.broadcasted_iota(jnp.int32, (LANES, gw), 1) // DIL_HEAD_DIM
              == lax.broadcasted_iota(jnp.int32, (LANES, gw), 0)).astype(BF16)
    mix = jnp.zeros((rows, gw), F32)
    for g in range(len(outs)):
        alpha = jnp.where(is_head, e[g] / denom, 0.0)
        hi = alpha.astype(BF16)
        lo = (alpha - hi.astype(F32)).astype(BF16)
        mix = mix + (_dot(hi, expand) + _dot(lo, expand)) * outs[g]
    out_ref[...] = res_ref[...] + _dot(mix.astype(BF16), w_ref[...])


def _attn_merge(outs, lses, w_o, res):
    n, d = res.shape
    gw = GROUP_WIDTH
    o_specs = [pl.BlockSpec((PROJ_ROWS // dil, dil * gw), lambda i: (i, 0)) for _, dil in DIL_PATTERNS]
    l_specs = [pl.BlockSpec((PROJ_ROWS // dil, dil * LANES), lambda i: (i, 0)) for _, dil in DIL_PATTERNS]
    full = pl.BlockSpec((PROJ_ROWS, d), lambda i: (i, 0))
    scratch = []
    for _, dil in DIL_PATTERNS:
        if dil > 1:
            scratch += [pltpu.VMEM((gw // LANES, PROJ_ROWS, LANES), F32), pltpu.VMEM((PROJ_ROWS, LANES), F32)]
    return pl.pallas_call(
        _attn_merge_kernel,
        grid=(n // PROJ_ROWS,),
        in_specs=o_specs + l_specs + [_const_spec(w_o.shape), full],
        out_specs=full,
        out_shape=jax.ShapeDtypeStruct((n, d), F32),
        scratch_shapes=scratch,
        compiler_params=_cparams("arbitrary"),
        name="attn_merge",
    )(*outs, *lses, w_o.astype(BF16), res)


def kernel(x, norm_mix_g, norm_ffn_g, rwkv_mu, rwkv_w_r, rwkv_w_k, rwkv_w_v, rwkv_w0, rwkv_w1, rwkv_w2,
           rwkv_a0, rwkv_a1, rwkv_a2, rwkv_g1, rwkv_g2, rwkv_k_k, rwkv_k_a, rwkv_r_k, rwkv_ln_w, rwkv_ln_b,
           rwkv_w_o, kv_norm_g, w_kv, attn_w_q, attn_w_o, moe_w_grp, moe_b_grp, moe_w_exp, moe_b_exp,
           moe_w_gate, moe_w_up, moe_w_down, final_norm_g):
    bsz, seq, d = x.shape
    n = bsz * seq
    depth = norm_mix_g.shape[0]
    n_rwkv = rwkv_mu.shape[0]
    h = x.reshape(n, d)
    expert_w = (moe_w_gate.astype(BF16), moe_w_up.astype(BF16), moe_w_down.astype(BF16))
    q = k_sh = v_sh = None
    for layer in range(depth):
        pending = None
        if layer < n_rwkv:
            i = layer
            r, k, v, a, lw, g = _rwkv_pre(h.reshape(bsz, seq, d), norm_mix_g[layer], rwkv_mu[i], rwkv_w0[i], rwkv_a0[i],
                                          rwkv_w_r[i], rwkv_w_k[i], rwkv_w_v[i], rwkv_w1[i], rwkv_w2[i],
                                          rwkv_a1[i], rwkv_a2[i], rwkv_g1[i], rwkv_g2[i])
            y = _rwkv_rec(r, k, v, a, lw, g, rwkv_k_k[i], rwkv_k_a[i], rwkv_r_k[i], rwkv_ln_w[i], rwkv_ln_b[i])
            pending = (y.reshape(n, d), rwkv_w_o[i])
        else:
            i = layer - n_rwkv
            qkv = _qkv_proj(h, norm_mix_g[layer], kv_norm_g, attn_w_q[i], w_kv)
            q = qkv[0::3]
            if i == 0:
                k_sh, v_sh = qkv[1::3], qkv[2::3]
            outs, lses = [], []
            for gi, (window, dilation) in enumerate(DIL_PATTERNS):
                assert window // dilation == ATT_BLOCK and seq % window == 0
                o, lse = _attn_group(q[gi], k_sh[gi], v_sh[gi], gi, dilation, bsz, seq)
                outs.append(o)
                lses.append(lse)
            h = _attn_merge(outs, lses, attn_w_o[i], h)
        last = layer == depth - 1
        h = _hier_moe_residual(h, norm_ffn_g[layer], moe_w_grp[layer], moe_b_grp[layer], moe_w_exp[layer],
                               moe_b_exp[layer], layer, *expert_w, final_g=final_norm_g if last else None,
                               pending=pending)
    return h.reshape(bsz, seq, d)
```

```python
import functools
import math

import jax
import jax.numpy as jnp
from jax import lax
from jax.experimental import pallas as pl
from jax.experimental.pallas import tpu as pltpu

F32 = jnp.float32
BF16 = jnp.bfloat16

NORM_EPS = 1e-6
LOG2_E = math.log2(math.e)
RWKV_HEAD = 64
GN_EPS = RWKV_HEAD * 1e-5
DIL_PATTERNS = ((128, 1), (512, 4), (2048, 16))
DIL_HEADS = 8
DIL_HEAD_DIM = 64
N_EXPERT_GROUPS = 4
EXPERTS_PER_GROUP = 4
LANES = 128
SUBLANES = 8
VMEM_LIMIT = 56 * 1024 * 1024

CHUNK = 64
REC_ROWS = 2048
REC_PAIRS = 2
PRE_ROWS = 512
PROJ_ROWS = 512
MOE_BLOCK = 1024
FFN_ROWS = 64
FFN_TILES = 8
ATT_BLOCK = 128
ATT_WINDOWS = 8


def _cparams(*sem):
    return pltpu.CompilerParams(dimension_semantics=sem, vmem_limit_bytes=VMEM_LIMIT)


def _dot(a, b):
    return jnp.dot(a, b, preferred_element_type=F32)


def _dot_nt(a, b):
    return lax.dot_general(a, b, (((1,), (1,)), ((), ())), preferred_element_type=F32)


def _dot_tn(a, b):
    return lax.dot_general(a, b, (((0,), (0,)), ((), ())), preferred_element_type=F32)


def _rms(x):
    return x * lax.rsqrt(jnp.mean(x * x, axis=-1, keepdims=True) + NORM_EPS)


def _sigmoid(z):
    return 1.0 / (1.0 + jnp.exp(-z))


def _const_spec(shape):
    nd = len(shape)
    return pl.BlockSpec(shape, lambda *_: (0,) * nd)


def _rwkv_pre_kernel(x_ref, gmix_ref, mu_ref, w0_ref, a0_ref, wr_ref, wk_ref, wv_ref,
                     w1_ref, w2_ref, a1_ref, a2_ref, g1_ref, g2_ref,
                     r_ref, k_ref, v_ref, a_ref, lw_ref, g_ref, prev_ref):
    @pl.when(pl.program_id(1) == 0)
    def _():
        prev_ref[...] = jnp.zeros_like(prev_ref)

    x = x_ref[0]
    rows = x.shape[0]
    xn = _rms(x) * gmix_ref[...]
    row = lax.broadcasted_iota(jnp.int32, xn.shape, 0)
    shifted = jnp.where(row == 0, prev_ref[SUBLANES - 1:, :], pltpu.roll(xn, 1, 0))
    prev_ref[...] = xn[rows - SUBLANES:, :]
    xx = shifted - xn

    def mix(i):
        return (xn + xx * mu_ref[i:i + 1, :]).astype(BF16)

    r_ref[0] = _dot(mix(0), wr_ref[...]).astype(BF16)
    k_ref[0] = _dot(mix(2), wk_ref[...]).astype(BF16)
    v_ref[0] = _dot(mix(3), wv_ref[...]).astype(BF16)
    u = w0_ref[...] + _dot(jnp.tanh(_dot(mix(1), w1_ref[...])).astype(BF16), w2_ref[...])
    w_log = -(jnp.maximum(-u, 0.0) + jnp.log(1.0 + jnp.exp(-jnp.abs(u)))) - 0.5
    lw_ref[0] = -jnp.exp(w_log) * LOG2_E
    a_ref[0] = _sigmoid(a0_ref[...] + _dot(_dot(mix(4), a1_ref[...]).astype(BF16), a2_ref[...])).astype(BF16)
    g_ref[0] = _dot(_sigmoid(_dot(mix(5), g1_ref[...])).astype(BF16), g2_ref[...]).astype(BF16)


def _pad_cols(w, n):
    return jnp.pad(w, ((0, 0), (0, n - w.shape[1])))


def _pad_rows(w, n):
    return jnp.pad(w, ((0, n - w.shape[0]), (0, 0)))


def _rwkv_pre(x, gmix, mu, w0, a0, w_r, w_k, w_v, w1, w2, a1, a2, g1, g2):
    bsz, seq, d = x.shape
    lw = -(-w1.shape[1] // LANES) * LANES
    la = -(-a1.shape[1] // LANES) * LANES
    lg = -(-g1.shape[1] // LANES) * LANES
    mu8 = jnp.pad(mu, ((0, SUBLANES - mu.shape[0]), (0, 0)))
    args = (x, gmix.reshape(1, d), mu8, w0.reshape(1, d), a0.reshape(1, d),
            w_r.astype(BF16), w_k.astype(BF16), w_v.astype(BF16),
            _pad_cols(w1, lw).astype(BF16), _pad_rows(w2, lw).astype(BF16),
            _pad_cols(a1, la).astype(BF16), _pad_rows(a2, la).astype(BF16),
            _pad_cols(g1, lg).astype(BF16), _pad_rows(g2, lg).astype(BF16))
    tile = pl.BlockSpec((1, PRE_ROWS, d), lambda b, s: (b, s, 0))
    in_specs = [tile] + [_const_spec(a.shape) for a in args[1:]]
    out_dt = (BF16, BF16, BF16, BF16, F32, BF16)
    return pl.pallas_call(
        _rwkv_pre_kernel,
        grid=(bsz, seq // PRE_ROWS),
        in_specs=in_specs,
        out_specs=[tile] * 6,
        out_shape=[jax.ShapeDtypeStruct((bsz, seq, d), t) for t in out_dt],
        scratch_shapes=[pltpu.VMEM((SUBLANES, d), F32)],
        compiler_params=_cparams("arbitrary", "arbitrary"),
        name="rwkv_pre",
    )(*args)


def _rwkv_rec_kernel(r_ref, k_ref, v_ref, a_ref, lw_ref, g_ref, kk_ref, ka_ref, rk_ref, lnw_ref, lnb_ref,
                     o_ref, state_ref):
    @pl.when(pl.program_id(2) == 0)
    def _():
        state_ref[...] = jnp.zeros_like(state_ref)

    c = CHUNK
    hd = RWKV_HEAD
    w = 2 * hd
    assert c == hd
    n = r_ref.shape[1] // c
    lane = lax.broadcasted_iota(jnp.int32, (c, w), 1)
    row = lax.broadcasted_iota(jnp.int32, (c, w), 0)
    head0 = lane < hd
    col = jnp.where(head0, lane, lane - hd)
    strict = row > col
    incl = row >= col
    eye = (row == col).astype(F32)
    tri = (lax.broadcasted_iota(jnp.int32, (c, c), 0) >= lax.broadcasted_iota(jnp.int32, (c, c), 1)).astype(BF16)
    pairs = r_ref.shape[2] // w

    def head_sum(x):
        s0 = jnp.sum(jnp.where(head0, x, 0.0), axis=-1, keepdims=True)
        s1 = jnp.sum(jnp.where(head0, 0.0, x), axis=-1, keepdims=True)
        return jnp.where(head0, s0, s1)

    def stack(x):
        return jnp.concatenate([jnp.where(head0, x, 0.0), jnp.where(head0, 0.0, x)], axis=0)

    def cumsum(x):
        hi = x.astype(BF16)
        lo = (x - hi.astype(F32)).astype(BF16)
        both = _dot(tri, jnp.concatenate([hi, lo], axis=1))
        return both[:, :w] + both[:, w:]

    zeros = jnp.zeros((c, w), BF16)
    carried = [state_ref[p] for p in range(pairs)]

    def chunk(j, p):
        rs = slice(j * c, (j + 1) * c)
        ls = slice(p * w, (p + 1) * w)
        kkp, kap, rkp, lnw, lnb = kk_ref[:, ls], ka_ref[:, ls], rk_ref[:, ls], lnw_ref[:, ls], lnb_ref[:, ls]
        r_ = r_ref[0, rs, ls].astype(F32)
        k_ = k_ref[0, rs, ls].astype(F32)
        v_ = v_ref[0, rs, ls].astype(F32)
        a_ = a_ref[0, rs, ls].astype(F32)
        lw_ = lw_ref[0, rs, ls]
        cum = cumsum(lw_)
        yield
        kk = k_ * kkp
        kk = kk * lax.rsqrt(jnp.maximum(head_sum(kk * kk), 1e-24))
        k2 = k_ * (1.0 + (a_ - 1.0) * kap)
        p_ = -(kk * a_)
        tot = cum[c - 1:c, :]
        e_neg = jnp.exp2(-cum)
        e_tot = jnp.exp2(tot - cum)
        rt = r_ * jnp.exp2(cum)
        qt = kk * jnp.exp2(cum - lw_)
        vb = v_.astype(BF16)
        amat = _dot_nt(jnp.concatenate([qt, rt], axis=0).astype(BF16),
                       jnp.concatenate([stack((k2 * e_neg).astype(BF16)), stack((p_ * e_neg).astype(BF16))], axis=0))
        yield
        a_qp = jnp.where(strict, amat[:c, w:], 0.0)
        a_rp = jnp.where(incl, amat[c:, w:], 0.0).astype(BF16)
        a_k = jnp.concatenate([jnp.where(strict, amat[:c, :w], 0.0),
                               jnp.where(incl, amat[c:, :w], 0.0)], axis=0).astype(BF16)
        av = _dot(a_k, stack(vb))
        inv = eye + a_qp
        power = a_qp.astype(BF16)
        power = _dot(power, stack(power))
        yield
        steps = int(math.log2(c)) - 1
        for i in range(steps):
            pb = power.astype(BF16)
            if i + 1 < steps:
                both = _dot(jnp.concatenate([inv.astype(BF16), pb], axis=0), stack(pb))
                inv = inv + both[:c]
                power = both[c:]
            else:
                inv = inv + _dot(inv.astype(BF16), stack(pb))
            yield
        wu = _dot(inv.astype(BF16),
                  jnp.concatenate([stack(qt.astype(BF16)), stack(av[:c].astype(BF16))], axis=1))
        wub = wu.astype(BF16)
        yield
        ry = _dot(a_rp, jnp.concatenate([stack(wub[:, :w]), stack(wub[:, w:])], axis=1))
        mg = _dot_tn(jnp.concatenate([p_ * e_tot, k2 * e_tot], axis=0).astype(BF16),
                     jnp.concatenate([wub, jnp.concatenate([zeros, vb], axis=1)], axis=0))
        yield
        rw = (rt + ry[:, :w]).astype(BF16)
        m_w = jnp.where(head0, mg[:c, :w], mg[c:, :w]).astype(BF16)
        g_w = jnp.where(head0, mg[:c, w:], mg[c:, w:])
        gam = head_sum(eye * jnp.exp2(tot))
        state = carried[p]
        both = _dot(jnp.concatenate([rw, m_w], axis=0), stack(state.astype(BF16)))
        y = both[:c] + av[c:] + ry[:, w:]
        carried[p] = gam * state + both[c:] + g_w
        yield
        yc = y - head_sum(y) * (1.0 / hd)
        var = head_sum(yc * yc) * (1.0 / hd)
        yn = yc * lax.rsqrt(var + GN_EPS) * lnw + lnb
        bonus = head_sum(r_ * k2 * rkp) * v_
        o_ref[0, rs, ls] = ((yn + bonus) * g_ref[0, rs, ls].astype(F32)).astype(BF16)

    live, started = [], 0
    while started < n or live:
        if started < n:
            live.extend(chunk(started, p) for p in range(pairs))
            started += 1
        for gen in list(live):
            if next(gen, "done") == "done":
                live.remove(gen)
    for p in range(pairs):
        state_ref[p] = carried[p]


def _rwkv_rec(r, k, v, a, lw, g, k_k, k_a, r_k, ln_w, ln_b):
    bsz, seq, d = r.shape
    pair_w = 2 * RWKV_HEAD
    hw = REC_PAIRS * pair_w
    tile = pl.BlockSpec((1, REC_ROWS, hw), lambda b, h, s: (b, s, h))
    par = pl.BlockSpec((1, hw), lambda b, h, s: (0, h))
    params = [p.reshape(1, d).astype(F32) for p in (k_k, k_a, r_k, ln_w, ln_b)]
    return pl.pallas_call(
        _rwkv_rec_kernel,
        grid=(bsz, d // hw, seq // REC_ROWS),
        in_specs=[tile] * 6 + [par] * 5,
        out_specs=tile,
        out_shape=jax.ShapeDtypeStruct((bsz, seq, d), BF16),
        scratch_shapes=[pltpu.VMEM((REC_PAIRS, CHUNK, pair_w), F32)],
        compiler_params=_cparams("arbitrary", "arbitrary", "arbitrary"),
        name="rwkv_rec",
    )(r, k, v, a, lw, g, *params)


def _moe_route_kernel(h_ref, g_ref, wr_ref, br_ref, xs_ref, gs_ref, dest_ref, cnt_ref):
    _route_block(h_ref[...], g_ref, wr_ref, br_ref, xs_ref, gs_ref, dest_ref, cnt_ref)


def _moe_route_proj_kernel(a_ref, w_ref, res_ref, g_ref, wr_ref, br_ref, h_ref, xs_ref, gs_ref, dest_ref, cnt_ref):
    h = res_ref[...] + _dot(a_ref[...], w_ref[...])
    h_ref[...] = h
    _route_block(h, g_ref, wr_ref, br_ref, xs_ref, gs_ref, dest_ref, cnt_ref)


def _route_block(h, g_ref, wr_ref, br_ref, xs_ref, gs_ref, dest_ref, cnt_ref):
    nb = h.shape[0]
    nbp = xs_ref.shape[0]
    tb = (_rms(h) * g_ref[...]).astype(BF16)
    logits = _dot(tb, wr_ref[...]) + br_ref[...]
    lane = lax.broadcasted_iota(jnp.int32, logits.shape, 1).astype(F32)
    neg = jnp.float32(-jnp.inf)
    big = jnp.float32(LANES)
    is_grp = lane < N_EXPERT_GROUPS
    gl = jnp.where(is_grp, logits, neg)
    gmax = jnp.max(gl, axis=-1, keepdims=True)
    grp = jnp.min(jnp.where(gl == gmax, lane, big), axis=-1, keepdims=True)
    p_grp = 1.0 / jnp.sum(jnp.where(is_grp, jnp.exp(gl - gmax), 0.0), axis=-1, keepdims=True)
    e_lo = N_EXPERT_GROUPS + grp * EXPERTS_PER_GROUP
    in_grp = (lane >= e_lo) & (lane < e_lo + EXPERTS_PER_GROUP)
    el = jnp.where(in_grp, logits, neg)
    v1 = jnp.max(el, axis=-1, keepdims=True)
    i1 = jnp.min(jnp.where(el == v1, lane, big), axis=-1, keepdims=True)
    el2 = jnp.where(lane == i1, neg, el)
    v2 = jnp.max(el2, axis=-1, keepdims=True)
    i2 = jnp.min(jnp.where(el2 == v2, lane, big), axis=-1, keepdims=True)
    e2 = jnp.exp(v2 - v1)
    w1 = 1.0 / (1.0 + e2)
    w2 = e2 / (1.0 + e2)
    gates = jnp.where(lane == i1, w1, jnp.where(lane == i2, w2, 0.0)) * p_grp

    onehot = (lane == grp).astype(BF16)
    ri = lax.broadcasted_iota(jnp.int32, (nb, nb), 0)
    ci = lax.broadcasted_iota(jnp.int32, (nb, nb), 1)
    rank = _dot((ri > ci).astype(BF16), onehot)
    counts = jnp.sum(onehot.astype(F32), axis=0, keepdims=True)
    padded = jnp.ceil(counts / FFN_ROWS) * FFN_ROWS
    lane1 = lax.broadcasted_iota(jnp.int32, (1, LANES), 1)
    offs = jnp.zeros((1, LANES), F32)
    for gidx in range(1, N_EXPERT_GROUPS):
        prev = jnp.sum(jnp.where(lane1 < gidx, padded, 0.0), axis=-1, keepdims=True)
        offs = jnp.where(lane1 == gidx, prev, offs)
    dest = jnp.sum(onehot.astype(F32) * (rank + offs), axis=-1, keepdims=True)
    slot = lax.broadcasted_iota(jnp.int32, (nb, nbp), 1).astype(F32)
    perm_t = (dest == slot).astype(BF16)
    xs_ref[...] = _dot_tn(perm_t, tb).astype(BF16)
    g_hi = gates.astype(BF16)
    g_lo = (gates - g_hi.astype(F32)).astype(BF16)
    moved = _dot_tn(perm_t, jnp.concatenate([g_hi, g_lo], axis=1))
    gs_ref[...] = moved[:, :LANES] + moved[:, LANES:]
    dest_ref[...] = dest
    cnt_ref[0] = jnp.where(lane1 < N_EXPERT_GROUPS, counts, 0.0)


def _moe_ffn_kernel(tile_ref, grp_ref, real_ref, *refs):
    nt = FFN_TILES
    xs_refs, gs_refs = refs[:nt], refs[nt:2 * nt]
    wg_ref, wu_ref, wd_ref, ys_ref = refs[2 * nt:]
    i = pl.program_id(0)

    @pl.when(real_ref[i] > 0)
    def _():
        x = jnp.concatenate([r[...] for r in xs_refs], axis=0)
        gates = jnp.concatenate([r[...] for r in gs_refs], axis=0)
        lane = lax.broadcasted_iota(jnp.int32, gates.shape, 1)
        base = N_EXPERT_GROUPS + grp_ref[i] * EXPERTS_PER_GROUP
        acc = jnp.zeros(ys_ref.shape, F32)
        for e in range(EXPERTS_PER_GROUP):
            ge = jnp.sum(jnp.where(lane == base + e, gates, 0.0), axis=-1, keepdims=True)
            gate_act = _dot(x, wg_ref[e])
            hdn = gate_act * _sigmoid(gate_act) * _dot(x, wu_ref[e])
            acc = acc + ge * _dot(hdn.astype(BF16), wd_ref[e])
        ys_ref[...] = acc.astype(BF16)

    @pl.when(real_ref[i] == 0)
    def _():
        ys_ref[...] = jnp.zeros_like(ys_ref)


def _moe_merge_kernel(slot_ref, h_ref, dest_ref, *refs):
    ys_refs, o_ref = refs[:-1], refs[-1]
    ys = jnp.concatenate([r[...] for r in ys_refs], axis=0)
    nb, nbp = h_ref.shape[0], ys.shape[0]
    slot = lax.broadcasted_iota(jnp.int32, (nb, nbp), 1).astype(F32)
    perm_t = (dest_ref[...] == slot).astype(BF16)
    o_ref[...] = h_ref[...] + _dot(perm_t, ys)


def _moe_merge_norm_kernel(slot_ref, h_ref, dest_ref, *refs):
    ys_refs, g_ref, o_ref = refs[:-2], refs[-2], refs[-1]
    ys = jnp.concatenate([r[...] for r in ys_refs], axis=0)
    nb, nbp = h_ref.shape[0], ys.shape[0]
    slot = lax.broadcasted_iota(jnp.int32, (nb, nbp), 1).astype(F32)
    perm_t = (dest_ref[...] == slot).astype(BF16)
    o_ref[...] = _rms(h_ref[...] + _dot(perm_t, ys)) * g_ref[...]


def _ffn_slots(n_tiles):
    return -(-(n_tiles + N_EXPERT_GROUPS * (FFN_TILES - 1)) // FFN_TILES) * FFN_TILES


def _ffn_schedule(counts, nblk, tiles_per_blk):
    i32 = jnp.int32
    tiles = (counts + FFN_ROWS - 1) // FFN_ROWS
    first_rel = jnp.cumsum(tiles, axis=1) - tiles
    used_blk = jnp.sum(tiles, axis=1)
    grp_pad = (jnp.sum(tiles, axis=0) + FFN_TILES - 1) // FFN_TILES * FFN_TILES
    grp_start = jnp.cumsum(grp_pad) - grp_pad
    before = jnp.cumsum(tiles, axis=0) - tiles
    n_tiles = nblk * tiles_per_blk
    t = jnp.arange(n_tiles, dtype=i32)
    b, k = t // tiles_per_blk, t % tiles_per_blk
    used = k < used_blk[b]
    grp_t = jnp.minimum(jnp.sum(((first_rel + tiles)[b] <= k[:, None]).astype(i32), axis=1), N_EXPERT_GROUPS - 1)
    pos = grp_start[grp_t] + before[b, grp_t] + (k - first_rel[b, grp_t])
    s = jnp.arange(_ffn_slots(n_tiles), dtype=i32)
    hit = used[None, :] & (pos[None, :] == s[:, None])
    slot_has = jnp.sum(hit.astype(i32), axis=1) > 0
    free_rank = jnp.cumsum((~slot_has).astype(i32)) - 1
    unused_rank = jnp.cumsum((~used).astype(i32)) - 1
    fill = (~used)[None, :] & (unused_rank[None, :] == free_rank[:, None])
    placed = jnp.where(slot_has[:, None], hit, fill)
    slot_tile = jnp.sum(jnp.where(placed, t[None, :], 0), axis=1).astype(i32)
    tile_slot = jnp.sum(jnp.where(placed, s[:, None], 0), axis=0).astype(i32)
    slot_grp = jnp.sum(jnp.where(hit, grp_t[None, :], 0), axis=1).reshape(-1, FFN_TILES)
    step_real = jnp.sum(slot_has.astype(i32).reshape(-1, FFN_TILES), axis=1).astype(i32)
    n_real_steps = jnp.sum(grp_pad) // FFN_TILES
    last_grp = slot_grp[jnp.maximum(n_real_steps - 1, 0), 0]
    step_grp = jnp.where(step_real > 0, slot_grp[:, 0], last_grp).astype(i32)
    return slot_tile, tile_slot, step_grp, step_real


def _hier_moe_residual(h, norm_g, w_grp, b_grp, w_exp, b_exp, layer, w_gate, w_up, w_down, final_g=None,
                       pending=None):
    n, d = h.shape
    nb = MOE_BLOCK
    nblk = n // nb
    nbp = nb + N_EXPERT_GROUPS * FFN_ROWS
    tiles_per_blk = nbp // FFN_ROWS
    n_exp = N_EXPERT_GROUPS * EXPERTS_PER_GROUP
    w_router = jnp.pad(jnp.concatenate([w_grp, w_exp], axis=1), ((0, 0), (0, LANES - N_EXPERT_GROUPS - n_exp)))
    b_router = jnp.pad(jnp.concatenate([b_grp, b_exp]), (0, LANES - N_EXPERT_GROUPS - n_exp)).reshape(1, LANES)

    rows = pl.BlockSpec((nb, d), lambda i: (i, 0))
    route_in = [_const_spec((1, d)), _const_spec((d, LANES)), _const_spec((1, LANES))]
    route_args = (norm_g.reshape(1, d), w_router.astype(BF16), b_router)
    route_out = [pl.BlockSpec((nbp, d), lambda i: (i, 0)), pl.BlockSpec((nbp, LANES), lambda i: (i, 0)),
                 pl.BlockSpec((nb, 1), lambda i: (i, 0)), pl.BlockSpec((1, 1, LANES), lambda i: (i, 0, 0))]
    route_shape = [jax.ShapeDtypeStruct((nblk * nbp, d), BF16), jax.ShapeDtypeStruct((nblk * nbp, LANES), F32),
                   jax.ShapeDtypeStruct((n, 1), F32), jax.ShapeDtypeStruct((nblk, 1, LANES), F32)]
    if pending is None:
        xs, gs, dest, counts = pl.pallas_call(
            _moe_route_kernel, grid=(nblk,), in_specs=[rows] + route_in, out_specs=route_out, out_shape=route_shape,
            compiler_params=_cparams("arbitrary"), name="moe_route",
        )(h, *route_args)
    else:
        a, w = pending
        h, xs, gs, dest, counts = pl.pallas_call(
            _moe_route_proj_kernel, grid=(nblk,),
            in_specs=[pl.BlockSpec((nb, a.shape[1]), lambda i: (i, 0)), _const_spec(w.shape), rows] + route_in,
            out_specs=[rows] + route_out, out_shape=[jax.ShapeDtypeStruct((n, d), F32)] + route_shape,
            compiler_params=_cparams("arbitrary"), name="moe_route_proj",
        )(a, w.astype(BF16), h, *route_args)

    n_tiles = nblk * tiles_per_blk
    n_slots = _ffn_slots(n_tiles)
    slot_tile, tile_slot, step_grp, step_real = _ffn_schedule(
        counts[:, 0, :N_EXPERT_GROUPS].astype(jnp.int32), nblk, tiles_per_blk)

    def tile_spec(width, j):
        return pl.BlockSpec((FFN_ROWS, width), lambda i, tile, grp, real: (tile[i * FFN_TILES + j], 0))

    def wspec(shape):
        return pl.BlockSpec((None, EXPERTS_PER_GROUP) + shape, lambda i, tile, grp, real: (layer, grp[i], 0, 0))

    f = w_gate.shape[-1]
    ys = pl.pallas_call(
        _moe_ffn_kernel,
        grid_spec=pltpu.PrefetchScalarGridSpec(
            num_scalar_prefetch=3,
            grid=(n_slots // FFN_TILES,),
            in_specs=[tile_spec(d, j) for j in range(FFN_TILES)] + [tile_spec(LANES, j) for j in range(FFN_TILES)]
            + [wspec((d, f)), wspec((d, f)), wspec((f, d))],
            out_specs=pl.BlockSpec((FFN_TILES * FFN_ROWS, d), lambda i, tile, grp, real: (i, 0)),
        ),
        out_shape=jax.ShapeDtypeStruct((n_slots * FFN_ROWS, d), BF16),
        compiler_params=_cparams("arbitrary"),
        name="moe_ffn",
    )(slot_tile, step_grp, step_real, *([xs] * FFN_TILES), *([gs] * FFN_TILES),
      w_gate, w_up, w_down)

    blk = pl.BlockSpec((nb, d), lambda i, slot: (i, 0))
    in_specs = [blk, pl.BlockSpec((nb, 1), lambda i, slot: (i, 0))]
    in_specs += [pl.BlockSpec((FFN_ROWS, d), functools.partial(lambda j, i, slot: (slot[i * tiles_per_blk + j], 0), j))
                 for j in range(tiles_per_blk)]
    args = [h, dest] + [ys] * tiles_per_blk
    body = _moe_merge_kernel
    if final_g is not None:
        in_specs.append(pl.BlockSpec((1, d), lambda i, slot: (0, 0)))
        args.append(final_g.reshape(1, d))
        body = _moe_merge_norm_kernel
    return pl.pallas_call(
        body,
        grid_spec=pltpu.PrefetchScalarGridSpec(num_scalar_prefetch=1, grid=(nblk,), in_specs=in_specs, out_specs=blk),
        out_shape=jax.ShapeDtypeStruct((n, d), F32),
        compiler_params=_cparams("arbitrary"),
        name="moe_merge",
    )(tile_slot, *args)


GROUP_WIDTH = DIL_HEADS * DIL_HEAD_DIM
N_STRIDED = 3 * sum(1 for _, dil in DIL_PATTERNS if dil > 1)


def _qkv_kernel(h_ref, gq_ref, gkv_ref, wq_ref, wk_ref, wv_ref, *refs):
    outs, scratch = refs[:-1], refs[-1]
    n = _rms(h_ref[...])
    xq = (n * gq_ref[...]).astype(BF16)
    xkv = (n * gkv_ref[...]).astype(BF16)
    gw = GROUP_WIDTH
    slot = 0
    full = (_dot(xq, wq_ref[...]), _dot(xkv, wk_ref[...]), _dot(xkv, wv_ref[...]))
    for gi, (_, dil) in enumerate(DIL_PATTERNS):
        cols = slice(gi * gw, (gi + 1) * gw)
        for t in range(3):
            res = full[t][:, cols]
            out_ref = outs[3 * gi + t]
            if dil == 1:
                out_ref[...] = res.astype(BF16)
                continue
            sub = res.shape[0] // dil
            for c in range(gw // LANES):
                scratch[slot, c] = res[:, c * LANES:(c + 1) * LANES]
            for r in range(dil):
                for c in range(gw // LANES):
                    lo = r * gw + c * LANES
                    out_ref[:, lo:lo + LANES] = scratch[slot, c, pl.ds(r, sub, stride=dil), :].astype(BF16)
            slot += 1


def _qkv_proj(h, g_q, g_kv, w_q, w_kv):
    n, d = h.shape
    qw = w_q.shape[1]
    gw = GROUP_WIDTH
    scale = math.log2(math.e) / math.sqrt(DIL_HEAD_DIM)
    rows = pl.BlockSpec((PROJ_ROWS, d), lambda i: (i, 0))
    out_specs, out_shape = [], []
    for _, dil in DIL_PATTERNS:
        for _ in range(3):
            out_specs.append(pl.BlockSpec((PROJ_ROWS // dil, dil * gw), lambda i: (i, 0)))
            out_shape.append(jax.ShapeDtypeStruct((n // dil, dil * gw), BF16))
    return pl.pallas_call(
        _qkv_kernel,
        grid=(n // PROJ_ROWS,),
        in_specs=[rows, _const_spec((1, d)), _const_spec((1, d))] + [_const_spec((d, qw))] * 3,
        out_specs=out_specs,
        out_shape=out_shape,
        scratch_shapes=[pltpu.VMEM((N_STRIDED, gw // LANES, PROJ_ROWS, LANES), F32)],
        compiler_params=_cparams("arbitrary"),
        name="qkv_proj",
    )(h, g_q.reshape(1, d), g_kv.reshape(1, d), (w_q * scale).astype(BF16),
      w_kv[:, :qw].astype(BF16), w_kv[:, qw:].astype(BF16))


def _attn_kernel(q_ref, kc_ref, kp_ref, vc_ref, vp_ref, o_ref, lse_ref):
    n = pl.program_id(2)
    blk = ATT_BLOCK
    pw = 2 * DIL_HEAD_DIM
    ri = lax.broadcasted_iota(jnp.int32, (blk, blk), 0)
    ci = lax.broadcasted_iota(jnp.int32, (blk, blk), 1)
    lowest = jnp.finfo(F32).min
    cap_prev = jnp.where(ci >= ri, jnp.inf, lowest).astype(F32)
    cap_first = jnp.where(n > 0, cap_prev, lowest)
    cap_own = jnp.where(ci <= ri, jnp.inf, lowest).astype(F32)
    head0 = lax.broadcasted_iota(jnp.int32, (blk, pw), 1) < DIL_HEAD_DIM
    zero = jnp.zeros((blk, pw), BF16)
    lane_t = lax.broadcasted_iota(jnp.int32, (blk, LANES), 1)

    def stack(x):
        return jnp.concatenate([jnp.where(head0, x, zero), jnp.where(head0, zero, x)], axis=0)

    streams = [(i, p) for i in range(q_ref.shape[1] // blk) for p in range(DIL_HEADS // 2)]

    def rows(i):
        return slice(i * blk, (i + 1) * blk)

    def lanes(p):
        return slice(p * pw, (p + 1) * pw)

    stacked = {}

    def kv_blocks(cur_ref, prev_ref, i, p):
        def one(w):
            name = (id(cur_ref), w, p)
            if name not in stacked:
                stacked[name] = stack(prev_ref[0, :, lanes(p)] if w < 0 else cur_ref[0, rows(w), lanes(p)])
            return stacked[name]
        return jnp.concatenate([one(i - 1), one(i)], axis=0)

    s = []
    for i, p in streams:
        raw = _dot_nt(q_ref[0, rows(i), lanes(p)], kv_blocks(kc_ref, kp_ref, i, p))
        cap = cap_first if i == 0 else cap_prev
        s.append([jnp.minimum(raw[:, :blk], cap), jnp.minimum(raw[:, blk:2 * blk], cap),
                  jnp.minimum(raw[:, 2 * blk:3 * blk], cap_own), jnp.minimum(raw[:, 3 * blk:], cap_own)])
    m_a = [jnp.max(jnp.maximum(x[0], x[2]), axis=-1, keepdims=True) for x in s]
    m_b = [jnp.max(jnp.maximum(x[1], x[3]), axis=-1, keepdims=True) for x in s]
    e = [[jnp.exp2(x[0] - ma), jnp.exp2(x[1] - mb), jnp.exp2(x[2] - ma), jnp.exp2(x[3] - mb)]
         for x, ma, mb in zip(s, m_a, m_b)]
    pb = [jnp.concatenate([y.astype(BF16) for y in x], axis=1) for x in e]
    blk_head = (lax.broadcasted_iota(jnp.int32, (4 * blk, LANES), 0) // blk) % 2
    sel_lane = lax.broadcasted_iota(jnp.int32, (4 * blk, LANES), 1) - DIL_HEADS
    sel = [(sel_lane == 2 * p + blk_head).astype(BF16) for p in range(DIL_HEADS // 2)]
    pv = [_dot(pb[idx], jnp.concatenate([kv_blocks(vc_ref, vp_ref, i, p), sel[p]], axis=1))
          for idx, (i, p) in enumerate(streams)]
    tiles = {}
    for idx, (i, p) in enumerate(streams):
        o_ref[0, rows(i), lanes(p)] = pv[idx][:, :pw].astype(BF16)
        tile = tiles.get(i, jnp.zeros((blk, LANES), F32)) + pv[idx][:, pw:]
        tiles[i] = jnp.where(lane_t == 2 * p, m_a[idx], jnp.where(lane_t == 2 * p + 1, m_b[idx], tile))
    for i, tile in tiles.items():
        lse_ref[0, rows(i), :] = tile


def _attn_group(q, k, v, gi, dilation, bsz, seq):
    gw = GROUP_WIDTH
    sub = seq // dilation
    windows = min(ATT_WINDOWS, sub // ATT_BLOCK)
    step_rows = windows * ATT_BLOCK
    assert sub % step_rows == 0
    view = (bsz, sub, dilation * gw)
    qv, kv_, vv = q.reshape(view), k.reshape(view), v.reshape(view)
    cur = pl.BlockSpec((1, step_rows, gw), lambda b, r, n: (b, n, r))
    prev = pl.BlockSpec((1, ATT_BLOCK, gw), lambda b, r, n: (b, jnp.maximum(n * windows - 1, 0), r))
    lse_spec = pl.BlockSpec((1, step_rows, LANES), lambda b, r, n: (b, n, r))
    o, lse = pl.pallas_call(
        _attn_kernel,
        grid=(bsz, dilation, sub // step_rows),
        in_specs=[cur, cur, prev, cur, prev],
        out_specs=[cur, lse_spec],
        out_shape=[jax.ShapeDtypeStruct(view, BF16),
                   jax.ShapeDtypeStruct((bsz, sub, dilation * LANES), F32)],
        compiler_params=_cparams("arbitrary", "arbitrary", "arbitrary"),
        name=f"attn_group{gi}",
    )(qv, kv_, kv_, vv, vv)
    return o.reshape(bsz * sub, dilation * gw), lse.reshape(bsz * sub, dilation * LANES)


def _attn_merge_kernel(o0_ref, o1_ref, o2_ref, l0_ref, l1_ref, l2_ref, w_ref, res_ref, out_ref, *scratch):
    rows = res_ref.shape[0]
    gw = GROUP_WIDTH
    outs, lses = [], []
    slot = 0
    for (_, dil), o_ref, l_ref in zip(DIL_PATTERNS, (o0_ref, o1_ref, o2_ref), (l0_ref, l1_ref, l2_ref)):
        if dil == 1:
            outs.append(o_ref[...].astype(F32))
            lses.append(l_ref[...])
            continue
        o_scr, l_scr = scratch[2 * slot], scratch[2 * slot + 1]
        sub = rows // dil
        for r in range(dil):
            for c in range(gw // LANES):
                lo = r * gw + c * LANES
                o_scr[c, pl.ds(r, sub, stride=dil), :] = o_ref[:, lo:lo + LANES].astype(F32)
            l_scr[pl.ds(r, sub, stride=dil), :] = l_ref[:, r * LANES:(r + 1) * LANES]
        outs.append(jnp.concatenate([o_scr[c] for c in range(gw // LANES)], axis=1))
        lses.append(l_scr[...])
        slot += 1
    m = jnp.maximum(jnp.maximum(lses[0], lses[1]), lses[2])
    e = [jnp.exp2(l - m) for l in lses]
    sums = [pltpu.roll(l, LANES - DIL_HEADS, 1) for l in lses]
    denom = e[0] * sums[0] + e[1] * sums[1] + e[2] * sums[2]
    is_head = lax.broadcasted_iota(jnp.int32, (rows, LANES), 1) < DIL_HEADS
    expand = (lax.broadcasted_iota(jnp.int32, (LANES, gw), 1) // DIL_HEAD_DIM
              == lax.broadcasted_iota(jnp.int32, (LANES, gw), 0)).astype(BF16)
    mix = jnp.zeros((rows, gw), F32)
    for g in range(len(outs)):
        alpha = jnp.where(is_head, e[g] / denom, 0.0)
        hi = alpha.astype(BF16)
        lo = (alpha - hi.astype(F32)).astype(BF16)
        mix = mix + (_dot(hi, expand) + _dot(lo, expand)) * outs[g]
    out_ref[...] = res_ref[...] + _dot(mix.astype(BF16), w_ref[...])


def _attn_merge(outs, lses, w_o, res):
    n, d = res.shape
    gw = GROUP_WIDTH
    o_specs = [pl.BlockSpec((PROJ_ROWS // dil, dil * gw), lambda i: (i, 0)) for _, dil in DIL_PATTERNS]
    l_specs = [pl.BlockSpec((PROJ_ROWS // dil, dil * LANES), lambda i: (i, 0)) for _, dil in DIL_PATTERNS]
    full = pl.BlockSpec((PROJ_ROWS, d), lambda i: (i, 0))
    scratch = []
    for _, dil in DIL_PATTERNS:
        if dil > 1:
            scratch += [pltpu.VMEM((gw // LANES, PROJ_ROWS, LANES), F32), pltpu.VMEM((PROJ_ROWS, LANES), F32)]
    return pl.pallas_call(
        _attn_merge_kernel,
        grid=(n // PROJ_ROWS,),
        in_specs=o_specs + l_specs + [_const_spec(w_o.shape), full],
        out_specs=full,
        out_shape=jax.ShapeDtypeStruct((n, d), F32),
        scratch_shapes=scratch,
        compiler_params=_cparams("arbitrary"),
        name="attn_merge",
    )(*outs, *lses, w_o.astype(BF16), res)


def kernel(x, norm_mix_g, norm_ffn_g, rwkv_mu, rwkv_w_r, rwkv_w_k, rwkv_w_v, rwkv_w0, rwkv_w1, rwkv_w2,
           rwkv_a0, rwkv_a1, rwkv_a2, rwkv_g1, rwkv_g2, rwkv_k_k, rwkv_k_a, rwkv_r_k, rwkv_ln_w, rwkv_ln_b,
           rwkv_w_o, kv_norm_g, w_kv, attn_w_q, attn_w_o, moe_w_grp, moe_b_grp, moe_w_exp, moe_b_exp,
           moe_w_gate, moe_w_up, moe_w_down, final_norm_g):
    bsz, seq, d = x.shape
    n = bsz * seq
    depth = norm_mix_g.shape[0]
    n_rwkv = rwkv_mu.shape[0]
    h = x.reshape(n, d)
    expert_w = (moe_w_gate.astype(BF16), moe_w_up.astype(BF16), moe_w_down.astype(BF16))
    q = k_sh = v_sh = None
    for layer in range(depth):
        pending = None
        if layer < n_rwkv:
            i = layer
            r, k, v, a, lw, g = _rwkv_pre(h.reshape(bsz, seq, d), norm_mix_g[layer], rwkv_mu[i], rwkv_w0[i], rwkv_a0[i],
                                          rwkv_w_r[i], rwkv_w_k[i], rwkv_w_v[i], rwkv_w1[i], rwkv_w2[i],
                                          rwkv_a1[i], rwkv_a2[i], rwkv_g1[i], rwkv_g2[i])
            y = _rwkv_rec(r, k, v, a, lw, g, rwkv_k_k[i], rwkv_k_a[i], rwkv_r_k[i], rwkv_ln_w[i], rwkv_ln_b[i])
            pending = (y.reshape(n, d), rwkv_w_o[i])
        else:
            i = layer - n_rwkv
            qkv = _qkv_proj(h, norm_mix_g[layer], kv_norm_g, attn_w_q[i], w_kv)
            q = qkv[0::3]
            if i == 0:
                k_sh, v_sh = qkv[1::3], qkv[2::3]
            outs, lses = [], []
            for gi, (window, dilation) in enumerate(DIL_PATTERNS):
                assert window // dilation == ATT_BLOCK and seq % window == 0
                o, lse = _attn_group(q[gi], k_sh[gi], v_sh[gi], gi, dilation, bsz, seq)
                outs.append(o)
                lses.append(lse)
            h = _attn_merge(outs, lses, attn_w_o[i], h)
        last = layer == depth - 1
        h = _hier_moe_residual(h, norm_ffn_g[layer], moe_w_grp[layer], moe_b_grp[layer], moe_w_exp[layer],
                               moe_b_exp[layer], layer, *expert_w, final_g=final_norm_g if last else None,
                               pending=pending)
    return h.reshape(bsz, seq, d)
```

```python
import functools
import math

import jax
import jax.numpy as jnp
from jax import lax
from jax.experimental import pallas as pl
from jax.experimental.pallas import tpu as pltpu

F32 = jnp.float32
BF16 = jnp.bfloat16

NORM_EPS = 1e-6
LOG2_E = math.log2(math.e)
RWKV_HEAD = 64
GN_EPS = RWKV_HEAD * 1e-5
DIL_PATTERNS = ((128, 1), (512, 4), (2048, 16))
DIL_HEADS = 8
DIL_HEAD_DIM = 64
N_EXPERT_GROUPS = 4
EXPERTS_PER_GROUP = 4
LANES = 128
SUBLANES = 8
VMEM_LIMIT = 56 * 1024 * 1024

CHUNK = 64
REC_ROWS = 2048
REC_PAIRS = 2
PRE_ROWS = 512
PROJ_ROWS = 512
MOE_BLOCK = 1024
FFN_ROWS = 64
FFN_TILES = 8
ATT_BLOCK = 128
ATT_WINDOWS = 8


def _cparams(*sem):
    return pltpu.CompilerParams(dimension_semantics=sem, vmem_limit_bytes=VMEM_LIMIT)


def _dot(a, b):
    return jnp.dot(a, b, preferred_element_type=F32)


def _dot_nt(a, b):
    return lax.dot_general(a, b, (((1,), (1,)), ((), ())), preferred_element_type=F32)


def _dot_tn(a, b):
    return lax.dot_general(a, b, (((0,), (0,)), ((), ())), preferred_element_type=F32)


def _rms(x):
    return x * lax.rsqrt(jnp.mean(x * x, axis=-1, keepdims=True) + NORM_EPS)


def _sigmoid(z):
    return 1.0 / (1.0 + jnp.exp(-z))


def _const_spec(shape):
    nd = len(shape)
    return pl.BlockSpec(shape, lambda *_: (0,) * nd)


def _rwkv_pre_kernel(x_ref, gmix_ref, mu_ref, w0_ref, a0_ref, wr_ref, wk_ref, wv_ref,
                     w1_ref, w2_ref, a1_ref, a2_ref, g1_ref, g2_ref,
                     r_ref, k_ref, v_ref, a_ref, lw_ref, g_ref, prev_ref):
    @pl.when(pl.program_id(1) == 0)
    def _():
        prev_ref[...] = jnp.zeros_like(prev_ref)

    x = x_ref[0]
    rows = x.shape[0]
    xn = _rms(x) * gmix_ref[...]
    row = lax.broadcasted_iota(jnp.int32, xn.shape, 0)
    shifted = jnp.where(row == 0, prev_ref[SUBLANES - 1:, :], pltpu.roll(xn, 1, 0))
    prev_ref[...] = xn[rows - SUBLANES:, :]
    xx = shifted - xn

    def mix(i):
        return (xn + xx * mu_ref[i:i + 1, :]).astype(BF16)

    r_ref[0] = _dot(mix(0), wr_ref[...]).astype(BF16)
    k_ref[0] = _dot(mix(2), wk_ref[...]).astype(BF16)
    v_ref[0] = _dot(mix(3), wv_ref[...]).astype(BF16)
    u = w0_ref[...] + _dot(jnp.tanh(_dot(mix(1), w1_ref[...])).astype(BF16), w2_ref[...])
    w_log = -(jnp.maximum(-u, 0.0) + jnp.log(1.0 + jnp.exp(-jnp.abs(u)))) - 0.5
    lw_ref[0] = -jnp.exp(w_log) * LOG2_E
    a_ref[0] = _sigmoid(a0_ref[...] + _dot(_dot(mix(4), a1_ref[...]).astype(BF16), a2_ref[...])).astype(BF16)
    g_ref[0] = _dot(_sigmoid(_dot(mix(5), g1_ref[...])).astype(BF16), g2_ref[...]).astype(BF16)


def _pad_cols(w, n):
    return jnp.pad(w, ((0, 0), (0, n - w.shape[1])))


def _pad_rows(w, n):
    return jnp.pad(w, ((0, n - w.shape[0]), (0, 0)))


def _rwkv_pre(x, gmix, mu, w0, a0, w_r, w_k, w_v, w1, w2, a1, a2, g1, g2):
    bsz, seq, d = x.shape
    lw = -(-w1.shape[1] // LANES) * LANES
    la = -(-a1.shape[1] // LANES) * LANES
    lg = -(-g1.shape[1] // LANES) * LANES
    mu8 = jnp.pad(mu, ((0, SUBLANES - mu.shape[0]), (0, 0)))
    args = (x, gmix.reshape(1, d), mu8, w0.reshape(1, d), a0.reshape(1, d),
            w_r.astype(BF16), w_k.astype(BF16), w_v.astype(BF16),
            _pad_cols(w1, lw).astype(BF16), _pad_rows(w2, lw).astype(BF16),
            _pad_cols(a1, la).astype(BF16), _pad_rows(a2, la).astype(BF16),
            _pad_cols(g1, lg).astype(BF16), _pad_rows(g2, lg).astype(BF16))
    tile = pl.BlockSpec((1, PRE_ROWS, d), lambda b, s: (b, s, 0))
    in_specs = [tile] + [_const_spec(a.shape) for a in args[1:]]
    out_dt = (BF16, BF16, BF16, BF16, F32, BF16)
    return pl.pallas_call(
        _rwkv_pre_kernel,
        grid=(bsz, seq // PRE_ROWS),
        in_specs=in_specs,
        out_specs=[tile] * 6,
        out_shape=[jax.ShapeDtypeStruct((bsz, seq, d), t) for t in out_dt],
        scratch_shapes=[pltpu.VMEM((SUBLANES, d), F32)],
        compiler_params=_cparams("arbitrary", "arbitrary"),
        name="rwkv_pre",
    )(*args)


def _rwkv_rec_kernel(r_ref, k_ref, v_ref, a_ref, lw_ref, g_ref, kk_ref, ka_ref, rk_ref, lnw_ref, lnb_ref,
                     o_ref, state_ref):
    @pl.when(pl.program_id(2) == 0)
    def _():
        state_ref[...] = jnp.zeros_like(state_ref)

    c = CHUNK
    hd = RWKV_HEAD
    w = 2 * hd
    assert c == hd
    n = r_ref.shape[1] // c
    lane = lax.broadcasted_iota(jnp.int32, (c, w), 1)
    row = lax.broadcasted_iota(jnp.int32, (c, w), 0)
    head0 = lane < hd
    col = jnp.where(head0, lane, lane - hd)
    strict = row > col
    incl = row >= col
    eye = (row == col).astype(F32)
    tri = (lax.broadcasted_iota(jnp.int32, (c, c), 0) >= lax.broadcasted_iota(jnp.int32, (c, c), 1)).astype(BF16)
    pairs = r_ref.shape[2] // w

    def head_sum(x):
        s0 = jnp.sum(jnp.where(head0, x, 0.0), axis=-1, keepdims=True)
        s1 = jnp.sum(jnp.where(head0, 0.0, x), axis=-1, keepdims=True)
        return jnp.where(head0, s0, s1)

    def stack(x):
        return jnp.concatenate([jnp.where(head0, x, 0.0), jnp.where(head0, 0.0, x)], axis=0)

    def cumsum(x):
        hi = x.astype(BF16)
        lo = (x - hi.astype(F32)).astype(BF16)
        both = _dot(tri, jnp.concatenate([hi, lo], axis=1))
        return both[:, :w] + both[:, w:]

    zeros = jnp.zeros((c, w), BF16)
    carried = [state_ref[p] for p in range(pairs)]

    def chunk(j, p):
        rs = slice(j * c, (j + 1) * c)
        ls = slice(p * w, (p + 1) * w)
        kkp, kap, rkp, lnw, lnb = kk_ref[:, ls], ka_ref[:, ls], rk_ref[:, ls], lnw_ref[:, ls], lnb_ref[:, ls]
        r_ = r_ref[0, rs, ls].astype(F32)
        k_ = k_ref[0, rs, ls].astype(F32)
        v_ = v_ref[0, rs, ls].astype(F32)
        a_ = a_ref[0, rs, ls].astype(F32)
        lw_ = lw_ref[0, rs, ls]
        cum = cumsum(lw_)
        yield
        kk = k_ * kkp
        kk = kk * lax.rsqrt(jnp.maximum(head_sum(kk * kk), 1e-24))
        k2 = k_ * (1.0 + (a_ - 1.0) * kap)
        p_ = -(kk * a_)
        tot = cum[c - 1:c, :]
        e_neg = jnp.exp2(-cum)
        e_tot = jnp.exp2(tot - cum)
        rt = r_ * jnp.exp2(cum)
        qt = kk * jnp.exp2(cum - lw_)
        vb = v_.astype(BF16)
        amat = _dot_nt(jnp.concatenate([qt, rt], axis=0).astype(BF16),
                       jnp.concatenate([stack((k2 * e_neg).astype(BF16)), stack((p_ * e_neg).astype(BF16))], axis=0))
        yield
        a_qp = jnp.where(strict, amat[:c, w:], 0.0)
        a_rp = jnp.where(incl, amat[c:, w:], 0.0).astype(BF16)
        a_k = jnp.concatenate([jnp.where(strict, amat[:c, :w], 0.0),
                               jnp.where(incl, amat[c:, :w], 0.0)], axis=0).astype(BF16)
        av = _dot(a_k, stack(vb))
        inv = eye + a_qp
        power = a_qp.astype(BF16)
        power = _dot(power, stack(power))
        yield
        steps = int(math.log2(c)) - 1
        for i in range(steps):
            pb = power.astype(BF16)
            if i + 1 < steps:
                both = _dot(jnp.concatenate([inv.astype(BF16), pb], axis=0), stack(pb))
                inv = inv + both[:c]
                power = both[c:]
            else:
                inv = inv + _dot(inv.astype(BF16), stack(pb))
            yield
        wu = _dot(inv.astype(BF16),
                  jnp.concatenate([stack(qt.astype(BF16)), stack(av[:c].astype(BF16))], axis=1))
        wub = wu.astype(BF16)
        yield
        ry = _dot(a_rp, jnp.concatenate([stack(wub[:, :w]), stack(wub[:, w:])], axis=1))
        mg = _dot_tn(jnp.concatenate([p_ * e_tot, k2 * e_tot], axis=0).astype(BF16),
                     jnp.concatenate([wub, jnp.concatenate([zeros, vb], axis=1)], axis=0))
        yield
        rw = (rt + ry[:, :w]).astype(BF16)
        m_w = jnp.where(head0, mg[:c, :w], mg[c:, :w]).astype(BF16)
        g_w = jnp.where(head0, mg[:c, w:], mg[c:, w:])
        gam = head_sum(eye * jnp.exp2(tot))
        state = carried[p]
        both = _dot(jnp.concatenate([rw, m_w], axis=0), stack(state.astype(BF16)))
        y = both[:c] + av[c:] + ry[:, w:]
        carried[p] = gam * state + both[c:] + g_w
        yield
        yc = y - head_sum(y) * (1.0 / hd)
        var = head_sum(yc * yc) * (1.0 / hd)
        yn = yc * lax.rsqrt(var + GN_EPS) * lnw + lnb
        bonus = head_sum(r_ * k2 * rkp) * v_
        o_ref[0, rs, ls] = ((yn + bonus) * g_ref[0, rs, ls].astype(F32)).astype(BF16)

    live, started = [], 0
    while started < n or live:
        if started < n:
            live.extend(chunk(started, p) for p in range(pairs))
            started += 1
        for gen in list(live):
            if next(gen, "done") == "done":
                live.remove(gen)
    for p in range(pairs):
        state_ref[p] = carried[p]


def _rwkv_rec(r, k, v, a, lw, g, k_k, k_a, r_k, ln_w, ln_b):
    bsz, seq, d = r.shape
    pair_w = 2 * RWKV_HEAD
    hw = REC_PAIRS * pair_w
    tile = pl.BlockSpec((1, REC_ROWS, hw), lambda b, h, s: (b, s, h))
    par = pl.BlockSpec((1, hw), lambda b, h, s: (0, h))
    params = [p.reshape(1, d).astype(F32) for p in (k_k, k_a, r_k, ln_w, ln_b)]
    return pl.pallas_call(
        _rwkv_rec_kernel,
        grid=(bsz, d // hw, seq // REC_ROWS),
        in_specs=[tile] * 6 + [par] * 5,
        out_specs=tile,
        out_shape=jax.ShapeDtypeStruct((bsz, seq, d), BF16),
        scratch_shapes=[pltpu.VMEM((REC_PAIRS, CHUNK, pair_w), F32)],
        compiler_params=_cparams("arbitrary", "arbitrary", "arbitrary"),
        name="rwkv_rec",
    )(r, k, v, a, lw, g, *params)


def _moe_route_kernel(h_ref, g_ref, wr_ref, br_ref, xs_ref, gs_ref, dest_ref, cnt_ref):
    _route_block(h_ref[...], g_ref, wr_ref, br_ref, xs_ref, gs_ref, dest_ref, cnt_ref)


def _moe_route_proj_kernel(a_ref, w_ref, res_ref, g_ref, wr_ref, br_ref, h_ref, xs_ref, gs_ref, dest_ref, cnt_ref):
    h = res_ref[...] + _dot(a_ref[...], w_ref[...])
    h_ref[...] = h
    _route_block(h, g_ref, wr_ref, br_ref, xs_ref, gs_ref, dest_ref, cnt_ref)


def _route_block(h, g_ref, wr_ref, br_ref, xs_ref, gs_ref, dest_ref, cnt_ref):
    nb = h.shape[0]
    nbp = xs_ref.shape[0]
    tb = (_rms(h) * g_ref[...]).astype(BF16)
    logits = _dot(tb, wr_ref[...]) + br_ref[...]
    lane = lax.broadcasted_iota(jnp.int32, logits.shape, 1).astype(F32)
    neg = jnp.float32(-jnp.inf)
    big = jnp.float32(LANES)
    is_grp = lane < N_EXPERT_GROUPS
    gl = jnp.where(is_grp, logits, neg)
    gmax = jnp.max(gl, axis=-1, keepdims=True)
    grp = jnp.min(jnp.where(gl == gmax, lane, big), axis=-1, keepdims=True)
    p_grp = 1.0 / jnp.sum(jnp.where(is_grp, jnp.exp(gl - gmax), 0.0), axis=-1, keepdims=True)
    e_lo = N_EXPERT_GROUPS + grp * EXPERTS_PER_GROUP
    in_grp = (lane >= e_lo) & (lane < e_lo + EXPERTS_PER_GROUP)
    el = jnp.where(in_grp, logits, neg)
    v1 = jnp.max(el, axis=-1, keepdims=True)
    i1 = jnp.min(jnp.where(el == v1, lane, big), axis=-1, keepdims=True)
    el2 = jnp.where(lane == i1, neg, el)
    v2 = jnp.max(el2, axis=-1, keepdims=True)
    i2 = jnp.min(jnp.where(el2 == v2, lane, big), axis=-1, keepdims=True)
    e2 = jnp.exp(v2 - v1)
    w1 = 1.0 / (1.0 + e2)
    w2 = e2 / (1.0 + e2)
    gates = jnp.where(lane == i1, w1, jnp.where(lane == i2, w2, 0.0)) * p_grp

    onehot = (lane == grp).astype(BF16)
    ri = lax.broadcasted_iota(jnp.int32, (nb, nb), 0)
    ci = lax.broadcasted_iota(jnp.int32, (nb, nb), 1)
    rank = _dot((ri > ci).astype(BF16), onehot)
    counts = jnp.sum(onehot.astype(F32), axis=0, keepdims=True)
    padded = jnp.ceil(counts / FFN_ROWS) * FFN_ROWS
    lane1 = lax.broadcasted_iota(jnp.int32, (1, LANES), 1)
    offs = jnp.zeros((1, LANES), F32)
    for gidx in range(1, N_EXPERT_GROUPS):
        prev = jnp.sum(jnp.where(lane1 < gidx, padded, 0.0), axis=-1, keepdims=True)
        offs = jnp.where(lane1 == gidx, prev, offs)
    dest = jnp.sum(onehot.astype(F32) * (rank + offs), axis=-1, keepdims=True)
    slot = lax.broadcasted_iota(jnp.int32, (nb, nbp), 1).astype(F32)
    perm_t = (dest == slot).astype(BF16)
    xs_ref[...] = _dot_tn(perm_t, tb).astype(BF16)
    g_hi = gates.astype(BF16)
    g_lo = (gates - g_hi.astype(F32)).astype(BF16)
    moved = _dot_tn(perm_t, jnp.concatenate([g_hi, g_lo], axis=1))
    gs_ref[...] = moved[:, :LANES] + moved[:, LANES:]
    dest_ref[...] = dest
    cnt_ref[0] = jnp.where(lane1 < N_EXPERT_GROUPS, counts, 0.0)


def _moe_ffn_kernel(tile_ref, grp_ref, real_ref, *refs):
    nt = FFN_TILES
    xs_refs, gs_refs = refs[:nt], refs[nt:2 * nt]
    wg_ref, wu_ref, wd_ref, ys_ref = refs[2 * nt:]
    i = pl.program_id(0)

    @pl.when(real_ref[i] > 0)
    def _():
        x = jnp.concatenate([r[...] for r in xs_refs], axis=0)
        gates = jnp.concatenate([r[...] for r in gs_refs], axis=0)
        lane = lax.broadcasted_iota(jnp.int32, gates.shape, 1)
        base = N_EXPERT_GROUPS + grp_ref[i] * EXPERTS_PER_GROUP
        acc = jnp.zeros(ys_ref.shape, F32)
        for e in range(EXPERTS_PER_GROUP):
            ge = jnp.sum(jnp.where(lane == base + e, gates, 0.0), axis=-1, keepdims=True)
            gate_act = _dot(x, wg_ref[e])
            hdn = gate_act * _sigmoid(gate_act) * _dot(x, wu_ref[e])
            acc = acc + ge * _dot(hdn.astype(BF16), wd_ref[e])
        ys_ref[...] = acc.astype(BF16)

    @pl.when(real_ref[i] == 0)
    def _():
        ys_ref[...] = jnp.zeros_like(ys_ref)


def _moe_merge_kernel(slot_ref, h_ref, dest_ref, *refs):
    ys_refs, o_ref = refs[:-1], refs[-1]
    ys = jnp.concatenate([r[...] for r in ys_refs], axis=0)
    nb, nbp = h_ref.shape[0], ys.shape[0]
    slot = lax.broadcasted_iota(jnp.int32, (nb, nbp), 1).astype(F32)
    perm_t = (dest_ref[...] == slot).astype(BF16)
    o_ref[...] = h_ref[...] + _dot(perm_t, ys)


def _moe_merge_norm_kernel(slot_ref, h_ref, dest_ref, *refs):
    ys_refs, g_ref, o_ref = refs[:-2], refs[-2], refs[-1]
    ys = jnp.concatenate([r[...] for r in ys_refs], axis=0)
    nb, nbp = h_ref.shape[0], ys.shape[0]
    slot = lax.broadcasted_iota(jnp.int32, (nb, nbp), 1).astype(F32)
    perm_t = (dest_ref[...] == slot).astype(BF16)
    o_ref[...] = _rms(h_ref[...] + _dot(perm_t, ys)) * g_ref[...]


def _ffn_slots(n_tiles):
    return -(-(n_tiles + N_EXPERT_GROUPS * (FFN_TILES - 1)) // FFN_TILES) * FFN_TILES


def _ffn_schedule(counts, nblk, tiles_per_blk):
    i32 = jnp.int32
    tiles = (counts + FFN_ROWS - 1) // FFN_ROWS
    first_rel = jnp.cumsum(tiles, axis=1) - tiles
    used_blk = jnp.sum(tiles, axis=1)
    grp_pad = (jnp.sum(tiles, axis=0) + FFN_TILES - 1) // FFN_TILES * FFN_TILES
    grp_start = jnp.cumsum(grp_pad) - grp_pad
    before = jnp.cumsum(tiles, axis=0) - tiles
    n_tiles = nblk * tiles_per_blk
    t = jnp.arange(n_tiles, dtype=i32)
    b, k = t // tiles_per_blk, t % tiles_per_blk
    used = k < used_blk[b]
    grp_t = jnp.minimum(jnp.sum(((first_rel + tiles)[b] <= k[:, None]).astype(i32), axis=1), N_EXPERT_GROUPS - 1)
    pos = grp_start[grp_t] + before[b, grp_t] + (k - first_rel[b, grp_t])
    s = jnp.arange(_ffn_slots(n_tiles), dtype=i32)
    hit = used[None, :] & (pos[None, :] == s[:, None])
    slot_has = jnp.sum(hit.astype(i32), axis=1) > 0
    free_rank = jnp.cumsum((~slot_has).astype(i32)) - 1
    unused_rank = jnp.cumsum((~used).astype(i32)) - 1
    fill = (~used)[None, :] & (unused_rank[None, :] == free_rank[:, None])
    placed = jnp.where(slot_has[:, None], hit, fill)
    slot_tile = jnp.sum(jnp.where(placed, t[None, :], 0), axis=1).astype(i32)
    tile_slot = jnp.sum(jnp.where(placed, s[:, None], 0), axis=0).astype(i32)
    slot_grp = jnp.sum(jnp.where(hit, grp_t[None, :], 0), axis=1).reshape(-1, FFN_TILES)
    step_real = jnp.sum(slot_has.astype(i32).reshape(-1, FFN_TILES), axis=1).astype(i32)
    n_real_steps = jnp.sum(grp_pad) // FFN_TILES
    last_grp = slot_grp[jnp.maximum(n_real_steps - 1, 0), 0]
    step_grp = jnp.where(step_real > 0, slot_grp[:, 0], last_grp).astype(i32)
    return slot_tile, tile_slot, step_grp, step_real


def _hier_moe_residual(h, norm_g, w_grp, b_grp, w_exp, b_exp, layer, w_gate, w_up, w_down, final_g=None,
                       pending=None):
    n, d = h.shape
    nb = MOE_BLOCK
    nblk = n // nb
    nbp = nb + N_EXPERT_GROUPS * FFN_ROWS
    tiles_per_blk = nbp // FFN_ROWS
    n_exp = N_EXPERT_GROUPS * EXPERTS_PER_GROUP
    w_router = jnp.pad(jnp.concatenate([w_grp, w_exp], axis=1), ((0, 0), (0, LANES - N_EXPERT_GROUPS - n_exp)))
    b_router = jnp.pad(jnp.concatenate([b_grp, b_exp]), (0, LANES - N_EXPERT_GROUPS - n_exp)).reshape(1, LANES)

    rows = pl.BlockSpec((nb, d), lambda i: (i, 0))
    route_in = [_const_spec((1, d)), _const_spec((d, LANES)), _const_spec((1, LANES))]
    route_args = (norm_g.reshape(1, d), w_router.astype(BF16), b_router)
    route_out = [pl.BlockSpec((nbp, d), lambda i: (i, 0)), pl.BlockSpec((nbp, LANES), lambda i: (i, 0)),
                 pl.BlockSpec((nb, 1), lambda i: (i, 0)), pl.BlockSpec((1, 1, LANES), lambda i: (i, 0, 0))]
    route_shape = [jax.ShapeDtypeStruct((nblk * nbp, d), BF16), jax.ShapeDtypeStruct((nblk * nbp, LANES), F32),
                   jax.ShapeDtypeStruct((n, 1), F32), jax.ShapeDtypeStruct((nblk, 1, LANES), F32)]
    if pending is None:
        xs, gs, dest, counts = pl.pallas_call(
            _moe_route_kernel, grid=(nblk,), in_specs=[rows] + route_in, out_specs=route_out, out_shape=route_shape,
            compiler_params=_cparams("arbitrary"), name="moe_route",
        )(h, *route_args)
    else:
        a, w = pending
        h, xs, gs, dest, counts = pl.pallas_call(
            _moe_route_proj_kernel, grid=(nblk,),
            in_specs=[pl.BlockSpec((nb, a.shape[1]), lambda i: (i, 0)), _const_spec(w.shape), rows] + route_in,
            out_specs=[rows] + route_out, out_shape=[jax.ShapeDtypeStruct((n, d), F32)] + route_shape,
            compiler_params=_cparams("arbitrary"), name="moe_route_proj",
        )(a, w.astype(BF16), h, *route_args)

    n_tiles = nblk * tiles_per_blk
    n_slots = _ffn_slots(n_tiles)
    slot_tile, tile_slot, step_grp, step_real = _ffn_schedule(
        counts[:, 0, :N_EXPERT_GROUPS].astype(jnp.int32), nblk, tiles_per_blk)

    def tile_spec(width, j):
        return pl.BlockSpec((FFN_ROWS, width), lambda i, tile, grp, real: (tile[i * FFN_TILES + j], 0))

    def wspec(shape):
        return pl.BlockSpec((None, EXPERTS_PER_GROUP) + shape, lambda i, tile, grp, real: (layer, grp[i], 0, 0))

    f = w_gate.shape[-1]
    ys = pl.pallas_call(
        _moe_ffn_kernel,
        grid_spec=pltpu.PrefetchScalarGridSpec(
            num_scalar_prefetch=3,
            grid=(n_slots // FFN_TILES,),
            in_specs=[tile_spec(d, j) for j in range(FFN_TILES)] + [tile_spec(LANES, j) for j in range(FFN_TILES)]
            + [wspec((d, f)), wspec((d, f)), wspec((f, d))],
            out_specs=pl.BlockSpec((FFN_TILES * FFN_ROWS, d), lambda i, tile, grp, real: (i, 0)),
        ),
        out_shape=jax.ShapeDtypeStruct((n_slots * FFN_ROWS, d), BF16),
        compiler_params=_cparams("arbitrary"),
        name="moe_ffn",
    )(slot_tile, step_grp, step_real, *([xs] * FFN_TILES), *([gs] * FFN_TILES),
      w_gate, w_up, w_down)

    blk = pl.BlockSpec((nb, d), lambda i, slot: (i, 0))
    in_specs = [blk, pl.BlockSpec((nb, 1), lambda i, slot: (i, 0))]
    in_specs += [pl.BlockSpec((FFN_ROWS, d), functools.partial(lambda j, i, slot: (slot[i * tiles_per_blk + j], 0), j))
                 for j in range(tiles_per_blk)]
    args = [h, dest] + [ys] * tiles_per_blk
    body = _moe_merge_kernel
    if final_g is not None:
        in_specs.append(pl.BlockSpec((1, d), lambda i, slot: (0, 0)))
        args.append(final_g.reshape(1, d))
        body = _moe_merge_norm_kernel
    return pl.pallas_call(
        body,
        grid_spec=pltpu.PrefetchScalarGridSpec(num_scalar_prefetch=1, grid=(nblk,), in_specs=in_specs, out_specs=blk),
        out_shape=jax.ShapeDtypeStruct((n, d), F32),
        compiler_params=_cparams("arbitrary"),
        name="moe_merge",
    )(tile_slot, *args)


GROUP_WIDTH = DIL_HEADS * DIL_HEAD_DIM
N_STRIDED = 3 * sum(1 for _, dil in DIL_PATTERNS if dil > 1)


def _qkv_kernel(h_ref, gq_ref, gkv_ref, wq_ref, wk_ref, wv_ref, *refs):
    outs, scratch = refs[:-1], refs[-1]
    n = _rms(h_ref[...])
    xq = (n * gq_ref[...]).astype(BF16)
    xkv = (n * gkv_ref[...]).astype(BF16)
    gw = GROUP_WIDTH
    slot = 0
    full = (_dot(xq, wq_ref[...]), _dot(xkv, wk_ref[...]), _dot(xkv, wv_ref[...]))
    for gi, (_, dil) in enumerate(DIL_PATTERNS):
        cols = slice(gi * gw, (gi + 1) * gw)
        for t in range(3):
            res = full[t][:, cols]
            out_ref = outs[3 * gi + t]
            if dil == 1:
                out_ref[...] = res.astype(BF16)
                continue
            sub = res.shape[0] // dil
            for c in range(gw // LANES):
                scratch[slot, c] = res[:, c * LANES:(c + 1) * LANES]
            for r in range(dil):
                for c in range(gw // LANES):
                    lo = r * gw + c * LANES
                    out_ref[:, lo:lo + LANES] = scratch[slot, c, pl.ds(r, sub, stride=dil), :].astype(BF16)
            slot += 1


def _qkv_proj(h, g_q, g_kv, w_q, w_kv):
    n, d = h.shape
    qw = w_q.shape[1]
    gw = GROUP_WIDTH
    scale = math.log2(math.e) / math.sqrt(DIL_HEAD_DIM)
    rows = pl.BlockSpec((PROJ_ROWS, d), lambda i: (i, 0))
    out_specs, out_shape = [], []
    for _, dil in DIL_PATTERNS:
        for _ in range(3):
            out_specs.append(pl.BlockSpec((PROJ_ROWS // dil, dil * gw), lambda i: (i, 0)))
            out_shape.append(jax.ShapeDtypeStruct((n // dil, dil * gw), BF16))
    return pl.pallas_call(
        _qkv_kernel,
        grid=(n // PROJ_ROWS,),
        in_specs=[rows, _const_spec((1, d)), _const_spec((1, d))] + [_const_spec((d, qw))] * 3,
        out_specs=out_specs,
        out_shape=out_shape,
        scratch_shapes=[pltpu.VMEM((N_STRIDED, gw // LANES, PROJ_ROWS, LANES), F32)],
        compiler_params=_cparams("arbitrary"),
        name="qkv_proj",
    )(h, g_q.reshape(1, d), g_kv.reshape(1, d), (w_q * scale).astype(BF16),
      w_kv[:, :qw].astype(BF16), w_kv[:, qw:].astype(BF16))


def _attn_kernel(q_ref, kc_ref, kp_ref, vc_ref, vp_ref, o_ref, lse_ref):
    n = pl.program_id(2)
    blk = ATT_BLOCK
    pw = 2 * DIL_HEAD_DIM
    ri = lax.broadcasted_iota(jnp.int32, (blk, blk), 0)
    ci = lax.broadcasted_iota(jnp.int32, (blk, blk), 1)
    lowest = jnp.finfo(F32).min
    cap_prev = jnp.where(ci >= ri, jnp.inf, lowest).astype(F32)
    cap_first = jnp.where(n > 0, cap_prev, lowest)
    cap_own = jnp.where(ci <= ri, jnp.inf, lowest).astype(F32)
    head0 = lax.broadcasted_iota(jnp.int32, (blk, pw), 1) < DIL_HEAD_DIM
    zero = jnp.zeros((blk, pw), BF16)
    lane_t = lax.broadcasted_iota(jnp.int32, (blk, LANES), 1)

    def stack(x):
        return jnp.concatenate([jnp.where(head0, x, zero), jnp.where(head0, zero, x)], axis=0)

    streams = [(i, p) for i in range(q_ref.shape[1] // blk) for p in range(DIL_HEADS // 2)]

    def rows(i):
        return slice(i * blk, (i + 1) * blk)

    def lanes(p):
        return slice(p * pw, (p + 1) * pw)

    stacked = {}

    def kv_blocks(cur_ref, prev_ref, i, p):
        def one(w):
            name = (id(cur_ref), w, p)
            if name not in stacked:
                stacked[name] = stack(prev_ref[0, :, lanes(p)] if w < 0 else cur_ref[0, rows(w), lanes(p)])
            return stacked[name]
        return jnp.concatenate([one(i - 1), one(i)], axis=0)

    s = []
    for i, p in streams:
        raw = _dot_nt(q_ref[0, rows(i), lanes(p)], kv_blocks(kc_ref, kp_ref, i, p))
        cap = cap_first if i == 0 else cap_prev
        s.append([jnp.minimum(raw[:, :blk], cap), jnp.minimum(raw[:, blk:2 * blk], cap),
                  jnp.minimum(raw[:, 2 * blk:3 * blk], cap_own), jnp.minimum(raw[:, 3 * blk:], cap_own)])
    m_a = [jnp.max(jnp.maximum(x[0], x[2]), axis=-1, keepdims=True) for x in s]
    m_b = [jnp.max(jnp.maximum(x[1], x[3]), axis=-1, keepdims=True) for x in s]
    e = [[jnp.exp2(x[0] - ma), jnp.exp2(x[1] - mb), jnp.exp2(x[2] - ma), jnp.exp2(x[3] - mb)]
         for x, ma, mb in zip(s, m_a, m_b)]
    pb = [jnp.concatenate([y.astype(BF16) for y in x], axis=1) for x in e]
    blk_head = (lax.broadcasted_iota(jnp.int32, (4 * blk, LANES), 0) // blk) % 2
    sel_lane = lax.broadcasted_iota(jnp.int32, (4 * blk, LANES), 1) - DIL_HEADS
    sel = [(sel_lane == 2 * p + blk_head).astype(BF16) for p in range(DIL_HEADS // 2)]
    pv = [_dot(pb[idx], jnp.concatenate([kv_blocks(vc_ref, vp_ref, i, p), sel[p]], axis=1))
          for idx, (i, p) in enumerate(streams)]
    tiles = {}
    for idx, (i, p) in enumerate(streams):
        o_ref[0, rows(i), lanes(p)] = pv[idx][:, :pw].astype(BF16)
        tile = tiles.get(i, jnp.zeros((blk, LANES), F32)) + pv[idx][:, pw:]
        tiles[i] = jnp.where(lane_t == 2 * p, m_a[idx], jnp.where(lane_t == 2 * p + 1, m_b[idx], tile))
    for i, tile in tiles.items():
        lse_ref[0, rows(i), :] = tile


def _attn_group(q, k, v, gi, dilation, bsz, seq):
    gw = GROUP_WIDTH
    sub = seq // dilation
    windows = min(ATT_WINDOWS, sub // ATT_BLOCK)
    step_rows = windows * ATT_BLOCK
    assert sub % step_rows == 0
    view = (bsz, sub, dilation * gw)
    qv, kv_, vv = q.reshape(view), k.reshape(view), v.reshape(view)
    cur = pl.BlockSpec((1, step_rows, gw), lambda b, r, n: (b, n, r))
    prev = pl.BlockSpec((1, ATT_BLOCK, gw), lambda b, r, n: (b, jnp.maximum(n * windows - 1, 0), r))
    lse_spec = pl.BlockSpec((1, step_rows, LANES), lambda b, r, n: (b, n, r))
    o, lse = pl.pallas_call(
        _attn_kernel,
        grid=(bsz, dilation, sub // step_rows),
        in_specs=[cur, cur, prev, cur, prev],
        out_specs=[cur, lse_spec],
        out_shape=[jax.ShapeDtypeStruct(view, BF16),
                   jax.ShapeDtypeStruct((bsz, sub, dilation * LANES), F32)],
        compiler_params=_cparams("arbitrary", "arbitrary", "arbitrary"),
        name=f"attn_group{gi}",
    )(qv, kv_, kv_, vv, vv)
    return o.reshape(bsz * sub, dilation * gw), lse.reshape(bsz * sub, dilation * LANES)


def _attn_merge_kernel(o0_ref, o1_ref, o2_ref, l0_ref, l1_ref, l2_ref, w_ref, res_ref, out_ref, *scratch):
    rows = res_ref.shape[0]
    gw = GROUP_WIDTH
    outs, lses = [], []
    slot = 0
    for (_, dil), o_ref, l_ref in zip(DIL_PATTERNS, (o0_ref, o1_ref, o2_ref), (l0_ref, l1_ref, l2_ref)):
        if dil == 1:
            outs.append(o_ref[...].astype(F32))
            lses.append(l_ref[...])
            continue
        o_scr, l_scr = scratch[2 * slot], scratch[2 * slot + 1]
        sub = rows // dil
        for r in range(dil):
            for c in range(gw // LANES):
                lo = r * gw + c * LANES
                o_scr[c, pl.ds(r, sub, stride=dil), :] = o_ref[:, lo:lo + LANES].astype(F32)
            l_scr[pl.ds(r, sub, stride=dil), :] = l_ref[:, r * LANES:(r + 1) * LANES]
        outs.append(jnp.concatenate([o_scr[c] for c in range(gw // LANES)], axis=1))
        lses.append(l_scr[...])
        slot += 1
    m = jnp.maximum(jnp.maximum(lses[0], lses[1]), lses[2])
    e = [jnp.exp2(l - m) for l in lses]
    sums = [pltpu.roll(l, LANES - DIL_HEADS, 1) for l in lses]
    denom = e[0] * sums[0] + e[1] * sums[1] + e[2] * sums[2]
    is_head = lax.broadcasted_iota(jnp.int32, (rows, LANES), 1) < DIL_HEADS
    expand = (lax.broadcasted_iota(jnp.int32, (LANES, gw), 1) // DIL_HEAD_DIM
              == lax.broadcasted_iota(jnp.int32, (LANES, gw), 0)).astype(BF16)
    alpha = jnp.concatenate([jnp.where(is_head, x / denom, 0.0) for x in e], axis=0)
    hi = alpha.astype(BF16)
    lo = (alpha - hi.astype(F32)).astype(BF16)
    wide = _dot(jnp.concatenate([hi, lo], axis=1), jnp.concatenate([expand, expand], axis=0))
    mix = jnp.zeros((rows, gw), F32)
    for g in range(len(outs)):
        mix = mix + wide[g * rows:(g + 1) * rows] * outs[g]
    out_ref[...] = res_ref[...] + _dot(mix.astype(BF16), w_ref[...])


def _attn_merge(outs, lses, w_o, res):
    n, d = res.shape
    gw = GROUP_WIDTH
    o_specs = [pl.BlockSpec((PROJ_ROWS // dil, dil * gw), lambda i: (i, 0)) for _, dil in DIL_PATTERNS]
    l_specs = [pl.BlockSpec((PROJ_ROWS // dil, dil * LANES), lambda i: (i, 0)) for _, dil in DIL_PATTERNS]
    full = pl.BlockSpec((PROJ_ROWS, d), lambda i: (i, 0))
    scratch = []
    for _, dil in DIL_PATTERNS:
        if dil > 1:
            scratch += [pltpu.VMEM((gw // LANES, PROJ_ROWS, LANES), F32), pltpu.VMEM((PROJ_ROWS, LANES), F32)]
    return pl.pallas_call(
        _attn_merge_kernel,
        grid=(n // PROJ_ROWS,),
        in_specs=o_specs + l_specs + [_const_spec(w_o.shape), full],
        out_specs=full,
        out_shape=jax.ShapeDtypeStruct((n, d), F32),
        scratch_shapes=scratch,
        compiler_params=_cparams("arbitrary"),
        name="attn_merge",
    )(*outs, *lses, w_o.astype(BF16), res)


def kernel(x, norm_mix_g, norm_ffn_g, rwkv_mu, rwkv_w_r, rwkv_w_k, rwkv_w_v, rwkv_w0, rwkv_w1, rwkv_w2,
           rwkv_a0, rwkv_a1, rwkv_a2, rwkv_g1, rwkv_g2, rwkv_k_k, rwkv_k_a, rwkv_r_k, rwkv_ln_w, rwkv_ln_b,
           rwkv_w_o, kv_norm_g, w_kv, attn_w_q, attn_w_o, moe_w_grp, moe_b_grp, moe_w_exp, moe_b_exp,
           moe_w_gate, moe_w_up, moe_w_down, final_norm_g):
    bsz, seq, d = x.shape
    n = bsz * seq
    depth = norm_mix_g.shape[0]
    n_rwkv = rwkv_mu.shape[0]
    h = x.reshape(n, d)
    expert_w = (moe_w_gate.astype(BF16), moe_w_up.astype(BF16), moe_w_down.astype(BF16))
    q = k_sh = v_sh = None
    for layer in range(depth):
        pending = None
        if layer < n_rwkv:
            i = layer
            r, k, v, a, lw, g = _rwkv_pre(h.reshape(bsz, seq, d), norm_mix_g[layer], rwkv_mu[i], rwkv_w0[i], rwkv_a0[i],
                                          rwkv_w_r[i], rwkv_w_k[i], rwkv_w_v[i], rwkv_w1[i], rwkv_w2[i],
                                          rwkv_a1[i], rwkv_a2[i], rwkv_g1[i], rwkv_g2[i])
            y = _rwkv_rec(r, k, v, a, lw, g, rwkv_k_k[i], rwkv_k_a[i], rwkv_r_k[i], rwkv_ln_w[i], rwkv_ln_b[i])
            pending = (y.reshape(n, d), rwkv_w_o[i])
        else:
            i = layer - n_rwkv
            qkv = _qkv_proj(h, norm_mix_g[layer], kv_norm_g, attn_w_q[i], w_kv)
            q = qkv[0::3]
            if i == 0:
                k_sh, v_sh = qkv[1::3], qkv[2::3]
            outs, lses = [], []
            for gi, (window, dilation) in enumerate(DIL_PATTERNS):
                assert window // dilation == ATT_BLOCK and seq % window == 0
                o, lse = _attn_group(q[gi], k_sh[gi], v_sh[gi], gi, dilation, bsz, seq)
                outs.append(o)
                lses.append(lse)
            h = _attn_merge(outs, lses, attn_w_o[i], h)
        last = layer == depth - 1
        h = _hier_moe_residual(h, norm_ffn_g[layer], moe_w_grp[layer], moe_b_grp[layer], moe_w_exp[layer],
                               moe_b_exp[layer], layer, *expert_w, final_g=final_norm_g if last else None,
                               pending=pending)
    return h.reshape(bsz, seq, d)
```

```python
import functools
import math

import jax
import jax.numpy as jnp
from jax import lax
from jax.experimental import pallas as pl
from jax.experimental.pallas import tpu as pltpu

F32 = jnp.float32
BF16 = jnp.bfloat16

NORM_EPS = 1e-6
LOG2_E = math.log2(math.e)
RWKV_HEAD = 64
GN_EPS = RWKV_HEAD * 1e-5
DIL_PATTERNS = ((128, 1), (512, 4), (2048, 16))
DIL_HEADS = 8
DIL_HEAD_DIM = 64
N_EXPERT_GROUPS = 4
EXPERTS_PER_GROUP = 4
LANES = 128
SUBLANES = 8
VMEM_LIMIT = 56 * 1024 * 1024

CHUNK = 64
REC_ROWS = 2048
REC_PAIRS = 2
PRE_ROWS = 512
PROJ_ROWS = 512
MOE_BLOCK = 1024
FFN_ROWS = 64
FFN_TILES = 8
ATT_BLOCK = 128
ATT_WINDOWS = 8


def _cparams(*sem):
    return pltpu.CompilerParams(dimension_semantics=sem, vmem_limit_bytes=VMEM_LIMIT)


def _dot(a, b):
    return jnp.dot(a, b, preferred_element_type=F32)


def _dot_nt(a, b):
    return lax.dot_general(a, b, (((1,), (1,)), ((), ())), preferred_element_type=F32)


def _dot_tn(a, b):
    return lax.dot_general(a, b, (((0,), (0,)), ((), ())), preferred_element_type=F32)


def _rms(x):
    return x * lax.rsqrt(jnp.mean(x * x, axis=-1, keepdims=True) + NORM_EPS)


def _sigmoid(z):
    return 1.0 / (1.0 + jnp.exp(-z))


def _const_spec(shape):
    nd = len(shape)
    return pl.BlockSpec(shape, lambda *_: (0,) * nd)


def _rwkv_pre_kernel(x_ref, gmix_ref, mu_ref, w0_ref, a0_ref, wr_ref, wk_ref, wv_ref,
                     w1_ref, w2_ref, a1_ref, a2_ref, g1_ref, g2_ref,
                     r_ref, k_ref, v_ref, a_ref, lw_ref, g_ref, prev_ref):
    @pl.when(pl.program_id(1) == 0)
    def _():
        prev_ref[...] = jnp.zeros_like(prev_ref)

    x = x_ref[0]
    rows = x.shape[0]
    xn = _rms(x) * gmix_ref[...]
    row = lax.broadcasted_iota(jnp.int32, xn.shape, 0)
    shifted = jnp.where(row == 0, prev_ref[SUBLANES - 1:, :], pltpu.roll(xn, 1, 0))
    prev_ref[...] = xn[rows - SUBLANES:, :]
    xx = shifted - xn

    def mix(i):
        return (xn + xx * mu_ref[i:i + 1, :]).astype(BF16)

    r_ref[0] = _dot(mix(0), wr_ref[...]).astype(BF16)
    k_ref[0] = _dot(mix(2), wk_ref[...]).astype(BF16)
    v_ref[0] = _dot(mix(3), wv_ref[...]).astype(BF16)
    u = w0_ref[...] + _dot(jnp.tanh(_dot(mix(1), w1_ref[...])).astype(BF16), w2_ref[...])
    w_log = -(jnp.maximum(-u, 0.0) + jnp.log(1.0 + jnp.exp(-jnp.abs(u)))) - 0.5
    lw_ref[0] = -jnp.exp(w_log) * LOG2_E
    a_ref[0] = _sigmoid(a0_ref[...] + _dot(_dot(mix(4), a1_ref[...]).astype(BF16), a2_ref[...])).astype(BF16)
    g_ref[0] = _dot(_sigmoid(_dot(mix(5), g1_ref[...])).astype(BF16), g2_ref[...]).astype(BF16)


def _pad_cols(w, n):
    return jnp.pad(w, ((0, 0), (0, n - w.shape[1])))


def _pad_rows(w, n):
    return jnp.pad(w, ((0, n - w.shape[0]), (0, 0)))


def _rwkv_pre(x, gmix, mu, w0, a0, w_r, w_k, w_v, w1, w2, a1, a2, g1, g2):
    bsz, seq, d = x.shape
    lw = -(-w1.shape[1] // LANES) * LANES
    la = -(-a1.shape[1] // LANES) * LANES
    lg = -(-g1.shape[1] // LANES) * LANES
    mu8 = jnp.pad(mu, ((0, SUBLANES - mu.shape[0]), (0, 0)))
    args = (x, gmix.reshape(1, d), mu8, w0.reshape(1, d), a0.reshape(1, d),
            w_r.astype(BF16), w_k.astype(BF16), w_v.astype(BF16),
            _pad_cols(w1, lw).astype(BF16), _pad_rows(w2, lw).astype(BF16),
            _pad_cols(a1, la).astype(BF16), _pad_rows(a2, la).astype(BF16),
            _pad_cols(g1, lg).astype(BF16), _pad_rows(g2, lg).astype(BF16))
    tile = pl.BlockSpec((1, PRE_ROWS, d), lambda b, s: (b, s, 0))
    in_specs = [tile] + [_const_spec(a.shape) for a in args[1:]]
    out_dt = (BF16, BF16, BF16, BF16, F32, BF16)
    return pl.pallas_call(
        _rwkv_pre_kernel,
        grid=(bsz, seq // PRE_ROWS),
        in_specs=in_specs,
        out_specs=[tile] * 6,
        out_shape=[jax.ShapeDtypeStruct((bsz, seq, d), t) for t in out_dt],
        scratch_shapes=[pltpu.VMEM((SUBLANES, d), F32)],
        compiler_params=_cparams("arbitrary", "arbitrary"),
        name="rwkv_pre",
    )(*args)


def _rwkv_rec_kernel(n_casts, r_ref, k_ref, v_ref, a_ref, lw_ref, g_ref, kk_ref, ka_ref, rk_ref, lnw_ref, lnb_ref,
                     *refs):
    cast_in, o_ref, cast_out, state_ref = refs[:n_casts], refs[n_casts], refs[n_casts + 1:-1], refs[-1]
    for src, dst in zip(cast_in, cast_out):
        dst[...] = src[...].astype(BF16)

    @pl.when(pl.program_id(2) == 0)
    def _():
        state_ref[...] = jnp.zeros_like(state_ref)

    c = CHUNK
    hd = RWKV_HEAD
    w = 2 * hd
    assert c == hd
    n = r_ref.shape[1] // c
    lane = lax.broadcasted_iota(jnp.int32, (c, w), 1)
    row = lax.broadcasted_iota(jnp.int32, (c, w), 0)
    head0 = lane < hd
    col = jnp.where(head0, lane, lane - hd)
    strict = row > col
    incl = row >= col
    eye = (row == col).astype(F32)
    tri = (lax.broadcasted_iota(jnp.int32, (c, c), 0) >= lax.broadcasted_iota(jnp.int32, (c, c), 1)).astype(BF16)
    pairs = r_ref.shape[2] // w

    def head_sum(x):
        s0 = jnp.sum(jnp.where(head0, x, 0.0), axis=-1, keepdims=True)
        s1 = jnp.sum(jnp.where(head0, 0.0, x), axis=-1, keepdims=True)
        return jnp.where(head0, s0, s1)

    def stack(x):
        return jnp.concatenate([jnp.where(head0, x, 0.0), jnp.where(head0, 0.0, x)], axis=0)

    def cumsum(x):
        hi = x.astype(BF16)
        lo = (x - hi.astype(F32)).astype(BF16)
        both = _dot(tri, jnp.concatenate([hi, lo], axis=1))
        return both[:, :w] + both[:, w:]

    zeros = jnp.zeros((c, w), BF16)
    carried = [state_ref[p] for p in range(pairs)]

    def chunk(j, p):
        rs = slice(j * c, (j + 1) * c)
        ls = slice(p * w, (p + 1) * w)
        kkp, kap, rkp, lnw, lnb = kk_ref[:, ls], ka_ref[:, ls], rk_ref[:, ls], lnw_ref[:, ls], lnb_ref[:, ls]
        r_ = r_ref[0, rs, ls].astype(F32)
        k_ = k_ref[0, rs, ls].astype(F32)
        v_ = v_ref[0, rs, ls].astype(F32)
        a_ = a_ref[0, rs, ls].astype(F32)
        lw_ = lw_ref[0, rs, ls]
        cum = cumsum(lw_)
        yield
        kk = k_ * kkp
        kk = kk * lax.rsqrt(jnp.maximum(head_sum(kk * kk), 1e-24))
        k2 = k_ * (1.0 + (a_ - 1.0) * kap)
        p_ = -(kk * a_)
        tot = cum[c - 1:c, :]
        e_neg = jnp.exp2(-cum)
        e_tot = jnp.exp2(tot - cum)
        rt = r_ * jnp.exp2(cum)
        qt = kk * jnp.exp2(cum - lw_)
        vb = v_.astype(BF16)
        amat = _dot_nt(jnp.concatenate([qt, rt], axis=0).astype(BF16),
                       jnp.concatenate([stack((k2 * e_neg).astype(BF16)), stack((p_ * e_neg).astype(BF16))], axis=0))
        yield
        a_qp = jnp.where(strict, amat[:c, w:], 0.0)
        a_rp = jnp.where(incl, amat[c:, w:], 0.0).astype(BF16)
        a_k = jnp.concatenate([jnp.where(strict, amat[:c, :w], 0.0),
                               jnp.where(incl, amat[c:, :w], 0.0)], axis=0).astype(BF16)
        av = _dot(a_k, stack(vb))
        inv = eye + a_qp
        power = a_qp.astype(BF16)
        power = _dot(power, stack(power))
        yield
        steps = int(math.log2(c)) - 1
        for i in range(steps):
            pb = power.astype(BF16)
            if i + 1 < steps:
                both = _dot(jnp.concatenate([inv.astype(BF16), pb], axis=0), stack(pb))
                inv = inv + both[:c]
                power = both[c:]
            else:
                inv = inv + _dot(inv.astype(BF16), stack(pb))
            yield
        wu = _dot(inv.astype(BF16),
                  jnp.concatenate([stack(qt.astype(BF16)), stack(av[:c].astype(BF16))], axis=1))
        wub = wu.astype(BF16)
        yield
        ry = _dot(a_rp, jnp.concatenate([stack(wub[:, :w]), stack(wub[:, w:])], axis=1))
        mg = _dot_tn(jnp.concatenate([p_ * e_tot, k2 * e_tot], axis=0).astype(BF16),
                     jnp.concatenate([wub, jnp.concatenate([zeros, vb], axis=1)], axis=0))
        yield
        rw = (rt + ry[:, :w]).astype(BF16)
        m_w = jnp.where(head0, mg[:c, :w], mg[c:, :w]).astype(BF16)
        g_w = jnp.where(head0, mg[:c, w:], mg[c:, w:])
        gam = head_sum(eye * jnp.exp2(tot))
        state = carried[p]
        both = _dot(jnp.concatenate([rw, m_w], axis=0), stack(state.astype(BF16)))
        y = both[:c] + av[c:] + ry[:, w:]
        carried[p] = gam * state + both[c:] + g_w
        yield
        yc = y - head_sum(y) * (1.0 / hd)
        var = head_sum(yc * yc) * (1.0 / hd)
        yn = yc * lax.rsqrt(var + GN_EPS) * lnw + lnb
        bonus = head_sum(r_ * k2 * rkp) * v_
        o_ref[0, rs, ls] = ((yn + bonus) * g_ref[0, rs, ls].astype(F32)).astype(BF16)

    live, started = [], 0
    while started < n or live:
        if started < n:
            live.extend(chunk(started, p) for p in range(pairs))
            started += 1
        for gen in list(live):
            if next(gen, "done") == "done":
                live.remove(gen)
    for p in range(pairs):
        state_ref[p] = carried[p]


def _rwkv_rec(r, k, v, a, lw, g, k_k, k_a, r_k, ln_w, ln_b, casts=()):
    bsz, seq, d = r.shape
    pair_w = 2 * RWKV_HEAD
    hw = REC_PAIRS * pair_w
    grid = (bsz, d // hw, seq // REC_ROWS)
    steps = grid[0] * grid[1] * grid[2]
    tile = pl.BlockSpec((1, REC_ROWS, hw), lambda b, h, s: (b, s, h))
    par = pl.BlockSpec((1, hw), lambda b, h, s: (0, h))
    params = [p.reshape(1, d).astype(F32) for p in (k_k, k_a, r_k, ln_w, ln_b)]
    cast_specs = []
    for w in casts:
        assert w.shape[0] % (steps * 2 * SUBLANES) == 0
        cast_specs.append(pl.BlockSpec((w.shape[0] // steps, w.shape[1]),
                                       lambda b, h, s: ((b * grid[1] + h) * grid[2] + s, 0)))
    out = pl.pallas_call(
        functools.partial(_rwkv_rec_kernel, len(casts)),
        grid=grid,
        in_specs=[tile] * 6 + [par] * 5 + cast_specs,
        out_specs=[tile] + cast_specs,
        out_shape=[jax.ShapeDtypeStruct((bsz, seq, d), BF16)] + [jax.ShapeDtypeStruct(w.shape, BF16) for w in casts],
        scratch_shapes=[pltpu.VMEM((REC_PAIRS, CHUNK, pair_w), F32)],
        compiler_params=_cparams("arbitrary", "arbitrary", "arbitrary"),
        name="rwkv_rec",
    )(r, k, v, a, lw, g, *params, *casts)
    return out[0], out[1:]


def _moe_route_kernel(h_ref, g_ref, wr_ref, br_ref, xs_ref, gs_ref, dest_ref, cnt_ref):
    _route_block(h_ref[...], g_ref, wr_ref, br_ref, xs_ref, gs_ref, dest_ref, cnt_ref)


def _moe_route_proj_kernel(a_ref, w_ref, res_ref, g_ref, wr_ref, br_ref, h_ref, xs_ref, gs_ref, dest_ref, cnt_ref):
    h = res_ref[...] + _dot(a_ref[...], w_ref[...])
    h_ref[...] = h
    _route_block(h, g_ref, wr_ref, br_ref, xs_ref, gs_ref, dest_ref, cnt_ref)


def _route_block(h, g_ref, wr_ref, br_ref, xs_ref, gs_ref, dest_ref, cnt_ref):
    nb = h.shape[0]
    nbp = xs_ref.shape[0]
    tb = (_rms(h) * g_ref[...]).astype(BF16)
    logits = _dot(tb, wr_ref[...]) + br_ref[...]
    lane = lax.broadcasted_iota(jnp.int32, logits.shape, 1).astype(F32)
    neg = jnp.float32(-jnp.inf)
    big = jnp.float32(LANES)
    is_grp = lane < N_EXPERT_GROUPS
    gl = jnp.where(is_grp, logits, neg)
    gmax = jnp.max(gl, axis=-1, keepdims=True)
    grp = jnp.min(jnp.where(gl == gmax, lane, big), axis=-1, keepdims=True)
    p_grp = 1.0 / jnp.sum(jnp.where(is_grp, jnp.exp(gl - gmax), 0.0), axis=-1, keepdims=True)
    e_lo = N_EXPERT_GROUPS + grp * EXPERTS_PER_GROUP
    in_grp = (lane >= e_lo) & (lane < e_lo + EXPERTS_PER_GROUP)
    el = jnp.where(in_grp, logits, neg)
    v1 = jnp.max(el, axis=-1, keepdims=True)
    i1 = jnp.min(jnp.where(el == v1, lane, big), axis=-1, keepdims=True)
    el2 = jnp.where(lane == i1, neg, el)
    v2 = jnp.max(el2, axis=-1, keepdims=True)
    i2 = jnp.min(jnp.where(el2 == v2, lane, big), axis=-1, keepdims=True)
    e2 = jnp.exp(v2 - v1)
    w1 = 1.0 / (1.0 + e2)
    w2 = e2 / (1.0 + e2)
    gates = jnp.where(lane == i1, w1, jnp.where(lane == i2, w2, 0.0)) * p_grp

    onehot = (lane == grp).astype(BF16)
    ri = lax.broadcasted_iota(jnp.int32, (nb, nb), 0)
    ci = lax.broadcasted_iota(jnp.int32, (nb, nb), 1)
    rank = _dot((ri > ci).astype(BF16), onehot)
    counts = jnp.sum(onehot.astype(F32), axis=0, keepdims=True)
    padded = jnp.ceil(counts / FFN_ROWS) * FFN_ROWS
    lane1 = lax.broadcasted_iota(jnp.int32, (1, LANES), 1)
    offs = jnp.zeros((1, LANES), F32)
    for gidx in range(1, N_EXPERT_GROUPS):
        prev = jnp.sum(jnp.where(lane1 < gidx, padded, 0.0), axis=-1, keepdims=True)
        offs = jnp.where(lane1 == gidx, prev, offs)
    dest = jnp.sum(onehot.astype(F32) * (rank + offs), axis=-1, keepdims=True)
    slot = lax.broadcasted_iota(jnp.int32, (nb, nbp), 1).astype(F32)
    perm_t = (dest == slot).astype(BF16)
    xs_ref[...] = _dot_tn(perm_t, tb).astype(BF16)
    g_hi = gates.astype(BF16)
    g_lo = (gates - g_hi.astype(F32)).astype(BF16)
    moved = _dot_tn(perm_t, jnp.concatenate([g_hi, g_lo], axis=1))
    gs_ref[...] = moved[:, :LANES] + moved[:, LANES:]
    dest_ref[...] = dest
    cnt_ref[0] = jnp.where(lane1 < N_EXPERT_GROUPS, counts, 0.0)


def _moe_ffn_kernel(tile_ref, grp_ref, real_ref, *refs):
    nt = FFN_TILES
    xs_refs, gs_refs = refs[:nt], refs[nt:2 * nt]
    wg_ref, wu_ref, wd_ref, ys_ref = refs[2 * nt:]
    i = pl.program_id(0)

    @pl.when(real_ref[i] > 0)
    def _():
        x = jnp.concatenate([r[...] for r in xs_refs], axis=0)
        gates = jnp.concatenate([r[...] for r in gs_refs], axis=0)
        lane = lax.broadcasted_iota(jnp.int32, gates.shape, 1)
        base = N_EXPERT_GROUPS + grp_ref[i] * EXPERTS_PER_GROUP
        acc = jnp.zeros(ys_ref.shape, F32)
        for e in range(EXPERTS_PER_GROUP):
            ge = jnp.sum(jnp.where(lane == base + e, gates, 0.0), axis=-1, keepdims=True)
            gate_act = _dot(x, wg_ref[e])
            hdn = gate_act * _sigmoid(gate_act) * _dot(x, wu_ref[e])
            acc = acc + ge * _dot(hdn.astype(BF16), wd_ref[e])
        ys_ref[...] = acc.astype(BF16)

    @pl.when(real_ref[i] == 0)
    def _():
        ys_ref[...] = jnp.zeros_like(ys_ref)


def _moe_merge_kernel(slot_ref, h_ref, dest_ref, *refs):
    ys_refs, o_ref = refs[:-1], refs[-1]
    ys = jnp.concatenate([r[...] for r in ys_refs], axis=0)
    nb, nbp = h_ref.shape[0], ys.shape[0]
    slot = lax.broadcasted_iota(jnp.int32, (nb, nbp), 1).astype(F32)
    perm_t = (dest_ref[...] == slot).astype(BF16)
    o_ref[...] = h_ref[...] + _dot(perm_t, ys)


def _moe_merge_norm_kernel(slot_ref, h_ref, dest_ref, *refs):
    ys_refs, g_ref, o_ref = refs[:-2], refs[-2], refs[-1]
    ys = jnp.concatenate([r[...] for r in ys_refs], axis=0)
    nb, nbp = h_ref.shape[0], ys.shape[0]
    slot = lax.broadcasted_iota(jnp.int32, (nb, nbp), 1).astype(F32)
    perm_t = (dest_ref[...] == slot).astype(BF16)
    o_ref[...] = _rms(h_ref[...] + _dot(perm_t, ys)) * g_ref[...]


def _ffn_slots(n_tiles):
    return -(-(n_tiles + N_EXPERT_GROUPS * (FFN_TILES - 1)) // FFN_TILES) * FFN_TILES


def _ffn_schedule(counts, nblk, tiles_per_blk):
    i32 = jnp.int32
    tiles = (counts + FFN_ROWS - 1) // FFN_ROWS
    first_rel = jnp.cumsum(tiles, axis=1) - tiles
    used_blk = jnp.sum(tiles, axis=1)
    grp_pad = (jnp.sum(tiles, axis=0) + FFN_TILES - 1) // FFN_TILES * FFN_TILES
    grp_start = jnp.cumsum(grp_pad) - grp_pad
    before = jnp.cumsum(tiles, axis=0) - tiles
    n_tiles = nblk * tiles_per_blk
    t = jnp.arange(n_tiles, dtype=i32)
    b, k = t // tiles_per_blk, t % tiles_per_blk
    used = k < used_blk[b]
    grp_t = jnp.minimum(jnp.sum(((first_rel + tiles)[b] <= k[:, None]).astype(i32), axis=1), N_EXPERT_GROUPS - 1)
    pos = grp_start[grp_t] + before[b, grp_t] + (k - first_rel[b, grp_t])
    s = jnp.arange(_ffn_slots(n_tiles), dtype=i32)
    hit = used[None, :] & (pos[None, :] == s[:, None])
    slot_has = jnp.sum(hit.astype(i32), axis=1) > 0
    free_rank = jnp.cumsum((~slot_has).astype(i32)) - 1
    unused_rank = jnp.cumsum((~used).astype(i32)) - 1
    fill = (~used)[None, :] & (unused_rank[None, :] == free_rank[:, None])
    placed = jnp.where(slot_has[:, None], hit, fill)
    slot_tile = jnp.sum(jnp.where(placed, t[None, :], 0), axis=1).astype(i32)
    tile_slot = jnp.sum(jnp.where(placed, s[:, None], 0), axis=0).astype(i32)
    slot_grp = jnp.sum(jnp.where(hit, grp_t[None, :], 0), axis=1).reshape(-1, FFN_TILES)
    step_real = jnp.sum(slot_has.astype(i32).reshape(-1, FFN_TILES), axis=1).astype(i32)
    n_real_steps = jnp.sum(grp_pad) // FFN_TILES
    last_grp = slot_grp[jnp.maximum(n_real_steps - 1, 0), 0]
    step_grp = jnp.where(step_real > 0, slot_grp[:, 0], last_grp).astype(i32)
    return slot_tile, tile_slot, step_grp, step_real


def _hier_moe_residual(h, norm_g, w_grp, b_grp, w_exp, b_exp, layer, w_gate, w_up, w_down, final_g=None,
                       pending=None):
    n, d = h.shape
    nb = MOE_BLOCK
    nblk = n // nb
    nbp = nb + N_EXPERT_GROUPS * FFN_ROWS
    tiles_per_blk = nbp // FFN_ROWS
    n_exp = N_EXPERT_GROUPS * EXPERTS_PER_GROUP
    w_router = jnp.pad(jnp.concatenate([w_grp, w_exp], axis=1), ((0, 0), (0, LANES - N_EXPERT_GROUPS - n_exp)))
    b_router = jnp.pad(jnp.concatenate([b_grp, b_exp]), (0, LANES - N_EXPERT_GROUPS - n_exp)).reshape(1, LANES)

    rows = pl.BlockSpec((nb, d), lambda i: (i, 0))
    route_in = [_const_spec((1, d)), _const_spec((d, LANES)), _const_spec((1, LANES))]
    route_args = (norm_g.reshape(1, d), w_router.astype(BF16), b_router)
    route_out = [pl.BlockSpec((nbp, d), lambda i: (i, 0)), pl.BlockSpec((nbp, LANES), lambda i: (i, 0)),
                 pl.BlockSpec((nb, 1), lambda i: (i, 0)), pl.BlockSpec((1, 1, LANES), lambda i: (i, 0, 0))]
    route_shape = [jax.ShapeDtypeStruct((nblk * nbp, d), BF16), jax.ShapeDtypeStruct((nblk * nbp, LANES), F32),
                   jax.ShapeDtypeStruct((n, 1), F32), jax.ShapeDtypeStruct((nblk, 1, LANES), F32)]
    if pending is None:
        xs, gs, dest, counts = pl.pallas_call(
            _moe_route_kernel, grid=(nblk,), in_specs=[rows] + route_in, out_specs=route_out, out_shape=route_shape,
            compiler_params=_cparams("arbitrary"), name="moe_route",
        )(h, *route_args)
    else:
        a, w = pending
        h, xs, gs, dest, counts = pl.pallas_call(
            _moe_route_proj_kernel, grid=(nblk,),
            in_specs=[pl.BlockSpec((nb, a.shape[1]), lambda i: (i, 0)), _const_spec(w.shape), rows] + route_in,
            out_specs=[rows] + route_out, out_shape=[jax.ShapeDtypeStruct((n, d), F32)] + route_shape,
            compiler_params=_cparams("arbitrary"), name="moe_route_proj",
        )(a, w.astype(BF16), h, *route_args)

    n_tiles = nblk * tiles_per_blk
    n_slots = _ffn_slots(n_tiles)
    slot_tile, tile_slot, step_grp, step_real = _ffn_schedule(
        counts[:, 0, :N_EXPERT_GROUPS].astype(jnp.int32), nblk, tiles_per_blk)

    def tile_spec(width, j):
        return pl.BlockSpec((FFN_ROWS, width), lambda i, tile, grp, real: (tile[i * FFN_TILES + j], 0))

    def wspec(shape):
        return pl.BlockSpec((None, EXPERTS_PER_GROUP) + shape, lambda i, tile, grp, real: (layer, grp[i], 0, 0))

    f = w_gate.shape[-1]
    ys = pl.pallas_call(
        _moe_ffn_kernel,
        grid_spec=pltpu.PrefetchScalarGridSpec(
            num_scalar_prefetch=3,
            grid=(n_slots // FFN_TILES,),
            in_specs=[tile_spec(d, j) for j in range(FFN_TILES)] + [tile_spec(LANES, j) for j in range(FFN_TILES)]
            + [wspec((d, f)), wspec((d, f)), wspec((f, d))],
            out_specs=pl.BlockSpec((FFN_TILES * FFN_ROWS, d), lambda i, tile, grp, real: (i, 0)),
        ),
        out_shape=jax.ShapeDtypeStruct((n_slots * FFN_ROWS, d), BF16),
        compiler_params=_cparams("arbitrary"),
        name="moe_ffn",
    )(slot_tile, step_grp, step_real, *([xs] * FFN_TILES), *([gs] * FFN_TILES),
      w_gate, w_up, w_down)

    blk = pl.BlockSpec((nb, d), lambda i, slot: (i, 0))
    in_specs = [blk, pl.BlockSpec((nb, 1), lambda i, slot: (i, 0))]
    in_specs += [pl.BlockSpec((FFN_ROWS, d), functools.partial(lambda j, i, slot: (slot[i * tiles_per_blk + j], 0), j))
                 for j in range(tiles_per_blk)]
    args = [h, dest] + [ys] * tiles_per_blk
    body = _moe_merge_kernel
    if final_g is not None:
        in_specs.append(pl.BlockSpec((1, d), lambda i, slot: (0, 0)))
        args.append(final_g.reshape(1, d))
        body = _moe_merge_norm_kernel
    return pl.pallas_call(
        body,
        grid_spec=pltpu.PrefetchScalarGridSpec(num_scalar_prefetch=1, grid=(nblk,), in_specs=in_specs, out_specs=blk),
        out_shape=jax.ShapeDtypeStruct((n, d), F32),
        compiler_params=_cparams("arbitrary"),
        name="moe_merge",
    )(tile_slot, *args)


GROUP_WIDTH = DIL_HEADS * DIL_HEAD_DIM
N_STRIDED = 3 * sum(1 for _, dil in DIL_PATTERNS if dil > 1)


def _qkv_kernel(h_ref, gq_ref, gkv_ref, wq_ref, wk_ref, wv_ref, *refs):
    outs, scratch = refs[:-1], refs[-1]
    n = _rms(h_ref[...])
    xq = (n * gq_ref[...]).astype(BF16)
    xkv = (n * gkv_ref[...]).astype(BF16)
    gw = GROUP_WIDTH
    slot = 0
    full = (_dot(xq, wq_ref[...]), _dot(xkv, wk_ref[...]), _dot(xkv, wv_ref[...]))
    for gi, (_, dil) in enumerate(DIL_PATTERNS):
        cols = slice(gi * gw, (gi + 1) * gw)
        for t in range(3):
            res = full[t][:, cols]
            out_ref = outs[3 * gi + t]
            if dil == 1:
                out_ref[...] = res.astype(BF16)
                continue
            sub = res.shape[0] // dil
            for c in range(gw // LANES):
                scratch[slot, c] = res[:, c * LANES:(c + 1) * LANES]
            for r in range(dil):
                for c in range(gw // LANES):
                    lo = r * gw + c * LANES
                    out_ref[:, lo:lo + LANES] = scratch[slot, c, pl.ds(r, sub, stride=dil), :].astype(BF16)
            slot += 1


def _qkv_proj(h, g_q, g_kv, w_q, w_kv):
    n, d = h.shape
    qw = w_q.shape[1]
    gw = GROUP_WIDTH
    scale = math.log2(math.e) / math.sqrt(DIL_HEAD_DIM)
    rows = pl.BlockSpec((PROJ_ROWS, d), lambda i: (i, 0))
    out_specs, out_shape = [], []
    for _, dil in DIL_PATTERNS:
        for _ in range(3):
            out_specs.append(pl.BlockSpec((PROJ_ROWS // dil, dil * gw), lambda i: (i, 0)))
            out_shape.append(jax.ShapeDtypeStruct((n // dil, dil * gw), BF16))
    return pl.pallas_call(
        _qkv_kernel,
        grid=(n // PROJ_ROWS,),
        in_specs=[rows, _const_spec((1, d)), _const_spec((1, d))] + [_const_spec((d, qw))] * 3,
        out_specs=out_specs,
        out_shape=out_shape,
        scratch_shapes=[pltpu.VMEM((N_STRIDED, gw // LANES, PROJ_ROWS, LANES), F32)],
        compiler_params=_cparams("arbitrary"),
        name="qkv_proj",
    )(h, g_q.reshape(1, d), g_kv.reshape(1, d), (w_q * scale).astype(BF16),
      w_kv[:, :qw].astype(BF16), w_kv[:, qw:].astype(BF16))


def _attn_kernel(q_ref, kc_ref, kp_ref, vc_ref, vp_ref, o_ref, lse_ref):
    n = pl.program_id(2)
    blk = ATT_BLOCK
    pw = 2 * DIL_HEAD_DIM
    ri = lax.broadcasted_iota(jnp.int32, (blk, blk), 0)
    ci = lax.broadcasted_iota(jnp.int32, (blk, blk), 1)
    lowest = jnp.finfo(F32).min
    cap_prev = jnp.where(ci >= ri, jnp.inf, lowest).astype(F32)
    cap_first = jnp.where(n > 0, cap_prev, lowest)
    cap_own = jnp.where(ci <= ri, jnp.inf, lowest).astype(F32)
    head0 = lax.broadcasted_iota(jnp.int32, (blk, pw), 1) < DIL_HEAD_DIM
    zero = jnp.zeros((blk, pw), BF16)
    lane_t = lax.broadcasted_iota(jnp.int32, (blk, LANES), 1)

    def stack(x):
        return jnp.concatenate([jnp.where(head0, x, zero), jnp.where(head0, zero, x)], axis=0)

    streams = [(i, p) for i in range(q_ref.shape[1] // blk) for p in range(DIL_HEADS // 2)]

    def rows(i):
        return slice(i * blk, (i + 1) * blk)

    def lanes(p):
        return slice(p * pw, (p + 1) * pw)

    stacked = {}

    def kv_blocks(cur_ref, prev_ref, i, p):
        def one(w):
            name = (id(cur_ref), w, p)
            if name not in stacked:
                stacked[name] = stack(prev_ref[0, :, lanes(p)] if w < 0 else cur_ref[0, rows(w), lanes(p)])
            return stacked[name]
        return jnp.concatenate([one(i - 1), one(i)], axis=0)

    s = []
    for i, p in streams:
        raw = _dot_nt(q_ref[0, rows(i), lanes(p)], kv_blocks(kc_ref, kp_ref, i, p))
        cap = cap_first if i == 0 else cap_prev
        s.append([jnp.minimum(raw[:, :blk], cap), jnp.minimum(raw[:, blk:2 * blk], cap),
                  jnp.minimum(raw[:, 2 * blk:3 * blk], cap_own), jnp.minimum(raw[:, 3 * blk:], cap_own)])
    m_a = [jnp.max(jnp.maximum(x[0], x[2]), axis=-1, keepdims=True) for x in s]
    m_b = [jnp.max(jnp.maximum(x[1], x[3]), axis=-1, keepdims=True) for x in s]
    e = [[jnp.exp2(x[0] - ma), jnp.exp2(x[1] - mb), jnp.exp2(x[2] - ma), jnp.exp2(x[3] - mb)]
         for x, ma, mb in zip(s, m_a, m_b)]
    pb = [jnp.concatenate([y.astype(BF16) for y in x], axis=1) for x in e]
    blk_head = (lax.broadcasted_iota(jnp.int32, (4 * blk, LANES), 0) // blk) % 2
    sel_lane = lax.broadcasted_iota(jnp.int32, (4 * blk, LANES), 1) - DIL_HEADS
    sel = [(sel_lane == 2 * p + blk_head).astype(BF16) for p in range(DIL_HEADS // 2)]
    pv = [_dot(pb[idx], jnp.concatenate([kv_blocks(vc_ref, vp_ref, i, p), sel[p]], axis=1))
          for idx, (i, p) in enumerate(streams)]
    tiles = {}
    for idx, (i, p) in enumerate(streams):
        o_ref[0, rows(i), lanes(p)] = pv[idx][:, :pw].astype(BF16)
        tile = tiles.get(i, jnp.zeros((blk, LANES), F32)) + pv[idx][:, pw:]
        tiles[i] = jnp.where(lane_t == 2 * p, m_a[idx], jnp.where(lane_t == 2 * p + 1, m_b[idx], tile))
    for i, tile in tiles.items():
        lse_ref[0, rows(i), :] = tile


def _attn_group(q, k, v, gi, dilation, bsz, seq):
    gw = GROUP_WIDTH
    sub = seq // dilation
    windows = min(ATT_WINDOWS, sub // ATT_BLOCK)
    step_rows = windows * ATT_BLOCK
    assert sub % step_rows == 0
    view = (bsz, sub, dilation * gw)
    qv, kv_, vv = q.reshape(view), k.reshape(view), v.reshape(view)
    cur = pl.BlockSpec((1, step_rows, gw), lambda b, r, n: (b, n, r))
    prev = pl.BlockSpec((1, ATT_BLOCK, gw), lambda b, r, n: (b, jnp.maximum(n * windows - 1, 0), r))
    lse_spec = pl.BlockSpec((1, step_rows, LANES), lambda b, r, n: (b, n, r))
    o, lse = pl.pallas_call(
        _attn_kernel,
        grid=(bsz, dilation, sub // step_rows),
        in_specs=[cur, cur, prev, cur, prev],
        out_specs=[cur, lse_spec],
        out_shape=[jax.ShapeDtypeStruct(view, BF16),
                   jax.ShapeDtypeStruct((bsz, sub, dilation * LANES), F32)],
        compiler_params=_cparams("arbitrary", "arbitrary", "arbitrary"),
        name=f"attn_group{gi}",
    )(qv, kv_, kv_, vv, vv)
    return o.reshape(bsz * sub, dilation * gw), lse.reshape(bsz * sub, dilation * LANES)


def _attn_merge_kernel(o0_ref, o1_ref, o2_ref, l0_ref, l1_ref, l2_ref, w_ref, res_ref, out_ref, *scratch):
    rows = res_ref.shape[0]
    gw = GROUP_WIDTH
    outs, lses = [], []
    slot = 0
    for (_, dil), o_ref, l_ref in zip(DIL_PATTERNS, (o0_ref, o1_ref, o2_ref), (l0_ref, l1_ref, l2_ref)):
        if dil == 1:
            outs.append(o_ref[...].astype(F32))
            lses.append(l_ref[...])
            continue
        o_scr, l_scr = scratch[2 * slot], scratch[2 * slot + 1]
        sub = rows // dil
        for r in range(dil):
            for c in range(gw // LANES):
                lo = r * gw + c * LANES
                o_scr[c, pl.ds(r, sub, stride=dil), :] = o_ref[:, lo:lo + LANES].astype(F32)
            l_scr[pl.ds(r, sub, stride=dil), :] = l_ref[:, r * LANES:(r + 1) * LANES]
        outs.append(jnp.concatenate([o_scr[c] for c in range(gw // LANES)], axis=1))
        lses.append(l_scr[...])
        slot += 1
    m = jnp.maximum(jnp.maximum(lses[0], lses[1]), lses[2])
    e = [jnp.exp2(l - m) for l in lses]
    sums = [pltpu.roll(l, LANES - DIL_HEADS, 1) for l in lses]
    denom = e[0] * sums[0] + e[1] * sums[1] + e[2] * sums[2]
    is_head = lax.broadcasted_iota(jnp.int32, (rows, LANES), 1) < DIL_HEADS
    expand = (lax.broadcasted_iota(jnp.int32, (LANES, gw), 1) // DIL_HEAD_DIM
              == lax.broadcasted_iota(jnp.int32, (LANES, gw), 0)).astype(BF16)
    alpha = jnp.concatenate([jnp.where(is_head, x / denom, 0.0) for x in e], axis=0)
    hi = alpha.astype(BF16)
    lo = (alpha - hi.astype(F32)).astype(BF16)
    wide = _dot(jnp.concatenate([hi, lo], axis=1), jnp.concatenate([expand, expand], axis=0))
    mix = jnp.zeros((rows, gw), F32)
    for g in range(len(outs)):
        mix = mix + wide[g * rows:(g + 1) * rows] * outs[g]
    out_ref[...] = res_ref[...] + _dot(mix.astype(BF16), w_ref[...])


def _attn_merge(outs, lses, w_o, res):
    n, d = res.shape
    gw = GROUP_WIDTH
    o_specs = [pl.BlockSpec((PROJ_ROWS // dil, dil * gw), lambda i: (i, 0)) for _, dil in DIL_PATTERNS]
    l_specs = [pl.BlockSpec((PROJ_ROWS // dil, dil * LANES), lambda i: (i, 0)) for _, dil in DIL_PATTERNS]
    full = pl.BlockSpec((PROJ_ROWS, d), lambda i: (i, 0))
    scratch = []
    for _, dil in DIL_PATTERNS:
        if dil > 1:
            scratch += [pltpu.VMEM((gw // LANES, PROJ_ROWS, LANES), F32), pltpu.VMEM((PROJ_ROWS, LANES), F32)]
    return pl.pallas_call(
        _attn_merge_kernel,
        grid=(n // PROJ_ROWS,),
        in_specs=o_specs + l_specs + [_const_spec(w_o.shape), full],
        out_specs=full,
        out_shape=jax.ShapeDtypeStruct((n, d), F32),
        scratch_shapes=scratch,
        compiler_params=_cparams("arbitrary"),
        name="attn_merge",
    )(*outs, *lses, w_o.astype(BF16), res)


def kernel(x, norm_mix_g, norm_ffn_g, rwkv_mu, rwkv_w_r, rwkv_w_k, rwkv_w_v, rwkv_w0, rwkv_w1, rwkv_w2,
           rwkv_a0, rwkv_a1, rwkv_a2, rwkv_g1, rwkv_g2, rwkv_k_k, rwkv_k_a, rwkv_r_k, rwkv_ln_w, rwkv_ln_b,
           rwkv_w_o, kv_norm_g, w_kv, attn_w_q, attn_w_o, moe_w_grp, moe_b_grp, moe_w_exp, moe_b_exp,
           moe_w_gate, moe_w_up, moe_w_down, final_norm_g):
    bsz, seq, d = x.shape
    n = bsz * seq
    depth = norm_mix_g.shape[0]
    n_rwkv = rwkv_mu.shape[0]
    h = x.reshape(n, d)
    expert_f32 = (moe_w_gate, moe_w_up, moe_w_down)
    expert_w = None
    q = k_sh = v_sh = None
    for layer in range(depth):
        pending = None
        if layer < n_rwkv:
            i = layer
            r, k, v, a, lw, g = _rwkv_pre(h.reshape(bsz, seq, d), norm_mix_g[layer], rwkv_mu[i], rwkv_w0[i], rwkv_a0[i],
                                          rwkv_w_r[i], rwkv_w_k[i], rwkv_w_v[i], rwkv_w1[i], rwkv_w2[i],
                                          rwkv_a1[i], rwkv_a2[i], rwkv_g1[i], rwkv_g2[i])
            casts = () if expert_w is not None else tuple(w.reshape(-1, w.shape[-1]) for w in expert_f32)
            y, cast = _rwkv_rec(r, k, v, a, lw, g, rwkv_k_k[i], rwkv_k_a[i], rwkv_r_k[i], rwkv_ln_w[i], rwkv_ln_b[i],
                                casts=casts)
            if casts:
                expert_w = tuple(c.reshape(w.shape) for c, w in zip(cast, expert_f32))
            pending = (y.reshape(n, d), rwkv_w_o[i])
        else:
            i = layer - n_rwkv
            qkv = _qkv_proj(h, norm_mix_g[layer], kv_norm_g, attn_w_q[i], w_kv)
            q = qkv[0::3]
            if i == 0:
                k_sh, v_sh = qkv[1::3], qkv[2::3]
            outs, lses = [], []
            for gi, (window, dilation) in enumerate(DIL_PATTERNS):
                assert window // dilation == ATT_BLOCK and seq % window == 0
                o, lse = _attn_group(q[gi], k_sh[gi], v_sh[gi], gi, dilation, bsz, seq)
                outs.append(o)
                lses.append(lse)
            h = _attn_merge(outs, lses, attn_w_o[i], h)
        last = layer == depth - 1
        if expert_w is None:
            expert_w = tuple(w.astype(BF16) for w in expert_f32)
        h = _hier_moe_residual(h, norm_ffn_g[layer], moe_w_grp[layer], moe_b_grp[layer], moe_w_exp[layer],
                               moe_b_exp[layer], layer, *expert_w, final_g=final_norm_g if last else None,
                               pending=pending)
    return h.reshape(bsz, seq, d)
```

```python
import functools
import math

import jax
import jax.numpy as jnp
from jax import lax
from jax.experimental import pallas as pl
from jax.experimental.pallas import tpu as pltpu

F32 = jnp.float32
BF16 = jnp.bfloat16

NORM_EPS = 1e-6
LOG2_E = math.log2(math.e)
RWKV_HEAD = 64
GN_EPS = RWKV_HEAD * 1e-5
DIL_PATTERNS = ((128, 1), (512, 4), (2048, 16))
DIL_HEADS = 8
DIL_HEAD_DIM = 64
QK_SCALE = LOG2_E / math.sqrt(DIL_HEAD_DIM)
N_EXPERT_GROUPS = 4
EXPERTS_PER_GROUP = 4
LANES = 128
SUBLANES = 8
VMEM_LIMIT = 56 * 1024 * 1024

CHUNK = 64
REC_ROWS = 2048
REC_PAIRS = 2
PRE_ROWS = 512
PROJ_ROWS = 512
MERGE_ROWS = 1024
MOE_BLOCK = 1024
FFN_ROWS = 64
FFN_TILES = 8
ATT_BLOCK = 128
ATT_WINDOWS = 8


def _cparams(*sem):
    return pltpu.CompilerParams(dimension_semantics=sem, vmem_limit_bytes=VMEM_LIMIT)


def _dot(a, b):
    return jnp.dot(a, b, preferred_element_type=F32)


def _dot_nt(a, b):
    return lax.dot_general(a, b, (((1,), (1,)), ((), ())), preferred_element_type=F32)


def _dot_tn(a, b):
    return lax.dot_general(a, b, (((0,), (0,)), ((), ())), preferred_element_type=F32)


def _rms(x):
    return x * lax.rsqrt(jnp.mean(x * x, axis=-1, keepdims=True) + NORM_EPS)


def _sigmoid(z):
    return 1.0 / (1.0 + jnp.exp(-z))


def _const_spec(shape):
    nd = len(shape)
    return pl.BlockSpec(shape, lambda *_: (0,) * nd)


def _rwkv_pre_kernel(x_ref, gmix_ref, mu_ref, w0_ref, a0_ref, wr_ref, wk_ref, wv_ref,
                     w1_ref, w2_ref, a1_ref, a2_ref, g1_ref, g2_ref,
                     r_ref, k_ref, v_ref, a_ref, lw_ref, g_ref, prev_ref):
    @pl.when(pl.program_id(1) == 0)
    def _():
        prev_ref[...] = jnp.zeros_like(prev_ref)

    x = x_ref[0]
    rows = x.shape[0]
    xn = _rms(x) * gmix_ref[...]
    row = lax.broadcasted_iota(jnp.int32, xn.shape, 0)
    shifted = jnp.where(row == 0, prev_ref[SUBLANES - 1:, :], pltpu.roll(xn, 1, 0))
    prev_ref[...] = xn[rows - SUBLANES:, :]
    xx = shifted - xn

    def mix(i):
        return (xn + xx * mu_ref[i:i + 1, :]).astype(BF16)

    r_ref[0] = _dot(mix(0), wr_ref[...]).astype(BF16)
    k_ref[0] = _dot(mix(2), wk_ref[...]).astype(BF16)
    v_ref[0] = _dot(mix(3), wv_ref[...]).astype(BF16)
    u = w0_ref[...] + _dot(jnp.tanh(_dot(mix(1), w1_ref[...])).astype(BF16), w2_ref[...])
    w_log = -(jnp.maximum(-u, 0.0) + jnp.log(1.0 + jnp.exp(-jnp.abs(u)))) - 0.5
    lw_ref[0] = -jnp.exp(w_log) * LOG2_E
    a_ref[0] = _sigmoid(a0_ref[...] + _dot(_dot(mix(4), a1_ref[...]).astype(BF16), a2_ref[...])).astype(BF16)
    g_ref[0] = _dot(_sigmoid(_dot(mix(5), g1_ref[...])).astype(BF16), g2_ref[...]).astype(BF16)


def _pad_cols(w, n):
    return jnp.pad(w, ((0, 0), (0, n - w.shape[1])))


def _pad_rows(w, n):
    return jnp.pad(w, ((0, n - w.shape[0]), (0, 0)))


def _rwkv_pre(x, gmix, mu, w0, a0, w_r, w_k, w_v, w1, w2, a1, a2, g1, g2):
    bsz, seq, d = x.shape
    lw = -(-w1.shape[1] // LANES) * LANES
    la = -(-a1.shape[1] // LANES) * LANES
    lg = -(-g1.shape[1] // LANES) * LANES
    mu8 = jnp.pad(mu, ((0, SUBLANES - mu.shape[0]), (0, 0)))
    args = (x, gmix.reshape(1, d), mu8, w0.reshape(1, d), a0.reshape(1, d),
            w_r.astype(BF16), w_k.astype(BF16), w_v.astype(BF16),
            _pad_cols(w1, lw).astype(BF16), _pad_rows(w2, lw).astype(BF16),
            _pad_cols(a1, la).astype(BF16), _pad_rows(a2, la).astype(BF16),
            _pad_cols(g1, lg).astype(BF16), _pad_rows(g2, lg).astype(BF16))
    tile = pl.BlockSpec((1, PRE_ROWS, d), lambda b, s: (b, s, 0))
    in_specs = [tile] + [_const_spec(a.shape) for a in args[1:]]
    out_dt = (BF16, BF16, BF16, BF16, F32, BF16)
    return pl.pallas_call(
        _rwkv_pre_kernel,
        grid=(bsz, seq // PRE_ROWS),
        in_specs=in_specs,
        out_specs=[tile] * 6,
        out_shape=[jax.ShapeDtypeStruct((bsz, seq, d), t) for t in out_dt],
        scratch_shapes=[pltpu.VMEM((SUBLANES, d), F32)],
        compiler_params=_cparams("arbitrary", "arbitrary"),
        name="rwkv_pre",
    )(*args)


def _rwkv_rec_kernel(cast_scales, r_ref, k_ref, v_ref, a_ref, lw_ref, g_ref, kk_ref, ka_ref, rk_ref, lnw_ref, lnb_ref,
                     *refs):
    n_casts = len(cast_scales)
    cast_in, o_ref, cast_out, state_ref = refs[:n_casts], refs[n_casts], refs[n_casts + 1:-1], refs[-1]
    for src, dst, scale in zip(cast_in, cast_out, cast_scales):
        dst[...] = (src[...] if scale == 1.0 else src[...] * scale).astype(BF16)

    @pl.when(pl.program_id(2) == 0)
    def _():
        state_ref[...] = jnp.zeros_like(state_ref)

    c = CHUNK
    hd = RWKV_HEAD
    w = 2 * hd
    assert c == hd
    n = r_ref.shape[1] // c
    lane = lax.broadcasted_iota(jnp.int32, (c, w), 1)
    row = lax.broadcasted_iota(jnp.int32, (c, w), 0)
    head0 = lane < hd
    col = jnp.where(head0, lane, lane - hd)
    strict = row > col
    incl = row >= col
    eye = (row == col).astype(F32)
    tri = (lax.broadcasted_iota(jnp.int32, (c, c), 0) >= lax.broadcasted_iota(jnp.int32, (c, c), 1)).astype(BF16)
    pairs = r_ref.shape[2] // w

    def head_sum(x):
        s0 = jnp.sum(jnp.where(head0, x, 0.0), axis=-1, keepdims=True)
        s1 = jnp.sum(jnp.where(head0, 0.0, x), axis=-1, keepdims=True)
        return jnp.where(head0, s0, s1)

    def stack(x):
        return jnp.concatenate([jnp.where(head0, x, 0.0), jnp.where(head0, 0.0, x)], axis=0)

    def cumsum(x):
        hi = x.astype(BF16)
        lo = (x - hi.astype(F32)).astype(BF16)
        both = _dot(tri, jnp.concatenate([hi, lo], axis=1))
        return both[:, :w] + both[:, w:]

    zeros = jnp.zeros((c, w), BF16)
    carried = [state_ref[p] for p in range(pairs)]

    def chunk(j, p):
        rs = slice(j * c, (j + 1) * c)
        ls = slice(p * w, (p + 1) * w)
        kkp, kap, rkp, lnw, lnb = kk_ref[:, ls], ka_ref[:, ls], rk_ref[:, ls], lnw_ref[:, ls], lnb_ref[:, ls]
        r_ = r_ref[0, rs, ls].astype(F32)
        k_ = k_ref[0, rs, ls].astype(F32)
        v_ = v_ref[0, rs, ls].astype(F32)
        a_ = a_ref[0, rs, ls].astype(F32)
        lw_ = lw_ref[0, rs, ls]
        cum = cumsum(lw_)
        yield
        kk = k_ * kkp
        kk = kk * lax.rsqrt(jnp.maximum(head_sum(kk * kk), 1e-24))
        k2 = k_ * (1.0 + (a_ - 1.0) * kap)
        p_ = -(kk * a_)
        tot = cum[c - 1:c, :]
        e_neg = jnp.exp2(-cum)
        e_tot = jnp.exp2(tot - cum)
        rt = r_ * jnp.exp2(cum)
        qt = kk * jnp.exp2(cum - lw_)
        vb = v_.astype(BF16)
        amat = _dot_nt(jnp.concatenate([qt, rt], axis=0).astype(BF16),
                       jnp.concatenate([stack((k2 * e_neg).astype(BF16)), stack((p_ * e_neg).astype(BF16))], axis=0))
        yield
        a_qp = jnp.where(strict, amat[:c, w:], 0.0)
        a_rp = jnp.where(incl, amat[c:, w:], 0.0).astype(BF16)
        a_k = jnp.concatenate([jnp.where(strict, amat[:c, :w], 0.0),
                               jnp.where(incl, amat[c:, :w], 0.0)], axis=0).astype(BF16)
        av = _dot(a_k, stack(vb))
        inv = eye + a_qp
        power = a_qp.astype(BF16)
        power = _dot(power, stack(power))
        yield
        steps = int(math.log2(c)) - 1
        for i in range(steps):
            pb = power.astype(BF16)
            if i + 1 < steps:
                both = _dot(jnp.concatenate([inv.astype(BF16), pb], axis=0), stack(pb))
                inv = inv + both[:c]
                power = both[c:]
            else:
                inv = inv + _dot(inv.astype(BF16), stack(pb))
            yield
        wu = _dot(inv.astype(BF16),
                  jnp.concatenate([stack(qt.astype(BF16)), stack(av[:c].astype(BF16))], axis=1))
        wub = wu.astype(BF16)
        yield
        ry = _dot(a_rp, jnp.concatenate([stack(wub[:, :w]), stack(wub[:, w:])], axis=1))
        mg = _dot_tn(jnp.concatenate([p_ * e_tot, k2 * e_tot], axis=0).astype(BF16),
                     jnp.concatenate([wub, jnp.concatenate([zeros, vb], axis=1)], axis=0))
        yield
        rw = (rt + ry[:, :w]).astype(BF16)
        m_w = jnp.where(head0, mg[:c, :w], mg[c:, :w]).astype(BF16)
        g_w = jnp.where(head0, mg[:c, w:], mg[c:, w:])
        gam = head_sum(eye * jnp.exp2(tot))
        state = carried[p]
        both = _dot(jnp.concatenate([rw, m_w], axis=0), stack(state.astype(BF16)))
        y = both[:c] + av[c:] + ry[:, w:]
        carried[p] = gam * state + both[c:] + g_w
        yield
        yc = y - head_sum(y) * (1.0 / hd)
        var = head_sum(yc * yc) * (1.0 / hd)
        yn = yc * lax.rsqrt(var + GN_EPS) * lnw + lnb
        bonus = head_sum(r_ * k2 * rkp) * v_
        o_ref[0, rs, ls] = ((yn + bonus) * g_ref[0, rs, ls].astype(F32)).astype(BF16)

    live, started = [], 0
    while started < n or live:
        if started < n:
            live.extend(chunk(started, p) for p in range(pairs))
            started += 1
        for gen in list(live):
            if next(gen, "done") == "done":
                live.remove(gen)
    for p in range(pairs):
        state_ref[p] = carried[p]


def _rwkv_rec(r, k, v, a, lw, g, k_k, k_a, r_k, ln_w, ln_b, casts=(), cast_scales=()):
    bsz, seq, d = r.shape
    pair_w = 2 * RWKV_HEAD
    hw = REC_PAIRS * pair_w
    grid = (bsz, d // hw, seq // REC_ROWS)
    steps = grid[0] * grid[1] * grid[2]
    tile = pl.BlockSpec((1, REC_ROWS, hw), lambda b, h, s: (b, s, h))
    par = pl.BlockSpec((1, hw), lambda b, h, s: (0, h))
    params = [p.reshape(1, d).astype(F32) for p in (k_k, k_a, r_k, ln_w, ln_b)]
    cast_specs = []
    for w in casts:
        assert w.shape[0] % (steps * 2 * SUBLANES) == 0
        cast_specs.append(pl.BlockSpec((w.shape[0] // steps, w.shape[1]),
                                       lambda b, h, s: ((b * grid[1] + h) * grid[2] + s, 0)))
    out = pl.pallas_call(
        functools.partial(_rwkv_rec_kernel, tuple(cast_scales)),
        grid=grid,
        in_specs=[tile] * 6 + [par] * 5 + cast_specs,
        out_specs=[tile] + cast_specs,
        out_shape=[jax.ShapeDtypeStruct((bsz, seq, d), BF16)] + [jax.ShapeDtypeStruct(w.shape, BF16) for w in casts],
        scratch_shapes=[pltpu.VMEM((REC_PAIRS, CHUNK, pair_w), F32)],
        compiler_params=_cparams("arbitrary", "arbitrary", "arbitrary"),
        name="rwkv_rec",
    )(r, k, v, a, lw, g, *params, *casts)
    return out[0], out[1:]


def _moe_route_kernel(h_ref, g_ref, wr_ref, br_ref, xs_ref, gs_ref, dest_ref, cnt_ref):
    _route_block(h_ref[...], g_ref, wr_ref, br_ref, xs_ref, gs_ref, dest_ref, cnt_ref)


def _moe_route_proj_kernel(a_ref, w_ref, res_ref, g_ref, wr_ref, br_ref, h_ref, xs_ref, gs_ref, dest_ref, cnt_ref):
    h = res_ref[...] + _dot(a_ref[...], w_ref[...])
    h_ref[...] = h
    _route_block(h, g_ref, wr_ref, br_ref, xs_ref, gs_ref, dest_ref, cnt_ref)


def _route_block(h, g_ref, wr_ref, br_ref, xs_ref, gs_ref, dest_ref, cnt_ref):
    nb = h.shape[0]
    nbp = xs_ref.shape[0]
    tb = (_rms(h) * g_ref[...]).astype(BF16)
    logits = _dot(tb, wr_ref[...]) + br_ref[...]
    lane = lax.broadcasted_iota(jnp.int32, logits.shape, 1).astype(F32)
    neg = jnp.float32(-jnp.inf)
    big = jnp.float32(LANES)
    is_grp = lane < N_EXPERT_GROUPS
    gl = jnp.where(is_grp, logits, neg)
    gmax = jnp.max(gl, axis=-1, keepdims=True)
    grp = jnp.min(jnp.where(gl == gmax, lane, big), axis=-1, keepdims=True)
    p_grp = 1.0 / jnp.sum(jnp.where(is_grp, jnp.exp(gl - gmax), 0.0), axis=-1, keepdims=True)
    e_lo = N_EXPERT_GROUPS + grp * EXPERTS_PER_GROUP
    in_grp = (lane >= e_lo) & (lane < e_lo + EXPERTS_PER_GROUP)
    el = jnp.where(in_grp, logits, neg)
    v1 = jnp.max(el, axis=-1, keepdims=True)
    i1 = jnp.min(jnp.where(el == v1, lane, big), axis=-1, keepdims=True)
    el2 = jnp.where(lane == i1, neg, el)
    v2 = jnp.max(el2, axis=-1, keepdims=True)
    i2 = jnp.min(jnp.where(el2 == v2, lane, big), axis=-1, keepdims=True)
    e2 = jnp.exp(v2 - v1)
    w1 = 1.0 / (1.0 + e2)
    w2 = e2 / (1.0 + e2)
    gates = jnp.where(lane == i1, w1, jnp.where(lane == i2, w2, 0.0)) * p_grp

    onehot = (lane == grp).astype(BF16)
    ri = lax.broadcasted_iota(jnp.int32, (nb, nb), 0)
    ci = lax.broadcasted_iota(jnp.int32, (nb, nb), 1)
    rank = _dot((ri > ci).astype(BF16), onehot)
    counts = jnp.sum(onehot.astype(F32), axis=0, keepdims=True)
    padded = jnp.ceil(counts / FFN_ROWS) * FFN_ROWS
    lane1 = lax.broadcasted_iota(jnp.int32, (1, LANES), 1)
    offs = jnp.zeros((1, LANES), F32)
    for gidx in range(1, N_EXPERT_GROUPS):
        prev = jnp.sum(jnp.where(lane1 < gidx, padded, 0.0), axis=-1, keepdims=True)
        offs = jnp.where(lane1 == gidx, prev, offs)
    dest = jnp.sum(onehot.astype(F32) * (rank + offs), axis=-1, keepdims=True)
    slot = lax.broadcasted_iota(jnp.int32, (nb, nbp), 1).astype(F32)
    perm_t = (dest == slot).astype(BF16)
    xs_ref[...] = _dot_tn(perm_t, tb).astype(BF16)
    g_hi = gates.astype(BF16)
    g_lo = (gates - g_hi.astype(F32)).astype(BF16)
    moved = _dot_tn(perm_t, jnp.concatenate([g_hi, g_lo], axis=1))
    gs_ref[...] = moved[:, :LANES] + moved[:, LANES:]
    dest_ref[...] = dest
    cnt_ref[0] = jnp.where(lane1 < N_EXPERT_GROUPS, counts, 0.0)


def _moe_ffn_kernel(tile_ref, grp_ref, real_ref, *refs):
    nt = FFN_TILES
    xs_refs, gs_refs = refs[:nt], refs[nt:2 * nt]
    wg_ref, wu_ref, wd_ref, ys_ref = refs[2 * nt:]
    i = pl.program_id(0)

    @pl.when(real_ref[i] > 0)
    def _():
        x = jnp.concatenate([r[...] for r in xs_refs], axis=0)
        gates = jnp.concatenate([r[...] for r in gs_refs], axis=0)
        lane = lax.broadcasted_iota(jnp.int32, gates.shape, 1)
        base = N_EXPERT_GROUPS + grp_ref[i] * EXPERTS_PER_GROUP
        acc = jnp.zeros(ys_ref.shape, F32)
        for e in range(EXPERTS_PER_GROUP):
            ge = jnp.sum(jnp.where(lane == base + e, gates, 0.0), axis=-1, keepdims=True)
            gate_act = _dot(x, wg_ref[e])
            hdn = gate_act * _sigmoid(gate_act) * _dot(x, wu_ref[e])
            acc = acc + ge * _dot(hdn.astype(BF16), wd_ref[e])
        ys_ref[...] = acc.astype(BF16)

    @pl.when(real_ref[i] == 0)
    def _():
        ys_ref[...] = jnp.zeros_like(ys_ref)


def _moe_merge_kernel(slot_ref, h_ref, dest_ref, *refs):
    ys_refs, o_ref = refs[:-1], refs[-1]
    ys = jnp.concatenate([r[...] for r in ys_refs], axis=0)
    nb, nbp = h_ref.shape[0], ys.shape[0]
    slot = lax.broadcasted_iota(jnp.int32, (nb, nbp), 1).astype(F32)
    perm_t = (dest_ref[...] == slot).astype(BF16)
    o_ref[...] = h_ref[...] + _dot(perm_t, ys)


def _moe_merge_norm_kernel(slot_ref, h_ref, dest_ref, *refs):
    ys_refs, g_ref, o_ref = refs[:-2], refs[-2], refs[-1]
    ys = jnp.concatenate([r[...] for r in ys_refs], axis=0)
    nb, nbp = h_ref.shape[0], ys.shape[0]
    slot = lax.broadcasted_iota(jnp.int32, (nb, nbp), 1).astype(F32)
    perm_t = (dest_ref[...] == slot).astype(BF16)
    o_ref[...] = _rms(h_ref[...] + _dot(perm_t, ys)) * g_ref[...]


def _ffn_slots(n_tiles):
    return -(-(n_tiles + N_EXPERT_GROUPS * (FFN_TILES - 1)) // FFN_TILES) * FFN_TILES


def _ffn_schedule(counts, nblk, tiles_per_blk):
    i32 = jnp.int32
    ft, g_n = FFN_TILES, N_EXPERT_GROUPS
    tiles = (counts + FFN_ROWS - 1) // FFN_ROWS
    seg_end = jnp.cumsum(tiles, axis=1)
    tot = jnp.sum(tiles, axis=0)
    grp_pad = (tot + ft - 1) // ft * ft
    grp_end = jnp.cumsum(grp_pad)
    grp_start = grp_end - grp_pad
    before = jnp.cumsum(tiles, axis=0) - tiles
    n_tiles = nblk * tiles_per_blk
    n_slots = _ffn_slots(n_tiles)
    k = jnp.arange(tiles_per_blk, dtype=i32)[None, :]
    used = k < seg_end[:, -1:]
    pos = jnp.zeros((nblk, tiles_per_blk), i32)
    for g in range(g_n):
        in_seg = (k >= seg_end[:, g:g + 1] - tiles[:, g:g + 1]) & (k < seg_end[:, g:g + 1])
        pos = jnp.where(in_seg, grp_start[g] + before[:, g:g + 1] + k - (seg_end[:, g:g + 1] - tiles[:, g:g + 1]), pos)
    gap = grp_pad - tot
    gap_end = jnp.cumsum(gap)
    rank = (jnp.cumsum(jnp.logical_not(used).astype(i32).reshape(-1)) - 1).reshape(nblk, tiles_per_blk)
    free = grp_end[-1] + rank - gap_end[-1]
    for g in range(g_n - 1, -1, -1):
        free = jnp.where(rank < gap_end[g], grp_start[g] + tot[g] + rank - (gap_end[g] - gap[g]), free)
    tile_slot = jnp.where(used, pos, free).reshape(-1).astype(i32)
    s = jnp.arange(n_slots, dtype=i32)
    t = jnp.arange(n_tiles, dtype=i32)
    slot_tile = jnp.sum(jnp.where(tile_slot[None, :] == s[:, None], t[None, :], 0), axis=1).astype(i32)
    s0 = (jnp.arange(n_slots // ft, dtype=i32) * ft)[:, None]
    in_grp = (s0 >= grp_start[None, :]) & (s0 < grp_end[None, :])
    step_real = jnp.sum(jnp.where(in_grp, jnp.clip(tot[None, :] - (s0 - grp_start[None, :]), 0, ft), 0), axis=1)
    last_grp = jnp.max(jnp.where(tot > 0, jnp.arange(g_n, dtype=i32), 0))
    step_grp = jnp.where(step_real > 0, jnp.sum(jnp.where(in_grp, jnp.arange(g_n, dtype=i32)[None, :], 0), axis=1),
                         last_grp)
    return slot_tile, tile_slot, step_grp.astype(i32), step_real.astype(i32)


def _hier_moe_residual(h, norm_g, w_grp, b_grp, w_exp, b_exp, layer, w_gate, w_up, w_down, final_g=None,
                       pending=None):
    n, d = h.shape
    nb = MOE_BLOCK
    nblk = n // nb
    nbp = nb + N_EXPERT_GROUPS * FFN_ROWS
    tiles_per_blk = nbp // FFN_ROWS
    n_exp = N_EXPERT_GROUPS * EXPERTS_PER_GROUP
    w_router = jnp.pad(jnp.concatenate([w_grp, w_exp], axis=1), ((0, 0), (0, LANES - N_EXPERT_GROUPS - n_exp)))
    b_router = jnp.pad(jnp.concatenate([b_grp, b_exp]), (0, LANES - N_EXPERT_GROUPS - n_exp)).reshape(1, LANES)

    rows = pl.BlockSpec((nb, d), lambda i: (i, 0))
    route_in = [_const_spec((1, d)), _const_spec((d, LANES)), _const_spec((1, LANES))]
    route_args = (norm_g.reshape(1, d), w_router.astype(BF16), b_router)
    route_out = [pl.BlockSpec((nbp, d), lambda i: (i, 0)), pl.BlockSpec((nbp, LANES), lambda i: (i, 0)),
                 pl.BlockSpec((nb, 1), lambda i: (i, 0)), pl.BlockSpec((1, 1, LANES), lambda i: (i, 0, 0))]
    route_shape = [jax.ShapeDtypeStruct((nblk * nbp, d), BF16), jax.ShapeDtypeStruct((nblk * nbp, LANES), F32),
                   jax.ShapeDtypeStruct((n, 1), F32), jax.ShapeDtypeStruct((nblk, 1, LANES), F32)]
    if pending is None:
        xs, gs, dest, counts = pl.pallas_call(
            _moe_route_kernel, grid=(nblk,), in_specs=[rows] + route_in, out_specs=route_out, out_shape=route_shape,
            compiler_params=_cparams("arbitrary"), name="moe_route",
        )(h, *route_args)
    else:
        a, w = pending
        h, xs, gs, dest, counts = pl.pallas_call(
            _moe_route_proj_kernel, grid=(nblk,),
            in_specs=[pl.BlockSpec((nb, a.shape[1]), lambda i: (i, 0)), _const_spec(w.shape), rows] + route_in,
            out_specs=[rows] + route_out, out_shape=[jax.ShapeDtypeStruct((n, d), F32)] + route_shape,
            compiler_params=_cparams("arbitrary"), name="moe_route_proj",
        )(a, w.astype(BF16), h, *route_args)

    n_tiles = nblk * tiles_per_blk
    n_slots = _ffn_slots(n_tiles)
    slot_tile, tile_slot, step_grp, step_real = _ffn_schedule(
        counts[:, 0, :N_EXPERT_GROUPS].astype(jnp.int32), nblk, tiles_per_blk)

    def tile_spec(width, j):
        return pl.BlockSpec((FFN_ROWS, width), lambda i, tile, grp, real: (tile[i * FFN_TILES + j], 0))

    def wspec(shape):
        return pl.BlockSpec((None, EXPERTS_PER_GROUP) + shape, lambda i, tile, grp, real: (layer, grp[i], 0, 0))

    f = w_gate.shape[-1]
    ys = pl.pallas_call(
        _moe_ffn_kernel,
        grid_spec=pltpu.PrefetchScalarGridSpec(
            num_scalar_prefetch=3,
            grid=(n_slots // FFN_TILES,),
            in_specs=[tile_spec(d, j) for j in range(FFN_TILES)] + [tile_spec(LANES, j) for j in range(FFN_TILES)]
            + [wspec((d, f)), wspec((d, f)), wspec((f, d))],
            out_specs=pl.BlockSpec((FFN_TILES * FFN_ROWS, d), lambda i, tile, grp, real: (i, 0)),
        ),
        out_shape=jax.ShapeDtypeStruct((n_slots * FFN_ROWS, d), BF16),
        compiler_params=_cparams("arbitrary"),
        name="moe_ffn",
    )(slot_tile, step_grp, step_real, *([xs] * FFN_TILES), *([gs] * FFN_TILES),
      w_gate, w_up, w_down)

    blk = pl.BlockSpec((nb, d), lambda i, slot: (i, 0))
    in_specs = [blk, pl.BlockSpec((nb, 1), lambda i, slot: (i, 0))]
    in_specs += [pl.BlockSpec((FFN_ROWS, d), functools.partial(lambda j, i, slot: (slot[i * tiles_per_blk + j], 0), j))
                 for j in range(tiles_per_blk)]
    args = [h, dest] + [ys] * tiles_per_blk
    body = _moe_merge_kernel
    if final_g is not None:
        in_specs.append(pl.BlockSpec((1, d), lambda i, slot: (0, 0)))
        args.append(final_g.reshape(1, d))
        body = _moe_merge_norm_kernel
    return pl.pallas_call(
        body,
        grid_spec=pltpu.PrefetchScalarGridSpec(num_scalar_prefetch=1, grid=(nblk,), in_specs=in_specs, out_specs=blk),
        out_shape=jax.ShapeDtypeStruct((n, d), F32),
        compiler_params=_cparams("arbitrary"),
        name="moe_merge",
    )(tile_slot, *args)


GROUP_WIDTH = DIL_HEADS * DIL_HEAD_DIM
N_STRIDED = 3 * sum(1 for _, dil in DIL_PATTERNS if dil > 1)


def _qkv_kernel(h_ref, gq_ref, gkv_ref, wq_ref, wk_ref, wv_ref, *refs):
    outs, scratch = refs[:-1], refs[-1]
    n = _rms(h_ref[...])
    xq = (n * gq_ref[...]).astype(BF16)
    xkv = (n * gkv_ref[...]).astype(BF16)
    gw = GROUP_WIDTH
    slot = 0
    full = (_dot(xq, wq_ref[...]), _dot(xkv, wk_ref[...]), _dot(xkv, wv_ref[...]))
    for gi, (_, dil) in enumerate(DIL_PATTERNS):
        cols = slice(gi * gw, (gi + 1) * gw)
        for t in range(3):
            res = full[t][:, cols]
            out_ref = outs[3 * gi + t]
            if dil == 1:
                out_ref[...] = res.astype(BF16)
                continue
            sub = res.shape[0] // dil
            for c in range(gw // LANES):
                scratch[slot, c] = res[:, c * LANES:(c + 1) * LANES]
            for r in range(dil):
                for c in range(gw // LANES):
                    lo = r * gw + c * LANES
                    out_ref[:, lo:lo + LANES] = scratch[slot, c, pl.ds(r, sub, stride=dil), :].astype(BF16)
            slot += 1


def _qkv_proj(h, g_q, g_kv, w_q, w_kv):
    n, d = h.shape
    qw = w_q.shape[1]
    gw = GROUP_WIDTH
    rows = pl.BlockSpec((PROJ_ROWS, d), lambda i: (i, 0))
    out_specs, out_shape = [], []
    for _, dil in DIL_PATTERNS:
        for _ in range(3):
            out_specs.append(pl.BlockSpec((PROJ_ROWS // dil, dil * gw), lambda i: (i, 0)))
            out_shape.append(jax.ShapeDtypeStruct((n // dil, dil * gw), BF16))
    return pl.pallas_call(
        _qkv_kernel,
        grid=(n // PROJ_ROWS,),
        in_specs=[rows, _const_spec((1, d)), _const_spec((1, d)), _const_spec((d, qw)),
                  pl.BlockSpec((d, qw), lambda i: (0, 0)), pl.BlockSpec((d, qw), lambda i: (0, 1))],
        out_specs=out_specs,
        out_shape=out_shape,
        scratch_shapes=[pltpu.VMEM((N_STRIDED, gw // LANES, PROJ_ROWS, LANES), F32)],
        compiler_params=_cparams("arbitrary"),
        name="qkv_proj",
    )(h, g_q.reshape(1, d), g_kv.reshape(1, d), w_q, w_kv, w_kv)


def _attn_kernel(q_ref, kc_ref, kp_ref, vc_ref, vp_ref, o_ref, lse_ref):
    n = pl.program_id(2)
    blk = ATT_BLOCK
    pw = 2 * DIL_HEAD_DIM
    ri = lax.broadcasted_iota(jnp.int32, (blk, blk), 0)
    ci = lax.broadcasted_iota(jnp.int32, (blk, blk), 1)
    lowest = jnp.finfo(F32).min
    cap_prev = jnp.where(ci >= ri, jnp.inf, lowest).astype(F32)
    cap_first = jnp.where(n > 0, cap_prev, lowest)
    cap_own = jnp.where(ci <= ri, jnp.inf, lowest).astype(F32)
    head0 = lax.broadcasted_iota(jnp.int32, (blk, pw), 1) < DIL_HEAD_DIM
    zero = jnp.zeros((blk, pw), BF16)
    lane_t = lax.broadcasted_iota(jnp.int32, (blk, LANES), 1)

    def stack(x):
        return jnp.concatenate([jnp.where(head0, x, zero), jnp.where(head0, zero, x)], axis=0)

    streams = [(i, p) for i in range(q_ref.shape[1] // blk) for p in range(DIL_HEADS // 2)]

    def rows(i):
        return slice(i * blk, (i + 1) * blk)

    def lanes(p):
        return slice(p * pw, (p + 1) * pw)

    stacked = {}

    def kv_blocks(cur_ref, prev_ref, i, p):
        def one(w):
            name = (id(cur_ref), w, p)
            if name not in stacked:
                stacked[name] = stack(prev_ref[0, :, lanes(p)] if w < 0 else cur_ref[0, rows(w), lanes(p)])
            return stacked[name]
        return jnp.concatenate([one(i - 1), one(i)], axis=0)

    s = []
    for i, p in streams:
        raw = _dot_nt(q_ref[0, rows(i), lanes(p)], kv_blocks(kc_ref, kp_ref, i, p))
        cap = cap_first if i == 0 else cap_prev
        s.append([jnp.minimum(raw[:, :blk], cap), jnp.minimum(raw[:, blk:2 * blk], cap),
                  jnp.minimum(raw[:, 2 * blk:3 * blk], cap_own), jnp.minimum(raw[:, 3 * blk:], cap_own)])
    m_a = [jnp.max(jnp.maximum(x[0], x[2]), axis=-1, keepdims=True) for x in s]
    m_b = [jnp.max(jnp.maximum(x[1], x[3]), axis=-1, keepdims=True) for x in s]
    e = [[jnp.exp2(x[0] - ma), jnp.exp2(x[1] - mb), jnp.exp2(x[2] - ma), jnp.exp2(x[3] - mb)]
         for x, ma, mb in zip(s, m_a, m_b)]
    pb = [jnp.concatenate([y.astype(BF16) for y in x], axis=1) for x in e]
    blk_head = (lax.broadcasted_iota(jnp.int32, (4 * blk, LANES), 0) // blk) % 2
    sel_lane = lax.broadcasted_iota(jnp.int32, (4 * blk, LANES), 1) - DIL_HEADS
    sel = [(sel_lane == 2 * p + blk_head).astype(BF16) for p in range(DIL_HEADS // 2)]
    pv = [_dot(pb[idx], jnp.concatenate([kv_blocks(vc_ref, vp_ref, i, p), sel[p]], axis=1))
          for idx, (i, p) in enumerate(streams)]
    tiles = {}
    for idx, (i, p) in enumerate(streams):
        o_ref[0, rows(i), lanes(p)] = pv[idx][:, :pw].astype(BF16)
        tile = tiles.get(i, jnp.zeros((blk, LANES), F32)) + pv[idx][:, pw:]
        tiles[i] = jnp.where(lane_t == 2 * p, m_a[idx], jnp.where(lane_t == 2 * p + 1, m_b[idx], tile))
    for i, tile in tiles.items():
        lse_ref[0, rows(i), :] = tile


def _attn_group(q, k, v, gi, dilation, bsz, seq):
    gw = GROUP_WIDTH
    sub = seq // dilation
    windows = min(ATT_WINDOWS, sub // ATT_BLOCK)
    step_rows = windows * ATT_BLOCK
    assert sub % step_rows == 0
    view = (bsz, sub, dilation * gw)
    qv, kv_, vv = q.reshape(view), k.reshape(view), v.reshape(view)
    cur = pl.BlockSpec((1, step_rows, gw), lambda b, r, n: (b, n, r))
    prev = pl.BlockSpec((1, ATT_BLOCK, gw), lambda b, r, n: (b, jnp.maximum(n * windows - 1, 0), r))
    lse_spec = pl.BlockSpec((1, step_rows, LANES), lambda b, r, n: (b, n, r))
    o, lse = pl.pallas_call(
        _attn_kernel,
        grid=(bsz, dilation, sub // step_rows),
        in_specs=[cur, cur, prev, cur, prev],
        out_specs=[cur, lse_spec],
        out_shape=[jax.ShapeDtypeStruct(view, BF16),
                   jax.ShapeDtypeStruct((bsz, sub, dilation * LANES), F32)],
        compiler_params=_cparams("arbitrary", "arbitrary", "arbitrary"),
        name=f"attn_group{gi}",
    )(qv, kv_, kv_, vv, vv)
    return o.reshape(bsz * sub, dilation * gw), lse.reshape(bsz * sub, dilation * LANES)


def _attn_merge_kernel(o0_ref, o1_ref, o2_ref, l0_ref, l1_ref, l2_ref, w_ref, res_ref, out_ref, *scratch):
    rows = res_ref.shape[0]
    gw = GROUP_WIDTH
    outs, lses = [], []
    slot = 0
    for (_, dil), o_ref, l_ref in zip(DIL_PATTERNS, (o0_ref, o1_ref, o2_ref), (l0_ref, l1_ref, l2_ref)):
        if dil == 1:
            outs.append(o_ref[...].astype(F32))
            lses.append(l_ref[...])
            continue
        o_scr, l_scr = scratch[2 * slot], scratch[2 * slot + 1]
        sub = rows // dil
        for r in range(dil):
            for c in range(gw // LANES):
                lo = r * gw + c * LANES
                o_scr[c, pl.ds(r, sub, stride=dil), :] = o_ref[:, lo:lo + LANES].astype(F32)
            l_scr[pl.ds(r, sub, stride=dil), :] = l_ref[:, r * LANES:(r + 1) * LANES]
        outs.append(jnp.concatenate([o_scr[c] for c in range(gw // LANES)], axis=1))
        lses.append(l_scr[...])
        slot += 1
    m = jnp.maximum(jnp.maximum(lses[0], lses[1]), lses[2])
    e = [jnp.exp2(l - m) for l in lses]
    sums = [pltpu.roll(l, LANES - DIL_HEADS, 1) for l in lses]
    denom = e[0] * sums[0] + e[1] * sums[1] + e[2] * sums[2]
    is_head = lax.broadcasted_iota(jnp.int32, (rows, LANES), 1) < DIL_HEADS
    expand = (lax.broadcasted_iota(jnp.int32, (LANES, gw), 1) // DIL_HEAD_DIM
              == lax.broadcasted_iota(jnp.int32, (LANES, gw), 0)).astype(BF16)
    alpha = jnp.concatenate([jnp.where(is_head, x / denom, 0.0) for x in e], axis=0)
    hi = alpha.astype(BF16)
    lo = (alpha - hi.astype(F32)).astype(BF16)
    wide = _dot(jnp.concatenate([hi, lo], axis=1), jnp.concatenate([expand, expand], axis=0))
    mix = jnp.zeros((rows, gw), F32)
    for g in range(len(outs)):
        mix = mix + wide[g * rows:(g + 1) * rows] * outs[g]
    out_ref[...] = res_ref[...] + _dot(mix.astype(BF16), w_ref[...])


def _attn_merge(outs, lses, w_o, res):
    n, d = res.shape
    gw = GROUP_WIDTH
    rows = MERGE_ROWS
    o_specs = [pl.BlockSpec((rows // dil, dil * gw), lambda i: (i, 0)) for _, dil in DIL_PATTERNS]
    l_specs = [pl.BlockSpec((rows // dil, dil * LANES), lambda i: (i, 0)) for _, dil in DIL_PATTERNS]
    full = pl.BlockSpec((rows, d), lambda i: (i, 0))
    scratch = []
    for _, dil in DIL_PATTERNS:
        if dil > 1:
            scratch += [pltpu.VMEM((gw // LANES, rows, LANES), F32), pltpu.VMEM((rows, LANES), F32)]
    return pl.pallas_call(
        _attn_merge_kernel,
        grid=(n // rows,),
        in_specs=o_specs + l_specs + [_const_spec(w_o.shape), full],
        out_specs=full,
        out_shape=jax.ShapeDtypeStruct((n, d), F32),
        scratch_shapes=scratch,
        compiler_params=_cparams("arbitrary"),
        name="attn_merge",
    )(*outs, *lses, w_o.astype(BF16), res)


def kernel(x, norm_mix_g, norm_ffn_g, rwkv_mu, rwkv_w_r, rwkv_w_k, rwkv_w_v, rwkv_w0, rwkv_w1, rwkv_w2,
           rwkv_a0, rwkv_a1, rwkv_a2, rwkv_g1, rwkv_g2, rwkv_k_k, rwkv_k_a, rwkv_r_k, rwkv_ln_w, rwkv_ln_b,
           rwkv_w_o, kv_norm_g, w_kv, attn_w_q, attn_w_o, moe_w_grp, moe_b_grp, moe_w_exp, moe_b_exp,
           moe_w_gate, moe_w_up, moe_w_down, final_norm_g):
    bsz, seq, d = x.shape
    n = bsz * seq
    depth = norm_mix_g.shape[0]
    n_rwkv = rwkv_mu.shape[0]
    h = x.reshape(n, d)
    later_f32 = (moe_w_gate, moe_w_up, moe_w_down, attn_w_q, w_kv)
    later_scale = (1.0, 1.0, 1.0, QK_SCALE, 1.0)
    later_w = None
    q = k_sh = v_sh = None
    for layer in range(depth):
        pending = None
        if layer < n_rwkv:
            i = layer
            r, k, v, a, lw, g = _rwkv_pre(h.reshape(bsz, seq, d), norm_mix_g[layer], rwkv_mu[i], rwkv_w0[i], rwkv_a0[i],
                                          rwkv_w_r[i], rwkv_w_k[i], rwkv_w_v[i], rwkv_w1[i], rwkv_w2[i],
                                          rwkv_a1[i], rwkv_a2[i], rwkv_g1[i], rwkv_g2[i])
            casts = () if later_w is not None else tuple(w.reshape(-1, w.shape[-1]) for w in later_f32)
            y, cast = _rwkv_rec(r, k, v, a, lw, g, rwkv_k_k[i], rwkv_k_a[i], rwkv_r_k[i], rwkv_ln_w[i], rwkv_ln_b[i],
                                casts=casts, cast_scales=later_scale if casts else ())
            if casts:
                later_w = tuple(c.reshape(w.shape) for c, w in zip(cast, later_f32))
            pending = (y.reshape(n, d), rwkv_w_o[i])
        else:
            i = layer - n_rwkv
            if later_w is None:
                later_w = tuple((w * s).astype(BF16) for w, s in zip(later_f32, later_scale))
            qkv = _qkv_proj(h, norm_mix_g[layer], kv_norm_g, later_w[3][i], later_w[4])
            q = qkv[0::3]
            if i == 0:
                k_sh, v_sh = qkv[1::3], qkv[2::3]
            outs, lses = [], []
            for gi, (window, dilation) in enumerate(DIL_PATTERNS):
                assert window // dilation == ATT_BLOCK and seq % window == 0
                o, lse = _attn_group(q[gi], k_sh[gi], v_sh[gi], gi, dilation, bsz, seq)
                outs.append(o)
                lses.append(lse)
            h = _attn_merge(outs, lses, attn_w_o[i], h)
        last = layer == depth - 1
        if later_w is None:
            later_w = tuple((w * s).astype(BF16) for w, s in zip(later_f32, later_scale))
        h = _hier_moe_residual(h, norm_ffn_g[layer], moe_w_grp[layer], moe_b_grp[layer], moe_w_exp[layer],
                               moe_b_exp[layer], layer, *later_w[:3], final_g=final_norm_g if last else None,
                               pending=pending)
    return h.reshape(bsz, seq, d)
```

```python
import functools
import math

import jax
import jax.numpy as jnp
from jax import lax
from jax.experimental import pallas as pl
from jax.experimental.pallas import tpu as pltpu

F32 = jnp.float32
BF16 = jnp.bfloat16

NORM_EPS = 1e-6
LOG2_E = math.log2(math.e)
RWKV_HEAD = 64
GN_EPS = RWKV_HEAD * 1e-5
DIL_PATTERNS = ((128, 1), (512, 4), (2048, 16))
DIL_HEADS = 8
DIL_HEAD_DIM = 64
QK_SCALE = LOG2_E / math.sqrt(DIL_HEAD_DIM)
N_EXPERT_GROUPS = 4
EXPERTS_PER_GROUP = 4
LANES = 128
SUBLANES = 8
VMEM_LIMIT = 56 * 1024 * 1024

CHUNK = 64
REC_ROWS = 2048
REC_PAIRS = 2
PRE_ROWS = 512
PROJ_ROWS = 512
MERGE_ROWS = 1024
MOE_BLOCK = 1024
FFN_ROWS = 64
FFN_TILES = 8
ATT_BLOCK = 128
ATT_WINDOWS = 8


def _cparams(*sem):
    return pltpu.CompilerParams(dimension_semantics=sem, vmem_limit_bytes=VMEM_LIMIT)


def _dot(a, b):
    return jnp.dot(a, b, preferred_element_type=F32)


def _dot_nt(a, b):
    return lax.dot_general(a, b, (((1,), (1,)), ((), ())), preferred_element_type=F32)


def _dot_tn(a, b):
    return lax.dot_general(a, b, (((0,), (0,)), ((), ())), preferred_element_type=F32)


def _rms(x):
    return x * lax.rsqrt(jnp.mean(x * x, axis=-1, keepdims=True) + NORM_EPS)


def _sigmoid(z):
    return 1.0 / (1.0 + jnp.exp(-z))


def _const_spec(shape):
    nd = len(shape)
    return pl.BlockSpec(shape, lambda *_: (0,) * nd)


def _rwkv_pre_kernel(x_ref, gmix_ref, mu_ref, w0_ref, a0_ref, wr_ref, wk_ref, wv_ref,
                     w1_ref, w2_ref, a1_ref, a2_ref, g1_ref, g2_ref,
                     r_ref, k_ref, v_ref, a_ref, lw_ref, g_ref, prev_ref):
    @pl.when(pl.program_id(1) == 0)
    def _():
        prev_ref[...] = jnp.zeros_like(prev_ref)

    x = x_ref[0]
    rows = x.shape[0]
    xn = _rms(x) * gmix_ref[...]
    row = lax.broadcasted_iota(jnp.int32, xn.shape, 0)
    shifted = jnp.where(row == 0, prev_ref[SUBLANES - 1:, :], pltpu.roll(xn, 1, 0))
    prev_ref[...] = xn[rows - SUBLANES:, :]
    xx = shifted - xn

    def mix(i):
        return (xn + xx * mu_ref[i:i + 1, :]).astype(BF16)

    r_ref[0] = _dot(mix(0), wr_ref[...]).astype(BF16)
    k_ref[0] = _dot(mix(2), wk_ref[...]).astype(BF16)
    v_ref[0] = _dot(mix(3), wv_ref[...]).astype(BF16)
    u = w0_ref[...] + _dot(jnp.tanh(_dot(mix(1), w1_ref[...])).astype(BF16), w2_ref[...])
    w_log = -(jnp.maximum(-u, 0.0) + jnp.log(1.0 + jnp.exp(-jnp.abs(u)))) - 0.5
    lw_ref[0] = -jnp.exp(w_log) * LOG2_E
    a_ref[0] = _sigmoid(a0_ref[...] + _dot(_dot(mix(4), a1_ref[...]).astype(BF16), a2_ref[...])).astype(BF16)
    g_ref[0] = _dot(_sigmoid(_dot(mix(5), g1_ref[...])).astype(BF16), g2_ref[...]).astype(BF16)


def _pad_cols(w, n):
    return jnp.pad(w, ((0, 0), (0, n - w.shape[1])))


def _pad_rows(w, n):
    return jnp.pad(w, ((0, n - w.shape[0]), (0, 0)))


def _rwkv_pre(x, gmix, mu, w0, a0, w_r, w_k, w_v, w1, w2, a1, a2, g1, g2):
    bsz, seq, d = x.shape
    lw = -(-w1.shape[1] // LANES) * LANES
    la = -(-a1.shape[1] // LANES) * LANES
    lg = -(-g1.shape[1] // LANES) * LANES
    mu8 = jnp.pad(mu, ((0, SUBLANES - mu.shape[0]), (0, 0)))
    args = (x, gmix.reshape(1, d), mu8, w0.reshape(1, d), a0.reshape(1, d),
            w_r.astype(BF16), w_k.astype(BF16), w_v.astype(BF16),
            _pad_cols(w1, lw).astype(BF16), _pad_rows(w2, lw).astype(BF16),
            _pad_cols(a1, la).astype(BF16), _pad_rows(a2, la).astype(BF16),
            _pad_cols(g1, lg).astype(BF16), _pad_rows(g2, lg).astype(BF16))
    tile = pl.BlockSpec((1, PRE_ROWS, d), lambda b, s: (b, s, 0))
    in_specs = [tile] + [_const_spec(a.shape) for a in args[1:]]
    out_dt = (BF16, BF16, BF16, BF16, F32, BF16)
    return pl.pallas_call(
        _rwkv_pre_kernel,
        grid=(bsz, seq // PRE_ROWS),
        in_specs=in_specs,
        out_specs=[tile] * 6,
        out_shape=[jax.ShapeDtypeStruct((bsz, seq, d), t) for t in out_dt],
        scratch_shapes=[pltpu.VMEM((SUBLANES, d), F32)],
        compiler_params=_cparams("arbitrary", "arbitrary"),
        name="rwkv_pre",
    )(*args)


def _rwkv_rec_kernel(cast_scales, r_ref, k_ref, v_ref, a_ref, lw_ref, g_ref, kk_ref, ka_ref, rk_ref, lnw_ref, lnb_ref,
                     *refs):
    n_casts = len(cast_scales)
    cast_in, o_ref, cast_out, state_ref = refs[:n_casts], refs[n_casts], refs[n_casts + 1:-1], refs[-1]
    for src, dst, scale in zip(cast_in, cast_out, cast_scales):
        dst[...] = (src[...] if scale == 1.0 else src[...] * scale).astype(BF16)

    @pl.when(pl.program_id(2) == 0)
    def _():
        state_ref[...] = jnp.zeros_like(state_ref)

    c = CHUNK
    hd = RWKV_HEAD
    w = 2 * hd
    assert c == hd
    n = r_ref.shape[1] // c
    lane = lax.broadcasted_iota(jnp.int32, (c, w), 1)
    row = lax.broadcasted_iota(jnp.int32, (c, w), 0)
    head0 = lane < hd
    col = jnp.where(head0, lane, lane - hd)
    strict = row > col
    incl = row >= col
    eye = (row == col).astype(F32)
    tri = (lax.broadcasted_iota(jnp.int32, (c, c), 0) >= lax.broadcasted_iota(jnp.int32, (c, c), 1)).astype(BF16)
    pairs = r_ref.shape[2] // w

    def head_sum(x):
        s0 = jnp.sum(jnp.where(head0, x, 0.0), axis=-1, keepdims=True)
        s1 = jnp.sum(jnp.where(head0, 0.0, x), axis=-1, keepdims=True)
        return jnp.where(head0, s0, s1)

    def stack(x):
        return jnp.concatenate([jnp.where(head0, x, 0.0), jnp.where(head0, 0.0, x)], axis=0)

    def cumsum(x):
        hi = x.astype(BF16)
        lo = (x - hi.astype(F32)).astype(BF16)
        both = _dot(tri, jnp.concatenate([hi, lo], axis=1))
        return both[:, :w] + both[:, w:]

    zeros = jnp.zeros((c, w), BF16)
    carried = [state_ref[p] for p in range(pairs)]

    def chunk(j, p):
        rs = slice(j * c, (j + 1) * c)
        ls = slice(p * w, (p + 1) * w)
        kkp, kap, rkp, lnw, lnb = kk_ref[:, ls], ka_ref[:, ls], rk_ref[:, ls], lnw_ref[:, ls], lnb_ref[:, ls]
        r_ = r_ref[0, rs, ls].astype(F32)
        k_ = k_ref[0, rs, ls].astype(F32)
        v_ = v_ref[0, rs, ls].astype(F32)
        a_ = a_ref[0, rs, ls].astype(F32)
        lw_ = lw_ref[0, rs, ls]
        cum = cumsum(lw_)
        yield
        kk = k_ * kkp
        kk = kk * lax.rsqrt(jnp.maximum(head_sum(kk * kk), 1e-24))
        k2 = k_ * (1.0 + (a_ - 1.0) * kap)
        p_ = -(kk * a_)
        tot = cum[c - 1:c, :]
        e_neg = jnp.exp2(-cum)
        e_tot = jnp.exp2(tot - cum)
        rt = r_ * jnp.exp2(cum)
        qt = kk * jnp.exp2(cum - lw_)
        vb = v_.astype(BF16)
        amat = _dot_nt(jnp.concatenate([qt, rt], axis=0).astype(BF16),
                       jnp.concatenate([stack((k2 * e_neg).astype(BF16)), stack((p_ * e_neg).astype(BF16))], axis=0))
        yield
        a_qp = jnp.where(strict, amat[:c, w:], 0.0)
        a_rp = jnp.where(incl, amat[c:, w:], 0.0).astype(BF16)
        a_k = jnp.concatenate([jnp.where(strict, amat[:c, :w], 0.0),
                               jnp.where(incl, amat[c:, :w], 0.0)], axis=0).astype(BF16)
        av = _dot(a_k, stack(vb))
        inv = eye + a_qp
        power = a_qp.astype(BF16)
        power = _dot(power, stack(power))
        yield
        steps = int(math.log2(c)) - 1
        for i in range(steps):
            pb = power.astype(BF16)
            if i + 1 < steps:
                both = _dot(jnp.concatenate([inv.astype(BF16), pb], axis=0), stack(pb))
                inv = inv + both[:c]
                power = both[c:]
            else:
                inv = inv + _dot(inv.astype(BF16), stack(pb))
            yield
        wu = _dot(inv.astype(BF16),
                  jnp.concatenate([stack(qt.astype(BF16)), stack(av[:c].astype(BF16))], axis=1))
        wub = wu.astype(BF16)
        yield
        ry = _dot(a_rp, jnp.concatenate([stack(wub[:, :w]), stack(wub[:, w:])], axis=1))
        mg = _dot_tn(jnp.concatenate([p_ * e_tot, k2 * e_tot], axis=0).astype(BF16),
                     jnp.concatenate([wub, jnp.concatenate([zeros, vb], axis=1)], axis=0))
        yield
        rw = (rt + ry[:, :w]).astype(BF16)
        m_w = jnp.where(head0, mg[:c, :w], mg[c:, :w]).astype(BF16)
        g_w = jnp.where(head0, mg[:c, w:], mg[c:, w:])
        gam = head_sum(eye * jnp.exp2(tot))
        state = carried[p]
        both = _dot(jnp.concatenate([rw, m_w], axis=0), stack(state.astype(BF16)))
        y = both[:c] + av[c:] + ry[:, w:]
        carried[p] = gam * state + both[c:] + g_w
        yield
        yc = y - head_sum(y) * (1.0 / hd)
        var = head_sum(yc * yc) * (1.0 / hd)
        yn = yc * lax.rsqrt(var + GN_EPS) * lnw + lnb
        bonus = head_sum(r_ * k2 * rkp) * v_
        o_ref[0, rs, ls] = ((yn + bonus) * g_ref[0, rs, ls].astype(F32)).astype(BF16)

    live, started = [], 0
    while started < n or live:
        if started < n:
            live.extend(chunk(started, p) for p in range(pairs))
            started += 1
        for gen in list(live):
            if next(gen, "done") == "done":
                live.remove(gen)
    for p in range(pairs):
        state_ref[p] = carried[p]


def _rwkv_rec(r, k, v, a, lw, g, k_k, k_a, r_k, ln_w, ln_b, casts=(), cast_scales=()):
    bsz, seq, d = r.shape
    pair_w = 2 * RWKV_HEAD
    hw = REC_PAIRS * pair_w
    grid = (bsz, d // hw, seq // REC_ROWS)
    steps = grid[0] * grid[1] * grid[2]
    tile = pl.BlockSpec((1, REC_ROWS, hw), lambda b, h, s: (b, s, h))
    par = pl.BlockSpec((1, hw), lambda b, h, s: (0, h))
    params = [p.reshape(1, d).astype(F32) for p in (k_k, k_a, r_k, ln_w, ln_b)]
    cast_specs = []
    for w in casts:
        assert w.shape[0] % (steps * 2 * SUBLANES) == 0
        cast_specs.append(pl.BlockSpec((w.shape[0] // steps, w.shape[1]),
                                       lambda b, h, s: ((b * grid[1] + h) * grid[2] + s, 0)))
    out = pl.pallas_call(
        functools.partial(_rwkv_rec_kernel, tuple(cast_scales)),
        grid=grid,
        in_specs=[tile] * 6 + [par] * 5 + cast_specs,
        out_specs=[tile] + cast_specs,
        out_shape=[jax.ShapeDtypeStruct((bsz, seq, d), BF16)] + [jax.ShapeDtypeStruct(w.shape, BF16) for w in casts],
        scratch_shapes=[pltpu.VMEM((REC_PAIRS, CHUNK, pair_w), F32)],
        compiler_params=_cparams("arbitrary", "arbitrary", "arbitrary"),
        name="rwkv_rec",
    )(r, k, v, a, lw, g, *params, *casts)
    return out[0], out[1:]


def _moe_route_kernel(h_ref, g_ref, wr_ref, br_ref, xs_ref, gs_ref, dest_ref, cnt_ref):
    _route_block(h_ref[...], g_ref, wr_ref, br_ref, xs_ref, gs_ref, dest_ref, cnt_ref)


def _moe_route_proj_kernel(a_ref, w_ref, res_ref, g_ref, wr_ref, br_ref, h_ref, xs_ref, gs_ref, dest_ref, cnt_ref):
    h = res_ref[...] + _dot(a_ref[...], w_ref[...])
    h_ref[...] = h
    _route_block(h, g_ref, wr_ref, br_ref, xs_ref, gs_ref, dest_ref, cnt_ref)


def _route_block(h, g_ref, wr_ref, br_ref, xs_ref, gs_ref, dest_ref, cnt_ref):
    nb = h.shape[0]
    nbp = xs_ref.shape[0]
    tb = (_rms(h) * g_ref[...]).astype(BF16)
    logits = _dot(tb, wr_ref[...]) + br_ref[...]
    lane = lax.broadcasted_iota(jnp.int32, logits.shape, 1).astype(F32)
    neg = jnp.float32(-jnp.inf)
    big = jnp.float32(LANES)
    is_grp = lane < N_EXPERT_GROUPS
    gl = jnp.where(is_grp, logits, neg)
    gmax = jnp.max(gl, axis=-1, keepdims=True)
    grp = jnp.min(jnp.where(gl == gmax, lane, big), axis=-1, keepdims=True)
    p_grp = 1.0 / jnp.sum(jnp.where(is_grp, jnp.exp(gl - gmax), 0.0), axis=-1, keepdims=True)
    e_lo = N_EXPERT_GROUPS + grp * EXPERTS_PER_GROUP
    in_grp = (lane >= e_lo) & (lane < e_lo + EXPERTS_PER_GROUP)
    el = jnp.where(in_grp, logits, neg)
    v1 = jnp.max(el, axis=-1, keepdims=True)
    i1 = jnp.min(jnp.where(el == v1, lane, big), axis=-1, keepdims=True)
    el2 = jnp.where(lane == i1, neg, el)
    v2 = jnp.max(el2, axis=-1, keepdims=True)
    i2 = jnp.min(jnp.where(el2 == v2, lane, big), axis=-1, keepdims=True)
    e2 = jnp.exp(v2 - v1)
    w1 = 1.0 / (1.0 + e2)
    w2 = e2 / (1.0 + e2)
    gates = jnp.where(lane == i1, w1, jnp.where(lane == i2, w2, 0.0)) * p_grp

    onehot = (lane == grp).astype(BF16)
    ri = lax.broadcasted_iota(jnp.int32, (nb, nb), 0)
    ci = lax.broadcasted_iota(jnp.int32, (nb, nb), 1)
    rank = _dot((ri > ci).astype(BF16), onehot)
    counts = jnp.sum(onehot.astype(F32), axis=0, keepdims=True)
    padded = jnp.ceil(counts / FFN_ROWS) * FFN_ROWS
    lane1 = lax.broadcasted_iota(jnp.int32, (1, LANES), 1)
    offs = jnp.zeros((1, LANES), F32)
    for gidx in range(1, N_EXPERT_GROUPS):
        prev = jnp.sum(jnp.where(lane1 < gidx, padded, 0.0), axis=-1, keepdims=True)
        offs = jnp.where(lane1 == gidx, prev, offs)
    dest = jnp.sum(onehot.astype(F32) * (rank + offs), axis=-1, keepdims=True)
    slot = lax.broadcasted_iota(jnp.int32, (nb, nbp), 1).astype(F32)
    perm_t = (dest == slot).astype(BF16)
    xs_ref[...] = _dot_tn(perm_t, tb).astype(BF16)
    g_hi = gates.astype(BF16)
    g_lo = (gates - g_hi.astype(F32)).astype(BF16)
    moved = _dot_tn(perm_t, jnp.concatenate([g_hi, g_lo], axis=1))
    gs_ref[...] = moved[:, :LANES] + moved[:, LANES:]
    dest_ref[...] = dest
    cnt_ref[0] = jnp.where(lane1 < N_EXPERT_GROUPS, counts, 0.0)


def _moe_ffn_kernel(tile_ref, grp_ref, real_ref, *refs):
    nt = FFN_TILES
    xs_refs, gs_refs = refs[:nt], refs[nt:2 * nt]
    wg_ref, wu_ref, wd_ref, ys_ref = refs[2 * nt:]
    i = pl.program_id(0)

    @pl.when(real_ref[i] > 0)
    def _():
        x = jnp.concatenate([r[...] for r in xs_refs], axis=0)
        gates = jnp.concatenate([r[...] for r in gs_refs], axis=0)
        lane = lax.broadcasted_iota(jnp.int32, gates.shape, 1)
        base = N_EXPERT_GROUPS + grp_ref[i] * EXPERTS_PER_GROUP
        acc = jnp.zeros(ys_ref.shape, F32)
        for e in range(EXPERTS_PER_GROUP):
            ge = jnp.sum(jnp.where(lane == base + e, gates, 0.0), axis=-1, keepdims=True)
            gate_act = _dot(x, wg_ref[e])
            hdn = gate_act * _sigmoid(gate_act) * _dot(x, wu_ref[e])
            acc = acc + ge * _dot(hdn.astype(BF16), wd_ref[e])
        ys_ref[...] = acc.astype(BF16)

    @pl.when(real_ref[i] == 0)
    def _():
        ys_ref[...] = jnp.zeros_like(ys_ref)


def _moe_merge_kernel(slot_ref, h_ref, dest_ref, *refs):
    ys_refs, o_ref = refs[:-1], refs[-1]
    ys = jnp.concatenate([r[...] for r in ys_refs], axis=0)
    nb, nbp = h_ref.shape[0], ys.shape[0]
    slot = lax.broadcasted_iota(jnp.int32, (nb, nbp), 1).astype(F32)
    perm_t = (dest_ref[...] == slot).astype(BF16)
    o_ref[...] = h_ref[...] + _dot(perm_t, ys)


def _moe_merge_norm_kernel(slot_ref, h_ref, dest_ref, *refs):
    ys_refs, g_ref, o_ref = refs[:-2], refs[-2], refs[-1]
    ys = jnp.concatenate([r[...] for r in ys_refs], axis=0)
    nb, nbp = h_ref.shape[0], ys.shape[0]
    slot = lax.broadcasted_iota(jnp.int32, (nb, nbp), 1).astype(F32)
    perm_t = (dest_ref[...] == slot).astype(BF16)
    o_ref[...] = _rms(h_ref[...] + _dot(perm_t, ys)) * g_ref[...]


def _ffn_slots(n_tiles):
    return -(-(n_tiles + N_EXPERT_GROUPS * (FFN_TILES - 1)) // FFN_TILES) * FFN_TILES


def _ffn_schedule(counts, nblk, tiles_per_blk):
    i32 = jnp.int32
    ft, g_n = FFN_TILES, N_EXPERT_GROUPS
    tiles = (counts + FFN_ROWS - 1) // FFN_ROWS
    seg_end = jnp.cumsum(tiles, axis=1)
    tot = jnp.sum(tiles, axis=0)
    grp_pad = (tot + ft - 1) // ft * ft
    grp_end = jnp.cumsum(grp_pad)
    grp_start = grp_end - grp_pad
    before = jnp.cumsum(tiles, axis=0) - tiles
    n_tiles = nblk * tiles_per_blk
    n_slots = _ffn_slots(n_tiles)
    k = jnp.arange(tiles_per_blk, dtype=i32)[None, :]
    used = k < seg_end[:, -1:]
    pos = jnp.zeros((nblk, tiles_per_blk), i32)
    for g in range(g_n):
        in_seg = (k >= seg_end[:, g:g + 1] - tiles[:, g:g + 1]) & (k < seg_end[:, g:g + 1])
        pos = jnp.where(in_seg, grp_start[g] + before[:, g:g + 1] + k - (seg_end[:, g:g + 1] - tiles[:, g:g + 1]), pos)
    gap = grp_pad - tot
    gap_end = jnp.cumsum(gap)
    rank = (jnp.cumsum(jnp.logical_not(used).astype(i32).reshape(-1)) - 1).reshape(nblk, tiles_per_blk)
    free = grp_end[-1] + rank - gap_end[-1]
    for g in range(g_n - 1, -1, -1):
        free = jnp.where(rank < gap_end[g], grp_start[g] + tot[g] + rank - (gap_end[g] - gap[g]), free)
    tile_slot = jnp.where(used, pos, free).reshape(-1).astype(i32)
    s = jnp.arange(n_slots, dtype=i32)
    t = jnp.arange(n_tiles, dtype=i32)
    slot_tile = jnp.sum(jnp.where(tile_slot[None, :] == s[:, None], t[None, :], 0), axis=1).astype(i32)
    s0 = (jnp.arange(n_slots // ft, dtype=i32) * ft)[:, None]
    in_grp = (s0 >= grp_start[None, :]) & (s0 < grp_end[None, :])
    step_real = jnp.sum(jnp.where(in_grp, jnp.clip(tot[None, :] - (s0 - grp_start[None, :]), 0, ft), 0), axis=1)
    last_grp = jnp.max(jnp.where(tot > 0, jnp.arange(g_n, dtype=i32), 0))
    step_grp = jnp.where(step_real > 0, jnp.sum(jnp.where(in_grp, jnp.arange(g_n, dtype=i32)[None, :], 0), axis=1),
                         last_grp)
    return slot_tile, tile_slot, step_grp.astype(i32), step_real.astype(i32)


def _hier_moe_residual(h, norm_g, w_grp, b_grp, w_exp, b_exp, layer, w_gate, w_up, w_down, final_g=None,
                       pending=None):
    n, d = h.shape
    nb = MOE_BLOCK
    nblk = n // nb
    nbp = nb + N_EXPERT_GROUPS * FFN_ROWS
    tiles_per_blk = nbp // FFN_ROWS
    n_exp = N_EXPERT_GROUPS * EXPERTS_PER_GROUP
    w_router = jnp.pad(jnp.concatenate([w_grp, w_exp], axis=1), ((0, 0), (0, LANES - N_EXPERT_GROUPS - n_exp)))
    b_router = jnp.pad(jnp.concatenate([b_grp, b_exp]), (0, LANES - N_EXPERT_GROUPS - n_exp)).reshape(1, LANES)

    rows = pl.BlockSpec((nb, d), lambda i: (i, 0))
    route_in = [_const_spec((1, d)), _const_spec((d, LANES)), _const_spec((1, LANES))]
    route_args = (norm_g.reshape(1, d), w_router.astype(BF16), b_router)
    route_out = [pl.BlockSpec((nbp, d), lambda i: (i, 0)), pl.BlockSpec((nbp, LANES), lambda i: (i, 0)),
                 pl.BlockSpec((nb, 1), lambda i: (i, 0)), pl.BlockSpec((1, 1, LANES), lambda i: (i, 0, 0))]
    route_shape = [jax.ShapeDtypeStruct((nblk * nbp, d), BF16), jax.ShapeDtypeStruct((nblk * nbp, LANES), F32),
                   jax.ShapeDtypeStruct((n, 1), F32), jax.ShapeDtypeStruct((nblk, 1, LANES), F32)]
    if pending is None:
        xs, gs, dest, counts = pl.pallas_call(
            _moe_route_kernel, grid=(nblk,), in_specs=[rows] + route_in, out_specs=route_out, out_shape=route_shape,
            compiler_params=_cparams("arbitrary"), name="moe_route",
        )(h, *route_args)
    else:
        a, w = pending
        h, xs, gs, dest, counts = pl.pallas_call(
            _moe_route_proj_kernel, grid=(nblk,),
            in_specs=[pl.BlockSpec((nb, a.shape[1]), lambda i: (i, 0)), _const_spec(w.shape), rows] + route_in,
            out_specs=[rows] + route_out, out_shape=[jax.ShapeDtypeStruct((n, d), F32)] + route_shape,
            compiler_params=_cparams("arbitrary"), name="moe_route_proj",
        )(a, w.astype(BF16), h, *route_args)

    n_tiles = nblk * tiles_per_blk
    n_slots = _ffn_slots(n_tiles)
    slot_tile, tile_slot, step_grp, step_real = _ffn_schedule(
        counts[:, 0, :N_EXPERT_GROUPS].astype(jnp.int32), nblk, tiles_per_blk)

    def tile_spec(width, j):
        return pl.BlockSpec((FFN_ROWS, width), lambda i, tile, grp, real: (tile[i * FFN_TILES + j], 0))

    def wspec(shape):
        return pl.BlockSpec((None, EXPERTS_PER_GROUP) + shape, lambda i, tile, grp, real: (layer, grp[i], 0, 0))

    f = w_gate.shape[-1]
    ys = pl.pallas_call(
        _moe_ffn_kernel,
        grid_spec=pltpu.PrefetchScalarGridSpec(
            num_scalar_prefetch=3,
            grid=(n_slots // FFN_TILES,),
            in_specs=[tile_spec(d, j) for j in range(FFN_TILES)] + [tile_spec(LANES, j) for j in range(FFN_TILES)]
            + [wspec((d, f)), wspec((d, f)), wspec((f, d))],
            out_specs=pl.BlockSpec((FFN_TILES * FFN_ROWS, d), lambda i, tile, grp, real: (i, 0)),
        ),
        out_shape=jax.ShapeDtypeStruct((n_slots * FFN_ROWS, d), BF16),
        compiler_params=_cparams("arbitrary"),
        name="moe_ffn",
    )(slot_tile, step_grp, step_real, *([xs] * FFN_TILES), *([gs] * FFN_TILES),
      w_gate, w_up, w_down)

    blk = pl.BlockSpec((nb, d), lambda i, slot: (i, 0))
    in_specs = [blk, pl.BlockSpec((nb, 1), lambda i, slot: (i, 0))]
    in_specs += [pl.BlockSpec((FFN_ROWS, d), functools.partial(lambda j, i, slot: (slot[i * tiles_per_blk + j], 0), j))
                 for j in range(tiles_per_blk)]
    args = [h, dest] + [ys] * tiles_per_blk
    body = _moe_merge_kernel
    if final_g is not None:
        in_specs.append(pl.BlockSpec((1, d), lambda i, slot: (0, 0)))
        args.append(final_g.reshape(1, d))
        body = _moe_merge_norm_kernel
    return pl.pallas_call(
        body,
        grid_spec=pltpu.PrefetchScalarGridSpec(num_scalar_prefetch=1, grid=(nblk,), in_specs=in_specs, out_specs=blk),
        out_shape=jax.ShapeDtypeStruct((n, d), F32),
        compiler_params=_cparams("arbitrary"),
        name="moe_merge",
    )(tile_slot, *args)


GROUP_WIDTH = DIL_HEADS * DIL_HEAD_DIM
N_STRIDED = 3 * sum(1 for _, dil in DIL_PATTERNS if dil > 1)


def _qkv_kernel(h_ref, gq_ref, gkv_ref, wq_ref, wk_ref, wv_ref, *refs):
    outs, scratch = refs[:-1], refs[-1]
    n = _rms(h_ref[...])
    xq = (n * gq_ref[...]).astype(BF16)
    xkv = (n * gkv_ref[...]).astype(BF16)
    gw = GROUP_WIDTH
    slot = 0
    full = (_dot(xq, wq_ref[...]), _dot(xkv, wk_ref[...]), _dot(xkv, wv_ref[...]))
    for gi, (_, dil) in enumerate(DIL_PATTERNS):
        cols = slice(gi * gw, (gi + 1) * gw)
        for t in range(3):
            res = full[t][:, cols]
            out_ref = outs[3 * gi + t]
            if dil == 1:
                out_ref[...] = res.astype(BF16)
                continue
            sub = res.shape[0] // dil
            for c in range(gw // LANES):
                scratch[slot, c] = res[:, c * LANES:(c + 1) * LANES]
            for r in range(dil):
                for c in range(gw // LANES):
                    lo = r * gw + c * LANES
                    out_ref[:, lo:lo + LANES] = scratch[slot, c, pl.ds(r, sub, stride=dil), :].astype(BF16)
            slot += 1


def _qkv_proj(h, g_q, g_kv, w_q, w_kv):
    n, d = h.shape
    qw = w_q.shape[1]
    gw = GROUP_WIDTH
    rows = pl.BlockSpec((PROJ_ROWS, d), lambda i: (i, 0))
    out_specs, out_shape = [], []
    for _, dil in DIL_PATTERNS:
        for _ in range(3):
            out_specs.append(pl.BlockSpec((PROJ_ROWS // dil, dil * gw), lambda i: (i, 0)))
            out_shape.append(jax.ShapeDtypeStruct((n // dil, dil * gw), BF16))
    return pl.pallas_call(
        _qkv_kernel,
        grid=(n // PROJ_ROWS,),
        in_specs=[rows, _const_spec((1, d)), _const_spec((1, d)), _const_spec((d, qw)),
                  pl.BlockSpec((d, qw), lambda i: (0, 0)), pl.BlockSpec((d, qw), lambda i: (0, 1))],
        out_specs=out_specs,
        out_shape=out_shape,
        scratch_shapes=[pltpu.VMEM((N_STRIDED, gw // LANES, PROJ_ROWS, LANES), F32)],
        compiler_params=_cparams("arbitrary"),
        name="qkv_proj",
    )(h, g_q.reshape(1, d), g_kv.reshape(1, d), w_q, w_kv, w_kv)


def _attn_kernel(q_ref, kc_ref, kp_ref, vc_ref, vp_ref, o_ref, lse_ref):
    n = pl.program_id(2)
    blk = ATT_BLOCK
    pw = 2 * DIL_HEAD_DIM
    ri = lax.broadcasted_iota(jnp.int32, (blk, blk), 0)
    ci = lax.broadcasted_iota(jnp.int32, (blk, blk), 1)
    lowest = jnp.finfo(F32).min
    cap_prev = jnp.where(ci >= ri, jnp.inf, lowest).astype(F32)
    cap_first = jnp.where(n > 0, cap_prev, lowest)
    cap_own = jnp.where(ci <= ri, jnp.inf, lowest).astype(F32)
    head0 = lax.broadcasted_iota(jnp.int32, (blk, pw), 1) < DIL_HEAD_DIM
    zero = jnp.zeros((blk, pw), BF16)
    lane_t = lax.broadcasted_iota(jnp.int32, (blk, LANES), 1)

    def stack(x):
        return jnp.concatenate([jnp.where(head0, x, zero), jnp.where(head0, zero, x)], axis=0)

    npairs = DIL_HEADS // 2
    streams = [(i, p) for i in range(q_ref.shape[1] // blk) for p in range(q_ref.shape[2] // pw)]

    def rows(i):
        return slice(i * blk, (i + 1) * blk)

    def lanes(p):
        return slice(p * pw, (p + 1) * pw)

    stacked = {}

    def kv_blocks(cur_ref, prev_ref, i, p):
        def one(w):
            name = (id(cur_ref), w, p)
            if name not in stacked:
                stacked[name] = stack(prev_ref[0, :, lanes(p)] if w < 0 else cur_ref[0, rows(w), lanes(p)])
            return stacked[name]
        return jnp.concatenate([one(i - 1), one(i)], axis=0)

    s = []
    for i, p in streams:
        raw = _dot_nt(q_ref[0, rows(i), lanes(p)], kv_blocks(kc_ref, kp_ref, i, p))
        cap = cap_first if i == 0 else cap_prev
        s.append([jnp.minimum(raw[:, :blk], cap), jnp.minimum(raw[:, blk:2 * blk], cap),
                  jnp.minimum(raw[:, 2 * blk:3 * blk], cap_own), jnp.minimum(raw[:, 3 * blk:], cap_own)])
    m_a = [jnp.max(jnp.maximum(x[0], x[2]), axis=-1, keepdims=True) for x in s]
    m_b = [jnp.max(jnp.maximum(x[1], x[3]), axis=-1, keepdims=True) for x in s]
    e = [[jnp.exp2(x[0] - ma), jnp.exp2(x[1] - mb), jnp.exp2(x[2] - ma), jnp.exp2(x[3] - mb)]
         for x, ma, mb in zip(s, m_a, m_b)]
    pb = [jnp.concatenate([y.astype(BF16) for y in x], axis=1) for x in e]
    blk_head = (lax.broadcasted_iota(jnp.int32, (4 * blk, LANES), 0) // blk) % 2
    sel_lane = lax.broadcasted_iota(jnp.int32, (4 * blk, LANES), 1) - DIL_HEADS
    sel = [(sel_lane == 2 * p + blk_head).astype(BF16) for p in range(DIL_HEADS // 2)]
    pv = [_dot(pb[idx], jnp.concatenate([kv_blocks(vc_ref, vp_ref, i, p), sel[p % npairs]], axis=1))
          for idx, (i, p) in enumerate(streams)]
    tiles = {}
    for idx, (i, p) in enumerate(streams):
        o_ref[0, rows(i), lanes(p)] = pv[idx][:, :pw].astype(BF16)
        where, pp = (i, p // npairs), p % npairs
        tile = tiles.get(where, jnp.zeros((blk, LANES), F32)) + pv[idx][:, pw:]
        tiles[where] = jnp.where(lane_t == 2 * pp, m_a[idx], jnp.where(lane_t == 2 * pp + 1, m_b[idx], tile))
    for (i, res), tile in tiles.items():
        lse_ref[0, rows(i), res * LANES:(res + 1) * LANES] = tile


def _attn_group(q, k, v, gi, dilation, bsz, seq):
    gw = GROUP_WIDTH
    sub = seq // dilation
    windows = min(ATT_WINDOWS, sub // ATT_BLOCK)
    step_rows = windows * ATT_BLOCK
    assert sub % step_rows == 0
    res = max(1, ATT_WINDOWS // windows)
    if dilation % res:
        res = 1
    view = (bsz, sub, dilation * gw)
    qv, kv_, vv = q.reshape(view), k.reshape(view), v.reshape(view)
    cur = pl.BlockSpec((1, step_rows, res * gw), lambda b, r, n: (b, n, r))
    prev = pl.BlockSpec((1, ATT_BLOCK, res * gw), lambda b, r, n: (b, jnp.maximum(n * windows - 1, 0), r))
    lse_spec = pl.BlockSpec((1, step_rows, res * LANES), lambda b, r, n: (b, n, r))
    o, lse = pl.pallas_call(
        _attn_kernel,
        grid=(bsz, dilation // res, sub // step_rows),
        in_specs=[cur, cur, prev, cur, prev],
        out_specs=[cur, lse_spec],
        out_shape=[jax.ShapeDtypeStruct(view, BF16),
                   jax.ShapeDtypeStruct((bsz, sub, dilation * LANES), F32)],
        compiler_params=_cparams("arbitrary", "arbitrary", "arbitrary"),
        name=f"attn_group{gi}",
    )(qv, kv_, kv_, vv, vv)
    return o.reshape(bsz * sub, dilation * gw), lse.reshape(bsz * sub, dilation * LANES)


def _attn_merge_kernel(o0_ref, o1_ref, o2_ref, l0_ref, l1_ref, l2_ref, w_ref, res_ref, out_ref, *scratch):
    rows = res_ref.shape[0]
    gw = GROUP_WIDTH
    outs, lses = [], []
    slot = 0
    for (_, dil), o_ref, l_ref in zip(DIL_PATTERNS, (o0_ref, o1_ref, o2_ref), (l0_ref, l1_ref, l2_ref)):
        if dil == 1:
            outs.append(o_ref[...].astype(F32))
            lses.append(l_ref[...])
            continue
        o_scr, l_scr = scratch[2 * slot], scratch[2 * slot + 1]
        sub = rows // dil
        for r in range(dil):
            for c in range(gw // LANES):
                lo = r * gw + c * LANES
                o_scr[c, pl.ds(r, sub, stride=dil), :] = o_ref[:, lo:lo + LANES].astype(F32)
            l_scr[pl.ds(r, sub, stride=dil), :] = l_ref[:, r * LANES:(r + 1) * LANES]
        outs.append(jnp.concatenate([o_scr[c] for c in range(gw // LANES)], axis=1))
        lses.append(l_scr[...])
        slot += 1
    m = jnp.maximum(jnp.maximum(lses[0], lses[1]), lses[2])
    e = [jnp.exp2(l - m) for l in lses]
    sums = [pltpu.roll(l, LANES - DIL_HEADS, 1) for l in lses]
    denom = e[0] * sums[0] + e[1] * sums[1] + e[2] * sums[2]
    is_head = lax.broadcasted_iota(jnp.int32, (rows, LANES), 1) < DIL_HEADS
    expand = (lax.broadcasted_iota(jnp.int32, (LANES, gw), 1) // DIL_HEAD_DIM
              == lax.broadcasted_iota(jnp.int32, (LANES, gw), 0)).astype(BF16)
    alpha = jnp.concatenate([jnp.where(is_head, x / denom, 0.0) for x in e], axis=0)
    hi = alpha.astype(BF16)
    lo = (alpha - hi.astype(F32)).astype(BF16)
    wide = _dot(jnp.concatenate([hi, lo], axis=1), jnp.concatenate([expand, expand], axis=0))
    mix = jnp.zeros((rows, gw), F32)
    for g in range(len(outs)):
        mix = mix + wide[g * rows:(g + 1) * rows] * outs[g]
    out_ref[...] = res_ref[...] + _dot(mix.astype(BF16), w_ref[...])


def _attn_merge(outs, lses, w_o, res):
    n, d = res.shape
    gw = GROUP_WIDTH
    rows = MERGE_ROWS
    o_specs = [pl.BlockSpec((rows // dil, dil * gw), lambda i: (i, 0)) for _, dil in DIL_PATTERNS]
    l_specs = [pl.BlockSpec((rows // dil, dil * LANES), lambda i: (i, 0)) for _, dil in DIL_PATTERNS]
    full = pl.BlockSpec((rows, d), lambda i: (i, 0))
    scratch = []
    for _, dil in DIL_PATTERNS:
        if dil > 1:
            scratch += [pltpu.VMEM((gw // LANES, rows, LANES), F32), pltpu.VMEM((rows, LANES), F32)]
    return pl.pallas_call(
        _attn_merge_kernel,
        grid=(n // rows,),
        in_specs=o_specs + l_specs + [_const_spec(w_o.shape), full],
        out_specs=full,
        out_shape=jax.ShapeDtypeStruct((n, d), F32),
        scratch_shapes=scratch,
        compiler_params=_cparams("arbitrary"),
        name="attn_merge",
    )(*outs, *lses, w_o.astype(BF16), res)


def kernel(x, norm_mix_g, norm_ffn_g, rwkv_mu, rwkv_w_r, rwkv_w_k, rwkv_w_v, rwkv_w0, rwkv_w1, rwkv_w2,
           rwkv_a0, rwkv_a1, rwkv_a2, rwkv_g1, rwkv_g2, rwkv_k_k, rwkv_k_a, rwkv_r_k, rwkv_ln_w, rwkv_ln_b,
           rwkv_w_o, kv_norm_g, w_kv, attn_w_q, attn_w_o, moe_w_grp, moe_b_grp, moe_w_exp, moe_b_exp,
           moe_w_gate, moe_w_up, moe_w_down, final_norm_g):
    bsz, seq, d = x.shape
    n = bsz * seq
    depth = norm_mix_g.shape[0]
    n_rwkv = rwkv_mu.shape[0]
    h = x.reshape(n, d)
    later_f32 = (moe_w_gate, moe_w_up, moe_w_down, attn_w_q, w_kv)
    later_scale = (1.0, 1.0, 1.0, QK_SCALE, 1.0)
    later_w = None
    q = k_sh = v_sh = None
    for layer in range(depth):
        pending = None
        if layer < n_rwkv:
            i = layer
            r, k, v, a, lw, g = _rwkv_pre(h.reshape(bsz, seq, d), norm_mix_g[layer], rwkv_mu[i], rwkv_w0[i], rwkv_a0[i],
                                          rwkv_w_r[i], rwkv_w_k[i], rwkv_w_v[i], rwkv_w1[i], rwkv_w2[i],
                                          rwkv_a1[i], rwkv_a2[i], rwkv_g1[i], rwkv_g2[i])
            casts = () if later_w is not None else tuple(w.reshape(-1, w.shape[-1]) for w in later_f32)
            y, cast = _rwkv_rec(r, k, v, a, lw, g, rwkv_k_k[i], rwkv_k_a[i], rwkv_r_k[i], rwkv_ln_w[i], rwkv_ln_b[i],
                                casts=casts, cast_scales=later_scale if casts else ())
            if casts:
                later_w = tuple(c.reshape(w.shape) for c, w in zip(cast, later_f32))
            pending = (y.reshape(n, d), rwkv_w_o[i])
        else:
            i = layer - n_rwkv
            if later_w is None:
                later_w = tuple((w * s).astype(BF16) for w, s in zip(later_f32, later_scale))
            qkv = _qkv_proj(h, norm_mix_g[layer], kv_norm_g, later_w[3][i], later_w[4])
            q = qkv[0::3]
            if i == 0:
                k_sh, v_sh = qkv[1::3], qkv[2::3]
            outs, lses = [], []
            for gi, (window, dilation) in enumerate(DIL_PATTERNS):
                assert window // dilation == ATT_BLOCK and seq % window == 0
                o, lse = _attn_group(q[gi], k_sh[gi], v_sh[gi], gi, dilation, bsz, seq)
                outs.append(o)
                lses.append(lse)
            h = _attn_merge(outs, lses, attn_w_o[i], h)
        last = layer == depth - 1
        if later_w is None:
            later_w = tuple((w * s).astype(BF16) for w, s in zip(later_f32, later_scale))
        h = _hier_moe_residual(h, norm_ffn_g[layer], moe_w_grp[layer], moe_b_grp[layer], moe_w_exp[layer],
                               moe_b_exp[layer], layer, *later_w[:3], final_g=final_norm_g if last else None,
                               pending=pending)
    return h.reshape(bsz, seq, d)
```

```python
import functools
import math

import jax
import jax.numpy as jnp
from jax import lax
from jax.experimental import pallas as pl
from jax.experimental.pallas import tpu as pltpu

F32 = jnp.float32
BF16 = jnp.bfloat16

NORM_EPS = 1e-6
LOG2_E = math.log2(math.e)
RWKV_HEAD = 64
GN_EPS = RWKV_HEAD * 1e-5
DIL_PATTERNS = ((128, 1), (512, 4), (2048, 16))
DIL_HEADS = 8
DIL_HEAD_DIM = 64
QK_SCALE = LOG2_E / math.sqrt(DIL_HEAD_DIM)
N_EXPERT_GROUPS = 4
EXPERTS_PER_GROUP = 4
LANES = 128
SUBLANES = 8
VMEM_LIMIT = 56 * 1024 * 1024

CHUNK = 64
REC_ROWS = 2048
REC_PAIRS = 2
PRE_ROWS = 512
PROJ_ROWS = 512
MERGE_ROWS = 1024
MOE_BLOCK = 1024
FFN_ROWS = 64
FFN_TILES = 8
ATT_BLOCK = 128
ATT_WINDOWS = 16


def _cparams(*sem):
    return pltpu.CompilerParams(dimension_semantics=sem, vmem_limit_bytes=VMEM_LIMIT)


def _dot(a, b):
    return jnp.dot(a, b, preferred_element_type=F32)


def _dot_nt(a, b):
    return lax.dot_general(a, b, (((1,), (1,)), ((), ())), preferred_element_type=F32)


def _dot_tn(a, b):
    return lax.dot_general(a, b, (((0,), (0,)), ((), ())), preferred_element_type=F32)


def _rms(x):
    return x * lax.rsqrt(jnp.mean(x * x, axis=-1, keepdims=True) + NORM_EPS)


def _sigmoid(z):
    return 1.0 / (1.0 + jnp.exp(-z))


def _const_spec(shape):
    nd = len(shape)
    return pl.BlockSpec(shape, lambda *_: (0,) * nd)


def _rwkv_pre_kernel(x_ref, gmix_ref, mu_ref, w0_ref, a0_ref, wr_ref, wk_ref, wv_ref,
                     w1_ref, w2_ref, a1_ref, a2_ref, g1_ref, g2_ref,
                     r_ref, k_ref, v_ref, a_ref, lw_ref, g_ref, prev_ref):
    @pl.when(pl.program_id(1) == 0)
    def _():
        prev_ref[...] = jnp.zeros_like(prev_ref)

    x = x_ref[0]
    rows = x.shape[0]
    xn = _rms(x) * gmix_ref[...]
    row = lax.broadcasted_iota(jnp.int32, xn.shape, 0)
    shifted = jnp.where(row == 0, prev_ref[SUBLANES - 1:, :], pltpu.roll(xn, 1, 0))
    prev_ref[...] = xn[rows - SUBLANES:, :]
    xx = shifted - xn

    def mix(i):
        return (xn + xx * mu_ref[i:i + 1, :]).astype(BF16)

    r_ref[0] = _dot(mix(0), wr_ref[...]).astype(BF16)
    k_ref[0] = _dot(mix(2), wk_ref[...]).astype(BF16)
    v_ref[0] = _dot(mix(3), wv_ref[...]).astype(BF16)
    u = w0_ref[...] + _dot(jnp.tanh(_dot(mix(1), w1_ref[...])).astype(BF16), w2_ref[...])
    w_log = -(jnp.maximum(-u, 0.0) + jnp.log(1.0 + jnp.exp(-jnp.abs(u)))) - 0.5
    lw_ref[0] = -jnp.exp(w_log) * LOG2_E
    a_ref[0] = _sigmoid(a0_ref[...] + _dot(_dot(mix(4), a1_ref[...]).astype(BF16), a2_ref[...])).astype(BF16)
    g_ref[0] = _dot(_sigmoid(_dot(mix(5), g1_ref[...])).astype(BF16), g2_ref[...]).astype(BF16)


def _pad_cols(w, n):
    return jnp.pad(w, ((0, 0), (0, n - w.shape[1])))


def _pad_rows(w, n):
    return jnp.pad(w, ((0, n - w.shape[0]), (0, 0)))


def _rwkv_pre(x, gmix, mu, w0, a0, w_r, w_k, w_v, w1, w2, a1, a2, g1, g2):
    bsz, seq, d = x.shape
    lw = -(-w1.shape[1] // LANES) * LANES
    la = -(-a1.shape[1] // LANES) * LANES
    lg = -(-g1.shape[1] // LANES) * LANES
    mu8 = jnp.pad(mu, ((0, SUBLANES - mu.shape[0]), (0, 0)))
    args = (x, gmix.reshape(1, d), mu8, w0.reshape(1, d), a0.reshape(1, d),
            w_r.astype(BF16), w_k.astype(BF16), w_v.astype(BF16),
            _pad_cols(w1, lw).astype(BF16), _pad_rows(w2, lw).astype(BF16),
            _pad_cols(a1, la).astype(BF16), _pad_rows(a2, la).astype(BF16),
            _pad_cols(g1, lg).astype(BF16), _pad_rows(g2, lg).astype(BF16))
    tile = pl.BlockSpec((1, PRE_ROWS, d), lambda b, s: (b, s, 0))
    in_specs = [tile] + [_const_spec(a.shape) for a in args[1:]]
    out_dt = (BF16, BF16, BF16, BF16, F32, BF16)
    return pl.pallas_call(
        _rwkv_pre_kernel,
        grid=(bsz, seq // PRE_ROWS),
        in_specs=in_specs,
        out_specs=[tile] * 6,
        out_shape=[jax.ShapeDtypeStruct((bsz, seq, d), t) for t in out_dt],
        scratch_shapes=[pltpu.VMEM((SUBLANES, d), F32)],
        compiler_params=_cparams("arbitrary", "arbitrary"),
        name="rwkv_pre",
    )(*args)


def _rwkv_rec_kernel(cast_scales, r_ref, k_ref, v_ref, a_ref, lw_ref, g_ref, kk_ref, ka_ref, rk_ref, lnw_ref, lnb_ref,
                     *refs):
    n_casts = len(cast_scales)
    cast_in, o_ref, cast_out, state_ref = refs[:n_casts], refs[n_casts], refs[n_casts + 1:-1], refs[-1]
    for src, dst, scale in zip(cast_in, cast_out, cast_scales):
        dst[...] = (src[...] if scale == 1.0 else src[...] * scale).astype(BF16)

    @pl.when(pl.program_id(2) == 0)
    def _():
        state_ref[...] = jnp.zeros_like(state_ref)

    c = CHUNK
    hd = RWKV_HEAD
    w = 2 * hd
    assert c == hd
    n = r_ref.shape[1] // c
    lane = lax.broadcasted_iota(jnp.int32, (c, w), 1)
    row = lax.broadcasted_iota(jnp.int32, (c, w), 0)
    head0 = lane < hd
    col = jnp.where(head0, lane, lane - hd)
    strict = row > col
    incl = row >= col
    eye = (row == col).astype(F32)
    tri = (lax.broadcasted_iota(jnp.int32, (c, c), 0) >= lax.broadcasted_iota(jnp.int32, (c, c), 1)).astype(BF16)
    pairs = r_ref.shape[2] // w

    def head_sum(x):
        s0 = jnp.sum(jnp.where(head0, x, 0.0), axis=-1, keepdims=True)
        s1 = jnp.sum(jnp.where(head0, 0.0, x), axis=-1, keepdims=True)
        return jnp.where(head0, s0, s1)

    def stack(x):
        return jnp.concatenate([jnp.where(head0, x, 0.0), jnp.where(head0, 0.0, x)], axis=0)

    def cumsum(x):
        hi = x.astype(BF16)
        lo = (x - hi.astype(F32)).astype(BF16)
        both = _dot(tri, jnp.concatenate([hi, lo], axis=1))
        return both[:, :w] + both[:, w:]

    zeros = jnp.zeros((c, w), BF16)
    carried = [state_ref[p] for p in range(pairs)]

    def chunk(j, p):
        rs = slice(j * c, (j + 1) * c)
        ls = slice(p * w, (p + 1) * w)
        kkp, kap, rkp, lnw, lnb = kk_ref[:, ls], ka_ref[:, ls], rk_ref[:, ls], lnw_ref[:, ls], lnb_ref[:, ls]
        r_ = r_ref[0, rs, ls].astype(F32)
        k_ = k_ref[0, rs, ls].astype(F32)
        v_ = v_ref[0, rs, ls].astype(F32)
        a_ = a_ref[0, rs, ls].astype(F32)
        lw_ = lw_ref[0, rs, ls]
        cum = cumsum(lw_)
        yield
        kk = k_ * kkp
        kk = kk * lax.rsqrt(jnp.maximum(head_sum(kk * kk), 1e-24))
        k2 = k_ * (1.0 + (a_ - 1.0) * kap)
        p_ = -(kk * a_)
        tot = cum[c - 1:c, :]
        e_neg = jnp.exp2(-cum)
        e_tot = jnp.exp2(tot - cum)
        rt = r_ * jnp.exp2(cum)
        qt = kk * jnp.exp2(cum - lw_)
        vb = v_.astype(BF16)
        amat = _dot_nt(jnp.concatenate([qt, rt], axis=0).astype(BF16),
                       jnp.concatenate([stack((k2 * e_neg).astype(BF16)), stack((p_ * e_neg).astype(BF16))], axis=0))
        yield
        a_qp = jnp.where(strict, amat[:c, w:], 0.0)
        a_rp = jnp.where(incl, amat[c:, w:], 0.0).astype(BF16)
        a_k = jnp.concatenate([jnp.where(strict, amat[:c, :w], 0.0),
                               jnp.where(incl, amat[c:, :w], 0.0)], axis=0).astype(BF16)
        av = _dot(a_k, stack(vb))
        inv = eye + a_qp
        power = a_qp.astype(BF16)
        power = _dot(power, stack(power))
        yield
        steps = int(math.log2(c)) - 1
        for i in range(steps):
            pb = power.astype(BF16)
            if i + 1 < steps:
                both = _dot(jnp.concatenate([inv.astype(BF16), pb], axis=0), stack(pb))
                inv = inv + both[:c]
                power = both[c:]
            else:
                inv = inv + _dot(inv.astype(BF16), stack(pb))
            yield
        wu = _dot(inv.astype(BF16),
                  jnp.concatenate([stack(qt.astype(BF16)), stack(av[:c].astype(BF16))], axis=1))
        wub = wu.astype(BF16)
        yield
        ry = _dot(a_rp, jnp.concatenate([stack(wub[:, :w]), stack(wub[:, w:])], axis=1))
        mg = _dot_tn(jnp.concatenate([p_ * e_tot, k2 * e_tot], axis=0).astype(BF16),
                     jnp.concatenate([wub, jnp.concatenate([zeros, vb], axis=1)], axis=0))
        yield
        rw = (rt + ry[:, :w]).astype(BF16)
        m_w = jnp.where(head0, mg[:c, :w], mg[c:, :w]).astype(BF16)
        g_w = jnp.where(head0, mg[:c, w:], mg[c:, w:])
        gam = head_sum(eye * jnp.exp2(tot))
        state = carried[p]
        both = _dot(jnp.concatenate([rw, m_w], axis=0), stack(state.astype(BF16)))
        y = both[:c] + av[c:] + ry[:, w:]
        carried[p] = gam * state + both[c:] + g_w
        yield
        yc = y - head_sum(y) * (1.0 / hd)
        var = head_sum(yc * yc) * (1.0 / hd)
        yn = yc * lax.rsqrt(var + GN_EPS) * lnw + lnb
        bonus = head_sum(r_ * k2 * rkp) * v_
        o_ref[0, rs, ls] = ((yn + bonus) * g_ref[0, rs, ls].astype(F32)).astype(BF16)

    live, started = [], 0
    while started < n or live:
        if started < n:
            live.extend(chunk(started, p) for p in range(pairs))
            started += 1
        for gen in list(live):
            if next(gen, "done") == "done":
                live.remove(gen)
    for p in range(pairs):
        state_ref[p] = carried[p]


def _rwkv_rec(r, k, v, a, lw, g, k_k, k_a, r_k, ln_w, ln_b, casts=(), cast_scales=()):
    bsz, seq, d = r.shape
    pair_w = 2 * RWKV_HEAD
    hw = REC_PAIRS * pair_w
    grid = (bsz, d // hw, seq // REC_ROWS)
    steps = grid[0] * grid[1] * grid[2]
    tile = pl.BlockSpec((1, REC_ROWS, hw), lambda b, h, s: (b, s, h))
    par = pl.BlockSpec((1, hw), lambda b, h, s: (0, h))
    params = [p.reshape(1, d).astype(F32) for p in (k_k, k_a, r_k, ln_w, ln_b)]
    cast_specs = []
    for w in casts:
        assert w.shape[0] % (steps * 2 * SUBLANES) == 0
        cast_specs.append(pl.BlockSpec((w.shape[0] // steps, w.shape[1]),
                                       lambda b, h, s: ((b * grid[1] + h) * grid[2] + s, 0)))
    out = pl.pallas_call(
        functools.partial(_rwkv_rec_kernel, tuple(cast_scales)),
        grid=grid,
        in_specs=[tile] * 6 + [par] * 5 + cast_specs,
        out_specs=[tile] + cast_specs,
        out_shape=[jax.ShapeDtypeStruct((bsz, seq, d), BF16)] + [jax.ShapeDtypeStruct(w.shape, BF16) for w in casts],
        scratch_shapes=[pltpu.VMEM((REC_PAIRS, CHUNK, pair_w), F32)],
        compiler_params=_cparams("arbitrary", "arbitrary", "arbitrary"),
        name="rwkv_rec",
    )(r, k, v, a, lw, g, *params, *casts)
    return out[0], out[1:]


def _moe_route_kernel(h_ref, g_ref, wr_ref, br_ref, xs_ref, gs_ref, dest_ref, cnt_ref):
    _route_block(h_ref[...], g_ref, wr_ref, br_ref, xs_ref, gs_ref, dest_ref, cnt_ref)


def _moe_route_proj_kernel(a_ref, w_ref, res_ref, g_ref, wr_ref, br_ref, h_ref, xs_ref, gs_ref, dest_ref, cnt_ref):
    h = res_ref[...] + _dot(a_ref[...], w_ref[...])
    h_ref[...] = h
    _route_block(h, g_ref, wr_ref, br_ref, xs_ref, gs_ref, dest_ref, cnt_ref)


def _route_block(h, g_ref, wr_ref, br_ref, xs_ref, gs_ref, dest_ref, cnt_ref):
    nb = h.shape[0]
    nbp = xs_ref.shape[0]
    tb = (_rms(h) * g_ref[...]).astype(BF16)
    logits = _dot(tb, wr_ref[...]) + br_ref[...]
    lane = lax.broadcasted_iota(jnp.int32, logits.shape, 1).astype(F32)
    neg = jnp.float32(-jnp.inf)
    big = jnp.float32(LANES)
    is_grp = lane < N_EXPERT_GROUPS
    gl = jnp.where(is_grp, logits, neg)
    gmax = jnp.max(gl, axis=-1, keepdims=True)
    grp = jnp.min(jnp.where(gl == gmax, lane, big), axis=-1, keepdims=True)
    p_grp = 1.0 / jnp.sum(jnp.where(is_grp, jnp.exp(gl - gmax), 0.0), axis=-1, keepdims=True)
    e_lo = N_EXPERT_GROUPS + grp * EXPERTS_PER_GROUP
    in_grp = (lane >= e_lo) & (lane < e_lo + EXPERTS_PER_GROUP)
    el = jnp.where(in_grp, logits, neg)
    v1 = jnp.max(el, axis=-1, keepdims=True)
    i1 = jnp.min(jnp.where(el == v1, lane, big), axis=-1, keepdims=True)
    el2 = jnp.where(lane == i1, neg, el)
    v2 = jnp.max(el2, axis=-1, keepdims=True)
    i2 = jnp.min(jnp.where(el2 == v2, lane, big), axis=-1, keepdims=True)
    e2 = jnp.exp(v2 - v1)
    w1 = 1.0 / (1.0 + e2)
    w2 = e2 / (1.0 + e2)
    gates = jnp.where(lane == i1, w1, jnp.where(lane == i2, w2, 0.0)) * p_grp

    onehot = (lane == grp).astype(BF16)
    ri = lax.broadcasted_iota(jnp.int32, (nb, nb), 0)
    ci = lax.broadcasted_iota(jnp.int32, (nb, nb), 1)
    rank = _dot((ri > ci).astype(BF16), onehot)
    counts = jnp.sum(onehot.astype(F32), axis=0, keepdims=True)
    padded = jnp.ceil(counts / FFN_ROWS) * FFN_ROWS
    lane1 = lax.broadcasted_iota(jnp.int32, (1, LANES), 1)
    offs = jnp.zeros((1, LANES), F32)
    for gidx in range(1, N_EXPERT_GROUPS):
        prev = jnp.sum(jnp.where(lane1 < gidx, padded, 0.0), axis=-1, keepdims=True)
        offs = jnp.where(lane1 == gidx, prev, offs)
    dest = jnp.sum(onehot.astype(F32) * (rank + offs), axis=-1, keepdims=True)
    slot = lax.broadcasted_iota(jnp.int32, (nb, nbp), 1).astype(F32)
    perm_t = (dest == slot).astype(BF16)
    xs_ref[...] = _dot_tn(perm_t, tb).astype(BF16)
    g_hi = gates.astype(BF16)
    g_lo = (gates - g_hi.astype(F32)).astype(BF16)
    moved = _dot_tn(perm_t, jnp.concatenate([g_hi, g_lo], axis=1))
    gs_ref[...] = moved[:, :LANES] + moved[:, LANES:]
    dest_ref[...] = dest
    cnt_ref[0] = jnp.where(lane1 < N_EXPERT_GROUPS, counts, 0.0)


def _moe_ffn_kernel(tile_ref, grp_ref, real_ref, *refs):
    nt = FFN_TILES
    xs_refs, gs_refs = refs[:nt], refs[nt:2 * nt]
    wg_ref, wu_ref, wd_ref, ys_ref = refs[2 * nt:]
    i = pl.program_id(0)

    @pl.when(real_ref[i] > 0)
    def _():
        x = jnp.concatenate([r[...] for r in xs_refs], axis=0)
        gates = jnp.concatenate([r[...] for r in gs_refs], axis=0)
        lane = lax.broadcasted_iota(jnp.int32, gates.shape, 1)
        base = N_EXPERT_GROUPS + grp_ref[i] * EXPERTS_PER_GROUP
        acc = jnp.zeros(ys_ref.shape, F32)
        for e in range(EXPERTS_PER_GROUP):
            ge = jnp.sum(jnp.where(lane == base + e, gates, 0.0), axis=-1, keepdims=True)
            gate_act = _dot(x, wg_ref[e])
            hdn = gate_act * _sigmoid(gate_act) * _dot(x, wu_ref[e])
            acc = acc + ge * _dot(hdn.astype(BF16), wd_ref[e])
        ys_ref[...] = acc.astype(BF16)

    @pl.when(real_ref[i] == 0)
    def _():
        ys_ref[...] = jnp.zeros_like(ys_ref)


def _moe_merge_kernel(slot_ref, h_ref, dest_ref, *refs):
    ys_refs, o_ref = refs[:-1], refs[-1]
    ys = jnp.concatenate([r[...] for r in ys_refs], axis=0)
    nb, nbp = h_ref.shape[0], ys.shape[0]
    slot = lax.broadcasted_iota(jnp.int32, (nb, nbp), 1).astype(F32)
    perm_t = (dest_ref[...] == slot).astype(BF16)
    o_ref[...] = h_ref[...] + _dot(perm_t, ys)


def _moe_merge_norm_kernel(slot_ref, h_ref, dest_ref, *refs):
    ys_refs, g_ref, o_ref = refs[:-2], refs[-2], refs[-1]
    ys = jnp.concatenate([r[...] for r in ys_refs], axis=0)
    nb, nbp = h_ref.shape[0], ys.shape[0]
    slot = lax.broadcasted_iota(jnp.int32, (nb, nbp), 1).astype(F32)
    perm_t = (dest_ref[...] == slot).astype(BF16)
    o_ref[...] = _rms(h_ref[...] + _dot(perm_t, ys)) * g_ref[...]


def _ffn_slots(n_tiles):
    return -(-(n_tiles + N_EXPERT_GROUPS * (FFN_TILES - 1)) // FFN_TILES) * FFN_TILES


def _ffn_schedule(counts, nblk, tiles_per_blk):
    i32 = jnp.int32
    ft, g_n = FFN_TILES, N_EXPERT_GROUPS
    tiles = (counts + FFN_ROWS - 1) // FFN_ROWS
    seg_end = jnp.cumsum(tiles, axis=1)
    tot = jnp.sum(tiles, axis=0)
    grp_pad = (tot + ft - 1) // ft * ft
    grp_end = jnp.cumsum(grp_pad)
    grp_start = grp_end - grp_pad
    before = jnp.cumsum(tiles, axis=0) - tiles
    n_tiles = nblk * tiles_per_blk
    n_slots = _ffn_slots(n_tiles)
    k = jnp.arange(tiles_per_blk, dtype=i32)[None, :]
    used = k < seg_end[:, -1:]
    pos = jnp.zeros((nblk, tiles_per_blk), i32)
    for g in range(g_n):
        in_seg = (k >= seg_end[:, g:g + 1] - tiles[:, g:g + 1]) & (k < seg_end[:, g:g + 1])
        pos = jnp.where(in_seg, grp_start[g] + before[:, g:g + 1] + k - (seg_end[:, g:g + 1] - tiles[:, g:g + 1]), pos)
    gap = grp_pad - tot
    gap_end = jnp.cumsum(gap)
    rank = (jnp.cumsum(jnp.logical_not(used).astype(i32).reshape(-1)) - 1).reshape(nblk, tiles_per_blk)
    free = grp_end[-1] + rank - gap_end[-1]
    for g in range(g_n - 1, -1, -1):
        free = jnp.where(rank < gap_end[g], grp_start[g] + tot[g] + rank - (gap_end[g] - gap[g]), free)
    tile_slot = jnp.where(used, pos, free).reshape(-1).astype(i32)
    s = jnp.arange(n_slots, dtype=i32)
    t = jnp.arange(n_tiles, dtype=i32)
    slot_tile = jnp.sum(jnp.where(tile_slot[None, :] == s[:, None], t[None, :], 0), axis=1).astype(i32)
    s0 = (jnp.arange(n_slots // ft, dtype=i32) * ft)[:, None]
    in_grp = (s0 >= grp_start[None, :]) & (s0 < grp_end[None, :])
    step_real = jnp.sum(jnp.where(in_grp, jnp.clip(tot[None, :] - (s0 - grp_start[None, :]), 0, ft), 0), axis=1)
    last_grp = jnp.max(jnp.where(tot > 0, jnp.arange(g_n, dtype=i32), 0))
    step_grp = jnp.where(step_real > 0, jnp.sum(jnp.where(in_grp, jnp.arange(g_n, dtype=i32)[None, :], 0), axis=1),
                         last_grp)
    return slot_tile, tile_slot, step_grp.astype(i32), step_real.astype(i32)


def _hier_moe_residual(h, norm_g, w_grp, b_grp, w_exp, b_exp, layer, w_gate, w_up, w_down, final_g=None,
                       pending=None):
    n, d = h.shape
    nb = MOE_BLOCK
    nblk = n // nb
    nbp = nb + N_EXPERT_GROUPS * FFN_ROWS
    tiles_per_blk = nbp // FFN_ROWS
    n_exp = N_EXPERT_GROUPS * EXPERTS_PER_GROUP
    w_router = jnp.pad(jnp.concatenate([w_grp, w_exp], axis=1), ((0, 0), (0, LANES - N_EXPERT_GROUPS - n_exp)))
    b_router = jnp.pad(jnp.concatenate([b_grp, b_exp]), (0, LANES - N_EXPERT_GROUPS - n_exp)).reshape(1, LANES)

    rows = pl.BlockSpec((nb, d), lambda i: (i, 0))
    route_in = [_const_spec((1, d)), _const_spec((d, LANES)), _const_spec((1, LANES))]
    route_args = (norm_g.reshape(1, d), w_router.astype(BF16), b_router)
    route_out = [pl.BlockSpec((nbp, d), lambda i: (i, 0)), pl.BlockSpec((nbp, LANES), lambda i: (i, 0)),
                 pl.BlockSpec((nb, 1), lambda i: (i, 0)), pl.BlockSpec((1, 1, LANES), lambda i: (i, 0, 0))]
    route_shape = [jax.ShapeDtypeStruct((nblk * nbp, d), BF16), jax.ShapeDtypeStruct((nblk * nbp, LANES), F32),
                   jax.ShapeDtypeStruct((n, 1), F32), jax.ShapeDtypeStruct((nblk, 1, LANES), F32)]
    if pending is None:
        xs, gs, dest, counts = pl.pallas_call(
            _moe_route_kernel, grid=(nblk,), in_specs=[rows] + route_in, out_specs=route_out, out_shape=route_shape,
            compiler_params=_cparams("arbitrary"), name="moe_route",
        )(h, *route_args)
    else:
        a, w = pending
        h, xs, gs, dest, counts = pl.pallas_call(
            _moe_route_proj_kernel, grid=(nblk,),
            in_specs=[pl.BlockSpec((nb, a.shape[1]), lambda i: (i, 0)), _const_spec(w.shape), rows] + route_in,
            out_specs=[rows] + route_out, out_shape=[jax.ShapeDtypeStruct((n, d), F32)] + route_shape,
            compiler_params=_cparams("arbitrary"), name="moe_route_proj",
        )(a, w.astype(BF16), h, *route_args)

    n_tiles = nblk * tiles_per_blk
    n_slots = _ffn_slots(n_tiles)
    slot_tile, tile_slot, step_grp, step_real = _ffn_schedule(
        counts[:, 0, :N_EXPERT_GROUPS].astype(jnp.int32), nblk, tiles_per_blk)

    def tile_spec(width, j):
        return pl.BlockSpec((FFN_ROWS, width), lambda i, tile, grp, real: (tile[i * FFN_TILES + j], 0))

    def wspec(shape):
        return pl.BlockSpec((None, EXPERTS_PER_GROUP) + shape, lambda i, tile, grp, real: (layer, grp[i], 0, 0))

    f = w_gate.shape[-1]
    ys = pl.pallas_call(
        _moe_ffn_kernel,
        grid_spec=pltpu.PrefetchScalarGridSpec(
            num_scalar_prefetch=3,
            grid=(n_slots // FFN_TILES,),
            in_specs=[tile_spec(d, j) for j in range(FFN_TILES)] + [tile_spec(LANES, j) for j in range(FFN_TILES)]
            + [wspec((d, f)), wspec((d, f)), wspec((f, d))],
            out_specs=pl.BlockSpec((FFN_TILES * FFN_ROWS, d), lambda i, tile, grp, real: (i, 0)),
        ),
        out_shape=jax.ShapeDtypeStruct((n_slots * FFN_ROWS, d), BF16),
        compiler_params=_cparams("arbitrary"),
        name="moe_ffn",
    )(slot_tile, step_grp, step_real, *([xs] * FFN_TILES), *([gs] * FFN_TILES),
      w_gate, w_up, w_down)

    blk = pl.BlockSpec((nb, d), lambda i, slot: (i, 0))
    in_specs = [blk, pl.BlockSpec((nb, 1), lambda i, slot: (i, 0))]
    in_specs += [pl.BlockSpec((FFN_ROWS, d), functools.partial(lambda j, i, slot: (slot[i * tiles_per_blk + j], 0), j))
                 for j in range(tiles_per_blk)]
    args = [h, dest] + [ys] * tiles_per_blk
    body = _moe_merge_kernel
    if final_g is not None:
        in_specs.append(pl.BlockSpec((1, d), lambda i, slot: (0, 0)))
        args.append(final_g.reshape(1, d))
        body = _moe_merge_norm_kernel
    return pl.pallas_call(
        body,
        grid_spec=pltpu.PrefetchScalarGridSpec(num_scalar_prefetch=1, grid=(nblk,), in_specs=in_specs, out_specs=blk),
        out_shape=jax.ShapeDtypeStruct((n, d), F32),
        compiler_params=_cparams("arbitrary"),
        name="moe_merge",
    )(tile_slot, *args)


GROUP_WIDTH = DIL_HEADS * DIL_HEAD_DIM
N_STRIDED = 3 * sum(1 for _, dil in DIL_PATTERNS if dil > 1)


def _qkv_kernel(h_ref, gq_ref, gkv_ref, wq_ref, wk_ref, wv_ref, *refs):
    outs, scratch = refs[:-1], refs[-1]
    n = _rms(h_ref[...])
    xq = (n * gq_ref[...]).astype(BF16)
    xkv = (n * gkv_ref[...]).astype(BF16)
    gw = GROUP_WIDTH
    slot = 0
    full = (_dot(xq, wq_ref[...]), _dot(xkv, wk_ref[...]), _dot(xkv, wv_ref[...]))
    for gi, (_, dil) in enumerate(DIL_PATTERNS):
        cols = slice(gi * gw, (gi + 1) * gw)
        for t in range(3):
            res = full[t][:, cols]
            out_ref = outs[3 * gi + t]
            if dil == 1:
                out_ref[...] = res.astype(BF16)
                continue
            sub = res.shape[0] // dil
            for c in range(gw // LANES):
                scratch[slot, c] = res[:, c * LANES:(c + 1) * LANES]
            for r in range(dil):
                for c in range(gw // LANES):
                    lo = r * gw + c * LANES
                    out_ref[:, lo:lo + LANES] = scratch[slot, c, pl.ds(r, sub, stride=dil), :].astype(BF16)
            slot += 1


def _qkv_proj(h, g_q, g_kv, w_q, w_kv):
    n, d = h.shape
    qw = w_q.shape[1]
    gw = GROUP_WIDTH
    rows = pl.BlockSpec((PROJ_ROWS, d), lambda i: (i, 0))
    out_specs, out_shape = [], []
    for _, dil in DIL_PATTERNS:
        for _ in range(3):
            out_specs.append(pl.BlockSpec((PROJ_ROWS // dil, dil * gw), lambda i: (i, 0)))
            out_shape.append(jax.ShapeDtypeStruct((n // dil, dil * gw), BF16))
    return pl.pallas_call(
        _qkv_kernel,
        grid=(n // PROJ_ROWS,),
        in_specs=[rows, _const_spec((1, d)), _const_spec((1, d)), _const_spec((d, qw)),
                  pl.BlockSpec((d, qw), lambda i: (0, 0)), pl.BlockSpec((d, qw), lambda i: (0, 1))],
        out_specs=out_specs,
        out_shape=out_shape,
        scratch_shapes=[pltpu.VMEM((N_STRIDED, gw // LANES, PROJ_ROWS, LANES), F32)],
        compiler_params=_cparams("arbitrary"),
        name="qkv_proj",
    )(h, g_q.reshape(1, d), g_kv.reshape(1, d), w_q, w_kv, w_kv)


def _attn_kernel(q_ref, kc_ref, kp_ref, vc_ref, vp_ref, o_ref, lse_ref):
    n = pl.program_id(2)
    blk = ATT_BLOCK
    pw = 2 * DIL_HEAD_DIM
    ri = lax.broadcasted_iota(jnp.int32, (blk, blk), 0)
    ci = lax.broadcasted_iota(jnp.int32, (blk, blk), 1)
    lowest = jnp.finfo(F32).min
    cap_prev = jnp.where(ci >= ri, jnp.inf, lowest).astype(F32)
    cap_first = jnp.where(n > 0, cap_prev, lowest)
    cap_own = jnp.where(ci <= ri, jnp.inf, lowest).astype(F32)
    head0 = lax.broadcasted_iota(jnp.int32, (blk, pw), 1) < DIL_HEAD_DIM
    zero = jnp.zeros((blk, pw), BF16)
    lane_t = lax.broadcasted_iota(jnp.int32, (blk, LANES), 1)

    def stack(x):
        return jnp.concatenate([jnp.where(head0, x, zero), jnp.where(head0, zero, x)], axis=0)

    npairs = DIL_HEADS // 2
    streams = [(i, p) for i in range(q_ref.shape[1] // blk) for p in range(q_ref.shape[2] // pw)]

    def rows(i):
        return slice(i * blk, (i + 1) * blk)

    def lanes(p):
        return slice(p * pw, (p + 1) * pw)

    stacked = {}

    def kv_blocks(cur_ref, prev_ref, i, p):
        def one(w):
            name = (id(cur_ref), w, p)
            if name not in stacked:
                stacked[name] = stack(prev_ref[0, :, lanes(p)] if w < 0 else cur_ref[0, rows(w), lanes(p)])
            return stacked[name]
        return jnp.concatenate([one(i - 1), one(i)], axis=0)

    s = []
    for i, p in streams:
        raw = _dot_nt(q_ref[0, rows(i), lanes(p)], kv_blocks(kc_ref, kp_ref, i, p))
        cap = cap_first if i == 0 else cap_prev
        s.append([jnp.minimum(raw[:, :blk], cap), jnp.minimum(raw[:, blk:2 * blk], cap),
                  jnp.minimum(raw[:, 2 * blk:3 * blk], cap_own), jnp.minimum(raw[:, 3 * blk:], cap_own)])
    m_a = [jnp.max(jnp.maximum(x[0], x[2]), axis=-1, keepdims=True) for x in s]
    m_b = [jnp.max(jnp.maximum(x[1], x[3]), axis=-1, keepdims=True) for x in s]
    e = [[jnp.exp2(x[0] - ma), jnp.exp2(x[1] - mb), jnp.exp2(x[2] - ma), jnp.exp2(x[3] - mb)]
         for x, ma, mb in zip(s, m_a, m_b)]
    pb = [jnp.concatenate([y.astype(BF16) for y in x], axis=1) for x in e]
    blk_head = (lax.broadcasted_iota(jnp.int32, (4 * blk, LANES), 0) // blk) % 2
    sel_lane = lax.broadcasted_iota(jnp.int32, (4 * blk, LANES), 1) - DIL_HEADS
    sel = [(sel_lane == 2 * p + blk_head).astype(BF16) for p in range(DIL_HEADS // 2)]
    pv = [_dot(pb[idx], jnp.concatenate([kv_blocks(vc_ref, vp_ref, i, p), sel[p % npairs]], axis=1))
          for idx, (i, p) in enumerate(streams)]
    tiles = {}
    for idx, (i, p) in enumerate(streams):
        o_ref[0, rows(i), lanes(p)] = pv[idx][:, :pw].astype(BF16)
        where, pp = (i, p // npairs), p % npairs
        tile = tiles.get(where, jnp.zeros((blk, LANES), F32)) + pv[idx][:, pw:]
        tiles[where] = jnp.where(lane_t == 2 * pp, m_a[idx], jnp.where(lane_t == 2 * pp + 1, m_b[idx], tile))
    for (i, res), tile in tiles.items():
        lse_ref[0, rows(i), res * LANES:(res + 1) * LANES] = tile


def _attn_group(q, k, v, gi, dilation, bsz, seq):
    gw = GROUP_WIDTH
    sub = seq // dilation
    windows = min(ATT_WINDOWS, sub // ATT_BLOCK)
    step_rows = windows * ATT_BLOCK
    assert sub % step_rows == 0
    res = max(1, ATT_WINDOWS // windows)
    if dilation % res:
        res = 1
    view = (bsz, sub, dilation * gw)
    qv, kv_, vv = q.reshape(view), k.reshape(view), v.reshape(view)
    cur = pl.BlockSpec((1, step_rows, res * gw), lambda b, r, n: (b, n, r))
    prev = pl.BlockSpec((1, ATT_BLOCK, res * gw), lambda b, r, n: (b, jnp.maximum(n * windows - 1, 0), r))
    lse_spec = pl.BlockSpec((1, step_rows, res * LANES), lambda b, r, n: (b, n, r))
    o, lse = pl.pallas_call(
        _attn_kernel,
        grid=(bsz, dilation // res, sub // step_rows),
        in_specs=[cur, cur, prev, cur, prev],
        out_specs=[cur, lse_spec],
        out_shape=[jax.ShapeDtypeStruct(view, BF16),
                   jax.ShapeDtypeStruct((bsz, sub, dilation * LANES), F32)],
        compiler_params=_cparams("arbitrary", "arbitrary", "arbitrary"),
        name=f"attn_group{gi}",
    )(qv, kv_, kv_, vv, vv)
    return o.reshape(bsz * sub, dilation * gw), lse.reshape(bsz * sub, dilation * LANES)


def _attn_merge_kernel(o0_ref, o1_ref, o2_ref, l0_ref, l1_ref, l2_ref, w_ref, res_ref, out_ref, *scratch):
    rows = res_ref.shape[0]
    gw = GROUP_WIDTH
    outs, lses = [], []
    slot = 0
    for (_, dil), o_ref, l_ref in zip(DIL_PATTERNS, (o0_ref, o1_ref, o2_ref), (l0_ref, l1_ref, l2_ref)):
        if dil == 1:
            outs.append(o_ref[...].astype(F32))
            lses.append(l_ref[...])
            continue
        o_scr, l_scr = scratch[2 * slot], scratch[2 * slot + 1]
        sub = rows // dil
        for r in range(dil):
            for c in range(gw // LANES):
                lo = r * gw + c * LANES
                o_scr[c, pl.ds(r, sub, stride=dil), :] = o_ref[:, lo:lo + LANES].astype(F32)
            l_scr[pl.ds(r, sub, stride=dil), :] = l_ref[:, r * LANES:(r + 1) * LANES]
        outs.append(jnp.concatenate([o_scr[c] for c in range(gw // LANES)], axis=1))
        lses.append(l_scr[...])
        slot += 1
    m = jnp.maximum(jnp.maximum(lses[0], lses[1]), lses[2])
    e = [jnp.exp2(l - m) for l in lses]
    sums = [pltpu.roll(l, LANES - DIL_HEADS, 1) for l in lses]
    denom = e[0] * sums[0] + e[1] * sums[1] + e[2] * sums[2]
    is_head = lax.broadcasted_iota(jnp.int32, (rows, LANES), 1) < DIL_HEADS
    expand = (lax.broadcasted_iota(jnp.int32, (LANES, gw), 1) // DIL_HEAD_DIM
              == lax.broadcasted_iota(jnp.int32, (LANES, gw), 0)).astype(BF16)
    alpha = jnp.concatenate([jnp.where(is_head, x / denom, 0.0) for x in e], axis=0)
    hi = alpha.astype(BF16)
    lo = (alpha - hi.astype(F32)).astype(BF16)
    wide = _dot(jnp.concatenate([hi, lo], axis=1), jnp.concatenate([expand, expand], axis=0))
    mix = jnp.zeros((rows, gw), F32)
    for g in range(len(outs)):
        mix = mix + wide[g * rows:(g + 1) * rows] * outs[g]
    out_ref[...] = res_ref[...] + _dot(mix.astype(BF16), w_ref[...])


def _attn_merge(outs, lses, w_o, res):
    n, d = res.shape
    gw = GROUP_WIDTH
    rows = MERGE_ROWS
    o_specs = [pl.BlockSpec((rows // dil, dil * gw), lambda i: (i, 0)) for _, dil in DIL_PATTERNS]
    l_specs = [pl.BlockSpec((rows // dil, dil * LANES), lambda i: (i, 0)) for _, dil in DIL_PATTERNS]
    full = pl.BlockSpec((rows, d), lambda i: (i, 0))
    scratch = []
    for _, dil in DIL_PATTERNS:
        if dil > 1:
            scratch += [pltpu.VMEM((gw // LANES, rows, LANES), F32), pltpu.VMEM((rows, LANES), F32)]
    return pl.pallas_call(
        _attn_merge_kernel,
        grid=(n // rows,),
        in_specs=o_specs + l_specs + [_const_spec(w_o.shape), full],
        out_specs=full,
        out_shape=jax.ShapeDtypeStruct((n, d), F32),
        scratch_shapes=scratch,
        compiler_params=_cparams("arbitrary"),
        name="attn_merge",
    )(*outs, *lses, w_o.astype(BF16), res)


def kernel(x, norm_mix_g, norm_ffn_g, rwkv_mu, rwkv_w_r, rwkv_w_k, rwkv_w_v, rwkv_w0, rwkv_w1, rwkv_w2,
           rwkv_a0, rwkv_a1, rwkv_a2, rwkv_g1, rwkv_g2, rwkv_k_k, rwkv_k_a, rwkv_r_k, rwkv_ln_w, rwkv_ln_b,
           rwkv_w_o, kv_norm_g, w_kv, attn_w_q, attn_w_o, moe_w_grp, moe_b_grp, moe_w_exp, moe_b_exp,
           moe_w_gate, moe_w_up, moe_w_down, final_norm_g):
    bsz, seq, d = x.shape
    n = bsz * seq
    depth = norm_mix_g.shape[0]
    n_rwkv = rwkv_mu.shape[0]
    h = x.reshape(n, d)
    later_f32 = (moe_w_gate, moe_w_up, moe_w_down, attn_w_q, w_kv)
    later_scale = (1.0, 1.0, 1.0, QK_SCALE, 1.0)
    later_w = None
    q = k_sh = v_sh = None
    for layer in range(depth):
        pending = None
        if layer < n_rwkv:
            i = layer
            r, k, v, a, lw, g = _rwkv_pre(h.reshape(bsz, seq, d), norm_mix_g[layer], rwkv_mu[i], rwkv_w0[i], rwkv_a0[i],
                                          rwkv_w_r[i], rwkv_w_k[i], rwkv_w_v[i], rwkv_w1[i], rwkv_w2[i],
                                          rwkv_a1[i], rwkv_a2[i], rwkv_g1[i], rwkv_g2[i])
            casts = () if later_w is not None else tuple(w.reshape(-1, w.shape[-1]) for w in later_f32)
            y, cast = _rwkv_rec(r, k, v, a, lw, g, rwkv_k_k[i], rwkv_k_a[i], rwkv_r_k[i], rwkv_ln_w[i], rwkv_ln_b[i],
                                casts=casts, cast_scales=later_scale if casts else ())
            if casts:
                later_w = tuple(c.reshape(w.shape) for c, w in zip(cast, later_f32))
            pending = (y.reshape(n, d), rwkv_w_o[i])
        else:
            i = layer - n_rwkv
            if later_w is None:
                later_w = tuple((w * s).astype(BF16) for w, s in zip(later_f32, later_scale))
            qkv = _qkv_proj(h, norm_mix_g[layer], kv_norm_g, later_w[3][i], later_w[4])
            q = qkv[0::3]
            if i == 0:
                k_sh, v_sh = qkv[1::3], qkv[2::3]
            outs, lses = [], []
            for gi, (window, dilation) in enumerate(DIL_PATTERNS):
                assert window // dilation == ATT_BLOCK and seq % window == 0
                o, lse = _attn_group(q[gi], k_sh[gi], v_sh[gi], gi, dilation, bsz, seq)
                outs.append(o)
                lses.append(lse)
            h = _attn_merge(outs, lses, attn_w_o[i], h)
        last = layer == depth - 1
        if later_w is None:
            later_w = tuple((w * s).astype(BF16) for w, s in zip(later_f32, later_scale))
        h = _hier_moe_residual(h, norm_ffn_g[layer], moe_w_grp[layer], moe_b_grp[layer], moe_w_exp[layer],
                               moe_b_exp[layer], layer, *later_w[:3], final_g=final_norm_g if last else None,
                               pending=pending)
    return h.reshape(bsz, seq, d)
```
